```python
import jax
import jax.numpy as jnp
from jax import lax
import numpy as np

D_MODEL = 1024
BATCH = 1
SEQ = 16384
DEPTH = 4

GRID_W = 64
CTX_LEN = 256
HEAD_DIM = 64
BRANCH_WIDTH = 256
N_BRANCH = 4
A_HEADS = 4
A_KV_HEADS = 2
A_GROUP = A_HEADS // A_KV_HEADS
Q_BLOCK = 128
ROPE_THETA = 10000.0
B_HEADS = 4
B_KEY_DIM = 64
B_VAL_DIM = 64
B_CHUNK = 64
C_GROUPS = 4
C_GROUP_DIM = BRANCH_WIDTH // C_GROUPS
C_WINDOWS = (2, 4, 8, 16)
D_HEADS = 4
NA_WIN_R = 8
NA_WIN_C = 16
D_FF = 2816
N_MOD = 9
EPS = 1e-6
ATTN_SCALE = HEAD_DIM ** -0.5

IN_SPLITS = (
    A_HEADS * HEAD_DIM, A_KV_HEADS * HEAD_DIM, A_KV_HEADS * HEAD_DIM,
    B_HEADS * B_KEY_DIM, B_HEADS * B_KEY_DIM, B_HEADS * B_KEY_DIM, B_HEADS * B_VAL_DIM, B_HEADS * B_VAL_DIM,
    BRANCH_WIDTH,
    D_HEADS * HEAD_DIM, D_HEADS * HEAD_DIM, D_HEADS * HEAD_DIM,
    N_BRANCH * D_MODEL,
)
IN_WIDTH = sum(IN_SPLITS)

kernel_name = 'hybrid_prefix_dit_block'


def rmsnorm(x, g):
    xf = x.astype(jnp.float32)
    y = xf * lax.rsqrt(jnp.mean(xf * xf, axis=-1, keepdims=True) + EPS)
    return (y * g.astype(jnp.float32)).astype(x.dtype)


def modulate(x, shift, scale):
    return x * (1 + scale) + shift


def swiglu(z, wg, wu, wd):
    return (jax.nn.silu(z @ wg) * (z @ wu)) @ wd


def ffn_half(z, shift, scale, gate, norm, wg, wu, wd):
    return z + 0.5 * gate * swiglu(modulate(rmsnorm(z, norm), shift, scale), wg, wu, wd)


def rope_tables(n):
    t = jnp.arange(n, dtype=jnp.int32)
    half = HEAD_DIM // 2
    inv = 1.0 / (ROPE_THETA ** (jnp.arange(0, half, 2, dtype=jnp.float32) / half))
    ang_r = (t // GRID_W).astype(jnp.float32)[:, None] * inv
    ang_c = (t % GRID_W).astype(jnp.float32)[:, None] * inv
    tab = jnp.stack([jnp.cos(ang_r), jnp.sin(ang_r), jnp.cos(ang_c), jnp.sin(ang_c)])
    return tab[:, :, None, :]


def _rotate(x, cos, sin):
    x1, x2 = jnp.split(x, 2, axis=-1)
    return jnp.concatenate([x1 * cos - x2 * sin, x2 * cos + x1 * sin], axis=-1)


def apply_axial_rope(x, rope):
    half = HEAD_DIM // 2
    xf = x.astype(jnp.float32)
    out = jnp.concatenate([_rotate(xf[..., :half], rope[0], rope[1]),
                           _rotate(xf[..., half:], rope[2], rope[3])], axis=-1)
    return out.astype(x.dtype)


def dense_attend(q, k, v):
    s = jnp.einsum('bqhgd,bkhd->bhgqk', q, k).astype(jnp.float32) * ATTN_SCALE
    p = jax.nn.softmax(s, axis=-1).astype(v.dtype)
    return jnp.einsum('bhgqk,bkhd->bqhgd', p, v)


def gqa_mixer(q, k, v, qc, kc, vc, q_gain, k_gain, rope, with_ctx_out):
    b, t = q.shape[:2]

    def norm_heads(a, n, g):
        return rmsnorm(a.reshape(a.shape[0], a.shape[1], n, HEAD_DIM), g)

    ql = apply_axial_rope(norm_heads(q, A_HEADS, q_gain), rope)
    kl = apply_axial_rope(norm_heads(k, A_KV_HEADS, k_gain), rope)
    vl = v.reshape(b, t, A_KV_HEADS, HEAD_DIM)
    kcx = norm_heads(kc, A_KV_HEADS, k_gain)
    vcx = vc.reshape(b, vc.shape[1], A_KV_HEADS, HEAD_DIM)
    k_all = jnp.concatenate([kcx, kl], axis=1)
    v_all = jnp.concatenate([vcx, vl], axis=1)
    qb = ql.reshape(b, t // Q_BLOCK, Q_BLOCK, A_KV_HEADS, A_GROUP, HEAD_DIM)
    ob = lax.map(lambda blk: dense_attend(blk, k_all, v_all), jnp.moveaxis(qb, 1, 0))
    y = jnp.moveaxis(ob, 0, 1).reshape(b, t, BRANCH_WIDTH)
    if not with_ctx_out:
        return y, None
    qcx = norm_heads(qc, A_HEADS, q_gain).reshape(b, qc.shape[1], A_KV_HEADS, A_GROUP, HEAD_DIM)
    yc = dense_attend(qcx, kcx, vcx).reshape(b, qc.shape[1], BRANCH_WIDTH)
    return y, yc


def gla_chunk_scan(q, k, v, log_f, s0):
    b, t, h, _ = q.shape
    n = t // B_CHUNK

    def to_chunks(a):
        return a.reshape(b, n, B_CHUNK, h, a.shape[-1]).transpose(1, 0, 3, 2, 4)

    causal = jnp.tril(jnp.ones((B_CHUNK, B_CHUNK), dtype=bool))[:, :, None]

    def step(s, inp):
        qc, kc, vc, lc = inp
        cum = jnp.cumsum(lc, axis=2)
        diff = cum[:, :, :, None, :] - cum[:, :, None, :, :]
        decay = jnp.exp(jnp.where(causal, diff, -jnp.inf))
        att = jnp.einsum('bhtd,bhsd,bhtsd->bhts', qc, kc, decay)
        o = jnp.einsum('bhts,bhsv->bhtv', att, vc) + jnp.einsum('bhtd,bhdv->bhtv', qc * jnp.exp(cum), s)
        last = cum[:, :, -1:, :]
        s_new = jnp.exp(last[:, :, 0, :])[..., None] * s + jnp.einsum('bhsd,bhsv->bhdv', kc * jnp.exp(last - cum), vc)
        return s_new, o

    s_fin, o = lax.scan(step, s0, (to_chunks(q), to_chunks(k), to_chunks(v), to_chunks(log_f)))
    return o.transpose(1, 0, 3, 2, 4).reshape(b, t, h, v.shape[-1]), s_fin


def hgrn2_mixer(lat, cxt, lb, o_gain, with_ctx_out):
    def heads(a):
        return a.reshape(a.shape[0], a.shape[1], B_HEADS, -1).astype(jnp.float32)

    def gates(f_pre, lbd):
        f = lbd + (1.0 - lbd) * jax.nn.sigmoid(heads(f_pre))
        return 1.0 - f, jnp.log(f)

    ql, ffl, fbl, il, gl = lat
    qc, ffc, fbc, ic, gc = cxt
    qscale = B_KEY_DIM ** -0.5
    qlh, ilh = heads(ql) * qscale, heads(il)
    qch, ich = heads(qc) * qscale, heads(ic)
    s0 = jnp.zeros((ql.shape[0], B_HEADS, B_KEY_DIM, B_VAL_DIM), jnp.float32)
    o_lat, o_ctx = None, None
    for f_l, f_c, lbd, reverse in ((ffl, ffc, lb[0], False), (fbl, fbc, lb[1], True)):
        rev = (lambda a: jnp.flip(a, axis=1)) if reverse else (lambda a: a)
        kl, lfl = gates(f_l, lbd)
        kc, lfc = gates(f_c, lbd)
        oc, s_ctx = gla_chunk_scan(rev(qch), rev(kc), rev(ich), rev(lfc), s0)
        ol, _ = gla_chunk_scan(rev(qlh), rev(kl), rev(ilh), rev(lfl), s_ctx)
        o_lat = rev(ol) if o_lat is None else o_lat + rev(ol)
        o_ctx = rev(oc) if o_ctx is None else o_ctx + rev(oc)

    def readout(o, g):
        y = rmsnorm(o, o_gain) * jax.nn.silu(heads(g))
        return y.reshape(g.shape[0], g.shape[1], BRANCH_WIDTH).astype(g.dtype)

    y = readout(o_lat, gl)
    if not with_ctx_out:
        return y, None
    return y, readout(o_ctx, gc)


def window_mean(x, w):
    t = x.shape[1]
    xf = x.astype(jnp.float32)
    cs = jnp.concatenate([jnp.zeros_like(xf[:, :1]), lax.cumsum(xf, axis=1)], axis=1)
    pos = jnp.arange(t)
    lo = jnp.clip(pos - w // 2, 0, t)
    hi = jnp.clip(pos - w // 2 + w, 0, t)
    mean = (cs[:, hi] - cs[:, lo]) / (hi - lo).astype(jnp.float32)[None, :, None]
    return mean.astype(x.dtype)


def pool_mixer(xp, w_group, scale):
    b, t, _ = xp.shape
    xg = xp.reshape(b, t, C_GROUPS, C_GROUP_DIM)
    pooled = jnp.stack([window_mean(xg[:, :, g], w) - xg[:, :, g] for g, w in enumerate(C_WINDOWS)], axis=2)
    y = jnp.einsum('btgc,gcd->btgd', pooled, w_group).reshape(b, t, BRANCH_WIDTH)
    return y * scale


def na_mixer(q, k, v, qc, kc, vc, rel_bias, with_ctx_out):
    b, t = q.shape[:2]
    rows = t // GRID_W
    wr = min(NA_WIN_R, rows)
    wc = NA_WIN_C

    def grid(a):
        return a.reshape(b, rows, GRID_W, D_HEADS, HEAD_DIM)

    qg, kg, vg = grid(q), grid(k), grid(v)
    n_ctx = kc.shape[1]
    kcx = kc.reshape(b, n_ctx, D_HEADS, HEAD_DIM)
    vcx = vc.reshape(b, n_ctx, D_HEADS, HEAD_DIM)
    col = jnp.arange(GRID_W)
    col_idx = jnp.clip(col - wc // 2, 0, GRID_W - wc)[:, None] + jnp.arange(wc)
    dc_idx = col_idx - col[:, None] + (NA_WIN_C - 1)

    def row_block(r):
        rs = jnp.clip(r - wr // 2, 0, rows - wr)
        k_win = lax.dynamic_slice_in_dim(kg, rs, wr, axis=1)[:, :, col_idx]
        v_win = lax.dynamic_slice_in_dim(vg, rs, wr, axis=1)[:, :, col_idx]
        q_row = lax.dynamic_index_in_dim(qg, r, axis=1, keepdims=False)
        dr_idx = rs + jnp.arange(wr) - r + (NA_WIN_R - 1)
        bias = rel_bias[:, dr_idx][:, :, dc_idx]
        bias = jnp.transpose(bias, (0, 2, 1, 3)).reshape(D_HEADS, GRID_W, wr * wc).astype(jnp.float32)
        s_loc = jnp.einsum('bqhd,bwqjhd->bhqwj', q_row, k_win).reshape(b, D_HEADS, GRID_W, wr * wc)
        s_loc = s_loc.astype(jnp.float32) * ATTN_SCALE + bias
        s_ctx = jnp.einsum('bqhd,bkhd->bhqk', q_row, kcx).astype(jnp.float32) * ATTN_SCALE
        p = jax.nn.softmax(jnp.concatenate([s_ctx, s_loc], axis=-1), axis=-1).astype(v.dtype)
        p_ctx = p[..., :n_ctx]
        p_loc = p[..., n_ctx:].reshape(b, D_HEADS, GRID_W, wr, wc)
        return jnp.einsum('bhqk,bkhd->bqhd', p_ctx, vcx) + jnp.einsum('bhqwj,bwqjhd->bqhd', p_loc, v_win)

    o = lax.map(row_block, jnp.arange(rows))
    y = jnp.moveaxis(o, 0, 1).reshape(b, t, BRANCH_WIDTH)
    if not with_ctx_out:
        return y, None
    qcx = qc.reshape(b, n_ctx, D_HEADS, 1, HEAD_DIM)
    yc = dense_attend(qcx, kcx, vcx).reshape(b, n_ctx, BRANCH_WIDTH)
    return y, yc


def merge_branches(branches, gate_pre, w_branch, w_out):
    gates = jax.nn.sigmoid(gate_pre).reshape(gate_pre.shape[0], gate_pre.shape[1], N_BRANCH, D_MODEL)
    merged = gates[..., 0, :] * (branches[0] @ w_branch[0])
    for n in range(1, N_BRANCH):
        merged = merged + gates[..., n, :] * (branches[n] @ w_branch[n])
    return merged @ w_out


def token_mixing(xn, cn, w_in, a_q_norm, a_k_norm, lb, b_o_norm, c_w_group, c_scale, d_rel_bias,
                 w_branch, w_out, rope, with_ctx_out):
    offs = np.cumsum(IN_SPLITS)[:-1].tolist()
    pl = jnp.split(xn @ w_in, offs, axis=-1)
    pc = jnp.split(cn @ w_in, offs, axis=-1)
    ya, ya_c = gqa_mixer(pl[0], pl[1], pl[2], pc[0], pc[1], pc[2], a_q_norm, a_k_norm, rope, with_ctx_out)
    yb, yb_c = hgrn2_mixer(pl[3:8], pc[3:8], lb, b_o_norm, with_ctx_out)
    yp = pool_mixer(pl[8], c_w_group, c_scale)
    yd, yd_c = na_mixer(pl[9], pl[10], pl[11], pc[9], pc[10], pc[11], d_rel_bias, with_ctx_out)
    y_lat = merge_branches((ya, yb, yp, yd), pl[12], w_branch, w_out)
    if not with_ctx_out:
        return y_lat, None
    yp_c = pool_mixer(pc[8], c_w_group, c_scale)
    y_ctx = merge_branches((ya_c, yb_c, yp_c, yd_c), pc[12], w_branch, w_out)
    return y_lat, y_ctx


def setup_inputs(seed: int = 0) -> dict:
    key = jax.random.key(seed)
    ks = jax.random.split(key, 26)
    L, D = DEPTH, D_MODEL

    def w(k, shape, fan_in, gain=1.0):
        return jax.random.normal(k, shape, jnp.float32) * (gain * fan_in ** -0.5)

    def g(k, shape):
        return 1.0 + 0.02 * jax.random.normal(k, shape, jnp.float32)

    def nrm(k, shape, s):
        return s * jax.random.normal(k, shape, jnp.float32)

    return {
        'x': nrm(ks[0], (BATCH, SEQ, D), 1.0),
        'c': nrm(ks[1], (BATCH, D), 1.0),
        'ctx': nrm(ks[2], (BATCH, CTX_LEN, D), 1.0),
        'c_ctx': nrm(ks[3], (D,), 1.0),
        'w_ada': w(ks[4], (L, D, N_MOD * D), D, 0.5),
        'b_ada': nrm(ks[5], (L, N_MOD * D), 0.01),
        'ffn1_norm': g(ks[6], (L, D)),
        'ffn1_w_gate': w(ks[7], (L, D, D_FF), D),
        'ffn1_w_up': w(ks[8], (L, D, D_FF), D),
        'ffn1_w_down': w(ks[9], (L, D_FF, D), D_FF),
        'mix_norm': g(ks[10], (L, D)),
        'w_in': w(ks[11], (L, D, IN_WIDTH), D),
        'a_q_norm': g(ks[12], (L, HEAD_DIM)),
        'a_k_norm': g(ks[13], (L, HEAD_DIM)),
        'b_lb_logits': nrm(ks[14], (L, 2, B_HEADS * B_KEY_DIM), 0.5),
        'b_o_norm': g(ks[15], (L, B_VAL_DIM)),
        'c_w_group': w(ks[16], (L, C_GROUPS, C_GROUP_DIM, C_GROUP_DIM), C_GROUP_DIM),
        'c_scale': g(ks[17], (L, BRANCH_WIDTH)),
        'd_rel_bias': nrm(ks[18], (L, D_HEADS, 2 * NA_WIN_R - 1, 2 * NA_WIN_C - 1), 0.1),
        'w_branch': w(ks[19], (L, N_BRANCH, BRANCH_WIDTH, D), BRANCH_WIDTH),
        'w_out': w(ks[20], (L, D, D), D),
        'ffn2_norm': g(ks[21], (L, D)),
        'ffn2_w_gate': w(ks[22], (L, D, D_FF), D),
        'ffn2_w_up': w(ks[23], (L, D, D_FF), D),
        'ffn2_w_down': w(ks[24], (L, D_FF, D), D_FF),
        'final_norm': g(ks[25], (D,)),
    }


def reference(x, c, ctx, c_ctx, w_ada, b_ada, ffn1_norm, ffn1_w_gate, ffn1_w_up, ffn1_w_down,
              mix_norm, w_in, a_q_norm, a_k_norm, b_lb_logits, b_o_norm, c_w_group, c_scale,
              d_rel_bias, w_branch, w_out, ffn2_norm, ffn2_w_gate, ffn2_w_up, ffn2_w_down, final_norm):
    rope = rope_tables(x.shape[1])
    lb_all = jnp.cumsum(jax.nn.softmax(b_lb_logits.astype(jnp.float32), axis=0), axis=0)
    lb_all = (lb_all - lb_all[:1]).reshape(DEPTH, 2, B_HEADS, B_KEY_DIM)
    h = ctx
    for l in range(DEPTH):
        with_ctx_out = l < DEPTH - 1
        mod = [m[:, None, :] for m in jnp.split(jax.nn.silu(c) @ w_ada[l] + b_ada[l], N_MOD, axis=-1)]
        mod_c = jnp.split(jax.nn.silu(c_ctx) @ w_ada[l] + b_ada[l], N_MOD, axis=-1)
        x = ffn_half(x, mod[0], mod[1], mod[2], ffn1_norm[l], ffn1_w_gate[l], ffn1_w_up[l], ffn1_w_down[l])
        h = ffn_half(h, mod_c[0], mod_c[1], mod_c[2], ffn1_norm[l], ffn1_w_gate[l], ffn1_w_up[l], ffn1_w_down[l])
        xn = modulate(rmsnorm(x, mix_norm[l]), mod[3], mod[4])
        hn = modulate(rmsnorm(h, mix_norm[l]), mod_c[3], mod_c[4])
        y, y_c = token_mixing(xn, hn, w_in[l], a_q_norm[l], a_k_norm[l], lb_all[l], b_o_norm[l],
                              c_w_group[l], c_scale[l], d_rel_bias[l], w_branch[l], w_out[l], rope, with_ctx_out)
        x = x + mod[5] * y
        x = ffn_half(x, mod[6], mod[7], mod[8], ffn2_norm[l], ffn2_w_gate[l], ffn2_w_up[l], ffn2_w_down[l])
        if with_ctx_out:
            h = h + mod_c[5] * y_c
            h = ffn_half(h, mod_c[6], mod_c[7], mod_c[8], ffn2_norm[l], ffn2_w_gate[l], ffn2_w_up[l], ffn2_w_down[l])
    return rmsnorm(x, final_norm)
```

```python
import functools

import numpy as np
import jax
import jax.numpy as jnp
from jax import lax
from jax.experimental import pallas as pl
from jax.experimental.pallas import tpu as pltpu

F32 = jnp.float32
BF16 = jnp.bfloat16

D_MODEL = 1024
GRID_W = 64
HEAD_DIM = 64
BRANCH_WIDTH = 256
N_BRANCH = 4
A_HEADS = 4
A_KV_HEADS = 2
ROPE_THETA = 10000.0
B_HEADS = 4
B_KEY_DIM = 64
C_WINDOWS = (2, 4, 8, 16)
D_HEADS = 4
NA_WIN_R = 8
NA_WIN_C = 16
D_FF = 2816
N_MOD = 9
EPS = 1e-6
ATTN_SCALE = HEAD_DIM ** -0.5
NEG_BIG = -1e30

V7X_VMEM_BYTES = 64 * 1024 * 1024
VMEM_LIMIT = V7X_VMEM_BYTES - 8 * 1024 * 1024

SEG_AQ = (0, 256)
SEG_AK = (256, 384)
SEG_AV = (384, 640)
SEG_BQ = (640, 896)
SEG_BFF = (896, 1152)
SEG_BFB = (1152, 1408)
SEG_BI = (1408, 1664)
SEG_BG = (1664, 1920)
SEG_CX = (1920, 2176)
SEG_DQ = (2176, 2432)
SEG_DK = (2432, 2688)
SEG_DV = (2688, 2944)
SEG_GATE = (2944, 2944 + N_BRANCH * D_MODEL)
W_ALL_WIDTH = SEG_GATE[1]

TOKEN_TILE = 256
FLASH_TQ = 512
NA_QROWS = 8
NA_QB = NA_QROWS * GRID_W
NA_HALO = 256
HG_CHUNK = 128
HG_SUB = 16


def _cparams(sem):
    return pltpu.CompilerParams(dimension_semantics=sem, vmem_limit_bytes=VMEM_LIMIT)


def _const_spec(shape):
    nd = len(shape)
    return pl.BlockSpec(shape, lambda *_: (0,) * nd, pipeline_mode=pl.Buffered(1))


def _sigmoid(x):
    return 1.0 / (1.0 + jnp.exp(-x))


def _silu(x):
    return x * _sigmoid(x)


def _dot(a, b):
    return jnp.dot(a, b, preferred_element_type=F32)


def _dot_nt(a, b):
    return lax.dot_general(a, b, (((1,), (1,)), ((), ())), preferred_element_type=F32)


def _dot_tn(a, b):
    return lax.dot_general(a, b, (((0,), (0,)), ((), ())), preferred_element_type=F32)


def _split_dot(x, m):
    hi = x.astype(BF16)
    r1 = x - hi.astype(F32)
    mid = r1.astype(BF16)
    lo = (r1 - mid.astype(F32)).astype(BF16)
    return _dot(hi, m) + _dot(mid, m) + _dot(lo, m)


def _split_dot_left(m, x):
    hi = x.astype(BF16)
    r1 = x - hi.astype(F32)
    mid = r1.astype(BF16)
    lo = (r1 - mid.astype(F32)).astype(BF16)
    return _dot(m, hi) + _dot(m, mid) + _dot(m, lo)


def _row_mod(mods_ref, k, is_lat):
    return jnp.where(is_lat, mods_ref[0, k:k + 1, :], mods_ref[1, k:k + 1, :])


def _rms_rows(x):
    return x * lax.rsqrt(jnp.mean(x * x, axis=-1, keepdims=True) + EPS)


def _ada_kernel(s_ref, w_ref, b_ref, o_ref):
    o_ref[0] = _dot(_silu(s_ref[...]).astype(BF16), w_ref[0].astype(BF16)) + b_ref[0]


def _ada_mods(c, c_ctx, w_ada, b_ada):
    depth = w_ada.shape[0]
    width = w_ada.shape[2]
    tn = 1024
    s = jnp.zeros((8, D_MODEL), F32).at[0].set(c[0]).at[1].set(c_ctx)
    out = pl.pallas_call(
        _ada_kernel,
        grid=(depth, width // tn),
        in_specs=[
            pl.BlockSpec((8, D_MODEL), lambda l, j: (0, 0)),
            pl.BlockSpec((1, D_MODEL, tn), lambda l, j: (l, 0, j)),
            pl.BlockSpec((1, 1, tn), lambda l, j: (l, 0, j)),
        ],
        out_specs=pl.BlockSpec((1, 8, tn), lambda l, j: (l, 0, j)),
        out_shape=jax.ShapeDtypeStruct((depth, 8, width), F32),
        compiler_params=_cparams(("parallel", "parallel")),
        name="ada_mods",
    )(s, w_ada, b_ada.reshape(depth, 1, width))
    return out[:, :2].reshape(depth, 2, N_MOD, D_MODEL)


def _ffn_kernel(x_ref, mods_ref, nrm_ref, wg_ref, wu_ref, wd_ref, o_ref, *, t_lat, tm, k0):
    i = pl.program_id(0)
    row = i * tm + lax.broadcasted_iota(jnp.int32, (tm, 1), 0)
    is_lat = row < t_lat
    x = x_ref[...]
    z = _rms_rows(x) * nrm_ref[...]
    z = (z * (1.0 + _row_mod(mods_ref, k0 + 1, is_lat)) + _row_mod(mods_ref, k0, is_lat)).astype(BF16)
    g = _dot(z, wg_ref[...])
    u = _dot(z, wu_ref[...])
    a = (_silu(g) * u).astype(BF16)
    y = _dot(a, wd_ref[...])
    o_ref[...] = x + 0.5 * _row_mod(mods_ref, k0 + 2, is_lat) * y


def _ffn_half(x_all, mods, nrm, wg, wu, wd, *, t_lat, k0, n_rows):
    n, d = x_all.shape
    tm = TOKEN_TILE
    f = wg.shape[1]
    return pl.pallas_call(
        functools.partial(_ffn_kernel, t_lat=t_lat, tm=tm, k0=k0),
        grid=(n_rows // tm,),
        in_specs=[
            pl.BlockSpec((tm, d), lambda i: (i, 0)),
            _const_spec((2, N_MOD, d)),
            _const_spec((1, d)),
            _const_spec((d, f)),
            _const_spec((d, f)),
            _const_spec((f, d)),
        ],
        out_specs=pl.BlockSpec((tm, d), lambda i: (i, 0)),
        out_shape=jax.ShapeDtypeStruct((n, d), F32),
        input_output_aliases={0: 0},
        compiler_params=_cparams(("parallel",)),
        name="ffn_half",
    )(x_all, mods, nrm, wg, wu, wd)


def _head_norm(y, gain, segm):
    ms = _split_dot(y * y, segm)
    return y * lax.rsqrt(ms + EPS) * gain


def _rope(y, cos, sin_signed):
    w = y.shape[1]
    lane = lax.broadcasted_iota(jnp.int32, y.shape, 1)
    partner = jnp.where((lane & 31) < 16, pltpu.roll(y, w - 16, 1), pltpu.roll(y, 16, 1))
    return y * cos + partner * sin_signed


def _inproj_kernel(x_ref, mods_ref, nrm_ref, w_ref, cos_ref, sin_ref, gq_ref, gk_ref, segm_ref,
                   aq_ref, ak_ref, av_ref, bq_ref, bff_ref, bfb_ref, bi_ref, bg_ref, cx_ref,
                   dq_ref, dk_ref, dv_ref, gt_ref, *, t_lat, tm):
    i = pl.program_id(0)
    row = i * tm + lax.broadcasted_iota(jnp.int32, (tm, 1), 0)
    is_lat = row < t_lat
    z = _rms_rows(x_ref[...]) * nrm_ref[...]
    z = (z * (1.0 + _row_mod(mods_ref, 4, is_lat)) + _row_mod(mods_ref, 3, is_lat)).astype(BF16)

    def proj(seg):
        return _dot(z, w_ref[:, seg[0]:seg[1]])

    cos = cos_ref[...]
    sin = sin_ref[...]
    q = _head_norm(proj(SEG_AQ), gq_ref[...], segm_ref[...])
    q = _rope(q, jnp.concatenate([cos, cos], axis=1), jnp.concatenate([sin, sin], axis=1))
    aq_ref[...] = (q * ATTN_SCALE).astype(BF16)
    k = _head_norm(proj(SEG_AK), gk_ref[...], segm_ref[0:128, 0:128])
    ak_ref[...] = _rope(k, cos, sin).astype(BF16)
    lane = lax.broadcasted_iota(jnp.int32, (tm, SEG_AV[1] - SEG_AV[0]), 1)
    av_ref[...] = (proj(SEG_AV) + jnp.where((lane & 127) == HEAD_DIM, 1.0, 0.0)).astype(BF16)

    bq_ref[...] = (proj(SEG_BQ) * (B_KEY_DIM ** -0.5)).astype(BF16)
    bff_ref[...] = proj(SEG_BFF)
    bfb_ref[...] = proj(SEG_BFB)
    bi_ref[...] = proj(SEG_BI).astype(BF16)
    bg_ref[...] = proj(SEG_BG).astype(BF16)
    cx_ref[...] = proj(SEG_CX)
    dq_ref[...] = (proj(SEG_DQ) * ATTN_SCALE).astype(BF16)
    dk_ref[...] = proj(SEG_DK).astype(BF16)
    dv_ref[...] = proj(SEG_DV).astype(BF16)
    gw = 512
    for c0 in range(SEG_GATE[0], SEG_GATE[1], gw):
        o0 = c0 - SEG_GATE[0]
        gt_ref[:, o0:o0 + gw] = _sigmoid(proj((c0, c0 + gw))).astype(BF16)


def _in_proj(x_all, mods, nrm, w_all, cos, sin, gq, gk, segm, *, t_lat):
    n, d = x_all.shape
    tm = TOKEN_TILE
    widths = [(256, BF16), (128, BF16), (256, BF16), (256, BF16), (256, F32), (256, F32), (256, BF16),
              (256, BF16), (256, F32), (256, BF16), (256, BF16), (256, BF16), (N_BRANCH * D_MODEL, BF16)]
    return pl.pallas_call(
        functools.partial(_inproj_kernel, t_lat=t_lat, tm=tm),
        grid=(n // tm,),
        in_specs=[
            pl.BlockSpec((tm, d), lambda i: (i, 0)),
            _const_spec((2, N_MOD, d)),
            _const_spec((1, d)),
            _const_spec((d, W_ALL_WIDTH)),
            pl.BlockSpec((tm, 128), lambda i: (i, 0)),
            pl.BlockSpec((tm, 128), lambda i: (i, 0)),
            _const_spec((1, 256)),
            _const_spec((1, 128)),
            _const_spec((256, 256)),
        ],
        out_specs=[pl.BlockSpec((tm, w), lambda i: (i, 0)) for w, _ in widths],
        out_shape=[jax.ShapeDtypeStruct((n, w), dt) for w, dt in widths],
        compiler_params=_cparams(("parallel",)),
        name="in_proj",
    )(x_all, mods, nrm, w_all, cos, sin, gq, gk, segm)


def _softmax_attend(q, k, v):
    s = _dot_nt(q, k)
    p = jnp.exp(s - jnp.max(s, axis=-1, keepdims=True))
    return _dot(p.astype(BF16), v) / jnp.sum(p, axis=-1, keepdims=True)


def _ctx_attn_kernel(aq_ref, ak_ref, av_ref, dq_ref, dk_ref, dv_ref, ya_ref, yd_ref):
    outs = []
    for h in range(A_HEADS):
        kv = h // (A_HEADS // A_KV_HEADS)
        outs.append(_softmax_attend(aq_ref[:, 64 * h:64 * h + 64], ak_ref[:, 64 * kv:64 * kv + 64],
                                    av_ref[:, 128 * kv:128 * kv + 64]))
    ya_ref[...] = jnp.concatenate(outs, axis=1).astype(BF16)
    outs = []
    for h in range(D_HEADS):
        hs = slice(64 * h, 64 * h + 64)
        outs.append(_softmax_attend(dq_ref[:, hs], dk_ref[:, hs], dv_ref[:, hs]))
    yd_ref[...] = jnp.concatenate(outs, axis=1).astype(BF16)


def _ctx_attn(aq, ak, av, dq, dk, dv, *, t_lat):
    n = aq.shape[0]
    n_ctx = n - t_lat
    blk = t_lat // n_ctx

    def spec(w):
        return pl.BlockSpec((n_ctx, w), lambda i: (blk, 0))

    return pl.pallas_call(
        _ctx_attn_kernel,
        grid=(1,),
        in_specs=[spec(256), spec(128), spec(256), spec(256), spec(256), spec(256)],
        out_specs=[spec(256), spec(256)],
        out_shape=[jax.ShapeDtypeStruct((n, BRANCH_WIDTH), BF16)] * 2,
        compiler_params=_cparams(("arbitrary",)),
        name="ctx_attn",
    )(aq, ak, av, dq, dk, dv)


def _flash_kernel(q_ref, k_ref, v_ref, prev_ref, o_ref, m_ref, acc_ref, *, tq):
    del prev_ref
    j = pl.program_id(1)

    @pl.when(j == 0)
    def _():
        m_ref[...] = jnp.full(m_ref.shape, -jnp.inf, F32)
        acc_ref[...] = jnp.zeros(acc_ref.shape, F32)

    group = A_HEADS // A_KV_HEADS
    for g in range(A_KV_HEADS):
        q2 = jnp.concatenate([q_ref[:, 64 * (group * g + r):64 * (group * g + r) + 64] for r in range(group)],
                             axis=0)
        s = _dot_nt(q2, k_ref[:, 64 * g:64 * g + 64])
        m_old = m_ref[g]
        m_new = jnp.maximum(m_old, jnp.max(s, axis=-1, keepdims=True))
        p = jnp.exp(s - m_new).astype(BF16)
        acc_ref[g] = jnp.exp(m_old - m_new) * acc_ref[g] + _dot(p, v_ref[:, 128 * g:128 * g + 128])
        m_ref[g] = m_new

    @pl.when(j == pl.num_programs(1) - 1)
    def _():
        outs = []
        for g in range(A_KV_HEADS):
            acc = acc_ref[g]
            o2 = acc[:, :HEAD_DIM] / acc[:, HEAD_DIM:HEAD_DIM + 1]
            outs.extend(o2[r * tq:(r + 1) * tq] for r in range(group))
        o_ref[...] = jnp.concatenate(outs, axis=1).astype(BF16)


def _flash_tk(n):
    for tk in (1664, 1280, 1024, 768, 512, 256):
        if n % tk == 0:
            return tk
    raise ValueError(f"unsupported key count {n}")


def _gqa_latent(aq, ak, av, ya_prev, *, t_lat):
    n = aq.shape[0]
    tq = FLASH_TQ
    tk = _flash_tk(n)
    group = A_HEADS // A_KV_HEADS
    return pl.pallas_call(
        functools.partial(_flash_kernel, tq=tq),
        grid=(t_lat // tq, n // tk),
        in_specs=[
            pl.BlockSpec((tq, 256), lambda i, j: (i, 0)),
            pl.BlockSpec((tk, 128), lambda i, j: (j, 0)),
            pl.BlockSpec((tk, 256), lambda i, j: (j, 0)),
            pl.BlockSpec(memory_space=pl.ANY),
        ],
        out_specs=pl.BlockSpec((tq, 256), lambda i, j: (i, 0)),
        out_shape=jax.ShapeDtypeStruct((n, BRANCH_WIDTH), BF16),
        scratch_shapes=[pltpu.VMEM((A_KV_HEADS, group * tq, 1), F32),
                        pltpu.VMEM((A_KV_HEADS, group * tq, 128), F32)],
        input_output_aliases={3: 0},
        compiler_params=_cparams(("parallel", "arbitrary")),
        name="gqa_flash",
    )(aq, ak, av, ya_prev)


def _na_kernel(q_ref, kp_ref, km_ref, kn_ref, vp_ref, vm_ref, vn_ref, kc_ref, vc_ref, bias_ref, prev_ref, o_ref):
    del prev_ref
    outs = []
    for h in range(D_HEADS):
        hs = slice(64 * h, 64 * h + 64)
        q = q_ref[:, hs]
        kcat = jnp.concatenate([kp_ref[:, hs], km_ref[:, hs], kn_ref[:, hs]], axis=0)
        vcat = jnp.concatenate([vp_ref[:, hs], vm_ref[:, hs], vn_ref[:, hs]], axis=0)
        s_loc = _dot_nt(q, kcat) + bias_ref[0, h]
        s_ctx = _dot_nt(q, kc_ref[:, hs])
        m = jnp.maximum(jnp.max(s_loc, axis=-1, keepdims=True), jnp.max(s_ctx, axis=-1, keepdims=True))
        p_loc = jnp.exp(s_loc - m)
        p_ctx = jnp.exp(s_ctx - m)
        denom = jnp.sum(p_loc, axis=-1, keepdims=True) + jnp.sum(p_ctx, axis=-1, keepdims=True)
        o = _dot(p_loc.astype(BF16), vcat) + _dot(p_ctx.astype(BF16), vc_ref[:, hs])
        outs.append(o / denom)
    o_ref[...] = jnp.concatenate(outs, axis=1).astype(BF16)


def _na_bias_tables(rel_bias, rows):
    wr = min(NA_WIN_R, rows)
    halo_rows = NA_HALO // GRID_W
    krows = NA_QROWS + 2 * halo_rows
    nb = rows // NA_QROWS
    qc = np.arange(GRID_W)[:, None]
    kc = np.arange(GRID_W)[None, :]
    cs = np.clip(qc - NA_WIN_C // 2, 0, GRID_W - NA_WIN_C)
    in_col = (kc >= cs) & (kc < cs + NA_WIN_C)
    e_col = (kc - qc + (NA_WIN_C - 1))[:, :, None] == np.arange(2 * NA_WIN_C - 1)
    qr_l = np.arange(NA_QROWS)[:, None]
    kr_l = np.arange(krows)[None, :]
    e_row = (kr_l - halo_rows - qr_l + (NA_WIN_R - 1))[:, :, None] == np.arange(2 * NA_WIN_R - 1)
    hi = lax.Precision.HIGHEST
    tmp = jnp.einsum("hrc,qkr->hqkc", rel_bias.astype(F32), e_row.astype(np.float32), precision=hi)
    full = jnp.einsum("hqkc,pjc->hqpkj", tmp, e_col.astype(np.float32), precision=hi)
    tabs = []
    for b in (0, min(1, nb - 1), nb - 1):
        qr = NA_QROWS * b + qr_l
        kr = NA_QROWS * b - halo_rows + kr_l
        rs = np.clip(qr - wr // 2, 0, rows - wr)
        in_row = (kr >= rs) & (kr < rs + wr)
        mask = in_row[:, None, :, None] & in_col[None, :, None, :]
        tabs.append(jnp.where(mask[None], full, NEG_BIG).reshape(D_HEADS, NA_QB, krows * GRID_W))
    return jnp.stack(tabs)


def _na_latent(dq, dk, dv, bias, yd_prev, *, t_lat):
    n = dq.shape[0]
    nb = t_lat // NA_QB
    r = NA_QB // NA_HALO
    last_halo = t_lat // NA_HALO - 1
    ctx_blk = t_lat // (n - t_lat)
    n_ctx = n - t_lat

    def prev_map(b):
        return (jnp.maximum(r * b - 1, 0), 0)

    def next_map(b):
        return (jnp.minimum(r * b + r, last_halo), 0)

    def variant(b):
        return (jnp.where(b == 0, 0, jnp.where(b == nb - 1, 2, 1)), 0, 0, 0)

    main = pl.BlockSpec((NA_QB, 256), lambda b: (b, 0))
    prev = pl.BlockSpec((NA_HALO, 256), prev_map)
    nxt = pl.BlockSpec((NA_HALO, 256), next_map)
    ctx = pl.BlockSpec((n_ctx, 256), lambda b: (ctx_blk, 0))
    return pl.pallas_call(
        _na_kernel,
        grid=(nb,),
        in_specs=[main, prev, main, nxt, prev, main, nxt, ctx, ctx,
                  pl.BlockSpec((1, D_HEADS, NA_QB, NA_QB + 2 * NA_HALO), variant),
                  pl.BlockSpec(memory_space=pl.ANY)],
        out_specs=main,
        out_shape=jax.ShapeDtypeStruct((n, BRANCH_WIDTH), BF16),
        input_output_aliases={10: 0},
        compiler_params=_cparams(("parallel",)),
        name="na_attn",
    )(dq, dk, dk, dk, dv, dv, dv, dk, dv, bias, yd_prev)


def _pool_kernel(x_ref, xp_ref, xn_ref, w_ref, sc_ref, o_ref, cat_ref, a_ref, b_ref, *, t_lat, n_all, tm):
    i = pl.program_id(0)
    start = i * tm
    in_lat = start < t_lat
    seg_lo = jnp.where(in_lat, 0, t_lat)
    seg_hi = jnp.where(in_lat, t_lat, n_all)
    x = x_ref[...]
    cat_ref[0:8, :] = jnp.where(start > seg_lo, xp_ref[...], 0.0)
    cat_ref[8:8 + tm, :] = x
    cat_ref[8 + tm:16 + tm, :] = jnp.where(start + tm < seg_hi, xn_ref[...], 0.0)
    cat_ref[16 + tm:32 + tm, :] = jnp.zeros((16, BRANCH_WIDTH), F32)
    a_ref[0:tm + 24, :] = cat_ref[0:tm + 24, :] + cat_ref[1:tm + 25, :]
    s2 = a_ref[7:7 + tm, :]
    b_ref[0:tm + 16, :] = a_ref[0:tm + 16, :] + a_ref[2:tm + 18, :]
    s4 = b_ref[6:6 + tm, :]
    a_ref[0:tm + 8, :] = b_ref[0:tm + 8, :] + b_ref[4:tm + 12, :]
    s8 = a_ref[4:4 + tm, :]
    s16 = a_ref[0:tm, :] + a_ref[8:8 + tm, :]

    pos = start - seg_lo + lax.broadcasted_iota(jnp.int32, (tm, 1), 0)
    seg_len = seg_hi - seg_lo

    def mean(sm, w):
        lo = jnp.clip(pos - w // 2, 0, seg_len)
        hi = jnp.clip(pos - w // 2 + w, 0, seg_len)
        return sm / (hi - lo).astype(F32)

    lane = lax.broadcasted_iota(jnp.int32, (tm, BRANCH_WIDTH), 1)
    gw = BRANCH_WIDTH // len(C_WINDOWS)
    pooled = jnp.where(lane < gw, mean(s2, 2),
                       jnp.where(lane < 2 * gw, mean(s4, 4),
                                 jnp.where(lane < 3 * gw, mean(s8, 8), mean(s16, 16)))) - x
    o_ref[...] = (_dot(pooled.astype(BF16), w_ref[...]) * sc_ref[...]).astype(BF16)


def _pool_mixer(cx, w_bd, scale, *, t_lat):
    n = cx.shape[0]
    tm = TOKEN_TILE
    r = tm // 8
    last8 = n // 8 - 1
    return pl.pallas_call(
        functools.partial(_pool_kernel, t_lat=t_lat, n_all=n, tm=tm),
        grid=(n // tm,),
        in_specs=[
            pl.BlockSpec((tm, 256), lambda i: (i, 0)),
            pl.BlockSpec((8, 256), lambda i: (jnp.maximum(r * i - 1, 0), 0)),
            pl.BlockSpec((8, 256), lambda i: (jnp.minimum(r * i + r, last8), 0)),
            _const_spec((256, 256)),
            _const_spec((1, 256)),
        ],
        out_specs=pl.BlockSpec((tm, 256), lambda i: (i, 0)),
        out_shape=jax.ShapeDtypeStruct((n, BRANCH_WIDTH), BF16),
        scratch_shapes=[pltpu.VMEM((tm + 32, 256), F32)] * 3,
        compiler_params=_cparams(("parallel",)),
        name="pool_mixer",
    )(cx, cx, cx, w_bd, scale)


def _hgrn_kernel(*refs, rev, readout):
    if readout:
        q_ref, f_ref, v_ref, lb_ref, of_ref, g_ref, gain_ref, segm_ref, o_ref, st_ref = refs
    else:
        q_ref, f_ref, v_ref, lb_ref, o_ref, st_ref = refs
    c_len = HG_CHUNK
    n_sub = c_len // HG_SUB
    w = B_HEADS * B_KEY_DIM

    @pl.when(pl.program_id(0) == 0)
    def _():
        st_ref[...] = jnp.zeros(st_ref.shape, F32)

    lb = lb_ref[...]
    f = lb + (1.0 - lb) * _sigmoid(f_ref[...])
    k = 1.0 - f
    lf = jnp.log(f)
    q = q_ref[...].astype(F32)
    v = v_ref[...]

    ti = lax.broadcasted_iota(jnp.int32, (c_len, c_len), 0)
    ui = lax.broadcasted_iota(jnp.int32, (c_len, c_len), 1)
    tri = jnp.where((ui >= ti) if rev else (ui <= ti), 1.0, 0.0).astype(BF16)
    c = _split_dot_left(tri, lf)

    row = lax.broadcasted_iota(jnp.int32, (c_len, 1), 0)
    bd_r = lax.broadcasted_iota(jnp.int32, (B_HEADS * HG_SUB, w), 0) // HG_SUB
    bd_c = lax.broadcasted_iota(jnp.int32, (B_HEADS * HG_SUB, w), 1) // B_KEY_DIM
    bd = bd_r == bd_c

    q3 = q.reshape(n_sub, HG_SUB, w)
    k3 = k.reshape(n_sub, HG_SUB, w)
    c3 = c.reshape(n_sub, HG_SUB, w)
    sub_pos = lax.broadcasted_iota(jnp.int32, (n_sub, HG_SUB, w), 1)
    hs_r = lax.broadcasted_iota(jnp.int32, (w, B_HEADS * HG_SUB), 0) // B_KEY_DIM
    hs_c = lax.broadcasted_iota(jnp.int32, (w, B_HEADS * HG_SUB), 1)
    a_diag = jnp.zeros((c_len, B_HEADS * HG_SUB), F32)
    for sg in range(HG_SUB):
        ks = jnp.broadcast_to(k3[:, sg:sg + 1, :], k3.shape)
        cs = jnp.broadcast_to(c3[:, sg:sg + 1, :], c3.shape)
        wgt = q3 * ks * jnp.exp(jnp.minimum(c3 - cs, 0.0))
        keep = (sub_pos <= sg) if rev else (sub_pos >= sg)
        wgt = jnp.where(keep, wgt, 0.0).reshape(c_len, w).astype(BF16)
        sel = jnp.where((hs_c == hs_r * HG_SUB + sg), 1.0, 0.0).astype(BF16)
        a_diag = a_diag + _dot(wgt, sel)

    o = jnp.zeros((c_len, w), F32)
    for j in range(n_sub):
        r0 = j * HG_SUB
        r_last = r0 if rev else r0 + HG_SUB - 1
        e_j = c[r_last:r_last + 1, :]
        qj = (q * jnp.exp(jnp.minimum(c - e_j, 0.0))).astype(BF16)
        kj = k[r0:r0 + HG_SUB, :] * jnp.exp(e_j - c[r0:r0 + HG_SUB, :])
        kbd = jnp.where(bd, jnp.concatenate([kj] * B_HEADS, axis=0), 0.0).astype(BF16)
        a_j = _dot_nt(qj, kbd)
        after = (row < r0) if rev else (row >= r0 + HG_SUB)
        inside = (row >= r0) & (row < r0 + HG_SUB)
        a_j = jnp.where(after, a_j, jnp.where(inside, a_diag, 0.0)).astype(BF16)
        vbd = jnp.where(bd, jnp.concatenate([v[r0:r0 + HG_SUB, :]] * B_HEADS, axis=0), jnp.zeros((), BF16))
        o = o + _dot(a_j, vbd)

    st = st_ref[...]
    o = o + _dot_nt((q * jnp.exp(c)).astype(BF16), st.astype(BF16))
    r_end = 0 if rev else c_len - 1
    c_end = c[r_end:r_end + 1, :]
    k_end = (k * jnp.exp(c_end - c)).astype(BF16)
    upd = _dot_tn(v, k_end)
    hh_r = lax.broadcasted_iota(jnp.int32, (w, w), 0) // B_KEY_DIM
    hh_c = lax.broadcasted_iota(jnp.int32, (w, w), 1) // B_KEY_DIM
    st_ref[...] = jnp.exp(c_end) * st + jnp.where(hh_r == hh_c, upd, 0.0)

    if readout:
        tot = o + of_ref[...]
        y = _head_norm(tot, gain_ref[...], segm_ref[...]) * _silu(g_ref[...].astype(F32))
        o_ref[...] = y.astype(BF16)
    else:
        o_ref[...] = o


def _hgrn_dir(bq, f_pre, bi, lb, *, t_lat, rev, readout_args=None):
    n = bq.shape[0]
    c_len = HG_CHUNK
    n_lat = t_lat // c_len
    n_all = n // c_len
    n_ctx = n_all - n_lat

    if rev:
        def blk(i):
            return (jnp.where(i < n_ctx, n_all - 1 - i, n_lat - 1 - (i - n_ctx)), 0)
    else:
        def blk(i):
            return (jnp.where(i < n_ctx, n_lat + i, i - n_ctx), 0)

    tile = pl.BlockSpec((c_len, 256), blk)
    in_specs = [tile, tile, tile, _const_spec((1, 256))]
    args = [bq, f_pre, bi, lb]
    readout = readout_args is not None
    if readout:
        o_fwd, bg, gain, segm = readout_args
        in_specs += [tile, tile, _const_spec((1, 256)), _const_spec((256, 256))]
        args += [o_fwd, bg, gain, segm]
    return pl.pallas_call(
        functools.partial(_hgrn_kernel, rev=rev, readout=readout),
        grid=(n_all,),
        in_specs=in_specs,
        out_specs=tile,
        out_shape=jax.ShapeDtypeStruct((n, BRANCH_WIDTH), BF16 if readout else F32),
        scratch_shapes=[pltpu.VMEM((256, 256), F32)],
        compiler_params=_cparams(("arbitrary",)),
        name="hgrn_rev" if rev else "hgrn_fwd",
    )(*args)


def _merge_kernel(x_ref, mods_ref, ya_ref, yb_ref, yc_ref, yd_ref, gt_ref, wb_ref, wo_ref, o_ref, *, t_lat, tm):
    i = pl.program_id(0)
    row = i * tm + lax.broadcasted_iota(jnp.int32, (tm, 1), 0)
    is_lat = row < t_lat
    merged = None
    for n, y_ref in enumerate((ya_ref, yb_ref, yc_ref, yd_ref)):
        term = gt_ref[:, n * D_MODEL:(n + 1) * D_MODEL].astype(F32) * _dot(y_ref[...], wb_ref[n])
        merged = term if merged is None else merged + term
    y = _dot(merged.astype(BF16), wo_ref[...])
    o_ref[...] = x_ref[...] + _row_mod(mods_ref, 5, is_lat) * y


def _merge(x_all, mods, ya, yb, yc, yd, gates, wb, wo, *, t_lat, n_rows):
    n, d = x_all.shape
    tm = TOKEN_TILE

    def tile(w):
        return pl.BlockSpec((tm, w), lambda i: (i, 0))

    return pl.pallas_call(
        functools.partial(_merge_kernel, t_lat=t_lat, tm=tm),
        grid=(n_rows // tm,),
        in_specs=[tile(d), _const_spec((2, N_MOD, d)), tile(256), tile(256), tile(256), tile(256),
                  tile(N_BRANCH * d), _const_spec((N_BRANCH, BRANCH_WIDTH, d)), _const_spec((d, d))],
        out_specs=tile(d),
        out_shape=jax.ShapeDtypeStruct((n, d), F32),
        input_output_aliases={0: 0},
        compiler_params=_cparams(("parallel",)),
        name="merge_out",
    )(x_all, mods, ya, yb, yc, yd, gates, wb, wo)


def _final_norm_kernel(x_ref, g_ref, o_ref):
    o_ref[...] = _rms_rows(x_ref[...]) * g_ref[...]


def _final_norm(x_all, gain, *, t_lat):
    d = x_all.shape[1]
    tm = 512
    return pl.pallas_call(
        _final_norm_kernel,
        grid=(t_lat // tm,),
        in_specs=[pl.BlockSpec((tm, d), lambda i: (i, 0)), _const_spec((1, d))],
        out_specs=pl.BlockSpec((tm, d), lambda i: (i, 0)),
        out_shape=jax.ShapeDtypeStruct((t_lat, d), F32),
        compiler_params=_cparams(("parallel",)),
        name="final_norm",
    )(x_all, gain)


def _rope_tables_padded(t_lat, n_ctx):
    t = jnp.arange(t_lat, dtype=jnp.int32)
    half = HEAD_DIM // 2
    inv = 1.0 / (ROPE_THETA ** (jnp.arange(0, half, 2, dtype=F32) / half))
    ang_r = (t // GRID_W).astype(F32)[:, None] * inv
    ang_c = (t % GRID_W).astype(F32)[:, None] * inv
    cos = jnp.concatenate([jnp.cos(ang_r), jnp.cos(ang_r), jnp.cos(ang_c), jnp.cos(ang_c)], axis=1)
    sin = jnp.concatenate([-jnp.sin(ang_r), jnp.sin(ang_r), -jnp.sin(ang_c), jnp.sin(ang_c)], axis=1)
    cos = jnp.concatenate([cos, jnp.ones((n_ctx, HEAD_DIM), F32)], axis=0)
    sin = jnp.concatenate([sin, jnp.zeros((n_ctx, HEAD_DIM), F32)], axis=0)
    return jnp.tile(cos, (1, 2)), jnp.tile(sin, (1, 2))


def _pack_w_in(w_in_l):
    d = w_in_l.shape[0]
    zeros = jnp.zeros((d, HEAD_DIM), w_in_l.dtype)
    av = w_in_l[:, 384:512]
    av_wide = jnp.concatenate([av[:, :HEAD_DIM], zeros, av[:, HEAD_DIM:], zeros], axis=1)
    return jnp.concatenate([w_in_l[:, :384], av_wide, w_in_l[:, 512:]], axis=1).astype(BF16)


def _block_diag(w_group):
    g, ci, co = w_group.shape
    out = jnp.zeros((g * ci, g * co), w_group.dtype)
    for n in range(g):
        out = out.at[n * ci:(n + 1) * ci, n * co:(n + 1) * co].set(w_group[n])
    return out


def kernel(x, c, ctx, c_ctx, w_ada, b_ada, ffn1_norm, ffn1_w_gate, ffn1_w_up, ffn1_w_down, mix_norm, w_in, a_q_norm, a_k_norm, b_lb_logits, b_o_norm, c_w_group, c_scale, d_rel_bias, w_branch, w_out, ffn2_norm, ffn2_w_gate, ffn2_w_up, ffn2_w_down, final_norm):
    assert x.shape[0] == 1 and ctx.shape[0] == 1
    depth = w_ada.shape[0]
    t_lat = x.shape[1]
    n_ctx = ctx.shape[1]
    n = t_lat + n_ctx
    rows = t_lat // GRID_W
    assert t_lat % max(NA_QB, FLASH_TQ, n_ctx) == 0 and n_ctx % TOKEN_TILE == 0 and rows >= 2 * NA_QROWS

    x_all = jnp.concatenate([x[0], ctx[0]], axis=0)
    mods_all = _ada_mods(c, c_ctx, w_ada, b_ada)
    cos, sin = _rope_tables_padded(t_lat, n_ctx)
    segm = (jnp.kron(jnp.eye(BRANCH_WIDTH // HEAD_DIM), jnp.ones((HEAD_DIM, HEAD_DIM))) / HEAD_DIM).astype(BF16)
    lb_all = jnp.cumsum(jax.nn.softmax(b_lb_logits.astype(F32), axis=0), axis=0)
    lb_all = lb_all - lb_all[:1]

    for l in range(depth):
        with_ctx_out = l < depth - 1
        n_rows = n if with_ctx_out else t_lat
        mods = mods_all[l]
        x_all = _ffn_half(x_all, mods, ffn1_norm[l][None], ffn1_w_gate[l].astype(BF16), ffn1_w_up[l].astype(BF16),
                          ffn1_w_down[l].astype(BF16), t_lat=t_lat, k0=0, n_rows=n)
        (aq, ak, av, bq, bff, bfb, bi, bg, cx, dq, dk, dv, gates) = _in_proj(
            x_all, mods, mix_norm[l][None], _pack_w_in(w_in[l]), cos, sin,
            jnp.tile(a_q_norm[l], A_HEADS)[None], jnp.tile(a_k_norm[l], A_KV_HEADS)[None], segm, t_lat=t_lat)
        if with_ctx_out:
            ya, yd = _ctx_attn(aq, ak, av, dq, dk, dv, t_lat=t_lat)
        else:
            ya = jnp.zeros((n, BRANCH_WIDTH), BF16)
            yd = ya
        ya = _gqa_latent(aq, ak, av, ya, t_lat=t_lat)
        yd = _na_latent(dq, dk, dv, _na_bias_tables(d_rel_bias[l], rows), yd, t_lat=t_lat)
        yc = _pool_mixer(cx, _block_diag(c_w_group[l]).astype(BF16), c_scale[l][None], t_lat=t_lat)
        o_fwd = _hgrn_dir(bq, bff, bi, lb_all[l, 0][None], t_lat=t_lat, rev=False)
        yb = _hgrn_dir(bq, bfb, bi, lb_all[l, 1][None], t_lat=t_lat, rev=True,
                       readout_args=(o_fwd, bg, jnp.tile(b_o_norm[l], B_HEADS)[None], segm))
        x_all = _merge(x_all, mods, ya, yb, yc, yd, gates, w_branch[l].astype(BF16), w_out[l].astype(BF16),
                       t_lat=t_lat, n_rows=n_rows)
        x_all = _ffn_half(x_all, mods, ffn2_norm[l][None], ffn2_w_gate[l].astype(BF16), ffn2_w_up[l].astype(BF16),
                          ffn2_w_down[l].astype(BF16), t_lat=t_lat, k0=6, n_rows=n_rows)
    return _final_norm(x_all, final_norm[None], t_lat=t_lat)[None]
```

```python
import functools

import numpy as np
import jax
import jax.numpy as jnp
from jax import lax
from jax.experimental import pallas as pl
from jax.experimental.pallas import tpu as pltpu

F32 = jnp.float32
BF16 = jnp.bfloat16

D_MODEL = 1024
GRID_W = 64
HEAD_DIM = 64
BRANCH_WIDTH = 256
N_BRANCH = 4
A_HEADS = 4
A_KV_HEADS = 2
ROPE_THETA = 10000.0
B_HEADS = 4
B_KEY_DIM = 64
C_WINDOWS = (2, 4, 8, 16)
D_HEADS = 4
NA_WIN_R = 8
NA_WIN_C = 16
D_FF = 2816
N_MOD = 9
EPS = 1e-6
ATTN_SCALE = HEAD_DIM ** -0.5
LOG2_E = 1.4426950408889634
NEG_BIG = -1e30

V7X_VMEM_BYTES = 64 * 1024 * 1024
VMEM_LIMIT = V7X_VMEM_BYTES - 8 * 1024 * 1024

SEG_AQ = (0, 256)
SEG_AK = (256, 512)
SEG_AV = (512, 768)
SEG_BQ = (768, 1024)
SEG_BFF = (1024, 1280)
SEG_BFB = (1280, 1536)
SEG_BI = (1536, 1792)
SEG_BG = (1792, 2048)
SEG_CX = (2048, 2304)
SEG_DQ = (2304, 2560)
SEG_DK = (2560, 2816)
SEG_DV = (2816, 3072)
SEG_GATE = (3072, 3072 + N_BRANCH * D_MODEL)
W_ALL_WIDTH = SEG_GATE[1]

TOKEN_TILE = 256
FLASH_TQ = 1024
FLASH_QC = 512
FLASH_SB = 256
FLASH_NBUF = 4
FLASH_PIECE = 64
NA_QROWS = 8
NA_QB = NA_QROWS * GRID_W
NA_HALO = 256
HG_CHUNK = 128
HG_SUB = 16


def _cparams(sem):
    return pltpu.CompilerParams(dimension_semantics=sem, vmem_limit_bytes=VMEM_LIMIT)


def _const_spec(shape):
    nd = len(shape)
    return pl.BlockSpec(shape, lambda *_: (0,) * nd, pipeline_mode=pl.Buffered(1))


def _sigmoid(x):
    return 1.0 / (1.0 + jnp.exp(-x))


def _silu(x):
    return x * _sigmoid(x)


def _dot(a, b):
    return jnp.dot(a, b, preferred_element_type=F32)


def _dot_nt(a, b):
    return lax.dot_general(a, b, (((1,), (1,)), ((), ())), preferred_element_type=F32)


def _dot_tn(a, b):
    return lax.dot_general(a, b, (((0,), (0,)), ((), ())), preferred_element_type=F32)


def _split_dot(x, m):
    hi = x.astype(BF16)
    r1 = x - hi.astype(F32)
    mid = r1.astype(BF16)
    lo = (r1 - mid.astype(F32)).astype(BF16)
    return _dot(hi, m) + _dot(mid, m) + _dot(lo, m)


def _split_dot_left(m, x):
    hi = x.astype(BF16)
    r1 = x - hi.astype(F32)
    mid = r1.astype(BF16)
    lo = (r1 - mid.astype(F32)).astype(BF16)
    return _dot(m, hi) + _dot(m, mid) + _dot(m, lo)


def _row_mod(mods_ref, k, is_lat):
    return jnp.where(is_lat, mods_ref[0, k:k + 1, :], mods_ref[1, k:k + 1, :])


def _rms_rows(x):
    return x * lax.rsqrt(jnp.mean(x * x, axis=-1, keepdims=True) + EPS)


def _ada_kernel(s_ref, w_ref, b_ref, o_ref):
    o_ref[0] = _dot(_silu(s_ref[...]).astype(BF16), w_ref[0].astype(BF16)) + b_ref[0]


def _ada_mods(c, c_ctx, w_ada, b_ada):
    depth = w_ada.shape[0]
    width = w_ada.shape[2]
    tn = 1024
    s = jnp.zeros((8, D_MODEL), F32).at[0].set(c[0]).at[1].set(c_ctx)
    out = pl.pallas_call(
        _ada_kernel,
        grid=(depth, width // tn),
        in_specs=[
            pl.BlockSpec((8, D_MODEL), lambda l, j: (0, 0)),
            pl.BlockSpec((1, D_MODEL, tn), lambda l, j: (l, 0, j)),
            pl.BlockSpec((1, 1, tn), lambda l, j: (l, 0, j)),
        ],
        out_specs=pl.BlockSpec((1, 8, tn), lambda l, j: (l, 0, j)),
        out_shape=jax.ShapeDtypeStruct((depth, 8, width), F32),
        compiler_params=_cparams(("parallel", "parallel")),
        name="ada_mods",
    )(s, w_ada, b_ada.reshape(depth, 1, width))
    return out[:, :2].reshape(depth, 2, N_MOD, D_MODEL)


def _ffn_kernel(x_ref, mods_ref, nrm_ref, wg_ref, wu_ref, wd_ref, o_ref, *, t_lat, tm, k0):
    i = pl.program_id(0)
    row = i * tm + lax.broadcasted_iota(jnp.int32, (tm, 1), 0)
    is_lat = row < t_lat
    x = x_ref[...]
    z = _rms_rows(x) * nrm_ref[...]
    z = (z * (1.0 + _row_mod(mods_ref, k0 + 1, is_lat)) + _row_mod(mods_ref, k0, is_lat)).astype(BF16)
    g = _dot(z, wg_ref[...])
    u = _dot(z, wu_ref[...])
    a = (_silu(g) * u).astype(BF16)
    y = _dot(a, wd_ref[...])
    o_ref[...] = x + 0.5 * _row_mod(mods_ref, k0 + 2, is_lat) * y


def _ffn_half(x_all, mods, nrm, wg, wu, wd, *, t_lat, k0, n_rows):
    n, d = x_all.shape
    tm = TOKEN_TILE
    f = wg.shape[1]
    return pl.pallas_call(
        functools.partial(_ffn_kernel, t_lat=t_lat, tm=tm, k0=k0),
        grid=(n_rows // tm,),
        in_specs=[
            pl.BlockSpec((tm, d), lambda i: (i, 0)),
            _const_spec((2, N_MOD, d)),
            _const_spec((1, d)),
            _const_spec((d, f)),
            _const_spec((d, f)),
            _const_spec((f, d)),
        ],
        out_specs=pl.BlockSpec((tm, d), lambda i: (i, 0)),
        out_shape=jax.ShapeDtypeStruct((n, d), F32),
        input_output_aliases={0: 0},
        compiler_params=_cparams(("parallel",)),
        name="ffn_half",
    )(x_all, mods, nrm, wg, wu, wd)


def _head_norm(y, gain, segm):
    ms = _split_dot(y * y, segm)
    return y * lax.rsqrt(ms + EPS) * gain


def _rope(y, cos, sin_signed):
    w = y.shape[1]
    lane = lax.broadcasted_iota(jnp.int32, y.shape, 1)
    partner = jnp.where((lane & 31) < 16, pltpu.roll(y, w - 16, 1), pltpu.roll(y, 16, 1))
    return y * cos + partner * sin_signed


def _inproj_kernel(x_ref, mods_ref, nrm_ref, w_ref, cos_ref, sin_ref, gq_ref, gk_ref, segm_ref,
                   aq_ref, ak_ref, av_ref, bq_ref, bff_ref, bfb_ref, bi_ref, bg_ref, cx_ref,
                   dq_ref, dk_ref, dv_ref, gt_ref, *, t_lat, tm):
    i = pl.program_id(0)
    row = i * tm + lax.broadcasted_iota(jnp.int32, (tm, 1), 0)
    is_lat = row < t_lat
    z = _rms_rows(x_ref[...]) * nrm_ref[...]
    z = (z * (1.0 + _row_mod(mods_ref, 4, is_lat)) + _row_mod(mods_ref, 3, is_lat)).astype(BF16)

    def proj(seg):
        return _dot(z, w_ref[:, seg[0]:seg[1]])

    cos = jnp.concatenate([cos_ref[...]] * 2, axis=1)
    sin = jnp.concatenate([sin_ref[...]] * 2, axis=1)
    q = _rope(_head_norm(proj(SEG_AQ), gq_ref[...], segm_ref[...]), cos, sin)
    aq_ref[...] = (q * (ATTN_SCALE * LOG2_E)).T.astype(BF16)
    k = _rope(_head_norm(proj(SEG_AK), gk_ref[...], segm_ref[...]), cos, sin)
    ak_ref[...] = k.astype(BF16)
    lane = lax.broadcasted_iota(jnp.int32, (tm, SEG_AV[1] - SEG_AV[0]), 1)
    av_ref[...] = (proj(SEG_AV) + jnp.where((lane & 127) == HEAD_DIM, 1.0, 0.0)).T.astype(BF16)

    bq_ref[...] = (proj(SEG_BQ) * (B_KEY_DIM ** -0.5)).astype(BF16)
    bff_ref[...] = proj(SEG_BFF)
    bfb_ref[...] = proj(SEG_BFB)
    bi_ref[...] = proj(SEG_BI).astype(BF16)
    bg_ref[...] = proj(SEG_BG).astype(BF16)
    cx_ref[...] = proj(SEG_CX)
    dq_ref[...] = (proj(SEG_DQ) * ATTN_SCALE).astype(BF16)
    dk_ref[...] = proj(SEG_DK).astype(BF16)
    dv_ref[...] = proj(SEG_DV).astype(BF16)
    gw = 512
    for c0 in range(SEG_GATE[0], SEG_GATE[1], gw):
        o0 = c0 - SEG_GATE[0]
        gt_ref[:, o0:o0 + gw] = _sigmoid(proj((c0, c0 + gw))).astype(BF16)


def _in_proj(x_all, mods, nrm, w_all, cos, sin, gq, gk, segm, *, t_lat):
    n, d = x_all.shape
    tm = TOKEN_TILE
    widths = [(256, BF16), (256, BF16), (256, BF16), (256, BF16), (256, F32), (256, F32), (256, BF16),
              (256, BF16), (256, F32), (256, BF16), (256, BF16), (256, BF16), (N_BRANCH * D_MODEL, BF16)]
    transposed = (0, 2)
    out_specs = [pl.BlockSpec((w, tm), lambda i: (0, i)) if o in transposed else pl.BlockSpec((tm, w), lambda i: (i, 0))
                 for o, (w, _) in enumerate(widths)]
    out_shape = [jax.ShapeDtypeStruct((w, n) if o in transposed else (n, w), dt) for o, (w, dt) in enumerate(widths)]
    return pl.pallas_call(
        functools.partial(_inproj_kernel, t_lat=t_lat, tm=tm),
        grid=(n // tm,),
        in_specs=[
            pl.BlockSpec((tm, d), lambda i: (i, 0)),
            _const_spec((2, N_MOD, d)),
            _const_spec((1, d)),
            _const_spec((d, W_ALL_WIDTH)),
            pl.BlockSpec((tm, 128), lambda i: (i, 0)),
            pl.BlockSpec((tm, 128), lambda i: (i, 0)),
            _const_spec((1, 256)),
            _const_spec((1, 256)),
            _const_spec((256, 256)),
        ],
        out_specs=out_specs,
        out_shape=out_shape,
        compiler_params=_cparams(("parallel",)),
        name="in_proj",
    )(x_all, mods, nrm, w_all, cos, sin, gq, gk, segm)


def _softmax_attend(q, k, v, exp_fn):
    s = _dot_nt(q, k)
    p = exp_fn(s - jnp.max(s, axis=-1, keepdims=True))
    return _dot(p.astype(BF16), v) / jnp.sum(p, axis=-1, keepdims=True)


def _ctx_attn_kernel(aqt_ref, ak_ref, avt_ref, dq_ref, dk_ref, dv_ref, ya_ref, yd_ref):
    outs = []
    for h in range(A_HEADS):
        g = h // (A_HEADS // A_KV_HEADS)
        st = _dot(ak_ref[:, 128 * g:128 * g + 64], aqt_ref[64 * h:64 * h + 64, :])
        pt = jnp.exp2(st - jnp.max(st, axis=0, keepdims=True)).astype(BF16)
        acc = _dot(avt_ref[128 * g:128 * g + 128, :], pt)
        outs.append((acc[:HEAD_DIM] / acc[HEAD_DIM:HEAD_DIM + 1]).T)
    ya_ref[...] = jnp.concatenate(outs, axis=1).astype(BF16)
    outs = []
    for h in range(D_HEADS):
        hs = slice(64 * h, 64 * h + 64)
        outs.append(_softmax_attend(dq_ref[:, hs], dk_ref[:, hs], dv_ref[:, hs], jnp.exp))
    yd_ref[...] = jnp.concatenate(outs, axis=1).astype(BF16)


def _ctx_attn(aqt, ak, avt, dq, dk, dv, *, t_lat):
    n = ak.shape[0]
    n_ctx = n - t_lat
    blk = t_lat // n_ctx

    def spec(w):
        return pl.BlockSpec((n_ctx, w), lambda i: (blk, 0))

    def spec_t(w):
        return pl.BlockSpec((w, n_ctx), lambda i: (0, blk))

    return pl.pallas_call(
        _ctx_attn_kernel,
        grid=(1,),
        in_specs=[spec_t(256), spec(256), spec_t(256), spec(256), spec(256), spec(256)],
        out_specs=[spec(256), spec(256)],
        out_shape=[jax.ShapeDtypeStruct((n, BRANCH_WIDTH), BF16)] * 2,
        compiler_params=_cparams(("arbitrary",)),
        name="ctx_attn",
    )(aqt, ak, avt, dq, dk, dv)


def _flash_kernel(qt_ref, k_ref, vt_ref, prev_ref, o_ref, m_ref, acc_ref, s_ref, p_ref, *, tk, tq):
    del prev_ref
    j = pl.program_id(1)

    @pl.when(j == 0)
    def _():
        m_ref[...] = jnp.full(m_ref.shape, -jnp.inf, F32)
        acc_ref[...] = jnp.zeros(acc_ref.shape, F32)

    group = A_HEADS // A_KV_HEADS
    n_qc = tq // FLASH_QC
    n_sb = tk // FLASH_SB
    chains = [(h, c) for h in range(A_HEADS) for c in range(n_qc)]

    def score_block(ci, sb):
        h, c = chains[ci]
        g = h // group
        rows = slice(sb * FLASH_SB, (sb + 1) * FLASH_SB)
        st = _dot(k_ref[rows, 128 * g:128 * g + 64], qt_ref[64 * h:64 * h + 64, c * FLASH_QC:(c + 1) * FLASH_QC])
        s_ref[ci % FLASH_NBUF, rows, :] = st
        return jnp.max(st, axis=0, keepdims=True)

    def value_block(ci, sb, m_new):
        g = chains[ci][0] // group
        rows = slice(sb * FLASH_SB, (sb + 1) * FLASH_SB)
        for r0 in range(0, FLASH_SB, FLASH_PIECE):
            piece = s_ref[ci % FLASH_NBUF, sb * FLASH_SB + r0:sb * FLASH_SB + r0 + FLASH_PIECE, :]
            p_ref[sb % 2, r0:r0 + FLASH_PIECE, :] = jnp.exp2(piece - m_new).astype(BF16)
        return _dot(vt_ref[128 * g:128 * g + 128, rows], p_ref[sb % 2])

    def fold(a, b):
        return b if a is None else a + b

    def fold_max(a, b):
        return b if a is None else jnp.maximum(a, b)

    blk_max = None
    for sb in range(n_sb):
        blk_max = fold_max(blk_max, score_block(0, sb))
    for ci, (h, c) in enumerate(chains):
        cols = slice(c * FLASH_QC, (c + 1) * FLASH_QC)
        m_old = m_ref[h, :, cols]
        m_new = jnp.maximum(m_old, blk_max)
        m_ref[h, :, cols] = m_new
        part, blk_max = None, None
        if ci + 1 < len(chains):
            for sb in range(n_sb):
                blk_max = fold_max(blk_max, score_block(ci + 1, sb))
        for sb in range(n_sb):
            part = fold(part, value_block(ci, sb, m_new))
        acc_ref[h, :, cols] = jnp.exp2(m_old - m_new) * acc_ref[h, :, cols] + part

    @pl.when(j == pl.num_programs(1) - 1)
    def _():
        outs = []
        for h in range(A_HEADS):
            acc = acc_ref[h]
            outs.append((acc[:HEAD_DIM] / acc[HEAD_DIM:HEAD_DIM + 1]).T)
        o_ref[...] = jnp.concatenate(outs, axis=1).astype(BF16)


def _flash_tk(n):
    for tk in (1280, 768, 512, 256):
        if n % tk == 0:
            return tk
    raise ValueError(f"unsupported key count {n}")


def _gqa_latent(aqt, ak, avt, ya_prev, *, t_lat):
    n = ak.shape[0]
    tq = FLASH_TQ
    tk = _flash_tk(n)
    return pl.pallas_call(
        functools.partial(_flash_kernel, tk=tk, tq=tq),
        grid=(t_lat // tq, n // tk),
        in_specs=[
            pl.BlockSpec((256, tq), lambda i, j: (0, i)),
            pl.BlockSpec((tk, 256), lambda i, j: (j, 0)),
            pl.BlockSpec((256, tk), lambda i, j: (0, j)),
            pl.BlockSpec(memory_space=pl.ANY),
        ],
        out_specs=pl.BlockSpec((tq, 256), lambda i, j: (i, 0)),
        out_shape=jax.ShapeDtypeStruct((n, BRANCH_WIDTH), BF16),
        scratch_shapes=[pltpu.VMEM((A_HEADS, 1, tq), F32),
                        pltpu.VMEM((A_HEADS, 128, tq), F32),
                        pltpu.VMEM((FLASH_NBUF, tk, FLASH_QC), F32),
                        pltpu.VMEM((2, FLASH_SB, FLASH_QC), BF16)],
        input_output_aliases={3: 0},
        compiler_params=_cparams(("parallel", "arbitrary")),
        name="gqa_flash",
    )(aqt, ak, avt, ya_prev)


def _na_kernel(q_ref, kp_ref, km_ref, kn_ref, vp_ref, vm_ref, vn_ref, kc_ref, vc_ref, bias_ref, prev_ref, o_ref):
    del prev_ref
    outs = []
    for h in range(D_HEADS):
        hs = slice(64 * h, 64 * h + 64)
        q = q_ref[:, hs]
        kcat = jnp.concatenate([kp_ref[:, hs], km_ref[:, hs], kn_ref[:, hs]], axis=0)
        vcat = jnp.concatenate([vp_ref[:, hs], vm_ref[:, hs], vn_ref[:, hs]], axis=0)
        s_loc = _dot_nt(q, kcat) + bias_ref[0, h]
        s_ctx = _dot_nt(q, kc_ref[:, hs])
        m = jnp.maximum(jnp.max(s_loc, axis=-1, keepdims=True), jnp.max(s_ctx, axis=-1, keepdims=True))
        p_loc = jnp.exp(s_loc - m)
        p_ctx = jnp.exp(s_ctx - m)
        denom = jnp.sum(p_loc, axis=-1, keepdims=True) + jnp.sum(p_ctx, axis=-1, keepdims=True)
        o = _dot(p_loc.astype(BF16), vcat) + _dot(p_ctx.astype(BF16), vc_ref[:, hs])
        outs.append(o / denom)
    o_ref[...] = jnp.concatenate(outs, axis=1).astype(BF16)


def _na_bias_tables(rel_bias, rows):
    wr = min(NA_WIN_R, rows)
    halo_rows = NA_HALO // GRID_W
    krows = NA_QROWS + 2 * halo_rows
    nb = rows // NA_QROWS
    qc = np.arange(GRID_W)[:, None]
    kc = np.arange(GRID_W)[None, :]
    cs = np.clip(qc - NA_WIN_C // 2, 0, GRID_W - NA_WIN_C)
    in_col = (kc >= cs) & (kc < cs + NA_WIN_C)
    e_col = (kc - qc + (NA_WIN_C - 1))[:, :, None] == np.arange(2 * NA_WIN_C - 1)
    qr_l = np.arange(NA_QROWS)[:, None]
    kr_l = np.arange(krows)[None, :]
    e_row = (kr_l - halo_rows - qr_l + (NA_WIN_R - 1))[:, :, None] == np.arange(2 * NA_WIN_R - 1)
    hi = lax.Precision.HIGHEST
    tmp = jnp.einsum("hrc,qkr->hqkc", rel_bias.astype(F32), e_row.astype(np.float32), precision=hi)
    full = jnp.einsum("hqkc,pjc->hqpkj", tmp, e_col.astype(np.float32), precision=hi)
    tabs = []
    for b in (0, min(1, nb - 1), nb - 1):
        qr = NA_QROWS * b + qr_l
        kr = NA_QROWS * b - halo_rows + kr_l
        rs = np.clip(qr - wr // 2, 0, rows - wr)
        in_row = (kr >= rs) & (kr < rs + wr)
        mask = in_row[:, None, :, None] & in_col[None, :, None, :]
        tabs.append(jnp.where(mask[None], full, NEG_BIG).reshape(D_HEADS, NA_QB, krows * GRID_W))
    return jnp.stack(tabs)


def _na_latent(dq, dk, dv, bias, yd_prev, *, t_lat):
    n = dq.shape[0]
    nb = t_lat // NA_QB
    r = NA_QB // NA_HALO
    last_halo = t_lat // NA_HALO - 1
    ctx_blk = t_lat // (n - t_lat)
    n_ctx = n - t_lat

    def prev_map(b):
        return (jnp.maximum(r * b - 1, 0), 0)

    def next_map(b):
        return (jnp.minimum(r * b + r, last_halo), 0)

    def variant(b):
        return (jnp.where(b == 0, 0, jnp.where(b == nb - 1, 2, 1)), 0, 0, 0)

    main = pl.BlockSpec((NA_QB, 256), lambda b: (b, 0))
    prev = pl.BlockSpec((NA_HALO, 256), prev_map)
    nxt = pl.BlockSpec((NA_HALO, 256), next_map)
    ctx = pl.BlockSpec((n_ctx, 256), lambda b: (ctx_blk, 0))
    return pl.pallas_call(
        _na_kernel,
        grid=(nb,),
        in_specs=[main, prev, main, nxt, prev, main, nxt, ctx, ctx,
                  pl.BlockSpec((1, D_HEADS, NA_QB, NA_QB + 2 * NA_HALO), variant),
                  pl.BlockSpec(memory_space=pl.ANY)],
        out_specs=main,
        out_shape=jax.ShapeDtypeStruct((n, BRANCH_WIDTH), BF16),
        input_output_aliases={10: 0},
        compiler_params=_cparams(("parallel",)),
        name="na_attn",
    )(dq, dk, dk, dk, dv, dv, dv, dk, dv, bias, yd_prev)


def _pool_kernel(x_ref, xp_ref, xn_ref, w_ref, sc_ref, o_ref, cat_ref, a_ref, b_ref, *, t_lat, n_all, tm):
    i = pl.program_id(0)
    start = i * tm
    in_lat = start < t_lat
    seg_lo = jnp.where(in_lat, 0, t_lat)
    seg_hi = jnp.where(in_lat, t_lat, n_all)
    x = x_ref[...]
    cat_ref[0:8, :] = jnp.where(start > seg_lo, xp_ref[...], 0.0)
    cat_ref[8:8 + tm, :] = x
    cat_ref[8 + tm:16 + tm, :] = jnp.where(start + tm < seg_hi, xn_ref[...], 0.0)
    cat_ref[16 + tm:32 + tm, :] = jnp.zeros((16, BRANCH_WIDTH), F32)
    a_ref[0:tm + 24, :] = cat_ref[0:tm + 24, :] + cat_ref[1:tm + 25, :]
    s2 = a_ref[7:7 + tm, :]
    b_ref[0:tm + 16, :] = a_ref[0:tm + 16, :] + a_ref[2:tm + 18, :]
    s4 = b_ref[6:6 + tm, :]
    a_ref[0:tm + 8, :] = b_ref[0:tm + 8, :] + b_ref[4:tm + 12, :]
    s8 = a_ref[4:4 + tm, :]
    s16 = a_ref[0:tm, :] + a_ref[8:8 + tm, :]

    pos = start - seg_lo + lax.broadcasted_iota(jnp.int32, (tm, 1), 0)
    seg_len = seg_hi - seg_lo

    def mean(sm, w):
        lo = jnp.clip(pos - w // 2, 0, seg_len)
        hi = jnp.clip(pos - w // 2 + w, 0, seg_len)
        return sm / (hi - lo).astype(F32)

    lane = lax.broadcasted_iota(jnp.int32, (tm, BRANCH_WIDTH), 1)
    gw = BRANCH_WIDTH // len(C_WINDOWS)
    pooled = jnp.where(lane < gw, mean(s2, 2),
                       jnp.where(lane < 2 * gw, mean(s4, 4),
                                 jnp.where(lane < 3 * gw, mean(s8, 8), mean(s16, 16)))) - x
    o_ref[...] = (_dot(pooled.astype(BF16), w_ref[...]) * sc_ref[...]).astype(BF16)


def _pool_mixer(cx, w_bd, scale, *, t_lat):
    n = cx.shape[0]
    tm = TOKEN_TILE
    r = tm // 8
    last8 = n // 8 - 1
    return pl.pallas_call(
        functools.partial(_pool_kernel, t_lat=t_lat, n_all=n, tm=tm),
        grid=(n // tm,),
        in_specs=[
            pl.BlockSpec((tm, 256), lambda i: (i, 0)),
            pl.BlockSpec((8, 256), lambda i: (jnp.maximum(r * i - 1, 0), 0)),
            pl.BlockSpec((8, 256), lambda i: (jnp.minimum(r * i + r, last8), 0)),
            _const_spec((256, 256)),
            _const_spec((1, 256)),
        ],
        out_specs=pl.BlockSpec((tm, 256), lambda i: (i, 0)),
        out_shape=jax.ShapeDtypeStruct((n, BRANCH_WIDTH), BF16),
        scratch_shapes=[pltpu.VMEM((tm + 32, 256), F32)] * 3,
        compiler_params=_cparams(("parallel",)),
        name="pool_mixer",
    )(cx, cx, cx, w_bd, scale)


def _hgrn_kernel(*refs, rev, readout):
    if readout:
        q_ref, f_ref, v_ref, lb_ref, of_ref, g_ref, gain_ref, segm_ref, o_ref, st_ref = refs
    else:
        q_ref, f_ref, v_ref, lb_ref, o_ref, st_ref = refs
    c_len = HG_CHUNK
    n_sub = c_len // HG_SUB
    w = B_HEADS * B_KEY_DIM

    @pl.when(pl.program_id(0) == 0)
    def _():
        st_ref[...] = jnp.zeros(st_ref.shape, F32)

    lb = lb_ref[...]
    f = lb + (1.0 - lb) * _sigmoid(f_ref[...])
    k = 1.0 - f
    lf = jnp.log(f)
    q = q_ref[...].astype(F32)
    v = v_ref[...]

    ti = lax.broadcasted_iota(jnp.int32, (c_len, c_len), 0)
    ui = lax.broadcasted_iota(jnp.int32, (c_len, c_len), 1)
    tri = jnp.where((ui >= ti) if rev else (ui <= ti), 1.0, 0.0).astype(BF16)
    c = _split_dot_left(tri, lf)

    row = lax.broadcasted_iota(jnp.int32, (c_len, 1), 0)
    bd_r = lax.broadcasted_iota(jnp.int32, (B_HEADS * HG_SUB, w), 0) // HG_SUB
    bd_c = lax.broadcasted_iota(jnp.int32, (B_HEADS * HG_SUB, w), 1) // B_KEY_DIM
    bd = bd_r == bd_c

    q3 = q.reshape(n_sub, HG_SUB, w)
    k3 = k.reshape(n_sub, HG_SUB, w)
    c3 = c.reshape(n_sub, HG_SUB, w)
    sub_pos = lax.broadcasted_iota(jnp.int32, (n_sub, HG_SUB, w), 1)
    hs_r = lax.broadcasted_iota(jnp.int32, (w, B_HEADS * HG_SUB), 0) // B_KEY_DIM
    hs_c = lax.broadcasted_iota(jnp.int32, (w, B_HEADS * HG_SUB), 1)
    a_diag = jnp.zeros((c_len, B_HEADS * HG_SUB), F32)
    for sg in range(HG_SUB):
        ks = jnp.broadcast_to(k3[:, sg:sg + 1, :], k3.shape)
        cs = jnp.broadcast_to(c3[:, sg:sg + 1, :], c3.shape)
        wgt = q3 * ks * jnp.exp(jnp.minimum(c3 - cs, 0.0))
        keep = (sub_pos <= sg) if rev else (sub_pos >= sg)
        wgt = jnp.where(keep, wgt, 0.0).reshape(c_len, w).astype(BF16)
        sel = jnp.where((hs_c == hs_r * HG_SUB + sg), 1.0, 0.0).astype(BF16)
        a_diag = a_diag + _dot(wgt, sel)

    o = jnp.zeros((c_len, w), F32)
    for j in range(n_sub):
        r0 = j * HG_SUB
        r_last = r0 if rev else r0 + HG_SUB - 1
        e_j = c[r_last:r_last + 1, :]
        qj = (q * jnp.exp(jnp.minimum(c - e_j, 0.0))).astype(BF16)
        kj = k[r0:r0 + HG_SUB, :] * jnp.exp(e_j - c[r0:r0 + HG_SUB, :])
        kbd = jnp.where(bd, jnp.concatenate([kj] * B_HEADS, axis=0), 0.0).astype(BF16)
        a_j = _dot_nt(qj, kbd)
        after = (row < r0) if rev else (row >= r0 + HG_SUB)
        inside = (row >= r0) & (row < r0 + HG_SUB)
        a_j = jnp.where(after, a_j, jnp.where(inside, a_diag, 0.0)).astype(BF16)
        vbd = jnp.where(bd, jnp.concatenate([v[r0:r0 + HG_SUB, :]] * B_HEADS, axis=0), jnp.zeros((), BF16))
        o = o + _dot(a_j, vbd)

    st = st_ref[...]
    o = o + _dot_nt((q * jnp.exp(c)).astype(BF16), st.astype(BF16))
    r_end = 0 if rev else c_len - 1
    c_end = c[r_end:r_end + 1, :]
    k_end = (k * jnp.exp(c_end - c)).astype(BF16)
    upd = _dot_tn(v, k_end)
    hh_r = lax.broadcasted_iota(jnp.int32, (w, w), 0) // B_KEY_DIM
    hh_c = lax.broadcasted_iota(jnp.int32, (w, w), 1) // B_KEY_DIM
    st_ref[...] = jnp.exp(c_end) * st + jnp.where(hh_r == hh_c, upd, 0.0)

    if readout:
        tot = o + of_ref[...]
        y = _head_norm(tot, gain_ref[...], segm_ref[...]) * _silu(g_ref[...].astype(F32))
        o_ref[...] = y.astype(BF16)
    else:
        o_ref[...] = o


def _hgrn_dir(bq, f_pre, bi, lb, *, t_lat, rev, readout_args=None):
    n = bq.shape[0]
    c_len = HG_CHUNK
    n_lat = t_lat // c_len
    n_all = n // c_len
    n_ctx = n_all - n_lat

    if rev:
        def blk(i):
            return (jnp.where(i < n_ctx, n_all - 1 - i, n_lat - 1 - (i - n_ctx)), 0)
    else:
        def blk(i):
            return (jnp.where(i < n_ctx, n_lat + i, i - n_ctx), 0)

    tile = pl.BlockSpec((c_len, 256), blk)
    in_specs = [tile, tile, tile, _const_spec((1, 256))]
    args = [bq, f_pre, bi, lb]
    readout = readout_args is not None
    if readout:
        o_fwd, bg, gain, segm = readout_args
        in_specs += [tile, tile, _const_spec((1, 256)), _const_spec((256, 256))]
        args += [o_fwd, bg, gain, segm]
    return pl.pallas_call(
        functools.partial(_hgrn_kernel, rev=rev, readout=readout),
        grid=(n_all,),
        in_specs=in_specs,
        out_specs=tile,
        out_shape=jax.ShapeDtypeStruct((n, BRANCH_WIDTH), BF16 if readout else F32),
        scratch_shapes=[pltpu.VMEM((256, 256), F32)],
        compiler_params=_cparams(("arbitrary",)),
        name="hgrn_rev" if rev else "hgrn_fwd",
    )(*args)


def _merge_kernel(x_ref, mods_ref, ya_ref, yb_ref, yc_ref, yd_ref, gt_ref, wb_ref, wo_ref, o_ref, *, t_lat, tm):
    i = pl.program_id(0)
    row = i * tm + lax.broadcasted_iota(jnp.int32, (tm, 1), 0)
    is_lat = row < t_lat
    merged = None
    for n, y_ref in enumerate((ya_ref, yb_ref, yc_ref, yd_ref)):
        term = gt_ref[:, n * D_MODEL:(n + 1) * D_MODEL].astype(F32) * _dot(y_ref[...], wb_ref[n])
        merged = term if merged is None else merged + term
    y = _dot(merged.astype(BF16), wo_ref[...])
    o_ref[...] = x_ref[...] + _row_mod(mods_ref, 5, is_lat) * y


def _merge(x_all, mods, ya, yb, yc, yd, gates, wb, wo, *, t_lat, n_rows):
    n, d = x_all.shape
    tm = TOKEN_TILE

    def tile(w):
        return pl.BlockSpec((tm, w), lambda i: (i, 0))

    return pl.pallas_call(
        functools.partial(_merge_kernel, t_lat=t_lat, tm=tm),
        grid=(n_rows // tm,),
        in_specs=[tile(d), _const_spec((2, N_MOD, d)), tile(256), tile(256), tile(256), tile(256),
                  tile(N_BRANCH * d), _const_spec((N_BRANCH, BRANCH_WIDTH, d)), _const_spec((d, d))],
        out_specs=tile(d),
        out_shape=jax.ShapeDtypeStruct((n, d), F32),
        input_output_aliases={0: 0},
        compiler_params=_cparams(("parallel",)),
        name="merge_out",
    )(x_all, mods, ya, yb, yc, yd, gates, wb, wo)


def _final_norm_kernel(x_ref, g_ref, o_ref):
    o_ref[...] = _rms_rows(x_ref[...]) * g_ref[...]


def _final_norm(x_all, gain, *, t_lat):
    d = x_all.shape[1]
    tm = 512
    return pl.pallas_call(
        _final_norm_kernel,
        grid=(t_lat // tm,),
        in_specs=[pl.BlockSpec((tm, d), lambda i: (i, 0)), _const_spec((1, d))],
        out_specs=pl.BlockSpec((tm, d), lambda i: (i, 0)),
        out_shape=jax.ShapeDtypeStruct((t_lat, d), F32),
        compiler_params=_cparams(("parallel",)),
        name="final_norm",
    )(x_all, gain)


def _rope_tables_padded(t_lat, n_ctx):
    t = jnp.arange(t_lat, dtype=jnp.int32)
    half = HEAD_DIM // 2
    inv = 1.0 / (ROPE_THETA ** (jnp.arange(0, half, 2, dtype=F32) / half))
    ang_r = (t // GRID_W).astype(F32)[:, None] * inv
    ang_c = (t % GRID_W).astype(F32)[:, None] * inv
    cos = jnp.concatenate([jnp.cos(ang_r), jnp.cos(ang_r), jnp.cos(ang_c), jnp.cos(ang_c)], axis=1)
    sin = jnp.concatenate([-jnp.sin(ang_r), jnp.sin(ang_r), -jnp.sin(ang_c), jnp.sin(ang_c)], axis=1)
    cos = jnp.concatenate([cos, jnp.ones((n_ctx, HEAD_DIM), F32)], axis=0)
    sin = jnp.concatenate([sin, jnp.zeros((n_ctx, HEAD_DIM), F32)], axis=0)
    return jnp.tile(cos, (1, 2)), jnp.tile(sin, (1, 2))


def _pack_w_in(w_in_l):
    d = w_in_l.shape[0]
    zeros = jnp.zeros((d, HEAD_DIM), w_in_l.dtype)

    def widen(cols):
        return jnp.concatenate([cols[:, :HEAD_DIM], zeros, cols[:, HEAD_DIM:], zeros], axis=1)

    return jnp.concatenate([w_in_l[:, :256], widen(w_in_l[:, 256:384]), widen(w_in_l[:, 384:512]),
                            w_in_l[:, 512:]], axis=1).astype(BF16)


def _block_diag(w_group):
    g, ci, co = w_group.shape
    out = jnp.zeros((g * ci, g * co), w_group.dtype)
    for n in range(g):
        out = out.at[n * ci:(n + 1) * ci, n * co:(n + 1) * co].set(w_group[n])
    return out


def kernel(x, c, ctx, c_ctx, w_ada, b_ada, ffn1_norm, ffn1_w_gate, ffn1_w_up, ffn1_w_down, mix_norm, w_in, a_q_norm, a_k_norm, b_lb_logits, b_o_norm, c_w_group, c_scale, d_rel_bias, w_branch, w_out, ffn2_norm, ffn2_w_gate, ffn2_w_up, ffn2_w_down, final_norm):
    assert x.shape[0] == 1 and ctx.shape[0] == 1
    depth = w_ada.shape[0]
    t_lat = x.shape[1]
    n_ctx = ctx.shape[1]
    n = t_lat + n_ctx
    rows = t_lat // GRID_W
    assert t_lat % max(NA_QB, FLASH_TQ, n_ctx) == 0 and n_ctx % TOKEN_TILE == 0 and rows >= 2 * NA_QROWS

    x_all = jnp.concatenate([x[0], ctx[0]], axis=0)
    mods_all = _ada_mods(c, c_ctx, w_ada, b_ada)
    cos, sin = _rope_tables_padded(t_lat, n_ctx)
    segm = (jnp.kron(jnp.eye(BRANCH_WIDTH // HEAD_DIM), jnp.ones((HEAD_DIM, HEAD_DIM))) / HEAD_DIM).astype(BF16)
    lb_all = jnp.cumsum(jax.nn.softmax(b_lb_logits.astype(F32), axis=0), axis=0)
    lb_all = lb_all - lb_all[:1]

    for l in range(depth):
        with_ctx_out = l < depth - 1
        n_rows = n if with_ctx_out else t_lat
        mods = mods_all[l]
        x_all = _ffn_half(x_all, mods, ffn1_norm[l][None], ffn1_w_gate[l].astype(BF16), ffn1_w_up[l].astype(BF16),
                          ffn1_w_down[l].astype(BF16), t_lat=t_lat, k0=0, n_rows=n)
        (aq, ak, av, bq, bff, bfb, bi, bg, cx, dq, dk, dv, gates) = _in_proj(
            x_all, mods, mix_norm[l][None], _pack_w_in(w_in[l]), cos, sin,
            jnp.tile(a_q_norm[l], A_HEADS)[None], jnp.tile(a_k_norm[l], 2 * A_KV_HEADS)[None], segm, t_lat=t_lat)
        if with_ctx_out:
            ya, yd = _ctx_attn(aq, ak, av, dq, dk, dv, t_lat=t_lat)
        else:
            ya = jnp.zeros((n, BRANCH_WIDTH), BF16)
            yd = ya
        ya = _gqa_latent(aq, ak, av, ya, t_lat=t_lat)
        yd = _na_latent(dq, dk, dv, _na_bias_tables(d_rel_bias[l], rows), yd, t_lat=t_lat)
        yc = _pool_mixer(cx, _block_diag(c_w_group[l]).astype(BF16), c_scale[l][None], t_lat=t_lat)
        o_fwd = _hgrn_dir(bq, bff, bi, lb_all[l, 0][None], t_lat=t_lat, rev=False)
        yb = _hgrn_dir(bq, bfb, bi, lb_all[l, 1][None], t_lat=t_lat, rev=True,
                       readout_args=(o_fwd, bg, jnp.tile(b_o_norm[l], B_HEADS)[None], segm))
        x_all = _merge(x_all, mods, ya, yb, yc, yd, gates, w_branch[l].astype(BF16), w_out[l].astype(BF16),
                       t_lat=t_lat, n_rows=n_rows)
        x_all = _ffn_half(x_all, mods, ffn2_norm[l][None], ffn2_w_gate[l].astype(BF16), ffn2_w_up[l].astype(BF16),
                          ffn2_w_down[l].astype(BF16), t_lat=t_lat, k0=6, n_rows=n_rows)
    return _final_norm(x_all, final_norm[None], t_lat=t_lat)[None]
```

```python
import functools

import numpy as np
import jax
import jax.numpy as jnp
from jax import lax
from jax.experimental import pallas as pl
from jax.experimental.pallas import tpu as pltpu

F32 = jnp.float32
BF16 = jnp.bfloat16

D_MODEL = 1024
GRID_W = 64
HEAD_DIM = 64
BRANCH_WIDTH = 256
N_BRANCH = 4
A_HEADS = 4
A_KV_HEADS = 2
ROPE_THETA = 10000.0
B_HEADS = 4
B_KEY_DIM = 64
C_WINDOWS = (2, 4, 8, 16)
D_HEADS = 4
NA_WIN_R = 8
NA_WIN_C = 16
D_FF = 2816
N_MOD = 9
EPS = 1e-6
ATTN_SCALE = HEAD_DIM ** -0.5
LOG2_E = 1.4426950408889634
NEG_BIG = -1e30

V7X_VMEM_BYTES = 64 * 1024 * 1024
VMEM_LIMIT = V7X_VMEM_BYTES - 8 * 1024 * 1024

SEG_AQ = (0, 256)
SEG_AK = (256, 512)
SEG_AV = (512, 768)
SEG_BQ = (768, 1024)
SEG_BFF = (1024, 1280)
SEG_BFB = (1280, 1536)
SEG_BI = (1536, 1792)
SEG_BG = (1792, 2048)
SEG_CX = (2048, 2304)
SEG_DQ = (2304, 2560)
SEG_DK = (2560, 2816)
SEG_DV = (2816, 3072)
SEG_GATE = (3072, 3072 + N_BRANCH * D_MODEL)
W_ALL_WIDTH = SEG_GATE[1]

TOKEN_TILE = 256
FLASH_TQ = 1024
FLASH_QC = 512
FLASH_SB = 256
FLASH_NBUF = 4
FLASH_PIECE = 64
NA_QROWS = 8
NA_QB = NA_QROWS * GRID_W
NA_HALO = 256
HG_BLOCK = 256
HG_CHUNK = 128
HG_SUB = 8


def _cparams(sem):
    return pltpu.CompilerParams(dimension_semantics=sem, vmem_limit_bytes=VMEM_LIMIT)


def _const_spec(shape):
    nd = len(shape)
    return pl.BlockSpec(shape, lambda *_: (0,) * nd, pipeline_mode=pl.Buffered(1))


def _sigmoid(x):
    return 1.0 / (1.0 + jnp.exp(-x))


def _silu(x):
    return x * _sigmoid(x)


def _dot(a, b):
    return jnp.dot(a, b, preferred_element_type=F32)


def _dot_nt(a, b):
    return lax.dot_general(a, b, (((1,), (1,)), ((), ())), preferred_element_type=F32)


def _dot_tn(a, b):
    return lax.dot_general(a, b, (((0,), (0,)), ((), ())), preferred_element_type=F32)


def _split_dot(x, m):
    hi = x.astype(BF16)
    r1 = x - hi.astype(F32)
    mid = r1.astype(BF16)
    lo = (r1 - mid.astype(F32)).astype(BF16)
    return _dot(hi, m) + _dot(mid, m) + _dot(lo, m)


def _split_dot_left(m, x):
    hi = x.astype(BF16)
    r1 = x - hi.astype(F32)
    mid = r1.astype(BF16)
    lo = (r1 - mid.astype(F32)).astype(BF16)
    return _dot(m, hi) + _dot(m, mid) + _dot(m, lo)


def _row_mod(mods_ref, k, is_lat):
    return jnp.where(is_lat, mods_ref[0, k:k + 1, :], mods_ref[1, k:k + 1, :])


def _rms_rows(x):
    return x * lax.rsqrt(jnp.mean(x * x, axis=-1, keepdims=True) + EPS)


def _ada_kernel(s_ref, w_ref, b_ref, o_ref):
    o_ref[0] = _dot(_silu(s_ref[...]).astype(BF16), w_ref[0].astype(BF16)) + b_ref[0]


def _ada_mods(c, c_ctx, w_ada, b_ada):
    depth = w_ada.shape[0]
    width = w_ada.shape[2]
    tn = 1024
    s = jnp.zeros((8, D_MODEL), F32).at[0].set(c[0]).at[1].set(c_ctx)
    out = pl.pallas_call(
        _ada_kernel,
        grid=(depth, width // tn),
        in_specs=[
            pl.BlockSpec((8, D_MODEL), lambda l, j: (0, 0)),
            pl.BlockSpec((1, D_MODEL, tn), lambda l, j: (l, 0, j)),
            pl.BlockSpec((1, 1, tn), lambda l, j: (l, 0, j)),
        ],
        out_specs=pl.BlockSpec((1, 8, tn), lambda l, j: (l, 0, j)),
        out_shape=jax.ShapeDtypeStruct((depth, 8, width), F32),
        compiler_params=_cparams(("parallel", "parallel")),
        name="ada_mods",
    )(s, w_ada, b_ada.reshape(depth, 1, width))
    return out[:, :2].reshape(depth, 2, N_MOD, D_MODEL)


def _ffn_kernel(*refs, t_lat, tm, k0, split_input):
    if split_input:
        xl_ref, xc_ref, mods_ref, nrm_ref, wg_ref, wu_ref, wd_ref, o_ref = refs
    else:
        x_ref, mods_ref, nrm_ref, wg_ref, wu_ref, wd_ref, o_ref = refs
    i = pl.program_id(0)
    row = i * tm + lax.broadcasted_iota(jnp.int32, (tm, 1), 0)
    is_lat = row < t_lat
    x = jnp.where(i * tm < t_lat, xl_ref[...], xc_ref[...]) if split_input else x_ref[...]
    z = _rms_rows(x) * nrm_ref[...]
    z = (z * (1.0 + _row_mod(mods_ref, k0 + 1, is_lat)) + _row_mod(mods_ref, k0, is_lat)).astype(BF16)
    g = _dot(z, wg_ref[...])
    u = _dot(z, wu_ref[...])
    a = (_silu(g) * u).astype(BF16)
    y = _dot(a, wd_ref[...])
    o_ref[...] = x + 0.5 * _row_mod(mods_ref, k0 + 2, is_lat) * y


def _ffn_half(x_all, mods, nrm, wg, wu, wd, *, t_lat, k0, n_rows, x_ctx=None):
    split_input = x_ctx is not None
    d = x_all.shape[1]
    n = x_all.shape[0] + (x_ctx.shape[0] if split_input else 0)
    tm = TOKEN_TILE
    f = wg.shape[1]
    if split_input:
        n_lat_tiles = t_lat // tm
        x_specs = [pl.BlockSpec((tm, d), lambda i: (jnp.minimum(i, n_lat_tiles - 1), 0)),
                   pl.BlockSpec((tm, d), lambda i: (jnp.maximum(i - n_lat_tiles, 0), 0))]
        x_args = [x_all, x_ctx]
    else:
        x_specs = [pl.BlockSpec((tm, d), lambda i: (i, 0))]
        x_args = [x_all]
    return pl.pallas_call(
        functools.partial(_ffn_kernel, t_lat=t_lat, tm=tm, k0=k0, split_input=split_input),
        grid=(n_rows // tm,),
        in_specs=x_specs + [
            _const_spec((2, N_MOD, d)),
            _const_spec((1, d)),
            _const_spec((d, f)),
            _const_spec((d, f)),
            _const_spec((f, d)),
        ],
        out_specs=pl.BlockSpec((tm, d), lambda i: (i, 0)),
        out_shape=jax.ShapeDtypeStruct((n, d), F32),
        input_output_aliases={} if split_input else {0: 0},
        compiler_params=_cparams(("parallel",)),
        name="ffn_half",
    )(*x_args, mods, nrm, wg, wu, wd)


def _head_norm(y, gain, segm):
    ms = _split_dot(y * y, segm)
    return y * lax.rsqrt(ms + EPS) * gain


def _rope(y, cos, sin_signed):
    w = y.shape[1]
    lane = lax.broadcasted_iota(jnp.int32, y.shape, 1)
    partner = jnp.where((lane & 31) < 16, pltpu.roll(y, w - 16, 1), pltpu.roll(y, 16, 1))
    return y * cos + partner * sin_signed


def _inproj_kernel(x_ref, mods_ref, nrm_ref, w_ref, cos_ref, sin_ref, gq_ref, gk_ref, segm_ref,
                   aq_ref, ak_ref, av_ref, bq_ref, bff_ref, bfb_ref, bi_ref, bg_ref, cx_ref,
                   dq_ref, dk_ref, dv_ref, gt_ref, *, t_lat, tm):
    i = pl.program_id(0)
    row = i * tm + lax.broadcasted_iota(jnp.int32, (tm, 1), 0)
    is_lat = row < t_lat
    z = _rms_rows(x_ref[...]) * nrm_ref[...]
    z = (z * (1.0 + _row_mod(mods_ref, 4, is_lat)) + _row_mod(mods_ref, 3, is_lat)).astype(BF16)

    def proj(seg):
        return _dot(z, w_ref[:, seg[0]:seg[1]])

    cos = jnp.concatenate([cos_ref[...]] * 2, axis=1)
    sin = jnp.concatenate([sin_ref[...]] * 2, axis=1)
    q = _rope(_head_norm(proj(SEG_AQ), gq_ref[...], segm_ref[...]), cos, sin)
    aq_ref[...] = (q * (ATTN_SCALE * LOG2_E)).T.astype(BF16)
    k = _rope(_head_norm(proj(SEG_AK), gk_ref[...], segm_ref[...]), cos, sin)
    ak_ref[...] = k.astype(BF16)
    lane = lax.broadcasted_iota(jnp.int32, (tm, SEG_AV[1] - SEG_AV[0]), 1)
    av_ref[...] = (proj(SEG_AV) + jnp.where((lane & 127) == HEAD_DIM, 1.0, 0.0)).T.astype(BF16)

    bq_ref[...] = (proj(SEG_BQ) * (B_KEY_DIM ** -0.5)).astype(BF16)
    bff_ref[...] = proj(SEG_BFF)
    bfb_ref[...] = proj(SEG_BFB)
    bi_ref[...] = proj(SEG_BI).astype(BF16)
    bg_ref[...] = proj(SEG_BG).astype(BF16)
    cx_ref[...] = proj(SEG_CX)
    dq_ref[...] = (proj(SEG_DQ) * ATTN_SCALE).astype(BF16)
    dk_ref[...] = proj(SEG_DK).astype(BF16)
    dv_ref[...] = proj(SEG_DV).astype(BF16)
    gw = 512
    for c0 in range(SEG_GATE[0], SEG_GATE[1], gw):
        o0 = c0 - SEG_GATE[0]
        gt_ref[:, o0:o0 + gw] = _sigmoid(proj((c0, c0 + gw))).astype(BF16)


def _in_proj(x_all, mods, nrm, w_all, cos, sin, gq, gk, segm, *, t_lat):
    n, d = x_all.shape
    tm = TOKEN_TILE
    widths = [(256, BF16), (256, BF16), (256, BF16), (256, BF16), (256, F32), (256, F32), (256, BF16),
              (256, BF16), (256, F32), (256, BF16), (256, BF16), (256, BF16), (N_BRANCH * D_MODEL, BF16)]
    transposed = (0, 2)
    out_specs = [pl.BlockSpec((w, tm), lambda i: (0, i)) if o in transposed else pl.BlockSpec((tm, w), lambda i: (i, 0))
                 for o, (w, _) in enumerate(widths)]
    out_shape = [jax.ShapeDtypeStruct((w, n) if o in transposed else (n, w), dt) for o, (w, dt) in enumerate(widths)]
    return pl.pallas_call(
        functools.partial(_inproj_kernel, t_lat=t_lat, tm=tm),
        grid=(n // tm,),
        in_specs=[
            pl.BlockSpec((tm, d), lambda i: (i, 0)),
            _const_spec((2, N_MOD, d)),
            _const_spec((1, d)),
            _const_spec((d, W_ALL_WIDTH)),
            pl.BlockSpec((tm, 128), lambda i: (i, 0)),
            pl.BlockSpec((tm, 128), lambda i: (i, 0)),
            _const_spec((1, 256)),
            _const_spec((1, 256)),
            _const_spec((256, 256)),
        ],
        out_specs=out_specs,
        out_shape=out_shape,
        compiler_params=_cparams(("parallel",)),
        name="in_proj",
    )(x_all, mods, nrm, w_all, cos, sin, gq, gk, segm)


def _softmax_attend(q, k, v, exp_fn):
    s = _dot_nt(q, k)
    p = exp_fn(s - jnp.max(s, axis=-1, keepdims=True))
    return _dot(p.astype(BF16), v) / jnp.sum(p, axis=-1, keepdims=True)


def _ctx_attn_kernel(aqt_ref, ak_ref, avt_ref, dq_ref, dk_ref, dv_ref, ya_ref, yd_ref):
    outs = []
    for h in range(A_HEADS):
        g = h // (A_HEADS // A_KV_HEADS)
        st = _dot(ak_ref[:, 128 * g:128 * g + 64], aqt_ref[64 * h:64 * h + 64, :])
        pt = jnp.exp2(st - jnp.max(st, axis=0, keepdims=True)).astype(BF16)
        acc = _dot(avt_ref[128 * g:128 * g + 128, :], pt)
        outs.append((acc[:HEAD_DIM] / acc[HEAD_DIM:HEAD_DIM + 1]).T)
    ya_ref[...] = jnp.concatenate(outs, axis=1).astype(BF16)
    outs = []
    for h in range(D_HEADS):
        hs = slice(64 * h, 64 * h + 64)
        outs.append(_softmax_attend(dq_ref[:, hs], dk_ref[:, hs], dv_ref[:, hs], jnp.exp))
    yd_ref[...] = jnp.concatenate(outs, axis=1).astype(BF16)


def _ctx_attn(aqt, ak, avt, dq, dk, dv, *, t_lat):
    n = ak.shape[0]
    n_ctx = n - t_lat
    blk = t_lat // n_ctx

    def spec(w):
        return pl.BlockSpec((n_ctx, w), lambda i: (blk, 0))

    def spec_t(w):
        return pl.BlockSpec((w, n_ctx), lambda i: (0, blk))

    return pl.pallas_call(
        _ctx_attn_kernel,
        grid=(1,),
        in_specs=[spec_t(256), spec(256), spec_t(256), spec(256), spec(256), spec(256)],
        out_specs=[spec(256), spec(256)],
        out_shape=[jax.ShapeDtypeStruct((n, BRANCH_WIDTH), BF16)] * 2,
        compiler_params=_cparams(("arbitrary",)),
        name="ctx_attn",
    )(aqt, ak, avt, dq, dk, dv)


def _flash_kernel(qt_ref, k_ref, vt_ref, prev_ref, o_ref, m_ref, acc_ref, *bufs, tk, tq):
    del prev_ref
    s_refs, p_refs = bufs[:FLASH_NBUF], bufs[FLASH_NBUF:]
    j = pl.program_id(1)

    @pl.when(j == 0)
    def _():
        m_ref[...] = jnp.full(m_ref.shape, -jnp.inf, F32)
        acc_ref[...] = jnp.zeros(acc_ref.shape, F32)

    group = A_HEADS // A_KV_HEADS
    n_qc = tq // FLASH_QC
    n_sb = tk // FLASH_SB
    chains = [(h, c) for h in range(A_HEADS) for c in range(n_qc)]

    def score_block(ci, sb):
        h, c = chains[ci]
        g = h // group
        rows = slice(sb * FLASH_SB, (sb + 1) * FLASH_SB)
        st = _dot(k_ref[rows, 128 * g:128 * g + 64], qt_ref[64 * h:64 * h + 64, c * FLASH_QC:(c + 1) * FLASH_QC])
        s_refs[ci % FLASH_NBUF][rows, :] = st
        return jnp.max(st, axis=0, keepdims=True)

    def exp_block(ci, sb, m_new):
        for r0 in range(sb * FLASH_SB, (sb + 1) * FLASH_SB, FLASH_PIECE):
            rows = slice(r0, r0 + FLASH_PIECE)
            p_refs[ci % 2][rows, :] = jnp.exp2(s_refs[ci % FLASH_NBUF][rows, :] - m_new).astype(BF16)

    def value_block(ci, sb):
        g = chains[ci][0] // group
        rows = slice(sb * FLASH_SB, (sb + 1) * FLASH_SB)
        return _dot(vt_ref[128 * g:128 * g + 128, rows], p_refs[ci % 2][rows, :])

    def fold(a, b):
        return b if a is None else a + b

    def fold_max(a, b):
        return b if a is None else jnp.maximum(a, b)

    n_ch = len(chains)
    blk_max, m_new, alpha = {}, {}, {}
    for stage in range(-1, n_ch + 1):
        c_s, c_e, c_v = stage + 1, stage, stage - 1
        if 0 <= c_e < n_ch:
            h, c = chains[c_e]
            cols = slice(c * FLASH_QC, (c + 1) * FLASH_QC)
            m_old = m_ref[h, :, cols]
            m_new[c_e] = jnp.maximum(m_old, blk_max.pop(c_e))
            m_ref[h, :, cols] = m_new[c_e]
            alpha[c_e] = jnp.exp2(m_old - m_new[c_e])
        part, mx = None, None
        for sb in range(n_sb):
            if c_s < n_ch:
                mx = fold_max(mx, score_block(c_s, sb))
            if 0 <= c_e < n_ch:
                exp_block(c_e, sb, m_new[c_e])
            if 0 <= c_v:
                part = fold(part, value_block(c_v, sb))
        if c_s < n_ch:
            blk_max[c_s] = mx
        if 0 <= c_v:
            h, c = chains[c_v]
            cols = slice(c * FLASH_QC, (c + 1) * FLASH_QC)
            acc_ref[h, :, cols] = alpha.pop(c_v) * acc_ref[h, :, cols] + part

    @pl.when(j == pl.num_programs(1) - 1)
    def _():
        outs = []
        for h in range(A_HEADS):
            acc = acc_ref[h]
            outs.append((acc[:HEAD_DIM] / acc[HEAD_DIM:HEAD_DIM + 1]).T)
        o_ref[...] = jnp.concatenate(outs, axis=1).astype(BF16)


def _flash_tk(n):
    for tk in (1280, 768, 512, 256):
        if n % tk == 0:
            return tk
    raise ValueError(f"unsupported key count {n}")


def _gqa_latent(aqt, ak, avt, ya_prev, *, t_lat):
    n = ak.shape[0]
    tq = FLASH_TQ
    tk = _flash_tk(n)
    return pl.pallas_call(
        functools.partial(_flash_kernel, tk=tk, tq=tq),
        grid=(t_lat // tq, n // tk),
        in_specs=[
            pl.BlockSpec((256, tq), lambda i, j: (0, i)),
            pl.BlockSpec((tk, 256), lambda i, j: (j, 0)),
            pl.BlockSpec((256, tk), lambda i, j: (0, j)),
            pl.BlockSpec(memory_space=pl.ANY),
        ],
        out_specs=pl.BlockSpec((tq, 256), lambda i, j: (i, 0)),
        out_shape=jax.ShapeDtypeStruct((n, BRANCH_WIDTH), BF16),
        scratch_shapes=[pltpu.VMEM((A_HEADS, 1, tq), F32),
                        pltpu.VMEM((A_HEADS, 128, tq), F32)]
                       + [pltpu.VMEM((tk, FLASH_QC), F32)] * FLASH_NBUF
                       + [pltpu.VMEM((tk, FLASH_QC), BF16)] * 2,
        input_output_aliases={3: 0},
        compiler_params=_cparams(("parallel", "arbitrary")),
        name="gqa_flash",
    )(aqt, ak, avt, ya_prev)


def _na_kernel(q_ref, kp_ref, km_ref, kn_ref, vp_ref, vm_ref, vn_ref, kc_ref, vc_ref, bias_ref, mask_ref, prev_ref,
               o_ref):
    del prev_ref
    outs = []
    for h in range(D_HEADS):
        hs = slice(64 * h, 64 * h + 64)
        q = q_ref[:, hs]
        kcat = jnp.concatenate([kp_ref[:, hs], km_ref[:, hs], kn_ref[:, hs]], axis=0)
        vcat = jnp.concatenate([vp_ref[:, hs], vm_ref[:, hs], vn_ref[:, hs]], axis=0)
        s_loc = _dot_nt(q, kcat) + (bias_ref[h] + mask_ref[0])
        s_ctx = _dot_nt(q, kc_ref[:, hs])
        m = jnp.maximum(jnp.max(s_loc, axis=-1, keepdims=True), jnp.max(s_ctx, axis=-1, keepdims=True))
        p_loc = jnp.exp(s_loc - m)
        p_ctx = jnp.exp(s_ctx - m)
        denom = jnp.sum(p_loc, axis=-1, keepdims=True) + jnp.sum(p_ctx, axis=-1, keepdims=True)
        o = _dot(p_loc.astype(BF16), vcat) + _dot(p_ctx.astype(BF16), vc_ref[:, hs])
        outs.append(o / denom)
    o_ref[...] = jnp.concatenate(outs, axis=1).astype(BF16)


def _na_bias_tables(rel_bias, rows):
    wr = min(NA_WIN_R, rows)
    halo_rows = NA_HALO // GRID_W
    krows = NA_QROWS + 2 * halo_rows
    nb = rows // NA_QROWS
    qc = np.arange(GRID_W)[:, None]
    kc = np.arange(GRID_W)[None, :]
    cs = np.clip(qc - NA_WIN_C // 2, 0, GRID_W - NA_WIN_C)
    in_col = (kc >= cs) & (kc < cs + NA_WIN_C)
    e_col = (kc - qc + (NA_WIN_C - 1))[:, :, None] == np.arange(2 * NA_WIN_C - 1)
    qr_l = np.arange(NA_QROWS)[:, None]
    kr_l = np.arange(krows)[None, :]
    e_row = (kr_l - halo_rows - qr_l + (NA_WIN_R - 1))[:, :, None] == np.arange(2 * NA_WIN_R - 1)
    hi = lax.Precision.HIGHEST
    tmp = jnp.einsum("hrc,qkr->hqkc", rel_bias.astype(F32), e_row.astype(np.float32), precision=hi)
    full = jnp.einsum("hqkc,pjc->hqpkj", tmp, e_col.astype(np.float32), precision=hi)
    masks = []
    for b in (0, min(1, nb - 1), nb - 1):
        qr = NA_QROWS * b + qr_l
        kr = NA_QROWS * b - halo_rows + kr_l
        rs = np.clip(qr - wr // 2, 0, rows - wr)
        in_row = (kr >= rs) & (kr < rs + wr)
        mask = in_row[:, None, :, None] & in_col[None, :, None, :]
        masks.append(np.where(mask, 0.0, NEG_BIG).reshape(NA_QB, krows * GRID_W))
    return full.reshape(D_HEADS, NA_QB, krows * GRID_W), np.stack(masks).astype(np.float32)


def _na_latent(dq, dk, dv, bias, masks, yd_prev, *, t_lat):
    n = dq.shape[0]
    nb = t_lat // NA_QB
    r = NA_QB // NA_HALO
    last_halo = t_lat // NA_HALO - 1
    ctx_blk = t_lat // (n - t_lat)
    n_ctx = n - t_lat

    def prev_map(b):
        return (jnp.maximum(r * b - 1, 0), 0)

    def next_map(b):
        return (jnp.minimum(r * b + r, last_halo), 0)

    def variant(b):
        return (jnp.where(b == 0, 0, jnp.where(b == nb - 1, 2, 1)), 0, 0)

    main = pl.BlockSpec((NA_QB, 256), lambda b: (b, 0))
    prev = pl.BlockSpec((NA_HALO, 256), prev_map)
    nxt = pl.BlockSpec((NA_HALO, 256), next_map)
    ctx = pl.BlockSpec((n_ctx, 256), lambda b: (ctx_blk, 0))
    return pl.pallas_call(
        _na_kernel,
        grid=(nb,),
        in_specs=[main, prev, main, nxt, prev, main, nxt, ctx, ctx,
                  _const_spec((D_HEADS, NA_QB, NA_QB + 2 * NA_HALO)),
                  pl.BlockSpec((1, NA_QB, NA_QB + 2 * NA_HALO), variant),
                  pl.BlockSpec(memory_space=pl.ANY)],
        out_specs=main,
        out_shape=jax.ShapeDtypeStruct((n, BRANCH_WIDTH), BF16),
        input_output_aliases={11: 0},
        compiler_params=_cparams(("parallel",)),
        name="na_attn",
    )(dq, dk, dk, dk, dv, dv, dv, dk, dv, bias, masks, yd_prev)


def _pool_kernel(x_ref, xp_ref, xn_ref, w_ref, sc_ref, o_ref, cat_ref, a_ref, b_ref, *, t_lat, n_all, tm):
    i = pl.program_id(0)
    start = i * tm
    in_lat = start < t_lat
    seg_lo = jnp.where(in_lat, 0, t_lat)
    seg_hi = jnp.where(in_lat, t_lat, n_all)
    x = x_ref[...]
    cat_ref[0:8, :] = jnp.where(start > seg_lo, xp_ref[...], 0.0)
    cat_ref[8:8 + tm, :] = x
    cat_ref[8 + tm:16 + tm, :] = jnp.where(start + tm < seg_hi, xn_ref[...], 0.0)
    cat_ref[16 + tm:32 + tm, :] = jnp.zeros((16, BRANCH_WIDTH), F32)
    a_ref[0:tm + 24, :] = cat_ref[0:tm + 24, :] + cat_ref[1:tm + 25, :]
    s2 = a_ref[7:7 + tm, :]
    b_ref[0:tm + 16, :] = a_ref[0:tm + 16, :] + a_ref[2:tm + 18, :]
    s4 = b_ref[6:6 + tm, :]
    a_ref[0:tm + 8, :] = b_ref[0:tm + 8, :] + b_ref[4:tm + 12, :]
    s8 = a_ref[4:4 + tm, :]
    s16 = a_ref[0:tm, :] + a_ref[8:8 + tm, :]

    pos = start - seg_lo + lax.broadcasted_iota(jnp.int32, (tm, 1), 0)
    seg_len = seg_hi - seg_lo

    def mean(sm, w):
        lo = jnp.clip(pos - w // 2, 0, seg_len)
        hi = jnp.clip(pos - w // 2 + w, 0, seg_len)
        return sm / (hi - lo).astype(F32)

    lane = lax.broadcasted_iota(jnp.int32, (tm, BRANCH_WIDTH), 1)
    gw = BRANCH_WIDTH // len(C_WINDOWS)
    pooled = jnp.where(lane < gw, mean(s2, 2),
                       jnp.where(lane < 2 * gw, mean(s4, 4),
                                 jnp.where(lane < 3 * gw, mean(s8, 8), mean(s16, 16)))) - x
    o_ref[...] = (_dot(pooled.astype(BF16), w_ref[...]) * sc_ref[...]).astype(BF16)


def _pool_mixer(cx, w_bd, scale, *, t_lat):
    n = cx.shape[0]
    tm = TOKEN_TILE
    r = tm // 8
    last8 = n // 8 - 1
    return pl.pallas_call(
        functools.partial(_pool_kernel, t_lat=t_lat, n_all=n, tm=tm),
        grid=(n // tm,),
        in_specs=[
            pl.BlockSpec((tm, 256), lambda i: (i, 0)),
            pl.BlockSpec((8, 256), lambda i: (jnp.maximum(r * i - 1, 0), 0)),
            pl.BlockSpec((8, 256), lambda i: (jnp.minimum(r * i + r, last8), 0)),
            _const_spec((256, 256)),
            _const_spec((1, 256)),
        ],
        out_specs=pl.BlockSpec((tm, 256), lambda i: (i, 0)),
        out_shape=jax.ShapeDtypeStruct((n, BRANCH_WIDTH), BF16),
        scratch_shapes=[pltpu.VMEM((tm + 32, 256), F32)] * 3,
        compiler_params=_cparams(("parallel",)),
        name="pool_mixer",
    )(cx, cx, cx, w_bd, scale)


def _hgrn_tables(rev):
    c_len, s, w = HG_CHUNK, HG_SUB, B_HEADS * B_KEY_DIM
    t = np.arange(c_len)
    tri = (t[None, :] >= t[:, None]) if rev else (t[None, :] <= t[:, None])
    pos = np.arange(s)
    keep = (pos[None, :] <= pos[:, None]) if rev else (pos[None, :] >= pos[:, None])
    keep_add = np.where(keep, 0.0, NEG_BIG)[:, :, None] * np.ones((1, 1, w))
    head_of_lane = np.arange(w) // B_KEY_DIM
    col = np.arange(B_HEADS * s)
    sel = (col[None, None, :] == (head_of_lane[None, :, None] * s + pos[:, None, None]))
    bd = (col[:, None] // s) == head_of_lane[None, :]
    hh = head_of_lane[:, None] == head_of_lane[None, :]
    return (jnp.asarray(tri, BF16), jnp.asarray(keep_add, F32), jnp.asarray(sel, BF16),
            jnp.asarray(bd, F32), jnp.asarray(hh, F32))


def _hgrn_kernel(q_ref, f_ref, v_ref, lb_ref, tri_ref, keep_ref, sel_ref, bd_ref, hh_ref, o_ref, st_ref, *, rev):
    @pl.when(pl.program_id(0) == 0)
    def _():
        st_ref[...] = jnp.zeros(st_ref.shape, F32)

    st = st_ref[...]
    order = range(HG_BLOCK // HG_CHUNK)
    for ch in (reversed(order) if rev else order):
        rows = slice(ch * HG_CHUNK, (ch + 1) * HG_CHUNK)
        o, st = _hgrn_chunk(q_ref[rows, :], f_ref[rows, :], v_ref[rows, :], st, lb_ref, tri_ref, keep_ref,
                            sel_ref, bd_ref, hh_ref, rev=rev)
        o_ref[rows, :] = o
    st_ref[...] = st


def _hgrn_chunk(q_bf, f_pre, v, st, lb_ref, tri_ref, keep_ref, sel_ref, bd_ref, hh_ref, *, rev):
    c_len = HG_CHUNK
    n_sub = c_len // HG_SUB
    w = B_HEADS * B_KEY_DIM
    lb = lb_ref[...]
    f = lb + (1.0 - lb) * _sigmoid(f_pre)
    k = 1.0 - f
    lf = jnp.log(f)
    q = q_bf.astype(F32)
    bd = bd_ref[...]

    c = _split_dot_left(tri_ref[...], lf)

    q3 = q.reshape(n_sub, HG_SUB, w)
    k3 = k.reshape(n_sub, HG_SUB, w)
    c3 = c.reshape(n_sub, HG_SUB, w)
    a_diag = None
    for sg in range(HG_SUB):
        ks = jnp.broadcast_to(k3[:, sg:sg + 1, :], k3.shape)
        cs = jnp.broadcast_to(c3[:, sg:sg + 1, :], c3.shape)
        wgt = (q3 * ks) * jnp.exp((c3 - cs) + keep_ref[sg][None])
        part = _dot(wgt.reshape(c_len, w).astype(BF16), sel_ref[sg])
        a_diag = part if a_diag is None else a_diag + part

    o_parts = [None] * n_sub
    for j in range(n_sub):
        r0 = j * HG_SUB
        r_last = r0 if rev else r0 + HG_SUB - 1
        e_j = c[r_last:r_last + 1, :]
        kj = k[r0:r0 + HG_SUB, :] * jnp.exp(e_j - c[r0:r0 + HG_SUB, :])
        kbd = (jnp.concatenate([kj] * B_HEADS, axis=0) * bd).astype(BF16)
        vbd = (jnp.concatenate([v[r0:r0 + HG_SUB, :].astype(F32)] * B_HEADS, axis=0) * bd).astype(BF16)
        lo, hi = (0, r0) if rev else (r0 + HG_SUB, c_len)
        pieces = [a_diag[r0:r0 + HG_SUB]]
        if hi > lo:
            qj = (q[lo:hi] * jnp.exp(c[lo:hi] - e_j)).astype(BF16)
            a_off = _dot_nt(qj, kbd)
            pieces = [a_off] + pieces if rev else pieces + [a_off]
        first = 0 if rev else j
        if (sum(p.shape[0] for p in pieces) % 16) != 0:
            pad = jnp.zeros((HG_SUB, a_diag.shape[1]), F32)
            pieces = pieces + [pad] if rev else [pad] + pieces
            first = first if rev else first - 1
        a_j = jnp.concatenate(pieces, axis=0).astype(BF16)
        contrib = _dot(a_j, vbd)
        for i in range(contrib.shape[0] // HG_SUB):
            piece = contrib[i * HG_SUB:(i + 1) * HG_SUB]
            o_parts[first + i] = piece if o_parts[first + i] is None else o_parts[first + i] + piece
    o = jnp.concatenate(o_parts, axis=0)

    o = o + _dot_nt((q * jnp.exp(c)).astype(BF16), st.astype(BF16))
    r_end = 0 if rev else c_len - 1
    c_end = c[r_end:r_end + 1, :]
    k_end = (k * jnp.exp(c_end - c)).astype(BF16)
    return o, jnp.exp(c_end) * st + _dot_tn(v, k_end) * hh_ref[...]


def _hgrn_dir(bq, f_pre, bi, lb, *, t_lat, rev):
    n = bq.shape[0]
    c_len = HG_BLOCK
    n_lat = t_lat // c_len
    n_all = n // c_len
    n_ctx = n_all - n_lat

    if rev:
        def blk(i):
            return (jnp.where(i < n_ctx, n_all - 1 - i, n_lat - 1 - (i - n_ctx)), 0)
    else:
        def blk(i):
            return (jnp.where(i < n_ctx, n_lat + i, i - n_ctx), 0)

    tile = pl.BlockSpec((c_len, 256), blk)
    tables = _hgrn_tables(rev)
    in_specs = [tile, tile, tile, _const_spec((1, 256))] + [_const_spec(t.shape) for t in tables]
    args = [bq, f_pre, bi, lb, *tables]
    return pl.pallas_call(
        functools.partial(_hgrn_kernel, rev=rev),
        grid=(n_all,),
        in_specs=in_specs,
        out_specs=tile,
        out_shape=jax.ShapeDtypeStruct((n, BRANCH_WIDTH), F32),
        scratch_shapes=[pltpu.VMEM((256, 256), F32)],
        compiler_params=_cparams(("arbitrary",)),
        name="hgrn_rev" if rev else "hgrn_fwd",
    )(*args)


def _merge_kernel(x_ref, mods_ref, ya_ref, of_ref, or_ref, bg_ref, gain_ref, segm_ref, yc_ref, yd_ref, gt_ref,
                  wb_ref, wo_ref, o_ref, *, t_lat, tm):
    i = pl.program_id(0)
    row = i * tm + lax.broadcasted_iota(jnp.int32, (tm, 1), 0)
    is_lat = row < t_lat
    yb = _head_norm(of_ref[...] + or_ref[...], gain_ref[...], segm_ref[...]) * _silu(bg_ref[...].astype(F32))
    branches = (ya_ref[...], yb.astype(BF16), yc_ref[...], yd_ref[...])
    merged = None
    for n, y in enumerate(branches):
        term = gt_ref[:, n * D_MODEL:(n + 1) * D_MODEL].astype(F32) * _dot(y, wb_ref[n])
        merged = term if merged is None else merged + term
    y = _dot(merged.astype(BF16), wo_ref[...])
    o_ref[...] = x_ref[...] + _row_mod(mods_ref, 5, is_lat) * y


def _merge(x_all, mods, ya, o_fwd, o_rev, bg, b_gain, segm, yc, yd, gates, wb, wo, *, t_lat, n_rows):
    n, d = x_all.shape
    tm = TOKEN_TILE

    def tile(w):
        return pl.BlockSpec((tm, w), lambda i: (i, 0))

    return pl.pallas_call(
        functools.partial(_merge_kernel, t_lat=t_lat, tm=tm),
        grid=(n_rows // tm,),
        in_specs=[tile(d), _const_spec((2, N_MOD, d)), tile(256), tile(256), tile(256), tile(256),
                  _const_spec((1, 256)), _const_spec((256, 256)), tile(256), tile(256),
                  tile(N_BRANCH * d), _const_spec((N_BRANCH, BRANCH_WIDTH, d)), _const_spec((d, d))],
        out_specs=tile(d),
        out_shape=jax.ShapeDtypeStruct((n, d), F32),
        input_output_aliases={0: 0},
        compiler_params=_cparams(("parallel",)),
        name="merge_out",
    )(x_all, mods, ya, o_fwd, o_rev, bg, b_gain, segm, yc, yd, gates, wb, wo)


def _final_norm_kernel(x_ref, g_ref, o_ref):
    o_ref[...] = _rms_rows(x_ref[...]) * g_ref[...]


def _final_norm(x_all, gain, *, t_lat):
    d = x_all.shape[1]
    tm = 512
    return pl.pallas_call(
        _final_norm_kernel,
        grid=(t_lat // tm,),
        in_specs=[pl.BlockSpec((tm, d), lambda i: (i, 0)), _const_spec((1, d))],
        out_specs=pl.BlockSpec((tm, d), lambda i: (i, 0)),
        out_shape=jax.ShapeDtypeStruct((t_lat, d), F32),
        compiler_params=_cparams(("parallel",)),
        name="final_norm",
    )(x_all, gain)


def _rope_tables_padded(t_lat, n_ctx):
    t = jnp.arange(t_lat, dtype=jnp.int32)
    half = HEAD_DIM // 2
    inv = 1.0 / (ROPE_THETA ** (jnp.arange(0, half, 2, dtype=F32) / half))
    ang_r = (t // GRID_W).astype(F32)[:, None] * inv
    ang_c = (t % GRID_W).astype(F32)[:, None] * inv
    cos = jnp.concatenate([jnp.cos(ang_r), jnp.cos(ang_r), jnp.cos(ang_c), jnp.cos(ang_c)], axis=1)
    sin = jnp.concatenate([-jnp.sin(ang_r), jnp.sin(ang_r), -jnp.sin(ang_c), jnp.sin(ang_c)], axis=1)
    cos = jnp.concatenate([cos, jnp.ones((n_ctx, HEAD_DIM), F32)], axis=0)
    sin = jnp.concatenate([sin, jnp.zeros((n_ctx, HEAD_DIM), F32)], axis=0)
    return jnp.tile(cos, (1, 2)), jnp.tile(sin, (1, 2))


def _pack_w_in(w_in_l):
    d = w_in_l.shape[0]
    zeros = jnp.zeros((d, HEAD_DIM), w_in_l.dtype)

    def widen(cols):
        return jnp.concatenate([cols[:, :HEAD_DIM], zeros, cols[:, HEAD_DIM:], zeros], axis=1)

    return jnp.concatenate([w_in_l[:, :256], widen(w_in_l[:, 256:384]), widen(w_in_l[:, 384:512]),
                            w_in_l[:, 512:]], axis=1).astype(BF16)


def _block_diag(w_group):
    g, ci, co = w_group.shape
    out = jnp.zeros((g * ci, g * co), w_group.dtype)
    for n in range(g):
        out = out.at[n * ci:(n + 1) * ci, n * co:(n + 1) * co].set(w_group[n])
    return out


def kernel(x, c, ctx, c_ctx, w_ada, b_ada, ffn1_norm, ffn1_w_gate, ffn1_w_up, ffn1_w_down, mix_norm, w_in, a_q_norm, a_k_norm, b_lb_logits, b_o_norm, c_w_group, c_scale, d_rel_bias, w_branch, w_out, ffn2_norm, ffn2_w_gate, ffn2_w_up, ffn2_w_down, final_norm):
    assert x.shape[0] == 1 and ctx.shape[0] == 1
    depth = w_ada.shape[0]
    t_lat = x.shape[1]
    n_ctx = ctx.shape[1]
    n = t_lat + n_ctx
    rows = t_lat // GRID_W
    assert t_lat % max(NA_QB, FLASH_TQ, n_ctx) == 0 and n_ctx % TOKEN_TILE == 0 and rows >= 2 * NA_QROWS

    x_all = None
    mods_all = _ada_mods(c, c_ctx, w_ada, b_ada)
    cos, sin = _rope_tables_padded(t_lat, n_ctx)
    segm = (jnp.kron(jnp.eye(BRANCH_WIDTH // HEAD_DIM), jnp.ones((HEAD_DIM, HEAD_DIM))) / HEAD_DIM).astype(BF16)
    lb_all = jnp.cumsum(jax.nn.softmax(b_lb_logits.astype(F32), axis=0), axis=0)
    lb_all = lb_all - lb_all[:1]

    for l in range(depth):
        with_ctx_out = l < depth - 1
        n_rows = n if with_ctx_out else t_lat
        mods = mods_all[l]
        x_all = _ffn_half(x[0] if l == 0 else x_all, mods, ffn1_norm[l][None], ffn1_w_gate[l].astype(BF16),
                          ffn1_w_up[l].astype(BF16), ffn1_w_down[l].astype(BF16), t_lat=t_lat, k0=0, n_rows=n,
                          x_ctx=ctx[0] if l == 0 else None)
        (aq, ak, av, bq, bff, bfb, bi, bg, cx, dq, dk, dv, gates) = _in_proj(
            x_all, mods, mix_norm[l][None], _pack_w_in(w_in[l]), cos, sin,
            jnp.tile(a_q_norm[l], A_HEADS)[None], jnp.tile(a_k_norm[l], 2 * A_KV_HEADS)[None], segm, t_lat=t_lat)
        if with_ctx_out:
            ya, yd = _ctx_attn(aq, ak, av, dq, dk, dv, t_lat=t_lat)
        else:
            ya = jnp.zeros((n, BRANCH_WIDTH), BF16)
            yd = ya
        ya = _gqa_latent(aq, ak, av, ya, t_lat=t_lat)
        na_bias, na_masks = _na_bias_tables(d_rel_bias[l], rows)
        yd = _na_latent(dq, dk, dv, na_bias, na_masks, yd, t_lat=t_lat)
        yc = _pool_mixer(cx, _block_diag(c_w_group[l]).astype(BF16), c_scale[l][None], t_lat=t_lat)
        o_fwd = _hgrn_dir(bq, bff, bi, lb_all[l, 0][None], t_lat=t_lat, rev=False)
        o_rev = _hgrn_dir(bq, bfb, bi, lb_all[l, 1][None], t_lat=t_lat, rev=True)
        x_all = _merge(x_all, mods, ya, o_fwd, o_rev, bg, jnp.tile(b_o_norm[l], B_HEADS)[None], segm, yc, yd, gates,
                       w_branch[l].astype(BF16), w_out[l].astype(BF16), t_lat=t_lat, n_rows=n_rows)
        x_all = _ffn_half(x_all, mods, ffn2_norm[l][None], ffn2_w_gate[l].astype(BF16), ffn2_w_up[l].astype(BF16),
                          ffn2_w_down[l].astype(BF16), t_lat=t_lat, k0=6, n_rows=n_rows)
    return _final_norm(x_all, final_norm[None], t_lat=t_lat)[None]
```

```python
import functools

import numpy as np
import jax
import jax.numpy as jnp
from jax import lax
from jax.experimental import pallas as pl
from jax.experimental.pallas import tpu as pltpu

F32 = jnp.float32
BF16 = jnp.bfloat16

D_MODEL = 1024
GRID_W = 64
HEAD_DIM = 64
BRANCH_WIDTH = 256
N_BRANCH = 4
A_HEADS = 4
A_KV_HEADS = 2
ROPE_THETA = 10000.0
B_HEADS = 4
B_KEY_DIM = 64
C_WINDOWS = (2, 4, 8, 16)
D_HEADS = 4
NA_WIN_R = 8
NA_WIN_C = 16
D_FF = 2816
N_MOD = 9
EPS = 1e-6
ATTN_SCALE = HEAD_DIM ** -0.5
LOG2_E = 1.4426950408889634
NEG_BIG = -1e30

V7X_VMEM_BYTES = 64 * 1024 * 1024
VMEM_LIMIT = V7X_VMEM_BYTES - 8 * 1024 * 1024

SEG_AQ = (0, 256)
SEG_AK = (256, 512)
SEG_AV = (512, 768)
SEG_BQ = (768, 1024)
SEG_BFF = (1024, 1280)
SEG_BFB = (1280, 1536)
SEG_BI = (1536, 1792)
SEG_BG = (1792, 2048)
SEG_CX = (2048, 2304)
SEG_DQ = (2304, 2560)
SEG_DK = (2560, 2816)
SEG_DV = (2816, 3072)
SEG_GATE = (3072, 3072 + N_BRANCH * D_MODEL)
W_ALL_WIDTH = SEG_GATE[1]

TOKEN_TILE = 256
FLASH_TQ = 2048
FLASH_QC = 512
FLASH_SB = 256
FLASH_NBUF = 4
NA_QROWS = 8
NA_QB = NA_QROWS * GRID_W
NA_HALO = 256
HG_BLOCK = 256
HG_CHUNK = 128
HG_SUB = 8


def _cparams(sem):
    return pltpu.CompilerParams(dimension_semantics=sem, vmem_limit_bytes=VMEM_LIMIT)


def _const_spec(shape):
    nd = len(shape)
    return pl.BlockSpec(shape, lambda *_: (0,) * nd, pipeline_mode=pl.Buffered(1))


def _sigmoid(x):
    return 1.0 / (1.0 + jnp.exp(-x))


def _silu(x):
    return x * _sigmoid(x)


def _dot(a, b):
    return jnp.dot(a, b, preferred_element_type=F32)


def _dot_nt(a, b):
    return lax.dot_general(a, b, (((1,), (1,)), ((), ())), preferred_element_type=F32)


def _dot_tn(a, b):
    return lax.dot_general(a, b, (((0,), (0,)), ((), ())), preferred_element_type=F32)


def _split_dot(x, m):
    hi = x.astype(BF16)
    r1 = x - hi.astype(F32)
    mid = r1.astype(BF16)
    lo = (r1 - mid.astype(F32)).astype(BF16)
    return _dot(hi, m) + _dot(mid, m) + _dot(lo, m)


def _split_dot_left(m, x):
    hi = x.astype(BF16)
    r1 = x - hi.astype(F32)
    mid = r1.astype(BF16)
    lo = (r1 - mid.astype(F32)).astype(BF16)
    return _dot(m, hi) + _dot(m, mid) + _dot(m, lo)


def _row_mod(mods_ref, k, is_lat):
    return jnp.where(is_lat, mods_ref[0, k:k + 1, :], mods_ref[1, k:k + 1, :])


def _rms_rows(x):
    return x * lax.rsqrt(jnp.mean(x * x, axis=-1, keepdims=True) + EPS)


def _ada_kernel(s_ref, w_ref, b_ref, o_ref):
    o_ref[0] = _dot(_silu(s_ref[...]).astype(BF16), w_ref[0].astype(BF16)) + b_ref[0]


def _ada_mods(c, c_ctx, w_ada, b_ada):
    depth = w_ada.shape[0]
    width = w_ada.shape[2]
    tn = 1024
    s = jnp.zeros((8, D_MODEL), F32).at[0].set(c[0]).at[1].set(c_ctx)
    out = pl.pallas_call(
        _ada_kernel,
        grid=(depth, width // tn),
        in_specs=[
            pl.BlockSpec((8, D_MODEL), lambda l, j: (0, 0)),
            pl.BlockSpec((1, D_MODEL, tn), lambda l, j: (l, 0, j)),
            pl.BlockSpec((1, 1, tn), lambda l, j: (l, 0, j)),
        ],
        out_specs=pl.BlockSpec((1, 8, tn), lambda l, j: (l, 0, j)),
        out_shape=jax.ShapeDtypeStruct((depth, 8, width), F32),
        compiler_params=_cparams(("parallel", "parallel")),
        name="ada_mods",
    )(s, w_ada, b_ada.reshape(depth, 1, width))
    return out[:, :2].reshape(depth, 2, N_MOD, D_MODEL)


def _ffn_kernel(*refs, t_lat, tm, k0, split_input, final_norm):
    fin_ref = None
    if split_input:
        xl_ref, xc_ref, mods_ref, nrm_ref, wg_ref, wu_ref, wd_ref, o_ref = refs
    elif final_norm:
        x_ref, mods_ref, nrm_ref, wg_ref, wu_ref, wd_ref, fin_ref, o_ref = refs
    else:
        x_ref, mods_ref, nrm_ref, wg_ref, wu_ref, wd_ref, o_ref = refs
    i = pl.program_id(0)
    row = i * tm + lax.broadcasted_iota(jnp.int32, (tm, 1), 0)
    is_lat = row < t_lat
    x = jnp.where(i * tm < t_lat, xl_ref[...], xc_ref[...]) if split_input else x_ref[...]
    z = _rms_rows(x) * nrm_ref[...]
    z = (z * (1.0 + _row_mod(mods_ref, k0 + 1, is_lat)) + _row_mod(mods_ref, k0, is_lat)).astype(BF16)
    g = _dot(z, wg_ref[...])
    u = _dot(z, wu_ref[...])
    a = (_silu(g) * u).astype(BF16)
    y = _dot(a, wd_ref[...])
    out = x + 0.5 * _row_mod(mods_ref, k0 + 2, is_lat) * y
    o_ref[...] = out if fin_ref is None else _rms_rows(out) * fin_ref[...]


def _ffn_half(x_all, mods, nrm, wg, wu, wd, *, t_lat, k0, n_rows, x_ctx=None, final_gain=None):
    split_input = x_ctx is not None
    final_norm = final_gain is not None
    d = x_all.shape[1]
    n = n_rows if final_norm else x_all.shape[0] + (x_ctx.shape[0] if split_input else 0)
    tm = TOKEN_TILE
    f = wg.shape[1]
    if split_input:
        n_lat_tiles = t_lat // tm
        x_specs = [pl.BlockSpec((tm, d), lambda i: (jnp.minimum(i, n_lat_tiles - 1), 0)),
                   pl.BlockSpec((tm, d), lambda i: (jnp.maximum(i - n_lat_tiles, 0), 0))]
        x_args = [x_all, x_ctx]
    else:
        x_specs = [pl.BlockSpec((tm, d), lambda i: (i, 0))]
        x_args = [x_all]
    extra_specs = [_const_spec((1, d))] if final_norm else []
    extra_args = [final_gain] if final_norm else []
    return pl.pallas_call(
        functools.partial(_ffn_kernel, t_lat=t_lat, tm=tm, k0=k0, split_input=split_input, final_norm=final_norm),
        grid=(n_rows // tm,),
        in_specs=x_specs + [
            _const_spec((2, N_MOD, d)),
            _const_spec((1, d)),
            _const_spec((d, f)),
            _const_spec((d, f)),
            _const_spec((f, d)),
        ] + extra_specs,
        out_specs=pl.BlockSpec((tm, d), lambda i: (i, 0)),
        out_shape=jax.ShapeDtypeStruct((n, d), F32),
        input_output_aliases={} if (split_input or final_norm) else {0: 0},
        compiler_params=_cparams(("parallel",)),
        name="ffn_half",
    )(*x_args, mods, nrm, wg, wu, wd, *extra_args)


def _head_norm(y, gain, segm):
    ms = _split_dot(y * y, segm)
    return y * lax.rsqrt(ms + EPS) * gain


def _rope(y, cos, sin_signed):
    w = y.shape[1]
    lane = lax.broadcasted_iota(jnp.int32, y.shape, 1)
    partner = jnp.where((lane & 31) < 16, pltpu.roll(y, w - 16, 1), pltpu.roll(y, 16, 1))
    return y * cos + partner * sin_signed


def _inproj_kernel(x_ref, mods_ref, nrm_ref, w_ref, cos_ref, sin_ref, gq_ref, gk_ref, segm_ref,
                   aq_ref, ak_ref, av_ref, bq_ref, bff_ref, bfb_ref, bi_ref, bg_ref, cx_ref,
                   dq_ref, dk_ref, dv_ref, gt_ref, *, t_lat, tm):
    i = pl.program_id(0)
    row = i * tm + lax.broadcasted_iota(jnp.int32, (tm, 1), 0)
    is_lat = row < t_lat
    z = _rms_rows(x_ref[...]) * nrm_ref[...]
    z = (z * (1.0 + _row_mod(mods_ref, 4, is_lat)) + _row_mod(mods_ref, 3, is_lat)).astype(BF16)

    def proj(seg):
        return _dot(z, w_ref[:, seg[0]:seg[1]])

    cos = jnp.concatenate([cos_ref[...]] * 2, axis=1)
    sin = jnp.concatenate([sin_ref[...]] * 2, axis=1)
    q = _rope(_head_norm(proj(SEG_AQ), gq_ref[...], segm_ref[...]), cos, sin)
    aq_ref[...] = (q * (ATTN_SCALE * LOG2_E)).T.astype(BF16)
    k = _rope(_head_norm(proj(SEG_AK), gk_ref[...], segm_ref[...]), cos, sin)
    ak_ref[...] = k.astype(BF16)
    lane = lax.broadcasted_iota(jnp.int32, (tm, SEG_AV[1] - SEG_AV[0]), 1)
    av_ref[...] = (proj(SEG_AV) + jnp.where((lane & 127) == HEAD_DIM, 1.0, 0.0)).T.astype(BF16)

    bq_ref[...] = (proj(SEG_BQ) * (B_KEY_DIM ** -0.5)).astype(BF16)
    bff_ref[...] = proj(SEG_BFF)
    bfb_ref[...] = proj(SEG_BFB)
    bi_ref[...] = proj(SEG_BI).astype(BF16)
    bg_ref[...] = proj(SEG_BG).astype(BF16)
    cx_ref[...] = proj(SEG_CX)
    dq_ref[...] = (proj(SEG_DQ) * ATTN_SCALE).astype(BF16)
    dk_ref[...] = proj(SEG_DK).astype(BF16)
    dv_ref[...] = proj(SEG_DV).astype(BF16)
    gw = 512
    for c0 in range(SEG_GATE[0], SEG_GATE[1], gw):
        o0 = c0 - SEG_GATE[0]
        gt_ref[:, o0:o0 + gw] = _sigmoid(proj((c0, c0 + gw))).astype(BF16)


def _in_proj(x_all, mods, nrm, w_all, cos, sin, gq, gk, segm, *, t_lat):
    n, d = x_all.shape
    tm = TOKEN_TILE
    widths = [(256, BF16), (256, BF16), (256, BF16), (256, BF16), (256, F32), (256, F32), (256, BF16),
              (256, BF16), (256, F32), (256, BF16), (256, BF16), (256, BF16), (N_BRANCH * D_MODEL, BF16)]
    transposed = (0, 2)
    out_specs = [pl.BlockSpec((w, tm), lambda i: (0, i)) if o in transposed else pl.BlockSpec((tm, w), lambda i: (i, 0))
                 for o, (w, _) in enumerate(widths)]
    out_shape = [jax.ShapeDtypeStruct((w, n) if o in transposed else (n, w), dt) for o, (w, dt) in enumerate(widths)]
    return pl.pallas_call(
        functools.partial(_inproj_kernel, t_lat=t_lat, tm=tm),
        grid=(n // tm,),
        in_specs=[
            pl.BlockSpec((tm, d), lambda i: (i, 0)),
            _const_spec((2, N_MOD, d)),
            _const_spec((1, d)),
            _const_spec((d, W_ALL_WIDTH)),
            pl.BlockSpec((tm, 128), lambda i: (i, 0)),
            pl.BlockSpec((tm, 128), lambda i: (i, 0)),
            _const_spec((1, 256)),
            _const_spec((1, 256)),
            _const_spec((256, 256)),
        ],
        out_specs=out_specs,
        out_shape=out_shape,
        compiler_params=_cparams(("parallel",)),
        name="in_proj",
    )(x_all, mods, nrm, w_all, cos, sin, gq, gk, segm)


def _softmax_attend(q, k, v, exp_fn):
    s = _dot_nt(q, k)
    p = exp_fn(s - jnp.max(s, axis=-1, keepdims=True))
    return _dot(p.astype(BF16), v) / jnp.sum(p, axis=-1, keepdims=True)


def _ctx_attn_kernel(aqt_ref, ak_ref, avt_ref, dq_ref, dk_ref, dv_ref, ya_ref, yd_ref):
    outs = []
    for h in range(A_HEADS):
        g = h // (A_HEADS // A_KV_HEADS)
        st = _dot(ak_ref[:, 128 * g:128 * g + 64], aqt_ref[64 * h:64 * h + 64, :])
        pt = jnp.exp2(st - jnp.max(st, axis=0, keepdims=True)).astype(BF16)
        acc = _dot(avt_ref[128 * g:128 * g + 128, :], pt)
        outs.append((acc[:HEAD_DIM] / acc[HEAD_DIM:HEAD_DIM + 1]).T)
    ya_ref[...] = jnp.concatenate(outs, axis=1).astype(BF16)
    outs = []
    for h in range(D_HEADS):
        hs = slice(64 * h, 64 * h + 64)
        outs.append(_softmax_attend(dq_ref[:, hs], dk_ref[:, hs], dv_ref[:, hs], jnp.exp))
    yd_ref[...] = jnp.concatenate(outs, axis=1).astype(BF16)


def _ctx_attn(aqt, ak, avt, dq, dk, dv, *, t_lat):
    n = ak.shape[0]
    n_ctx = n - t_lat
    blk = t_lat // n_ctx

    def spec(w):
        return pl.BlockSpec((n_ctx, w), lambda i: (blk, 0))

    def spec_t(w):
        return pl.BlockSpec((w, n_ctx), lambda i: (0, blk))

    return pl.pallas_call(
        _ctx_attn_kernel,
        grid=(1,),
        in_specs=[spec_t(256), spec(256), spec_t(256), spec(256), spec(256), spec(256)],
        out_specs=[spec(256), spec(256)],
        out_shape=[jax.ShapeDtypeStruct((n, BRANCH_WIDTH), BF16)] * 2,
        compiler_params=_cparams(("arbitrary",)),
        name="ctx_attn",
    )(aqt, ak, avt, dq, dk, dv)


def _flash_kernel(qt_ref, k_ref, vt_ref, prev_ref, o_ref, m_ref, acc_ref, *bufs, tk, tq):
    del prev_ref
    s_refs, p_refs = bufs[:FLASH_NBUF], bufs[FLASH_NBUF:]
    j = pl.program_id(1)

    @pl.when(j == 0)
    def _():
        m_ref[...] = jnp.full(m_ref.shape, -jnp.inf, F32)
        acc_ref[...] = jnp.zeros(acc_ref.shape, F32)

    group = A_HEADS // A_KV_HEADS
    n_qc = tq // FLASH_QC
    n_sb = tk // FLASH_SB
    chains = [(h, c) for h in range(A_HEADS) for c in range(n_qc)]

    def score_block(ci, sb):
        h, c = chains[ci]
        g = h // group
        rows = slice(sb * FLASH_SB, (sb + 1) * FLASH_SB)
        st = _dot(k_ref[rows, 128 * g:128 * g + 64], qt_ref[64 * h:64 * h + 64, c * FLASH_QC:(c + 1) * FLASH_QC])
        s_refs[ci % FLASH_NBUF][rows, :] = st
        return jnp.max(st, axis=0, keepdims=True)

    def exp_block(ci, sb, m_new):
        rows = slice(sb * FLASH_SB, (sb + 1) * FLASH_SB)
        p_refs[ci % 2][rows, :] = jnp.exp2(s_refs[ci % FLASH_NBUF][rows, :] - m_new).astype(BF16)

    def value_block(ci, sb):
        g = chains[ci][0] // group
        rows = slice(sb * FLASH_SB, (sb + 1) * FLASH_SB)
        return _dot(vt_ref[128 * g:128 * g + 128, rows], p_refs[ci % 2][rows, :])

    def fold(a, b):
        return b if a is None else a + b

    def fold_max(a, b):
        return b if a is None else jnp.maximum(a, b)

    n_ch = len(chains)
    blk_max, m_new, alpha = {}, {}, {}
    for stage in range(-1, n_ch + 1):
        c_s, c_e, c_v = stage + 1, stage, stage - 1
        if 0 <= c_e < n_ch:
            h, c = chains[c_e]
            cols = slice(c * FLASH_QC, (c + 1) * FLASH_QC)
            m_old = m_ref[h, :, cols]
            m_new[c_e] = jnp.maximum(m_old, blk_max.pop(c_e))
            m_ref[h, :, cols] = m_new[c_e]
            alpha[c_e] = jnp.exp2(m_old - m_new[c_e])
        part, mx = None, None
        for sb in range(n_sb):
            if c_s < n_ch:
                mx = fold_max(mx, score_block(c_s, sb))
            if 0 <= c_e < n_ch:
                exp_block(c_e, sb, m_new[c_e])
            if 0 <= c_v:
                part = fold(part, value_block(c_v, sb))
        if c_s < n_ch:
            blk_max[c_s] = mx
        if 0 <= c_v:
            h, c = chains[c_v]
            cols = slice(c * FLASH_QC, (c + 1) * FLASH_QC)
            acc_ref[h, :, cols] = alpha.pop(c_v) * acc_ref[h, :, cols] + part

    @pl.when(j == pl.num_programs(1) - 1)
    def _():
        outs = []
        for h in range(A_HEADS):
            acc = acc_ref[h]
            outs.append((acc[:HEAD_DIM] / acc[HEAD_DIM:HEAD_DIM + 1]).T)
        o_ref[...] = jnp.concatenate(outs, axis=1).astype(BF16)


def _flash_tk(n):
    for tk in (1280, 768, 512, 256):
        if n % tk == 0:
            return tk
    raise ValueError(f"unsupported key count {n}")


def _gqa_latent(aqt, ak, avt, ya_prev, *, t_lat):
    n = ak.shape[0]
    tq = FLASH_TQ
    tk = _flash_tk(n)
    return pl.pallas_call(
        functools.partial(_flash_kernel, tk=tk, tq=tq),
        grid=(t_lat // tq, n // tk),
        in_specs=[
            pl.BlockSpec((256, tq), lambda i, j: (0, i)),
            pl.BlockSpec((tk, 256), lambda i, j: (j, 0)),
            pl.BlockSpec((256, tk), lambda i, j: (0, j)),
            pl.BlockSpec(memory_space=pl.ANY),
        ],
        out_specs=pl.BlockSpec((tq, 256), lambda i, j: (i, 0)),
        out_shape=jax.ShapeDtypeStruct((n, BRANCH_WIDTH), BF16),
        scratch_shapes=[pltpu.VMEM((A_HEADS, 1, tq), F32),
                        pltpu.VMEM((A_HEADS, 128, tq), F32)]
                       + [pltpu.VMEM((tk, FLASH_QC), F32)] * FLASH_NBUF
                       + [pltpu.VMEM((tk, FLASH_QC), BF16)] * 2,
        input_output_aliases={3: 0},
        compiler_params=_cparams(("parallel", "arbitrary")),
        name="gqa_flash",
    )(aqt, ak, avt, ya_prev)


def _na_kernel(q_ref, kp_ref, km_ref, kn_ref, vp_ref, vm_ref, vn_ref, kc_ref, vc_ref, bias_ref, mask_ref, prev_ref,
               o_ref):
    del prev_ref
    outs = []
    for h in range(D_HEADS):
        hs = slice(64 * h, 64 * h + 64)
        q = q_ref[:, hs]
        kcat = jnp.concatenate([kp_ref[:, hs], km_ref[:, hs], kn_ref[:, hs]], axis=0)
        vcat = jnp.concatenate([vp_ref[:, hs], vm_ref[:, hs], vn_ref[:, hs]], axis=0)
        s_loc = _dot_nt(q, kcat) + (bias_ref[h] + mask_ref[0])
        s_ctx = _dot_nt(q, kc_ref[:, hs])
        m = jnp.maximum(jnp.max(s_loc, axis=-1, keepdims=True), jnp.max(s_ctx, axis=-1, keepdims=True))
        p_loc = jnp.exp(s_loc - m)
        p_ctx = jnp.exp(s_ctx - m)
        denom = jnp.sum(p_loc, axis=-1, keepdims=True) + jnp.sum(p_ctx, axis=-1, keepdims=True)
        o = _dot(p_loc.astype(BF16), vcat) + _dot(p_ctx.astype(BF16), vc_ref[:, hs])
        outs.append(o / denom)
    o_ref[...] = jnp.concatenate(outs, axis=1).astype(BF16)


def _na_bias_tables(rel_bias, rows):
    wr = min(NA_WIN_R, rows)
    halo_rows = NA_HALO // GRID_W
    krows = NA_QROWS + 2 * halo_rows
    nb = rows // NA_QROWS
    qc = np.arange(GRID_W)[:, None]
    kc = np.arange(GRID_W)[None, :]
    cs = np.clip(qc - NA_WIN_C // 2, 0, GRID_W - NA_WIN_C)
    in_col = (kc >= cs) & (kc < cs + NA_WIN_C)
    e_col = (kc - qc + (NA_WIN_C - 1))[:, :, None] == np.arange(2 * NA_WIN_C - 1)
    qr_l = np.arange(NA_QROWS)[:, None]
    kr_l = np.arange(krows)[None, :]
    e_row = (kr_l - halo_rows - qr_l + (NA_WIN_R - 1))[:, :, None] == np.arange(2 * NA_WIN_R - 1)
    hi = lax.Precision.HIGHEST
    tmp = jnp.einsum("hrc,qkr->hqkc", rel_bias.astype(F32), e_row.astype(np.float32), precision=hi)
    full = jnp.einsum("hqkc,pjc->hqpkj", tmp, e_col.astype(np.float32), precision=hi)
    masks = []
    for b in (0, min(1, nb - 1), nb - 1):
        qr = NA_QROWS * b + qr_l
        kr = NA_QROWS * b - halo_rows + kr_l
        rs = np.clip(qr - wr // 2, 0, rows - wr)
        in_row = (kr >= rs) & (kr < rs + wr)
        mask = in_row[:, None, :, None] & in_col[None, :, None, :]
        masks.append(np.where(mask, 0.0, NEG_BIG).reshape(NA_QB, krows * GRID_W))
    return full.reshape(D_HEADS, NA_QB, krows * GRID_W), np.stack(masks).astype(np.float32)


def _na_latent(dq, dk, dv, bias, masks, yd_prev, *, t_lat):
    n = dq.shape[0]
    nb = t_lat // NA_QB
    r = NA_QB // NA_HALO
    last_halo = t_lat // NA_HALO - 1
    ctx_blk = t_lat // (n - t_lat)
    n_ctx = n - t_lat

    def prev_map(b):
        return (jnp.maximum(r * b - 1, 0), 0)

    def next_map(b):
        return (jnp.minimum(r * b + r, last_halo), 0)

    def variant(b):
        return (jnp.where(b == 0, 0, jnp.where(b == nb - 1, 2, 1)), 0, 0)

    main = pl.BlockSpec((NA_QB, 256), lambda b: (b, 0))
    prev = pl.BlockSpec((NA_HALO, 256), prev_map)
    nxt = pl.BlockSpec((NA_HALO, 256), next_map)
    ctx = pl.BlockSpec((n_ctx, 256), lambda b: (ctx_blk, 0))
    return pl.pallas_call(
        _na_kernel,
        grid=(nb,),
        in_specs=[main, prev, main, nxt, prev, main, nxt, ctx, ctx,
                  _const_spec((D_HEADS, NA_QB, NA_QB + 2 * NA_HALO)),
                  pl.BlockSpec((1, NA_QB, NA_QB + 2 * NA_HALO), variant),
                  pl.BlockSpec(memory_space=pl.ANY)],
        out_specs=main,
        out_shape=jax.ShapeDtypeStruct((n, BRANCH_WIDTH), BF16),
        input_output_aliases={11: 0},
        compiler_params=_cparams(("parallel",)),
        name="na_attn",
    )(dq, dk, dk, dk, dv, dv, dv, dk, dv, bias, masks, yd_prev)


def _pool_kernel(x_ref, xp_ref, xn_ref, w_ref, sc_ref, o_ref, cat_ref, a_ref, b_ref, *, t_lat, n_all, tm):
    i = pl.program_id(0)
    start = i * tm
    in_lat = start < t_lat
    seg_lo = jnp.where(in_lat, 0, t_lat)
    seg_hi = jnp.where(in_lat, t_lat, n_all)
    x = x_ref[...]
    cat_ref[0:8, :] = jnp.where(start > seg_lo, xp_ref[...], 0.0)
    cat_ref[8:8 + tm, :] = x
    cat_ref[8 + tm:16 + tm, :] = jnp.where(start + tm < seg_hi, xn_ref[...], 0.0)
    cat_ref[16 + tm:32 + tm, :] = jnp.zeros((16, BRANCH_WIDTH), F32)
    a_ref[0:tm + 24, :] = cat_ref[0:tm + 24, :] + cat_ref[1:tm + 25, :]
    s2 = a_ref[7:7 + tm, :]
    b_ref[0:tm + 16, :] = a_ref[0:tm + 16, :] + a_ref[2:tm + 18, :]
    s4 = b_ref[6:6 + tm, :]
    a_ref[0:tm + 8, :] = b_ref[0:tm + 8, :] + b_ref[4:tm + 12, :]
    s8 = a_ref[4:4 + tm, :]
    s16 = a_ref[0:tm, :] + a_ref[8:8 + tm, :]

    pos = start - seg_lo + lax.broadcasted_iota(jnp.int32, (tm, 1), 0)
    seg_len = seg_hi - seg_lo

    def mean(sm, w):
        lo = jnp.clip(pos - w // 2, 0, seg_len)
        hi = jnp.clip(pos - w // 2 + w, 0, seg_len)
        return sm / (hi - lo).astype(F32)

    lane = lax.broadcasted_iota(jnp.int32, (tm, BRANCH_WIDTH), 1)
    gw = BRANCH_WIDTH // len(C_WINDOWS)
    pooled = jnp.where(lane < gw, mean(s2, 2),
                       jnp.where(lane < 2 * gw, mean(s4, 4),
                                 jnp.where(lane < 3 * gw, mean(s8, 8), mean(s16, 16)))) - x
    o_ref[...] = (_dot(pooled.astype(BF16), w_ref[...]) * sc_ref[...]).astype(BF16)


def _pool_mixer(cx, w_bd, scale, *, t_lat):
    n = cx.shape[0]
    tm = TOKEN_TILE
    r = tm // 8
    last8 = n // 8 - 1
    return pl.pallas_call(
        functools.partial(_pool_kernel, t_lat=t_lat, n_all=n, tm=tm),
        grid=(n // tm,),
        in_specs=[
            pl.BlockSpec((tm, 256), lambda i: (i, 0)),
            pl.BlockSpec((8, 256), lambda i: (jnp.maximum(r * i - 1, 0), 0)),
            pl.BlockSpec((8, 256), lambda i: (jnp.minimum(r * i + r, last8), 0)),
            _const_spec((256, 256)),
            _const_spec((1, 256)),
        ],
        out_specs=pl.BlockSpec((tm, 256), lambda i: (i, 0)),
        out_shape=jax.ShapeDtypeStruct((n, BRANCH_WIDTH), BF16),
        scratch_shapes=[pltpu.VMEM((tm + 32, 256), F32)] * 3,
        compiler_params=_cparams(("parallel",)),
        name="pool_mixer",
    )(cx, cx, cx, w_bd, scale)


def _hgrn_tables(rev):
    c_len, s, w = HG_CHUNK, HG_SUB, B_HEADS * B_KEY_DIM
    t = np.arange(c_len)
    tri = (t[None, :] >= t[:, None]) if rev else (t[None, :] <= t[:, None])
    pos = np.arange(s)
    keep = (pos[None, :] <= pos[:, None]) if rev else (pos[None, :] >= pos[:, None])
    keep_add = np.where(keep, 0.0, NEG_BIG)[:, :, None] * np.ones((1, 1, w))
    head_of_lane = np.arange(w) // B_KEY_DIM
    col = np.arange(B_HEADS * s)
    sel = (col[None, None, :] == (head_of_lane[None, :, None] * s + pos[:, None, None]))
    bd = (col[:, None] // s) == head_of_lane[None, :]
    hh = head_of_lane[:, None] == head_of_lane[None, :]
    return (jnp.asarray(tri, BF16), jnp.asarray(keep_add, F32), jnp.asarray(sel, BF16),
            jnp.asarray(bd, F32), jnp.asarray(hh, F32))


def _hgrn_kernel(q_ref, f_ref, v_ref, lb_ref, tri_ref, keep_ref, sel_ref, bd_ref, hh_ref, o_ref, st_ref, *, rev):
    @pl.when(pl.program_id(0) == 0)
    def _():
        st_ref[...] = jnp.zeros(st_ref.shape, F32)

    st = st_ref[...]
    order = range(HG_BLOCK // HG_CHUNK)
    for ch in (reversed(order) if rev else order):
        rows = slice(ch * HG_CHUNK, (ch + 1) * HG_CHUNK)
        o, st = _hgrn_chunk(q_ref[rows, :], f_ref[rows, :], v_ref[rows, :], st, lb_ref, tri_ref, keep_ref,
                            sel_ref, bd_ref, hh_ref, rev=rev)
        o_ref[rows, :] = o
    st_ref[...] = st


def _hgrn_chunk(q_bf, f_pre, v, st, lb_ref, tri_ref, keep_ref, sel_ref, bd_ref, hh_ref, *, rev):
    c_len = HG_CHUNK
    n_sub = c_len // HG_SUB
    w = B_HEADS * B_KEY_DIM
    lb = lb_ref[...]
    f = lb + (1.0 - lb) * _sigmoid(f_pre)
    k = 1.0 - f
    lf = jnp.log(f)
    q = q_bf.astype(F32)
    bd = bd_ref[...]

    c = _split_dot_left(tri_ref[...], lf)

    q3 = q.reshape(n_sub, HG_SUB, w)
    k3 = k.reshape(n_sub, HG_SUB, w)
    c3 = c.reshape(n_sub, HG_SUB, w)
    a_diag = None
    for sg in range(HG_SUB):
        ks = jnp.broadcast_to(k3[:, sg:sg + 1, :], k3.shape)
        cs = jnp.broadcast_to(c3[:, sg:sg + 1, :], c3.shape)
        wgt = (q3 * ks) * jnp.exp((c3 - cs) + keep_ref[sg][None])
        part = _dot(wgt.reshape(c_len, w).astype(BF16), sel_ref[sg])
        a_diag = part if a_diag is None else a_diag + part

    o_parts = [None] * n_sub
    for j in range(n_sub):
        r0 = j * HG_SUB
        r_last = r0 if rev else r0 + HG_SUB - 1
        e_j = c[r_last:r_last + 1, :]
        kj = k[r0:r0 + HG_SUB, :] * jnp.exp(e_j - c[r0:r0 + HG_SUB, :])
        kbd = (jnp.concatenate([kj] * B_HEADS, axis=0) * bd).astype(BF16)
        vbd = (jnp.concatenate([v[r0:r0 + HG_SUB, :].astype(F32)] * B_HEADS, axis=0) * bd).astype(BF16)
        lo, hi = (0, r0) if rev else (r0 + HG_SUB, c_len)
        pieces = [a_diag[r0:r0 + HG_SUB]]
        if hi > lo:
            qj = (q[lo:hi] * jnp.exp(c[lo:hi] - e_j)).astype(BF16)
            a_off = _dot_nt(qj, kbd)
            pieces = [a_off] + pieces if rev else pieces + [a_off]
        first = 0 if rev else j
        if (sum(p.shape[0] for p in pieces) % 16) != 0:
            pad = jnp.zeros((HG_SUB, a_diag.shape[1]), F32)
            pieces = pieces + [pad] if rev else [pad] + pieces
            first = first if rev else first - 1
        a_j = jnp.concatenate(pieces, axis=0).astype(BF16)
        contrib = _dot(a_j, vbd)
        for i in range(contrib.shape[0] // HG_SUB):
            piece = contrib[i * HG_SUB:(i + 1) * HG_SUB]
            o_parts[first + i] = piece if o_parts[first + i] is None else o_parts[first + i] + piece
    o = jnp.concatenate(o_parts, axis=0)

    o = o + _dot_nt((q * jnp.exp(c)).astype(BF16), st.astype(BF16))
    r_end = 0 if rev else c_len - 1
    c_end = c[r_end:r_end + 1, :]
    k_end = (k * jnp.exp(c_end - c)).astype(BF16)
    return o, jnp.exp(c_end) * st + _dot_tn(v, k_end) * hh_ref[...]


def _hgrn_dir(bq, f_pre, bi, lb, *, t_lat, rev):
    n = bq.shape[0]
    c_len = HG_BLOCK
    n_lat = t_lat // c_len
    n_all = n // c_len
    n_ctx = n_all - n_lat

    if rev:
        def blk(i):
            return (jnp.where(i < n_ctx, n_all - 1 - i, n_lat - 1 - (i - n_ctx)), 0)
    else:
        def blk(i):
            return (jnp.where(i < n_ctx, n_lat + i, i - n_ctx), 0)

    tile = pl.BlockSpec((c_len, 256), blk)
    tables = _hgrn_tables(rev)
    in_specs = [tile, tile, tile, _const_spec((1, 256))] + [_const_spec(t.shape) for t in tables]
    args = [bq, f_pre, bi, lb, *tables]
    return pl.pallas_call(
        functools.partial(_hgrn_kernel, rev=rev),
        grid=(n_all,),
        in_specs=in_specs,
        out_specs=tile,
        out_shape=jax.ShapeDtypeStruct((n, BRANCH_WIDTH), F32),
        scratch_shapes=[pltpu.VMEM((256, 256), F32)],
        compiler_params=_cparams(("arbitrary",)),
        name="hgrn_rev" if rev else "hgrn_fwd",
    )(*args)


def _merge_kernel(x_ref, mods_ref, ya_ref, of_ref, or_ref, bg_ref, gain_ref, segm_ref, yc_ref, yd_ref, gt_ref,
                  wb_ref, wo_ref, o_ref, *, t_lat, tm):
    i = pl.program_id(0)
    row = i * tm + lax.broadcasted_iota(jnp.int32, (tm, 1), 0)
    is_lat = row < t_lat
    yb = _head_norm(of_ref[...] + or_ref[...], gain_ref[...], segm_ref[...]) * _silu(bg_ref[...].astype(F32))
    branches = (ya_ref[...], yb.astype(BF16), yc_ref[...], yd_ref[...])
    merged = None
    for n, y in enumerate(branches):
        term = gt_ref[:, n * D_MODEL:(n + 1) * D_MODEL].astype(F32) * _dot(y, wb_ref[n])
        merged = term if merged is None else merged + term
    y = _dot(merged.astype(BF16), wo_ref[...])
    o_ref[...] = x_ref[...] + _row_mod(mods_ref, 5, is_lat) * y


def _merge(x_all, mods, ya, o_fwd, o_rev, bg, b_gain, segm, yc, yd, gates, wb, wo, *, t_lat, n_rows):
    n, d = x_all.shape
    tm = TOKEN_TILE

    def tile(w):
        return pl.BlockSpec((tm, w), lambda i: (i, 0))

    return pl.pallas_call(
        functools.partial(_merge_kernel, t_lat=t_lat, tm=tm),
        grid=(n_rows // tm,),
        in_specs=[tile(d), _const_spec((2, N_MOD, d)), tile(256), tile(256), tile(256), tile(256),
                  _const_spec((1, 256)), _const_spec((256, 256)), tile(256), tile(256),
                  tile(N_BRANCH * d), _const_spec((N_BRANCH, BRANCH_WIDTH, d)), _const_spec((d, d))],
        out_specs=tile(d),
        out_shape=jax.ShapeDtypeStruct((n, d), F32),
        input_output_aliases={0: 0},
        compiler_params=_cparams(("parallel",)),
        name="merge_out",
    )(x_all, mods, ya, o_fwd, o_rev, bg, b_gain, segm, yc, yd, gates, wb, wo)


def _rope_tables_padded(t_lat, n_ctx):
    rows = t_lat // GRID_W
    half = HEAD_DIM // 2
    nf = half // 2
    inv = 1.0 / (ROPE_THETA ** (jnp.arange(0, half, 2, dtype=F32) / half))
    ang_r = jnp.arange(rows, dtype=F32)[:, None] * inv
    ang_c = jnp.arange(GRID_W, dtype=F32)[:, None] * inv

    def over_rows(a):
        return jnp.broadcast_to(a[:, None, :], (rows, GRID_W, nf))

    def over_cols(a):
        return jnp.broadcast_to(a[None, :, :], (rows, GRID_W, nf))

    cos = jnp.concatenate([over_rows(jnp.cos(ang_r))] * 2 + [over_cols(jnp.cos(ang_c))] * 2, axis=2)
    sin = jnp.concatenate([over_rows(-jnp.sin(ang_r)), over_rows(jnp.sin(ang_r)),
                           over_cols(-jnp.sin(ang_c)), over_cols(jnp.sin(ang_c))], axis=2)
    cos = cos.reshape(t_lat, HEAD_DIM)
    sin = sin.reshape(t_lat, HEAD_DIM)
    cos = jnp.concatenate([cos, jnp.ones((n_ctx, HEAD_DIM), F32)], axis=0)
    sin = jnp.concatenate([sin, jnp.zeros((n_ctx, HEAD_DIM), F32)], axis=0)
    return jnp.tile(cos, (1, 2)), jnp.tile(sin, (1, 2))


def _pack_w_in(w_in_l):
    d = w_in_l.shape[0]
    zeros = jnp.zeros((d, HEAD_DIM), w_in_l.dtype)

    def widen(cols):
        return jnp.concatenate([cols[:, :HEAD_DIM], zeros, cols[:, HEAD_DIM:], zeros], axis=1)

    return jnp.concatenate([w_in_l[:, :256], widen(w_in_l[:, 256:384]), widen(w_in_l[:, 384:512]),
                            w_in_l[:, 512:]], axis=1).astype(BF16)


def _block_diag(w_group):
    g, ci, co = w_group.shape
    out = jnp.zeros((g * ci, g * co), w_group.dtype)
    for n in range(g):
        out = out.at[n * ci:(n + 1) * ci, n * co:(n + 1) * co].set(w_group[n])
    return out


def kernel(x, c, ctx, c_ctx, w_ada, b_ada, ffn1_norm, ffn1_w_gate, ffn1_w_up, ffn1_w_down, mix_norm, w_in, a_q_norm, a_k_norm, b_lb_logits, b_o_norm, c_w_group, c_scale, d_rel_bias, w_branch, w_out, ffn2_norm, ffn2_w_gate, ffn2_w_up, ffn2_w_down, final_norm):
    assert x.shape[0] == 1 and ctx.shape[0] == 1
    depth = w_ada.shape[0]
    t_lat = x.shape[1]
    n_ctx = ctx.shape[1]
    n = t_lat + n_ctx
    rows = t_lat // GRID_W
    assert t_lat % max(NA_QB, FLASH_TQ, n_ctx) == 0 and n_ctx % TOKEN_TILE == 0 and rows >= 2 * NA_QROWS

    x_all = None
    mods_all = _ada_mods(c, c_ctx, w_ada, b_ada)
    cos, sin = _rope_tables_padded(t_lat, n_ctx)
    segm = (jnp.kron(jnp.eye(BRANCH_WIDTH // HEAD_DIM), jnp.ones((HEAD_DIM, HEAD_DIM))) / HEAD_DIM).astype(BF16)
    lb_all = jnp.cumsum(jax.nn.softmax(b_lb_logits.astype(F32), axis=0), axis=0)
    lb_all = lb_all - lb_all[:1]

    for l in range(depth):
        with_ctx_out = l < depth - 1
        n_rows = n if with_ctx_out else t_lat
        mods = mods_all[l]
        x_all = _ffn_half(x[0] if l == 0 else x_all, mods, ffn1_norm[l][None], ffn1_w_gate[l].astype(BF16),
                          ffn1_w_up[l].astype(BF16), ffn1_w_down[l].astype(BF16), t_lat=t_lat, k0=0, n_rows=n,
                          x_ctx=ctx[0] if l == 0 else None)
        (aq, ak, av, bq, bff, bfb, bi, bg, cx, dq, dk, dv, gates) = _in_proj(
            x_all, mods, mix_norm[l][None], _pack_w_in(w_in[l]), cos, sin,
            jnp.tile(a_q_norm[l], A_HEADS)[None], jnp.tile(a_k_norm[l], 2 * A_KV_HEADS)[None], segm, t_lat=t_lat)
        if with_ctx_out:
            ya, yd = _ctx_attn(aq, ak, av, dq, dk, dv, t_lat=t_lat)
        else:
            ya = jnp.zeros((n, BRANCH_WIDTH), BF16)
            yd = ya
        ya = _gqa_latent(aq, ak, av, ya, t_lat=t_lat)
        na_bias, na_masks = _na_bias_tables(d_rel_bias[l], rows)
        yd = _na_latent(dq, dk, dv, na_bias, na_masks, yd, t_lat=t_lat)
        yc = _pool_mixer(cx, _block_diag(c_w_group[l]).astype(BF16), c_scale[l][None], t_lat=t_lat)
        o_fwd = _hgrn_dir(bq, bff, bi, lb_all[l, 0][None], t_lat=t_lat, rev=False)
        o_rev = _hgrn_dir(bq, bfb, bi, lb_all[l, 1][None], t_lat=t_lat, rev=True)
        x_all = _merge(x_all, mods, ya, o_fwd, o_rev, bg, jnp.tile(b_o_norm[l], B_HEADS)[None], segm, yc, yd, gates,
                       w_branch[l].astype(BF16), w_out[l].astype(BF16), t_lat=t_lat, n_rows=n_rows)
        x_all = _ffn_half(x_all, mods, ffn2_norm[l][None], ffn2_w_gate[l].astype(BF16), ffn2_w_up[l].astype(BF16),
                          ffn2_w_down[l].astype(BF16), t_lat=t_lat, k0=6, n_rows=n_rows,
                          final_gain=None if with_ctx_out else final_norm[None])
    return x_all[None]
```

```python
import functools

import numpy as np
import jax
import jax.numpy as jnp
from jax import lax
from jax.experimental import pallas as pl
from jax.experimental.pallas import tpu as pltpu

F32 = jnp.float32
BF16 = jnp.bfloat16

D_MODEL = 1024
GRID_W = 64
HEAD_DIM = 64
BRANCH_WIDTH = 256
N_BRANCH = 4
A_HEADS = 4
A_KV_HEADS = 2
ROPE_THETA = 10000.0
B_HEADS = 4
B_KEY_DIM = 64
C_WINDOWS = (2, 4, 8, 16)
D_HEADS = 4
NA_WIN_R = 8
NA_WIN_C = 16
D_FF = 2816
N_MOD = 9
EPS = 1e-6
ATTN_SCALE = HEAD_DIM ** -0.5
LOG2_E = 1.4426950408889634
NEG_BIG = -1e30

V7X_VMEM_BYTES = 64 * 1024 * 1024
VMEM_LIMIT = V7X_VMEM_BYTES - 8 * 1024 * 1024

SEG_AQ = (0, 256)
SEG_AK = (256, 384)
SEG_AV = (384, 512)
SEG_BQ = (512, 768)
SEG_BFF = (768, 1024)
SEG_BFB = (1024, 1280)
SEG_BI = (1280, 1536)
SEG_BG = (1536, 1792)
SEG_CX = (1792, 2048)
SEG_DQ = (2048, 2304)
SEG_DK = (2304, 2560)
SEG_DV = (2560, 2816)
SEG_GATE = (2816, 2816 + N_BRANCH * D_MODEL)
IN_WIDTH = SEG_GATE[1]

TOKEN_TILE = 256
FLASH_TQ = 2048
FLASH_QC = 512
FLASH_SB = 256
FLASH_NBUF = 4
NA_QROWS = 8
NA_QB = NA_QROWS * GRID_W
NA_HALO = 256
HG_BLOCK = 256
HG_CHUNK = 128
HG_SUB = 8


def _cparams(sem):
    return pltpu.CompilerParams(dimension_semantics=sem, vmem_limit_bytes=VMEM_LIMIT)


def _const_spec(shape):
    nd = len(shape)
    return pl.BlockSpec(shape, lambda *_: (0,) * nd, pipeline_mode=pl.Buffered(1))


def _layer_spec(shape, layer):
    nd = len(shape)
    return pl.BlockSpec((None,) + tuple(shape[1:]), lambda *_: (layer,) + (0,) * (nd - 1),
                        pipeline_mode=pl.Buffered(1))


def _sigmoid(x):
    return 1.0 / (1.0 + jnp.exp(-x))


def _silu(x):
    return x * _sigmoid(x)


def _dot(a, b):
    return jnp.dot(a, b, preferred_element_type=F32)


def _dot_nt(a, b):
    return lax.dot_general(a, b, (((1,), (1,)), ((), ())), preferred_element_type=F32)


def _dot_tn(a, b):
    return lax.dot_general(a, b, (((0,), (0,)), ((), ())), preferred_element_type=F32)


def _split_dot(x, m):
    hi = x.astype(BF16)
    r1 = x - hi.astype(F32)
    mid = r1.astype(BF16)
    lo = (r1 - mid.astype(F32)).astype(BF16)
    return _dot(hi, m) + _dot(mid, m) + _dot(lo, m)


def _split_dot_left(m, x):
    hi = x.astype(BF16)
    r1 = x - hi.astype(F32)
    mid = r1.astype(BF16)
    lo = (r1 - mid.astype(F32)).astype(BF16)
    return _dot(m, hi) + _dot(m, mid) + _dot(m, lo)


def _row_mod(mods_ref, k, is_lat):
    return jnp.where(is_lat, mods_ref[0, k:k + 1, :], mods_ref[1, k:k + 1, :])


def _rms_rows(x):
    return x * lax.rsqrt(jnp.mean(x * x, axis=-1, keepdims=True) + EPS)


def _ada_kernel(s_ref, w_ref, b_ref, o_ref):
    o_ref[0] = _dot(_silu(s_ref[...]).astype(BF16), w_ref[0].astype(BF16)) + b_ref[0]


def _ada_mods(c, c_ctx, w_ada, b_ada):
    depth = w_ada.shape[0]
    width = w_ada.shape[2]
    tn = 1024
    s = jnp.zeros((8, D_MODEL), F32).at[0].set(c[0]).at[1].set(c_ctx)
    out = pl.pallas_call(
        _ada_kernel,
        grid=(depth, width // tn),
        in_specs=[
            pl.BlockSpec((8, D_MODEL), lambda l, j: (0, 0)),
            pl.BlockSpec((1, D_MODEL, tn), lambda l, j: (l, 0, j)),
            pl.BlockSpec((1, 1, tn), lambda l, j: (l, 0, j)),
        ],
        out_specs=pl.BlockSpec((1, 8, tn), lambda l, j: (l, 0, j)),
        out_shape=jax.ShapeDtypeStruct((depth, 8, width), F32),
        compiler_params=_cparams(("parallel", "parallel")),
        name="ada_mods",
    )(s, w_ada, b_ada.reshape(depth, 1, width))
    return out[:, :2].reshape(depth, 2, N_MOD, D_MODEL)


def _ffn_kernel(*refs, t_lat, tm, k0, split_input, final_norm):
    fin_ref = None
    if split_input:
        xl_ref, xc_ref, mods_ref, nrm_ref, wg_ref, wu_ref, wd_ref, o_ref = refs
    elif final_norm:
        x_ref, mods_ref, nrm_ref, wg_ref, wu_ref, wd_ref, fin_ref, o_ref = refs
    else:
        x_ref, mods_ref, nrm_ref, wg_ref, wu_ref, wd_ref, o_ref = refs
    i = pl.program_id(0)
    row = i * tm + lax.broadcasted_iota(jnp.int32, (tm, 1), 0)
    is_lat = row < t_lat
    x = jnp.where(i * tm < t_lat, xl_ref[...], xc_ref[...]) if split_input else x_ref[...]
    z = _rms_rows(x) * nrm_ref[...]
    z = (z * (1.0 + _row_mod(mods_ref, k0 + 1, is_lat)) + _row_mod(mods_ref, k0, is_lat)).astype(BF16)
    g = _dot(z, wg_ref[...])
    u = _dot(z, wu_ref[...])
    a = (_silu(g) * u).astype(BF16)
    y = _dot(a, wd_ref[...])
    out = x + 0.5 * _row_mod(mods_ref, k0 + 2, is_lat) * y
    o_ref[...] = out if fin_ref is None else _rms_rows(out) * fin_ref[...]


def _ffn_half(x_all, mods, nrm, wg, wu, wd, *, layer, t_lat, k0, n_rows, x_ctx=None, final_gain=None):
    split_input = x_ctx is not None
    final_norm = final_gain is not None
    d = x_all.shape[1]
    n = n_rows if final_norm else x_all.shape[0] + (x_ctx.shape[0] if split_input else 0)
    tm = TOKEN_TILE
    if split_input:
        n_lat_tiles = t_lat // tm
        x_specs = [pl.BlockSpec((tm, d), lambda i: (jnp.minimum(i, n_lat_tiles - 1), 0)),
                   pl.BlockSpec((tm, d), lambda i: (jnp.maximum(i - n_lat_tiles, 0), 0))]
        x_args = [x_all, x_ctx]
    else:
        x_specs = [pl.BlockSpec((tm, d), lambda i: (i, 0))]
        x_args = [x_all]
    extra_specs = [_const_spec((1, d))] if final_norm else []
    extra_args = [final_gain] if final_norm else []
    return pl.pallas_call(
        functools.partial(_ffn_kernel, t_lat=t_lat, tm=tm, k0=k0, split_input=split_input, final_norm=final_norm),
        grid=(n_rows // tm,),
        in_specs=x_specs + [
            _const_spec((2, N_MOD, d)),
            _const_spec((1, d)),
            _layer_spec(wg.shape, layer),
            _layer_spec(wu.shape, layer),
            _layer_spec(wd.shape, layer),
        ] + extra_specs,
        out_specs=pl.BlockSpec((tm, d), lambda i: (i, 0)),
        out_shape=jax.ShapeDtypeStruct((n, d), F32),
        input_output_aliases={} if (split_input or final_norm) else {0: 0},
        compiler_params=_cparams(("parallel",)),
        name="ffn_half",
    )(*x_args, mods, nrm, wg, wu, wd, *extra_args)


def _head_norm(y, gain, segm):
    ms = _split_dot(y * y, segm)
    return y * lax.rsqrt(ms + EPS) * gain


def _rope(y, cos, sin_signed):
    w = y.shape[1]
    lane = lax.broadcasted_iota(jnp.int32, y.shape, 1)
    partner = jnp.where((lane & 31) < 16, pltpu.roll(y, w - 16, 1), pltpu.roll(y, 16, 1))
    return y * cos + partner * sin_signed


def _inproj_kernel(x_ref, mods_ref, nrm_ref, w_ref, cos_ref, sin_ref, gq_ref, gk_ref, segm_ref,
                   aq_ref, ak_ref, av_ref, bq_ref, bff_ref, bfb_ref, bi_ref, bg_ref, cx_ref,
                   dq_ref, dk_ref, dv_ref, gt_ref, *, t_lat, tm):
    i = pl.program_id(0)
    row = i * tm + lax.broadcasted_iota(jnp.int32, (tm, 1), 0)
    is_lat = row < t_lat
    z = _rms_rows(x_ref[...]) * nrm_ref[...]
    z = (z * (1.0 + _row_mod(mods_ref, 4, is_lat)) + _row_mod(mods_ref, 3, is_lat)).astype(BF16)

    def proj(seg):
        return _dot(z, w_ref[:, seg[0]:seg[1]])

    cos = cos_ref[...]
    sin = sin_ref[...]
    q = _rope(_head_norm(proj(SEG_AQ), gq_ref[...], segm_ref[...]),
              jnp.concatenate([cos, cos], axis=1), jnp.concatenate([sin, sin], axis=1))
    aq_ref[...] = (q * (ATTN_SCALE * LOG2_E)).T.astype(BF16)
    k = _rope(_head_norm(proj(SEG_AK), gk_ref[...], segm_ref[0:128, 0:128]), cos, sin)
    zeros = jnp.zeros((tm, HEAD_DIM), F32)
    ak_ref[...] = jnp.concatenate([k[:, :HEAD_DIM], zeros, k[:, HEAD_DIM:], zeros], axis=1).astype(BF16)
    v = proj(SEG_AV)
    one_hot = jnp.where(lax.broadcasted_iota(jnp.int32, (tm, HEAD_DIM), 1) == 0, 1.0, 0.0)
    av_ref[...] = jnp.concatenate([v[:, :HEAD_DIM], one_hot, v[:, HEAD_DIM:], one_hot], axis=1).T.astype(BF16)

    bq_ref[...] = (proj(SEG_BQ) * (B_KEY_DIM ** -0.5)).astype(BF16)
    bff_ref[...] = proj(SEG_BFF)
    bfb_ref[...] = proj(SEG_BFB)
    bi_ref[...] = proj(SEG_BI).astype(BF16)
    bg_ref[...] = proj(SEG_BG).astype(BF16)
    cx_ref[...] = proj(SEG_CX)
    dq_ref[...] = (proj(SEG_DQ) * ATTN_SCALE).astype(BF16)
    dk_ref[...] = proj(SEG_DK).astype(BF16)
    dv_ref[...] = proj(SEG_DV).astype(BF16)
    gw = 512
    for c0 in range(SEG_GATE[0], SEG_GATE[1], gw):
        o0 = c0 - SEG_GATE[0]
        gt_ref[:, o0:o0 + gw] = _sigmoid(proj((c0, c0 + gw))).astype(BF16)


def _in_proj(x_all, mods, nrm, w_in, cos, sin, gq, gk, segm, *, layer, t_lat):
    n, d = x_all.shape
    tm = TOKEN_TILE
    widths = [(256, BF16), (256, BF16), (256, BF16), (256, BF16), (256, F32), (256, F32), (256, BF16),
              (256, BF16), (256, F32), (256, BF16), (256, BF16), (256, BF16), (N_BRANCH * D_MODEL, BF16)]
    transposed = (0, 2)
    out_specs = [pl.BlockSpec((w, tm), lambda i: (0, i)) if o in transposed else pl.BlockSpec((tm, w), lambda i: (i, 0))
                 for o, (w, _) in enumerate(widths)]
    out_shape = [jax.ShapeDtypeStruct((w, n) if o in transposed else (n, w), dt) for o, (w, dt) in enumerate(widths)]
    return pl.pallas_call(
        functools.partial(_inproj_kernel, t_lat=t_lat, tm=tm),
        grid=(n // tm,),
        in_specs=[
            pl.BlockSpec((tm, d), lambda i: (i, 0)),
            _const_spec((2, N_MOD, d)),
            _const_spec((1, d)),
            _layer_spec(w_in.shape, layer),
            pl.BlockSpec((tm, 128), lambda i: (i, 0)),
            pl.BlockSpec((tm, 128), lambda i: (i, 0)),
            _const_spec((1, 256)),
            _const_spec((1, 128)),
            _const_spec((256, 256)),
        ],
        out_specs=out_specs,
        out_shape=out_shape,
        compiler_params=_cparams(("parallel",)),
        name="in_proj",
    )(x_all, mods, nrm, w_in, cos, sin, gq, gk, segm)


def _softmax_attend(q, k, v, exp_fn):
    s = _dot_nt(q, k)
    p = exp_fn(s - jnp.max(s, axis=-1, keepdims=True))
    return _dot(p.astype(BF16), v) / jnp.sum(p, axis=-1, keepdims=True)


def _ctx_attn_kernel(aqt_ref, ak_ref, avt_ref, dq_ref, dk_ref, dv_ref, ya_ref, yd_ref):
    outs = []
    for h in range(A_HEADS):
        g = h // (A_HEADS // A_KV_HEADS)
        st = _dot(ak_ref[:, 128 * g:128 * g + 64], aqt_ref[64 * h:64 * h + 64, :])
        pt = jnp.exp2(st - jnp.max(st, axis=0, keepdims=True)).astype(BF16)
        acc = _dot(avt_ref[128 * g:128 * g + 128, :], pt)
        outs.append((acc[:HEAD_DIM] / acc[HEAD_DIM:HEAD_DIM + 1]).T)
    ya_ref[...] = jnp.concatenate(outs, axis=1).astype(BF16)
    outs = []
    for h in range(D_HEADS):
        hs = slice(64 * h, 64 * h + 64)
        outs.append(_softmax_attend(dq_ref[:, hs], dk_ref[:, hs], dv_ref[:, hs], jnp.exp))
    yd_ref[...] = jnp.concatenate(outs, axis=1).astype(BF16)


def _ctx_attn(aqt, ak, avt, dq, dk, dv, *, t_lat):
    n = ak.shape[0]
    n_ctx = n - t_lat
    blk = t_lat // n_ctx

    def spec(w):
        return pl.BlockSpec((n_ctx, w), lambda i: (blk, 0))

    def spec_t(w):
        return pl.BlockSpec((w, n_ctx), lambda i: (0, blk))

    return pl.pallas_call(
        _ctx_attn_kernel,
        grid=(1,),
        in_specs=[spec_t(256), spec(256), spec_t(256), spec(256), spec(256), spec(256)],
        out_specs=[spec(256), spec(256)],
        out_shape=[jax.ShapeDtypeStruct((n, BRANCH_WIDTH), BF16)] * 2,
        compiler_params=_cparams(("arbitrary",)),
        name="ctx_attn",
    )(aqt, ak, avt, dq, dk, dv)


def _flash_kernel(qt_ref, k_ref, vt_ref, prev_ref, o_ref, m_ref, acc_ref, *bufs, tk, tq):
    del prev_ref
    s_refs, p_refs = bufs[:FLASH_NBUF], bufs[FLASH_NBUF:]
    j = pl.program_id(1)

    @pl.when(j == 0)
    def _():
        m_ref[...] = jnp.full(m_ref.shape, -jnp.inf, F32)
        acc_ref[...] = jnp.zeros(acc_ref.shape, F32)

    group = A_HEADS // A_KV_HEADS
    n_qc = tq // FLASH_QC
    n_sb = tk // FLASH_SB
    chains = [(h, c) for h in range(A_HEADS) for c in range(n_qc)]

    def score_block(ci, sb):
        h, c = chains[ci]
        g = h // group
        rows = slice(sb * FLASH_SB, (sb + 1) * FLASH_SB)
        st = _dot(k_ref[rows, 128 * g:128 * g + 64], qt_ref[64 * h:64 * h + 64, c * FLASH_QC:(c + 1) * FLASH_QC])
        s_refs[ci % FLASH_NBUF][rows, :] = st
        return jnp.max(st, axis=0, keepdims=True)

    def exp_block(ci, sb, m_new):
        rows = slice(sb * FLASH_SB, (sb + 1) * FLASH_SB)
        p_refs[ci % 2][rows, :] = jnp.exp2(s_refs[ci % FLASH_NBUF][rows, :] - m_new).astype(BF16)

    def value_block(ci, sb):
        g = chains[ci][0] // group
        rows = slice(sb * FLASH_SB, (sb + 1) * FLASH_SB)
        return _dot(vt_ref[128 * g:128 * g + 128, rows], p_refs[ci % 2][rows, :])

    def fold(a, b):
        return b if a is None else a + b

    def fold_max(a, b):
        return b if a is None else jnp.maximum(a, b)

    n_ch = len(chains)
    blk_max, m_new, alpha = {}, {}, {}
    for stage in range(-1, n_ch + 1):
        c_s, c_e, c_v = stage + 1, stage, stage - 1
        if 0 <= c_e < n_ch:
            h, c = chains[c_e]
            cols = slice(c * FLASH_QC, (c + 1) * FLASH_QC)
            m_old = m_ref[h, :, cols]
            m_new[c_e] = jnp.maximum(m_old, blk_max.pop(c_e))
            m_ref[h, :, cols] = m_new[c_e]
            alpha[c_e] = jnp.exp2(m_old - m_new[c_e])
        part, mx = None, None
        for sb in range(n_sb):
            if c_s < n_ch:
                mx = fold_max(mx, score_block(c_s, sb))
            if 0 <= c_e < n_ch:
                exp_block(c_e, sb, m_new[c_e])
            if 0 <= c_v:
                part = fold(part, value_block(c_v, sb))
        if c_s < n_ch:
            blk_max[c_s] = mx
        if 0 <= c_v:
            h, c = chains[c_v]
            cols = slice(c * FLASH_QC, (c + 1) * FLASH_QC)
            acc_ref[h, :, cols] = alpha.pop(c_v) * acc_ref[h, :, cols] + part

    @pl.when(j == pl.num_programs(1) - 1)
    def _():
        outs = []
        for h in range(A_HEADS):
            acc = acc_ref[h]
            outs.append((acc[:HEAD_DIM] / acc[HEAD_DIM:HEAD_DIM + 1]).T)
        o_ref[...] = jnp.concatenate(outs, axis=1).astype(BF16)


def _flash_tk(n):
    for tk in (1280, 768, 512, 256):
        if n % tk == 0:
            return tk
    raise ValueError(f"unsupported key count {n}")


def _gqa_latent(aqt, ak, avt, ya_prev, *, t_lat):
    n = ak.shape[0]
    tq = FLASH_TQ
    tk = _flash_tk(n)
    return pl.pallas_call(
        functools.partial(_flash_kernel, tk=tk, tq=tq),
        grid=(t_lat // tq, n // tk),
        in_specs=[
            pl.BlockSpec((256, tq), lambda i, j: (0, i)),
            pl.BlockSpec((tk, 256), lambda i, j: (j, 0)),
            pl.BlockSpec((256, tk), lambda i, j: (0, j)),
            pl.BlockSpec(memory_space=pl.ANY),
        ],
        out_specs=pl.BlockSpec((tq, 256), lambda i, j: (i, 0)),
        out_shape=jax.ShapeDtypeStruct((n, BRANCH_WIDTH), BF16),
        scratch_shapes=[pltpu.VMEM((A_HEADS, 1, tq), F32),
                        pltpu.VMEM((A_HEADS, 128, tq), F32)]
                       + [pltpu.VMEM((tk, FLASH_QC), F32)] * FLASH_NBUF
                       + [pltpu.VMEM((tk, FLASH_QC), BF16)] * 2,
        input_output_aliases={3: 0},
        compiler_params=_cparams(("parallel", "arbitrary")),
        name="gqa_flash",
    )(aqt, ak, avt, ya_prev)


def _na_kernel(q_ref, kp_ref, km_ref, kn_ref, vp_ref, vm_ref, vn_ref, kc_ref, vc_ref, bias_ref, mask_ref, prev_ref,
               o_ref):
    del prev_ref
    outs = []
    for h in range(D_HEADS):
        hs = slice(64 * h, 64 * h + 64)
        q = q_ref[:, hs]
        kcat = jnp.concatenate([kp_ref[:, hs], km_ref[:, hs], kn_ref[:, hs]], axis=0)
        vcat = jnp.concatenate([vp_ref[:, hs], vm_ref[:, hs], vn_ref[:, hs]], axis=0)
        s_loc = _dot_nt(q, kcat) + (bias_ref[h] + mask_ref[0])
        s_ctx = _dot_nt(q, kc_ref[:, hs])
        m = jnp.maximum(jnp.max(s_loc, axis=-1, keepdims=True), jnp.max(s_ctx, axis=-1, keepdims=True))
        p_loc = jnp.exp(s_loc - m)
        p_ctx = jnp.exp(s_ctx - m)
        denom = jnp.sum(p_loc, axis=-1, keepdims=True) + jnp.sum(p_ctx, axis=-1, keepdims=True)
        o = _dot(p_loc.astype(BF16), vcat) + _dot(p_ctx.astype(BF16), vc_ref[:, hs])
        outs.append(o / denom)
    o_ref[...] = jnp.concatenate(outs, axis=1).astype(BF16)


def _na_bias_tables(rel_bias, rows):
    wr = min(NA_WIN_R, rows)
    halo_rows = NA_HALO // GRID_W
    krows = NA_QROWS + 2 * halo_rows
    nb = rows // NA_QROWS
    qc = np.arange(GRID_W)[:, None]
    kc = np.arange(GRID_W)[None, :]
    cs = np.clip(qc - NA_WIN_C // 2, 0, GRID_W - NA_WIN_C)
    in_col = (kc >= cs) & (kc < cs + NA_WIN_C)
    e_col = (kc - qc + (NA_WIN_C - 1))[:, :, None] == np.arange(2 * NA_WIN_C - 1)
    qr_l = np.arange(NA_QROWS)[:, None]
    kr_l = np.arange(krows)[None, :]
    e_row = (kr_l - halo_rows - qr_l + (NA_WIN_R - 1))[:, :, None] == np.arange(2 * NA_WIN_R - 1)
    hi = lax.Precision.HIGHEST
    tmp = jnp.einsum("lhrc,qkr->lhqkc", rel_bias.astype(F32), e_row.astype(np.float32), precision=hi)
    full = jnp.einsum("lhqkc,pjc->lhqpkj", tmp, e_col.astype(np.float32), precision=hi)
    masks = []
    for b in (0, min(1, nb - 1), nb - 1):
        qr = NA_QROWS * b + qr_l
        kr = NA_QROWS * b - halo_rows + kr_l
        rs = np.clip(qr - wr // 2, 0, rows - wr)
        in_row = (kr >= rs) & (kr < rs + wr)
        mask = in_row[:, None, :, None] & in_col[None, :, None, :]
        masks.append(np.where(mask, 0.0, NEG_BIG).reshape(NA_QB, krows * GRID_W))
    return full.reshape(-1, D_HEADS, NA_QB, krows * GRID_W), np.stack(masks).astype(np.float32)


def _na_latent(dq, dk, dv, bias, masks, yd_prev, *, layer, t_lat):
    n = dq.shape[0]
    nb = t_lat // NA_QB
    r = NA_QB // NA_HALO
    last_halo = t_lat // NA_HALO - 1
    ctx_blk = t_lat // (n - t_lat)
    n_ctx = n - t_lat

    def prev_map(b):
        return (jnp.maximum(r * b - 1, 0), 0)

    def next_map(b):
        return (jnp.minimum(r * b + r, last_halo), 0)

    def variant(b):
        return (jnp.where(b == 0, 0, jnp.where(b == nb - 1, 2, 1)), 0, 0)

    main = pl.BlockSpec((NA_QB, 256), lambda b: (b, 0))
    prev = pl.BlockSpec((NA_HALO, 256), prev_map)
    nxt = pl.BlockSpec((NA_HALO, 256), next_map)
    ctx = pl.BlockSpec((n_ctx, 256), lambda b: (ctx_blk, 0))
    return pl.pallas_call(
        _na_kernel,
        grid=(nb,),
        in_specs=[main, prev, main, nxt, prev, main, nxt, ctx, ctx,
                  _layer_spec(bias.shape, layer),
                  pl.BlockSpec((1, NA_QB, NA_QB + 2 * NA_HALO), variant),
                  pl.BlockSpec(memory_space=pl.ANY)],
        out_specs=main,
        out_shape=jax.ShapeDtypeStruct((n, BRANCH_WIDTH), BF16),
        input_output_aliases={11: 0},
        compiler_params=_cparams(("parallel",)),
        name="na_attn",
    )(dq, dk, dk, dk, dv, dv, dv, dk, dv, bias, masks, yd_prev)


def _pool_kernel(x_ref, xp_ref, xn_ref, w_ref, sc_ref, o_ref, cat_ref, a_ref, b_ref, *, t_lat, n_all, tm):
    i = pl.program_id(0)
    start = i * tm
    in_lat = start < t_lat
    seg_lo = jnp.where(in_lat, 0, t_lat)
    seg_hi = jnp.where(in_lat, t_lat, n_all)
    x = x_ref[...]
    cat_ref[0:8, :] = jnp.where(start > seg_lo, xp_ref[...], 0.0)
    cat_ref[8:8 + tm, :] = x
    cat_ref[8 + tm:16 + tm, :] = jnp.where(start + tm < seg_hi, xn_ref[...], 0.0)
    cat_ref[16 + tm:32 + tm, :] = jnp.zeros((16, BRANCH_WIDTH), F32)
    a_ref[0:tm + 24, :] = cat_ref[0:tm + 24, :] + cat_ref[1:tm + 25, :]
    s2 = a_ref[7:7 + tm, :]
    b_ref[0:tm + 16, :] = a_ref[0:tm + 16, :] + a_ref[2:tm + 18, :]
    s4 = b_ref[6:6 + tm, :]
    a_ref[0:tm + 8, :] = b_ref[0:tm + 8, :] + b_ref[4:tm + 12, :]
    s8 = a_ref[4:4 + tm, :]
    s16 = a_ref[0:tm, :] + a_ref[8:8 + tm, :]

    pos = start - seg_lo + lax.broadcasted_iota(jnp.int32, (tm, 1), 0)
    seg_len = seg_hi - seg_lo

    def mean(sm, w):
        lo = jnp.clip(pos - w // 2, 0, seg_len)
        hi = jnp.clip(pos - w // 2 + w, 0, seg_len)
        return sm / (hi - lo).astype(F32)

    lane = lax.broadcasted_iota(jnp.int32, (tm, BRANCH_WIDTH), 1)
    gw = BRANCH_WIDTH // len(C_WINDOWS)
    pooled = jnp.where(lane < gw, mean(s2, 2),
                       jnp.where(lane < 2 * gw, mean(s4, 4),
                                 jnp.where(lane < 3 * gw, mean(s8, 8), mean(s16, 16)))) - x
    o_ref[...] = (_dot(pooled.astype(BF16), w_ref[...]) * sc_ref[...]).astype(BF16)


def _pool_mixer(cx, w_bd, scale, *, t_lat):
    n = cx.shape[0]
    tm = TOKEN_TILE
    r = tm // 8
    last8 = n // 8 - 1
    return pl.pallas_call(
        functools.partial(_pool_kernel, t_lat=t_lat, n_all=n, tm=tm),
        grid=(n // tm,),
        in_specs=[
            pl.BlockSpec((tm, 256), lambda i: (i, 0)),
            pl.BlockSpec((8, 256), lambda i: (jnp.maximum(r * i - 1, 0), 0)),
            pl.BlockSpec((8, 256), lambda i: (jnp.minimum(r * i + r, last8), 0)),
            _const_spec((256, 256)),
            _const_spec((1, 256)),
        ],
        out_specs=pl.BlockSpec((tm, 256), lambda i: (i, 0)),
        out_shape=jax.ShapeDtypeStruct((n, BRANCH_WIDTH), BF16),
        scratch_shapes=[pltpu.VMEM((tm + 32, 256), F32)] * 3,
        compiler_params=_cparams(("parallel",)),
        name="pool_mixer",
    )(cx, cx, cx, w_bd, scale)


def _hgrn_tables(rev):
    c_len, s, w = HG_CHUNK, HG_SUB, B_HEADS * B_KEY_DIM
    t = np.arange(c_len)
    tri = (t[None, :] >= t[:, None]) if rev else (t[None, :] <= t[:, None])
    pos = np.arange(s)
    keep = (pos[None, :] <= pos[:, None]) if rev else (pos[None, :] >= pos[:, None])
    keep_add = np.where(keep, 0.0, NEG_BIG)[:, :, None] * np.ones((1, 1, w))
    head_of_lane = np.arange(w) // B_KEY_DIM
    col = np.arange(B_HEADS * s)
    sel = (col[None, None, :] == (head_of_lane[None, :, None] * s + pos[:, None, None]))
    bd = (col[:, None] // s) == head_of_lane[None, :]
    hh = head_of_lane[:, None] == head_of_lane[None, :]
    return (jnp.asarray(tri, BF16), jnp.asarray(keep_add, F32), jnp.asarray(sel, BF16),
            jnp.asarray(bd, F32), jnp.asarray(hh, F32))


def _hgrn_kernel(q_ref, f_ref, v_ref, lb_ref, tri_ref, keep_ref, sel_ref, bd_ref, hh_ref, o_ref, st_ref, *, rev):
    @pl.when(pl.program_id(0) == 0)
    def _():
        st_ref[...] = jnp.zeros(st_ref.shape, F32)

    st = st_ref[...]
    order = range(HG_BLOCK // HG_CHUNK)
    for ch in (reversed(order) if rev else order):
        rows = slice(ch * HG_CHUNK, (ch + 1) * HG_CHUNK)
        o, st = _hgrn_chunk(q_ref[rows, :], f_ref[rows, :], v_ref[rows, :], st, lb_ref, tri_ref, keep_ref,
                            sel_ref, bd_ref, hh_ref, rev=rev)
        o_ref[rows, :] = o
    st_ref[...] = st


def _hgrn_chunk(q_bf, f_pre, v, st, lb_ref, tri_ref, keep_ref, sel_ref, bd_ref, hh_ref, *, rev):
    c_len = HG_CHUNK
    n_sub = c_len // HG_SUB
    w = B_HEADS * B_KEY_DIM
    lb = lb_ref[...]
    f = lb + (1.0 - lb) * _sigmoid(f_pre)
    k = 1.0 - f
    lf = jnp.log(f)
    q = q_bf.astype(F32)
    bd = bd_ref[...]

    c = _split_dot_left(tri_ref[...], lf)

    q3 = q.reshape(n_sub, HG_SUB, w)
    k3 = k.reshape(n_sub, HG_SUB, w)
    c3 = c.reshape(n_sub, HG_SUB, w)
    a_diag = None
    for sg in range(HG_SUB):
        ks = jnp.broadcast_to(k3[:, sg:sg + 1, :], k3.shape)
        cs = jnp.broadcast_to(c3[:, sg:sg + 1, :], c3.shape)
        wgt = (q3 * ks) * jnp.exp((c3 - cs) + keep_ref[sg][None])
        part = _dot(wgt.reshape(c_len, w).astype(BF16), sel_ref[sg])
        a_diag = part if a_diag is None else a_diag + part

    o_parts = [None] * n_sub
    for j in range(n_sub):
        r0 = j * HG_SUB
        r_last = r0 if rev else r0 + HG_SUB - 1
        e_j = c[r_last:r_last + 1, :]
        kj = k[r0:r0 + HG_SUB, :] * jnp.exp(e_j - c[r0:r0 + HG_SUB, :])
        kbd = (jnp.concatenate([kj] * B_HEADS, axis=0) * bd).astype(BF16)
        vbd = (jnp.concatenate([v[r0:r0 + HG_SUB, :].astype(F32)] * B_HEADS, axis=0) * bd).astype(BF16)
        lo, hi = (0, r0) if rev else (r0 + HG_SUB, c_len)
        pieces = [a_diag[r0:r0 + HG_SUB]]
        if hi > lo:
            qj = (q[lo:hi] * jnp.exp(c[lo:hi] - e_j)).astype(BF16)
            a_off = _dot_nt(qj, kbd)
            pieces = [a_off] + pieces if rev else pieces + [a_off]
        first = 0 if rev else j
        if (sum(p.shape[0] for p in pieces) % 16) != 0:
            pad = jnp.zeros((HG_SUB, a_diag.shape[1]), F32)
            pieces = pieces + [pad] if rev else [pad] + pieces
            first = first if rev else first - 1
        a_j = jnp.concatenate(pieces, axis=0).astype(BF16)
        contrib = _dot(a_j, vbd)
        for i in range(contrib.shape[0] // HG_SUB):
            piece = contrib[i * HG_SUB:(i + 1) * HG_SUB]
            o_parts[first + i] = piece if o_parts[first + i] is None else o_parts[first + i] + piece
    o = jnp.concatenate(o_parts, axis=0)

    o = o + _dot_nt((q * jnp.exp(c)).astype(BF16), st.astype(BF16))
    r_end = 0 if rev else c_len - 1
    c_end = c[r_end:r_end + 1, :]
    k_end = (k * jnp.exp(c_end - c)).astype(BF16)
    return o, jnp.exp(c_end) * st + _dot_tn(v, k_end) * hh_ref[...]


def _hgrn_dir(bq, f_pre, bi, lb, *, t_lat, rev):
    n = bq.shape[0]
    c_len = HG_BLOCK
    n_lat = t_lat // c_len
    n_all = n // c_len
    n_ctx = n_all - n_lat

    if rev:
        def blk(i):
            return (jnp.where(i < n_ctx, n_all - 1 - i, n_lat - 1 - (i - n_ctx)), 0)
    else:
        def blk(i):
            return (jnp.where(i < n_ctx, n_lat + i, i - n_ctx), 0)

    tile = pl.BlockSpec((c_len, 256), blk)
    tables = _hgrn_tables(rev)
    in_specs = [tile, tile, tile, _const_spec((1, 256))] + [_const_spec(t.shape) for t in tables]
    args = [bq, f_pre, bi, lb, *tables]
    return pl.pallas_call(
        functools.partial(_hgrn_kernel, rev=rev),
        grid=(n_all,),
        in_specs=in_specs,
        out_specs=tile,
        out_shape=jax.ShapeDtypeStruct((n, BRANCH_WIDTH), F32),
        scratch_shapes=[pltpu.VMEM((256, 256), F32)],
        compiler_params=_cparams(("arbitrary",)),
        name="hgrn_rev" if rev else "hgrn_fwd",
    )(*args)


def _merge_kernel(x_ref, mods_ref, ya_ref, of_ref, or_ref, bg_ref, gain_ref, segm_ref, yc_ref, yd_ref, gt_ref,
                  wb_ref, wo_ref, o_ref, *, t_lat, tm):
    i = pl.program_id(0)
    row = i * tm + lax.broadcasted_iota(jnp.int32, (tm, 1), 0)
    is_lat = row < t_lat
    yb = _head_norm(of_ref[...] + or_ref[...], gain_ref[...], segm_ref[...]) * _silu(bg_ref[...].astype(F32))
    branches = (ya_ref[...], yb.astype(BF16), yc_ref[...], yd_ref[...])
    merged = None
    for n, y in enumerate(branches):
        term = gt_ref[:, n * D_MODEL:(n + 1) * D_MODEL].astype(F32) * _dot(y, wb_ref[n])
        merged = term if merged is None else merged + term
    y = _dot(merged.astype(BF16), wo_ref[...])
    o_ref[...] = x_ref[...] + _row_mod(mods_ref, 5, is_lat) * y


def _merge(x_all, mods, ya, o_fwd, o_rev, bg, b_gain, segm, yc, yd, gates, wb, wo, *, layer, t_lat, n_rows):
    n, d = x_all.shape
    tm = TOKEN_TILE

    def tile(w):
        return pl.BlockSpec((tm, w), lambda i: (i, 0))

    return pl.pallas_call(
        functools.partial(_merge_kernel, t_lat=t_lat, tm=tm),
        grid=(n_rows // tm,),
        in_specs=[tile(d), _const_spec((2, N_MOD, d)), tile(256), tile(256), tile(256), tile(256),
                  _const_spec((1, 256)), _const_spec((256, 256)), tile(256), tile(256),
                  tile(N_BRANCH * d), _layer_spec(wb.shape, layer), _layer_spec(wo.shape, layer)],
        out_specs=tile(d),
        out_shape=jax.ShapeDtypeStruct((n, d), F32),
        input_output_aliases={0: 0},
        compiler_params=_cparams(("parallel",)),
        name="merge_out",
    )(x_all, mods, ya, o_fwd, o_rev, bg, b_gain, segm, yc, yd, gates, wb, wo)


def _rope_tables_padded(t_lat, n_ctx):
    rows = t_lat // GRID_W
    half = HEAD_DIM // 2
    nf = half // 2
    inv = 1.0 / (ROPE_THETA ** (jnp.arange(0, half, 2, dtype=F32) / half))
    ang_r = jnp.arange(rows, dtype=F32)[:, None] * inv
    ang_c = jnp.arange(GRID_W, dtype=F32)[:, None] * inv

    def over_rows(a):
        return jnp.broadcast_to(a[:, None, :], (rows, GRID_W, nf))

    def over_cols(a):
        return jnp.broadcast_to(a[None, :, :], (rows, GRID_W, nf))

    cos = jnp.concatenate([over_rows(jnp.cos(ang_r))] * 2 + [over_cols(jnp.cos(ang_c))] * 2, axis=2)
    sin = jnp.concatenate([over_rows(-jnp.sin(ang_r)), over_rows(jnp.sin(ang_r)),
                           over_cols(-jnp.sin(ang_c)), over_cols(jnp.sin(ang_c))], axis=2)
    cos = cos.reshape(t_lat, HEAD_DIM)
    sin = sin.reshape(t_lat, HEAD_DIM)
    cos = jnp.concatenate([cos, jnp.ones((n_ctx, HEAD_DIM), F32)], axis=0)
    sin = jnp.concatenate([sin, jnp.zeros((n_ctx, HEAD_DIM), F32)], axis=0)
    return jnp.tile(cos, (1, 2)), jnp.tile(sin, (1, 2))


def _block_diag(w_group):
    g, ci, co = w_group.shape
    out = jnp.zeros((g * ci, g * co), w_group.dtype)
    for n in range(g):
        out = out.at[n * ci:(n + 1) * ci, n * co:(n + 1) * co].set(w_group[n])
    return out


def kernel(x, c, ctx, c_ctx, w_ada, b_ada, ffn1_norm, ffn1_w_gate, ffn1_w_up, ffn1_w_down, mix_norm, w_in, a_q_norm, a_k_norm, b_lb_logits, b_o_norm, c_w_group, c_scale, d_rel_bias, w_branch, w_out, ffn2_norm, ffn2_w_gate, ffn2_w_up, ffn2_w_down, final_norm):
    assert x.shape[0] == 1 and ctx.shape[0] == 1
    depth = w_ada.shape[0]
    t_lat = x.shape[1]
    n_ctx = ctx.shape[1]
    n = t_lat + n_ctx
    rows = t_lat // GRID_W
    assert t_lat % max(NA_QB, FLASH_TQ, n_ctx) == 0 and n_ctx % TOKEN_TILE == 0 and rows >= 2 * NA_QROWS

    x_all = None
    mods_all = _ada_mods(c, c_ctx, w_ada, b_ada)
    cos, sin = _rope_tables_padded(t_lat, n_ctx)
    segm = (jnp.kron(jnp.eye(BRANCH_WIDTH // HEAD_DIM), jnp.ones((HEAD_DIM, HEAD_DIM))) / HEAD_DIM).astype(BF16)
    lb_all = jnp.cumsum(jax.nn.softmax(b_lb_logits.astype(F32), axis=0), axis=0)
    lb_all = lb_all - lb_all[:1]
    wg1, wu1, wd1 = ffn1_w_gate.astype(BF16), ffn1_w_up.astype(BF16), ffn1_w_down.astype(BF16)
    wg2, wu2, wd2 = ffn2_w_gate.astype(BF16), ffn2_w_up.astype(BF16), ffn2_w_down.astype(BF16)
    w_in_b, w_branch_b, w_out_b = w_in.astype(BF16), w_branch.astype(BF16), w_out.astype(BF16)
    na_bias, na_masks = _na_bias_tables(d_rel_bias, rows)

    for l in range(depth):
        with_ctx_out = l < depth - 1
        n_rows = n if with_ctx_out else t_lat
        mods = mods_all[l]
        x_all = _ffn_half(x[0] if l == 0 else x_all, mods, ffn1_norm[l][None], wg1, wu1, wd1, layer=l, t_lat=t_lat,
                          k0=0, n_rows=n, x_ctx=ctx[0] if l == 0 else None)
        (aq, ak, av, bq, bff, bfb, bi, bg, cx, dq, dk, dv, gates) = _in_proj(
            x_all, mods, mix_norm[l][None], w_in_b, cos, sin,
            jnp.tile(a_q_norm[l], A_HEADS)[None], jnp.tile(a_k_norm[l], A_KV_HEADS)[None], segm, layer=l, t_lat=t_lat)
        if with_ctx_out:
            ya, yd = _ctx_attn(aq, ak, av, dq, dk, dv, t_lat=t_lat)
        else:
            ya = jnp.zeros((n, BRANCH_WIDTH), BF16)
            yd = ya
        ya = _gqa_latent(aq, ak, av, ya, t_lat=t_lat)
        yd = _na_latent(dq, dk, dv, na_bias, na_masks, yd, layer=l, t_lat=t_lat)
        yc = _pool_mixer(cx, _block_diag(c_w_group[l]).astype(BF16), c_scale[l][None], t_lat=t_lat)
        o_fwd = _hgrn_dir(bq, bff, bi, lb_all[l, 0][None], t_lat=t_lat, rev=False)
        o_rev = _hgrn_dir(bq, bfb, bi, lb_all[l, 1][None], t_lat=t_lat, rev=True)
        x_all = _merge(x_all, mods, ya, o_fwd, o_rev, bg, jnp.tile(b_o_norm[l], B_HEADS)[None], segm, yc, yd, gates,
                       w_branch_b, w_out_b, layer=l, t_lat=t_lat, n_rows=n_rows)
        x_all = _ffn_half(x_all, mods, ffn2_norm[l][None], wg2, wu2, wd2, layer=l, t_lat=t_lat, k0=6, n_rows=n_rows,
                          final_gain=None if with_ctx_out else final_norm[None])
    return x_all[None]
```

```python
import functools

import numpy as np
import jax
import jax.numpy as jnp
from jax import lax
from jax.experimental import pallas as pl
from jax.experimental.pallas import tpu as pltpu

F32 = jnp.float32
BF16 = jnp.bfloat16

D_MODEL = 1024
GRID_W = 64
HEAD_DIM = 64
BRANCH_WIDTH = 256
N_BRANCH = 4
A_HEADS = 4
A_KV_HEADS = 2
ROPE_THETA = 10000.0
B_HEADS = 4
B_KEY_DIM = 64
C_WINDOWS = (2, 4, 8, 16)
D_HEADS = 4
NA_WIN_R = 8
NA_WIN_C = 16
D_FF = 2816
N_MOD = 9
EPS = 1e-6
ATTN_SCALE = HEAD_DIM ** -0.5
LOG2_E = 1.4426950408889634
NEG_BIG = -1e30

V7X_VMEM_BYTES = 64 * 1024 * 1024
VMEM_LIMIT = V7X_VMEM_BYTES - 8 * 1024 * 1024

SEG_AQ = (0, 256)
SEG_AK = (256, 384)
SEG_AV = (384, 512)
SEG_BQ = (512, 768)
SEG_BFF = (768, 1024)
SEG_BFB = (1024, 1280)
SEG_BI = (1280, 1536)
SEG_BG = (1536, 1792)
SEG_CX = (1792, 2048)
SEG_DQ = (2048, 2304)
SEG_DK = (2304, 2560)
SEG_DV = (2560, 2816)
SEG_GATE = (2816, 2816 + N_BRANCH * D_MODEL)
IN_WIDTH = SEG_GATE[1]

TOKEN_TILE = 256
POOL_TILE = 2048
FLASH_TQ = 2048
FLASH_QC = 512
FLASH_SB = 256
FLASH_NBUF = 4
NA_QROWS = 8
NA_QB = NA_QROWS * GRID_W
NA_HALO = 256
HG_BLOCK = 256
HG_CHUNK = 128
HG_SUB = 8


def _cparams(sem):
    return pltpu.CompilerParams(dimension_semantics=sem, vmem_limit_bytes=VMEM_LIMIT)


def _const_spec(shape):
    nd = len(shape)
    return pl.BlockSpec(shape, lambda *_: (0,) * nd, pipeline_mode=pl.Buffered(1))


def _layer_spec(shape, layer):
    nd = len(shape)
    return pl.BlockSpec((None,) + tuple(shape[1:]), lambda *_: (layer,) + (0,) * (nd - 1),
                        pipeline_mode=pl.Buffered(1))


def _sigmoid(x):
    return 1.0 / (1.0 + jnp.exp(-x))


def _silu(x):
    return x * _sigmoid(x)


def _dot(a, b):
    return jnp.dot(a, b, preferred_element_type=F32)


def _dot_nt(a, b):
    return lax.dot_general(a, b, (((1,), (1,)), ((), ())), preferred_element_type=F32)


def _dot_tn(a, b):
    return lax.dot_general(a, b, (((0,), (0,)), ((), ())), preferred_element_type=F32)


def _split_dot(x, m):
    hi = x.astype(BF16)
    r1 = x - hi.astype(F32)
    mid = r1.astype(BF16)
    lo = (r1 - mid.astype(F32)).astype(BF16)
    return _dot(hi, m) + _dot(mid, m) + _dot(lo, m)


def _split_dot_left(m, x):
    hi = x.astype(BF16)
    r1 = x - hi.astype(F32)
    mid = r1.astype(BF16)
    lo = (r1 - mid.astype(F32)).astype(BF16)
    return _dot(m, hi) + _dot(m, mid) + _dot(m, lo)


def _row_mod(mods_ref, k, is_lat):
    return jnp.where(is_lat, mods_ref[0, k:k + 1, :], mods_ref[1, k:k + 1, :])


def _rms_rows(x):
    return x * lax.rsqrt(jnp.mean(x * x, axis=-1, keepdims=True) + EPS)


def _ada_kernel(s_ref, w_ref, b_ref, o_ref):
    o_ref[0] = _dot(_silu(s_ref[...]).astype(BF16), w_ref[0].astype(BF16)) + b_ref[0]


def _ada_mods(c, c_ctx, w_ada, b_ada):
    depth = w_ada.shape[0]
    width = w_ada.shape[2]
    tn = 1024
    s = jnp.zeros((8, D_MODEL), F32).at[0].set(c[0]).at[1].set(c_ctx)
    out = pl.pallas_call(
        _ada_kernel,
        grid=(depth, width // tn),
        in_specs=[
            pl.BlockSpec((8, D_MODEL), lambda l, j: (0, 0)),
            pl.BlockSpec((1, D_MODEL, tn), lambda l, j: (l, 0, j)),
            pl.BlockSpec((1, 1, tn), lambda l, j: (l, 0, j)),
        ],
        out_specs=pl.BlockSpec((1, 8, tn), lambda l, j: (l, 0, j)),
        out_shape=jax.ShapeDtypeStruct((depth, 8, width), F32),
        compiler_params=_cparams(("parallel", "parallel")),
        name="ada_mods",
    )(s, w_ada, b_ada.reshape(depth, 1, width))
    return out[:, :2].reshape(depth, 2, N_MOD, D_MODEL)


def _ffn_math(x, is_lat, mods_ref, nrm_ref, wg_ref, wu_ref, wd_ref, k0):
    z = _rms_rows(x) * nrm_ref[...]
    z = (z * (1.0 + _row_mod(mods_ref, k0 + 1, is_lat)) + _row_mod(mods_ref, k0, is_lat)).astype(BF16)
    g = _dot(z, wg_ref[...])
    u = _dot(z, wu_ref[...])
    a = (_silu(g) * u).astype(BF16)
    return x + 0.5 * _row_mod(mods_ref, k0 + 2, is_lat) * _dot(a, wd_ref[...])


def _ffn_kernel(*refs, t_lat, tm, k0, split_input):
    if split_input:
        xl_ref, xc_ref, mods_ref, nrm_ref, wg_ref, wu_ref, wd_ref, o_ref = refs
    else:
        x_ref, mods_ref, nrm_ref, wg_ref, wu_ref, wd_ref, o_ref = refs
    i = pl.program_id(0)
    row = i * tm + lax.broadcasted_iota(jnp.int32, (tm, 1), 0)
    is_lat = row < t_lat
    x = jnp.where(i * tm < t_lat, xl_ref[...], xc_ref[...]) if split_input else x_ref[...]
    o_ref[...] = _ffn_math(x, is_lat, mods_ref, nrm_ref, wg_ref, wu_ref, wd_ref, k0)


def _ffn_half(x_all, mods, nrm, wg, wu, wd, *, layer, t_lat, k0, n_rows, x_ctx=None):
    split_input = x_ctx is not None
    d = x_all.shape[1]
    n = x_all.shape[0] + (x_ctx.shape[0] if split_input else 0)
    tm = TOKEN_TILE
    if split_input:
        n_lat_tiles = t_lat // tm
        x_specs = [pl.BlockSpec((tm, d), lambda i: (jnp.minimum(i, n_lat_tiles - 1), 0)),
                   pl.BlockSpec((tm, d), lambda i: (jnp.maximum(i - n_lat_tiles, 0), 0))]
        x_args = [x_all, x_ctx]
    else:
        x_specs = [pl.BlockSpec((tm, d), lambda i: (i, 0))]
        x_args = [x_all]
    return pl.pallas_call(
        functools.partial(_ffn_kernel, t_lat=t_lat, tm=tm, k0=k0, split_input=split_input),
        grid=(n_rows // tm,),
        in_specs=x_specs + [
            _const_spec((2, N_MOD, d)),
            _const_spec((1, d)),
            _layer_spec(wg.shape, layer),
            _layer_spec(wu.shape, layer),
            _layer_spec(wd.shape, layer),
        ],
        out_specs=pl.BlockSpec((tm, d), lambda i: (i, 0)),
        out_shape=jax.ShapeDtypeStruct((n, d), F32),
        input_output_aliases={} if split_input else {0: 0},
        compiler_params=_cparams(("parallel",)),
        name="ffn_half",
    )(*x_args, mods, nrm, wg, wu, wd)


def _head_norm(y, gain, segm):
    ms = _split_dot(y * y, segm)
    return y * lax.rsqrt(ms + EPS) * gain


def _rope(y, cos, sin_signed):
    w = y.shape[1]
    lane = lax.broadcasted_iota(jnp.int32, y.shape, 1)
    partner = jnp.where((lane & 31) < 16, pltpu.roll(y, w - 16, 1), pltpu.roll(y, 16, 1))
    return y * cos + partner * sin_signed


def _inproj_kernel(x_ref, mods_ref, nrm_ref, w_ref, cos_ref, sin_ref, gq_ref, gk_ref, segm_ref,
                   aq_ref, ak_ref, av_ref, bq_ref, bff_ref, bfb_ref, bi_ref, bg_ref, cx_ref,
                   dq_ref, dk_ref, dv_ref, gt_ref, *, t_lat, tm):
    i = pl.program_id(0)
    row = i * tm + lax.broadcasted_iota(jnp.int32, (tm, 1), 0)
    is_lat = row < t_lat
    z = _rms_rows(x_ref[...]) * nrm_ref[...]
    z = (z * (1.0 + _row_mod(mods_ref, 4, is_lat)) + _row_mod(mods_ref, 3, is_lat)).astype(BF16)

    def proj(seg):
        return _dot(z, w_ref[:, seg[0]:seg[1]])

    cos = cos_ref[...]
    sin = sin_ref[...]
    q = _rope(_head_norm(proj(SEG_AQ), gq_ref[...], segm_ref[...]),
              jnp.concatenate([cos, cos], axis=1), jnp.concatenate([sin, sin], axis=1))
    aq_ref[...] = (q * (ATTN_SCALE * LOG2_E)).T.astype(BF16)
    k = _rope(_head_norm(proj(SEG_AK), gk_ref[...], segm_ref[0:128, 0:128]), cos, sin)
    zeros = jnp.zeros((tm, HEAD_DIM), F32)
    ak_ref[...] = jnp.concatenate([k[:, :HEAD_DIM], zeros, k[:, HEAD_DIM:], zeros], axis=1).astype(BF16)
    v = proj(SEG_AV)
    one_hot = jnp.where(lax.broadcasted_iota(jnp.int32, (tm, HEAD_DIM), 1) == 0, 1.0, 0.0)
    av_ref[...] = jnp.concatenate([v[:, :HEAD_DIM], one_hot, v[:, HEAD_DIM:], one_hot], axis=1).T.astype(BF16)

    bq_ref[...] = (proj(SEG_BQ) * (B_KEY_DIM ** -0.5)).astype(BF16)
    bff_ref[...] = proj(SEG_BFF)
    bfb_ref[...] = proj(SEG_BFB)
    bi_ref[...] = proj(SEG_BI).astype(BF16)
    bg_ref[...] = proj(SEG_BG).astype(BF16)
    cx_ref[...] = proj(SEG_CX)
    dq_ref[...] = (proj(SEG_DQ) * ATTN_SCALE).astype(BF16)
    dk_ref[...] = proj(SEG_DK).astype(BF16)
    dv = proj(SEG_DV)
    dv_ref[...] = jnp.concatenate(
        [piece for h in range(D_HEADS) for piece in (dv[:, HEAD_DIM * h:HEAD_DIM * (h + 1)], one_hot)],
        axis=1).astype(BF16)
    gw = 512
    for c0 in range(SEG_GATE[0], SEG_GATE[1], gw):
        o0 = c0 - SEG_GATE[0]
        gt_ref[:, o0:o0 + gw] = _sigmoid(proj((c0, c0 + gw))).astype(BF16)


def _in_proj(x_all, mods, nrm, w_in, cos, sin, gq, gk, segm, *, layer, t_lat):
    n, d = x_all.shape
    tm = TOKEN_TILE
    widths = [(256, BF16), (256, BF16), (256, BF16), (256, BF16), (256, F32), (256, F32), (256, BF16),
              (256, BF16), (256, F32), (256, BF16), (256, BF16), (2 * D_HEADS * HEAD_DIM, BF16),
              (N_BRANCH * D_MODEL, BF16)]
    transposed = (0, 2)
    out_specs = [pl.BlockSpec((w, tm), lambda i: (0, i)) if o in transposed else pl.BlockSpec((tm, w), lambda i: (i, 0))
                 for o, (w, _) in enumerate(widths)]
    out_shape = [jax.ShapeDtypeStruct((w, n) if o in transposed else (n, w), dt) for o, (w, dt) in enumerate(widths)]
    return pl.pallas_call(
        functools.partial(_inproj_kernel, t_lat=t_lat, tm=tm),
        grid=(n // tm,),
        in_specs=[
            pl.BlockSpec((tm, d), lambda i: (i, 0)),
            _const_spec((2, N_MOD, d)),
            _const_spec((1, d)),
            _layer_spec(w_in.shape, layer),
            pl.BlockSpec((tm, 128), lambda i: (i, 0)),
            pl.BlockSpec((tm, 128), lambda i: (i, 0)),
            _const_spec((1, 256)),
            _const_spec((1, 128)),
            _const_spec((256, 256)),
        ],
        out_specs=out_specs,
        out_shape=out_shape,
        compiler_params=_cparams(("parallel",)),
        name="in_proj",
    )(x_all, mods, nrm, w_in, cos, sin, gq, gk, segm)


def _softmax_attend(q, k, v, exp_fn):
    s = _dot_nt(q, k)
    p = exp_fn(s - jnp.max(s, axis=-1, keepdims=True))
    return _dot(p.astype(BF16), v) / jnp.sum(p, axis=-1, keepdims=True)


def _ctx_attn_kernel(aqt_ref, ak_ref, avt_ref, dq_ref, dk_ref, dv_ref, ya_ref, yd_ref):
    outs = []
    for h in range(A_HEADS):
        g = h // (A_HEADS // A_KV_HEADS)
        st = _dot(ak_ref[:, 128 * g:128 * g + 64], aqt_ref[64 * h:64 * h + 64, :])
        pt = jnp.exp2(st - jnp.max(st, axis=0, keepdims=True)).astype(BF16)
        acc = _dot(avt_ref[128 * g:128 * g + 128, :], pt)
        outs.append((acc[:HEAD_DIM] / acc[HEAD_DIM:HEAD_DIM + 1]).T)
    ya_ref[...] = jnp.concatenate(outs, axis=1).astype(BF16)
    outs = []
    for h in range(D_HEADS):
        hs = slice(64 * h, 64 * h + 64)
        outs.append(_softmax_attend(dq_ref[:, hs], dk_ref[:, hs], dv_ref[:, 128 * h:128 * h + 64], jnp.exp))
    yd_ref[...] = jnp.concatenate(outs, axis=1).astype(BF16)


def _ctx_attn(aqt, ak, avt, dq, dk, dv, *, t_lat):
    n = ak.shape[0]
    n_ctx = n - t_lat
    blk = t_lat // n_ctx

    def spec(w):
        return pl.BlockSpec((n_ctx, w), lambda i: (blk, 0))

    def spec_t(w):
        return pl.BlockSpec((w, n_ctx), lambda i: (0, blk))

    return pl.pallas_call(
        _ctx_attn_kernel,
        grid=(1,),
        in_specs=[spec_t(256), spec(256), spec_t(256), spec(256), spec(256), spec(dv.shape[1])],
        out_specs=[spec(256), spec(256)],
        out_shape=[jax.ShapeDtypeStruct((n, BRANCH_WIDTH), BF16)] * 2,
        compiler_params=_cparams(("arbitrary",)),
        name="ctx_attn",
    )(aqt, ak, avt, dq, dk, dv)


def _flash_kernel(qt_ref, k_ref, vt_ref, prev_ref, o_ref, m_ref, acc_ref, *bufs, tk, tq):
    del prev_ref
    s_refs, p_refs = bufs[:FLASH_NBUF], bufs[FLASH_NBUF:]
    j = pl.program_id(1)

    @pl.when(j == 0)
    def _():
        m_ref[...] = jnp.full(m_ref.shape, -jnp.inf, F32)
        acc_ref[...] = jnp.zeros(acc_ref.shape, F32)

    group = A_HEADS // A_KV_HEADS
    n_qc = tq // FLASH_QC
    n_sb = tk // FLASH_SB
    chains = [(h, c) for h in range(A_HEADS) for c in range(n_qc)]

    def score_block(ci, sb):
        h, c = chains[ci]
        g = h // group
        rows = slice(sb * FLASH_SB, (sb + 1) * FLASH_SB)
        st = _dot(k_ref[rows, 128 * g:128 * g + 64], qt_ref[64 * h:64 * h + 64, c * FLASH_QC:(c + 1) * FLASH_QC])
        s_refs[ci % FLASH_NBUF][rows, :] = st
        return jnp.max(st, axis=0, keepdims=True)

    def exp_block(ci, sb, m_new):
        rows = slice(sb * FLASH_SB, (sb + 1) * FLASH_SB)
        p_refs[ci % 2][rows, :] = jnp.exp2(s_refs[ci % FLASH_NBUF][rows, :] - m_new).astype(BF16)

    def value_block(ci, sb):
        g = chains[ci][0] // group
        rows = slice(sb * FLASH_SB, (sb + 1) * FLASH_SB)
        return _dot(vt_ref[128 * g:128 * g + 128, rows], p_refs[ci % 2][rows, :])

    def fold(a, b):
        return b if a is None else a + b

    def fold_max(a, b):
        return b if a is None else jnp.maximum(a, b)

    n_ch = len(chains)
    blk_max, m_new, alpha = {}, {}, {}
    for stage in range(-1, n_ch + 1):
        c_s, c_e, c_v = stage + 1, stage, stage - 1
        if 0 <= c_e < n_ch:
            h, c = chains[c_e]
            cols = slice(c * FLASH_QC, (c + 1) * FLASH_QC)
            m_old = m_ref[h, :, cols]
            m_new[c_e] = jnp.maximum(m_old, blk_max.pop(c_e))
            m_ref[h, :, cols] = m_new[c_e]
            alpha[c_e] = jnp.exp2(m_old - m_new[c_e])
        part, mx = None, None
        for sb in range(n_sb):
            if c_s < n_ch:
                mx = fold_max(mx, score_block(c_s, sb))
            if 0 <= c_e < n_ch:
                exp_block(c_e, sb, m_new[c_e])
            if 0 <= c_v:
                part = fold(part, value_block(c_v, sb))
        if c_s < n_ch:
            blk_max[c_s] = mx
        if 0 <= c_v:
            h, c = chains[c_v]
            cols = slice(c * FLASH_QC, (c + 1) * FLASH_QC)
            acc_ref[h, :, cols] = alpha.pop(c_v) * acc_ref[h, :, cols] + part

    @pl.when(j == pl.num_programs(1) - 1)
    def _():
        outs = []
        for h in range(A_HEADS):
            acc = acc_ref[h]
            outs.append((acc[:HEAD_DIM] / acc[HEAD_DIM:HEAD_DIM + 1]).T)
        o_ref[...] = jnp.concatenate(outs, axis=1).astype(BF16)


def _flash_tk(n):
    for tk in (1280, 768, 512, 256):
        if n % tk == 0:
            return tk
    raise ValueError(f"unsupported key count {n}")


def _gqa_latent(aqt, ak, avt, ya_prev, *, t_lat):
    n = ak.shape[0]
    tq = min(FLASH_TQ, t_lat)
    tk = _flash_tk(n)
    return pl.pallas_call(
        functools.partial(_flash_kernel, tk=tk, tq=tq),
        grid=(t_lat // tq, n // tk),
        in_specs=[
            pl.BlockSpec((256, tq), lambda i, j: (0, i)),
            pl.BlockSpec((tk, 256), lambda i, j: (j, 0)),
            pl.BlockSpec((256, tk), lambda i, j: (0, j)),
            pl.BlockSpec(memory_space=pl.ANY),
        ],
        out_specs=pl.BlockSpec((tq, 256), lambda i, j: (i, 0)),
        out_shape=jax.ShapeDtypeStruct((n, BRANCH_WIDTH), BF16),
        scratch_shapes=[pltpu.VMEM((A_HEADS, 1, tq), F32),
                        pltpu.VMEM((A_HEADS, 128, tq), F32)]
                       + [pltpu.VMEM((tk, FLASH_QC), F32)] * FLASH_NBUF
                       + [pltpu.VMEM((tk, FLASH_QC), BF16)] * 2,
        input_output_aliases={3: 0},
        compiler_params=_cparams(("parallel", "arbitrary")),
        name="gqa_flash",
    )(aqt, ak, avt, ya_prev)


def _na_kernel(q_ref, kp_ref, km_ref, kn_ref, vp_ref, vm_ref, vn_ref, kc_ref, vc_ref, bias_ref, mask_ref, prev_ref,
               o_ref, bm_ref, *, nb):
    del prev_ref
    b = pl.program_id(0)

    @pl.when((b == 0) | (b == 1) | (b == nb - 1))
    def _():
        for h in range(D_HEADS):
            bm_ref[h] = bias_ref[h] + mask_ref[0]

    def scores(h):
        hs = slice(64 * h, 64 * h + 64)
        q = q_ref[:, hs]
        kcat = jnp.concatenate([kp_ref[:, hs], km_ref[:, hs], kn_ref[:, hs]], axis=0)
        return _dot_nt(q, kcat) + bm_ref[h], _dot_nt(q, kc_ref[:, hs])

    def softmax(s_loc, s_ctx):
        m = jnp.maximum(jnp.max(s_loc, axis=-1, keepdims=True), jnp.max(s_ctx, axis=-1, keepdims=True))
        return jnp.exp(s_loc - m).astype(BF16), jnp.exp(s_ctx - m).astype(BF16)

    def attend(h, p_loc, p_ctx):
        vs = slice(128 * h, 128 * h + 128)
        vcat = jnp.concatenate([vp_ref[:, vs], vm_ref[:, vs], vn_ref[:, vs]], axis=0)
        o = _dot(p_loc, vcat) + _dot(p_ctx, vc_ref[:, vs])
        return o[:, :HEAD_DIM] / o[:, HEAD_DIM:HEAD_DIM + 1]

    s = {0: scores(0)}
    p, outs = {}, []
    for h in range(D_HEADS):
        if h + 1 < D_HEADS:
            s[h + 1] = scores(h + 1)
        p[h] = softmax(*s.pop(h))
        if h >= 1:
            outs.append(attend(h - 1, *p.pop(h - 1)))
    outs.append(attend(D_HEADS - 1, *p.pop(D_HEADS - 1)))
    o_ref[...] = jnp.concatenate(outs, axis=1).astype(BF16)


def _na_bias_tables(rel_bias, rows):
    wr = min(NA_WIN_R, rows)
    halo_rows = NA_HALO // GRID_W
    krows = NA_QROWS + 2 * halo_rows
    nb = rows // NA_QROWS
    qc = np.arange(GRID_W)[:, None]
    kc = np.arange(GRID_W)[None, :]
    cs = np.clip(qc - NA_WIN_C // 2, 0, GRID_W - NA_WIN_C)
    in_col = (kc >= cs) & (kc < cs + NA_WIN_C)
    e_col = (kc - qc + (NA_WIN_C - 1))[:, :, None] == np.arange(2 * NA_WIN_C - 1)
    qr_l = np.arange(NA_QROWS)[:, None]
    kr_l = np.arange(krows)[None, :]
    e_row = (kr_l - halo_rows - qr_l + (NA_WIN_R - 1))[:, :, None] == np.arange(2 * NA_WIN_R - 1)
    hi = lax.Precision.HIGHEST
    tmp = jnp.einsum("lhrc,qkr->lhqkc", rel_bias.astype(F32), e_row.astype(np.float32), precision=hi)
    full = jnp.einsum("lhqkc,pjc->lhqpkj", tmp, e_col.astype(np.float32), precision=hi)
    masks = []
    for b in (0, min(1, nb - 1), nb - 1):
        qr = NA_QROWS * b + qr_l
        kr = NA_QROWS * b - halo_rows + kr_l
        rs = np.clip(qr - wr // 2, 0, rows - wr)
        in_row = (kr >= rs) & (kr < rs + wr)
        mask = in_row[:, None, :, None] & in_col[None, :, None, :]
        masks.append(np.where(mask, 0.0, NEG_BIG).reshape(NA_QB, krows * GRID_W))
    return full.reshape(-1, D_HEADS, NA_QB, krows * GRID_W), np.stack(masks).astype(np.float32)


def _na_latent(dq, dk, dv, bias, masks, yd_prev, *, layer, t_lat):
    n = dq.shape[0]
    nb = t_lat // NA_QB
    r = NA_QB // NA_HALO
    last_halo = t_lat // NA_HALO - 1
    ctx_blk = t_lat // (n - t_lat)
    n_ctx = n - t_lat

    def prev_map(b):
        return (jnp.maximum(r * b - 1, 0), 0)

    def next_map(b):
        return (jnp.minimum(r * b + r, last_halo), 0)

    def variant(b):
        return (jnp.where(b == 0, 0, jnp.where(b == nb - 1, 2, 1)), 0, 0)

    def specs(w):
        return (pl.BlockSpec((NA_QB, w), lambda b: (b, 0)), pl.BlockSpec((NA_HALO, w), prev_map),
                pl.BlockSpec((NA_HALO, w), next_map), pl.BlockSpec((n_ctx, w), lambda b: (ctx_blk, 0)))

    main, prev, nxt, ctx = specs(256)
    vmain, vprev, vnxt, vctx = specs(dv.shape[1])
    return pl.pallas_call(
        functools.partial(_na_kernel, nb=nb),
        grid=(nb,),
        in_specs=[main, prev, main, nxt, vprev, vmain, vnxt, ctx, vctx,
                  _layer_spec(bias.shape, layer),
                  pl.BlockSpec((1, NA_QB, NA_QB + 2 * NA_HALO), variant),
                  pl.BlockSpec(memory_space=pl.ANY)],
        out_specs=main,
        out_shape=jax.ShapeDtypeStruct((n, BRANCH_WIDTH), BF16),
        scratch_shapes=[pltpu.VMEM((D_HEADS, NA_QB, NA_QB + 2 * NA_HALO), F32)],
        input_output_aliases={11: 0},
        compiler_params=_cparams(("arbitrary",)),
        name="na_attn",
    )(dq, dk, dk, dk, dv, dv, dv, dk, dv, bias, masks, yd_prev)


def _pool_kernel(x_ref, xp_ref, xn_ref, w_ref, sc_ref, *rest, seg_lo, seg_hi, tm):
    o_ref, cat_ref, a_ref, b_ref = rest[-4:]
    start = seg_lo + pl.program_id(0) * tm
    x = x_ref[...]
    cat_ref[0:8, :] = jnp.where(start > seg_lo, xp_ref[...], 0.0)
    cat_ref[8:8 + tm, :] = x
    cat_ref[8 + tm:16 + tm, :] = jnp.where(start + tm < seg_hi, xn_ref[...], 0.0)
    cat_ref[16 + tm:32 + tm, :] = jnp.zeros((16, BRANCH_WIDTH), F32)
    a_ref[0:tm + 24, :] = cat_ref[0:tm + 24, :] + cat_ref[1:tm + 25, :]
    s2 = a_ref[7:7 + tm, :]
    b_ref[0:tm + 16, :] = a_ref[0:tm + 16, :] + a_ref[2:tm + 18, :]
    s4 = b_ref[6:6 + tm, :]
    a_ref[0:tm + 8, :] = b_ref[0:tm + 8, :] + b_ref[4:tm + 12, :]
    s8 = a_ref[4:4 + tm, :]
    s16 = a_ref[0:tm, :] + a_ref[8:8 + tm, :]

    pos = start - seg_lo + lax.broadcasted_iota(jnp.int32, (tm, 1), 0)
    seg_len = seg_hi - seg_lo

    def mean(sm, w):
        lo = jnp.clip(pos - w // 2, 0, seg_len)
        hi = jnp.clip(pos - w // 2 + w, 0, seg_len)
        return sm / (hi - lo).astype(F32)

    lane = lax.broadcasted_iota(jnp.int32, (tm, BRANCH_WIDTH), 1)
    gw = BRANCH_WIDTH // len(C_WINDOWS)
    pooled = jnp.where(lane < gw, mean(s2, 2),
                       jnp.where(lane < 2 * gw, mean(s4, 4),
                                 jnp.where(lane < 3 * gw, mean(s8, 8), mean(s16, 16)))) - x
    o_ref[...] = (_dot(pooled.astype(BF16), w_ref[...]) * sc_ref[...]).astype(BF16)


def _pool_segment(cx, w_bd, scale, prev_out, *, seg_lo, seg_hi, tm):
    n = cx.shape[0]
    r = tm // 8
    b0 = seg_lo // tm
    last8 = n // 8 - 1
    in_specs = [
        pl.BlockSpec((tm, 256), lambda i: (b0 + i, 0)),
        pl.BlockSpec((8, 256), lambda i: (jnp.maximum(r * (b0 + i) - 1, 0), 0)),
        pl.BlockSpec((8, 256), lambda i: (jnp.minimum(r * (b0 + i) + r, last8), 0)),
        _const_spec((256, 256)),
        _const_spec((1, 256)),
    ]
    args = [cx, cx, cx, w_bd, scale]
    aliases = {}
    if prev_out is not None:
        in_specs.append(pl.BlockSpec(memory_space=pl.ANY))
        args.append(prev_out)
        aliases = {5: 0}
    return pl.pallas_call(
        functools.partial(_pool_kernel, seg_lo=seg_lo, seg_hi=seg_hi, tm=tm),
        grid=((seg_hi - seg_lo) // tm,),
        in_specs=in_specs,
        out_specs=pl.BlockSpec((tm, 256), lambda i: (b0 + i, 0)),
        out_shape=jax.ShapeDtypeStruct((n, BRANCH_WIDTH), BF16),
        scratch_shapes=[pltpu.VMEM((tm + 32, 256), F32)] * 3,
        input_output_aliases=aliases,
        compiler_params=_cparams(("parallel",)),
        name="pool_mixer",
    )(*args)


def _pool_mixer(cx, w_bd, scale, *, t_lat):
    n = cx.shape[0]
    n_ctx = n - t_lat
    out = _pool_segment(cx, w_bd, scale, None, seg_lo=t_lat, seg_hi=n, tm=n_ctx)
    return _pool_segment(cx, w_bd, scale, out, seg_lo=0, seg_hi=t_lat, tm=min(POOL_TILE, t_lat))


def _hgrn_tables(rev):
    c_len, s, w = HG_CHUNK, HG_SUB, B_HEADS * B_KEY_DIM
    t = np.arange(c_len)
    tri = (t[None, :] >= t[:, None]) if rev else (t[None, :] <= t[:, None])
    pos = np.arange(s)
    keep = (pos[None, :] <= pos[:, None]) if rev else (pos[None, :] >= pos[:, None])
    keep_add = np.where(keep, 0.0, NEG_BIG)[:, :, None] * np.ones((1, 1, w))
    head_of_lane = np.arange(w) // B_KEY_DIM
    col = np.arange(B_HEADS * s)
    sel = (col[None, None, :] == (head_of_lane[None, :, None] * s + pos[:, None, None]))
    bd = (col[:, None] // s) == head_of_lane[None, :]
    hh = head_of_lane[:, None] == head_of_lane[None, :]
    return (jnp.asarray(tri, BF16), jnp.asarray(keep_add, F32), jnp.asarray(sel, BF16),
            jnp.asarray(bd, F32), jnp.asarray(hh, F32))


def _hgrn_kernel(q_ref, f_ref, v_ref, lb_ref, tri_ref, keep_ref, sel_ref, bd_ref, hh_ref, o_ref, st_ref, *, rev):
    @pl.when(pl.program_id(0) == 0)
    def _():
        st_ref[...] = jnp.zeros(st_ref.shape, F32)

    st = st_ref[...]
    order = range(HG_BLOCK // HG_CHUNK)
    for ch in (reversed(order) if rev else order):
        rows = slice(ch * HG_CHUNK, (ch + 1) * HG_CHUNK)
        o, st = _hgrn_chunk(q_ref[rows, :], f_ref[rows, :], v_ref[rows, :], st, lb_ref, tri_ref, keep_ref,
                            sel_ref, bd_ref, hh_ref, rev=rev)
        o_ref[rows, :] = o
    st_ref[...] = st


def _hgrn_chunk(q_bf, f_pre, v, st, lb_ref, tri_ref, keep_ref, sel_ref, bd_ref, hh_ref, *, rev):
    c_len = HG_CHUNK
    n_sub = c_len // HG_SUB
    w = B_HEADS * B_KEY_DIM
    lb = lb_ref[...]
    f = lb + (1.0 - lb) * _sigmoid(f_pre)
    k = 1.0 - f
    lf = jnp.log(f)
    q = q_bf.astype(F32)
    bd = bd_ref[...]

    c = _split_dot_left(tri_ref[...], lf)

    q3 = q.reshape(n_sub, HG_SUB, w)
    k3 = k.reshape(n_sub, HG_SUB, w)
    c3 = c.reshape(n_sub, HG_SUB, w)
    a_diag = None
    for sg in range(HG_SUB):
        ks = jnp.broadcast_to(k3[:, sg:sg + 1, :], k3.shape)
        cs = jnp.broadcast_to(c3[:, sg:sg + 1, :], c3.shape)
        wgt = (q3 * ks) * jnp.exp((c3 - cs) + keep_ref[sg][None])
        part = _dot(wgt.reshape(c_len, w).astype(BF16), sel_ref[sg])
        a_diag = part if a_diag is None else a_diag + part

    o_parts = [None] * n_sub
    for j in range(n_sub):
        r0 = j * HG_SUB
        r_last = r0 if rev else r0 + HG_SUB - 1
        e_j = c[r_last:r_last + 1, :]
        kj = k[r0:r0 + HG_SUB, :] * jnp.exp(e_j - c[r0:r0 + HG_SUB, :])
        kbd = (jnp.concatenate([kj] * B_HEADS, axis=0) * bd).astype(BF16)
        vbd = (jnp.concatenate([v[r0:r0 + HG_SUB, :].astype(F32)] * B_HEADS, axis=0) * bd).astype(BF16)
        lo, hi = (0, r0) if rev else (r0 + HG_SUB, c_len)
        pieces = [a_diag[r0:r0 + HG_SUB]]
        if hi > lo:
            qj = (q[lo:hi] * jnp.exp(c[lo:hi] - e_j)).astype(BF16)
            a_off = _dot_nt(qj, kbd)
            pieces = [a_off] + pieces if rev else pieces + [a_off]
        first = 0 if rev else j
        if (sum(p.shape[0] for p in pieces) % 16) != 0:
            pad = jnp.zeros((HG_SUB, a_diag.shape[1]), F32)
            pieces = pieces + [pad] if rev else [pad] + pieces
            first = first if rev else first - 1
        a_j = jnp.concatenate(pieces, axis=0).astype(BF16)
        contrib = _dot(a_j, vbd)
        for i in range(contrib.shape[0] // HG_SUB):
            piece = contrib[i * HG_SUB:(i + 1) * HG_SUB]
            o_parts[first + i] = piece if o_parts[first + i] is None else o_parts[first + i] + piece
    o = jnp.concatenate(o_parts, axis=0)

    o = o + _dot_nt((q * jnp.exp(c)).astype(BF16), st.astype(BF16))
    r_end = 0 if rev else c_len - 1
    c_end = c[r_end:r_end + 1, :]
    k_end = (k * jnp.exp(c_end - c)).astype(BF16)
    return o, jnp.exp(c_end) * st + _dot_tn(v, k_end) * hh_ref[...]


def _hgrn_dir(bq, f_pre, bi, lb, *, t_lat, rev):
    n = bq.shape[0]
    c_len = HG_BLOCK
    n_lat = t_lat // c_len
    n_all = n // c_len
    n_ctx = n_all - n_lat

    if rev:
        def blk(i):
            return (jnp.where(i < n_ctx, n_all - 1 - i, n_lat - 1 - (i - n_ctx)), 0)
    else:
        def blk(i):
            return (jnp.where(i < n_ctx, n_lat + i, i - n_ctx), 0)

    tile = pl.BlockSpec((c_len, 256), blk)
    tables = _hgrn_tables(rev)
    in_specs = [tile, tile, tile, _const_spec((1, 256))] + [_const_spec(t.shape) for t in tables]
    args = [bq, f_pre, bi, lb, *tables]
    return pl.pallas_call(
        functools.partial(_hgrn_kernel, rev=rev),
        grid=(n_all,),
        in_specs=in_specs,
        out_specs=tile,
        out_shape=jax.ShapeDtypeStruct((n, BRANCH_WIDTH), F32),
        scratch_shapes=[pltpu.VMEM((256, 256), F32)],
        compiler_params=_cparams(("arbitrary",)),
        name="hgrn_rev" if rev else "hgrn_fwd",
    )(*args)


def _merge_kernel(x_ref, mods_ref, ya_ref, of_ref, or_ref, bg_ref, gain_ref, segm_ref, yc_ref, yd_ref, gt_ref,
                  wb_ref, wo_ref, nrm_ref, wg_ref, wu_ref, wd_ref, *rest, t_lat, tm):
    fin_ref = rest[0] if len(rest) == 2 else None
    o_ref = rest[-1]
    i = pl.program_id(0)
    row = i * tm + lax.broadcasted_iota(jnp.int32, (tm, 1), 0)
    is_lat = row < t_lat
    yb = _head_norm(of_ref[...] + or_ref[...], gain_ref[...], segm_ref[...]) * _silu(bg_ref[...].astype(F32))
    branches = (ya_ref[...], yb.astype(BF16), yc_ref[...], yd_ref[...])
    merged = None
    for n, y in enumerate(branches):
        term = gt_ref[:, n * D_MODEL:(n + 1) * D_MODEL].astype(F32) * _dot(y, wb_ref[n])
        merged = term if merged is None else merged + term
    y = _dot(merged.astype(BF16), wo_ref[...])
    x_mid = x_ref[...] + _row_mod(mods_ref, 5, is_lat) * y
    out = _ffn_math(x_mid, is_lat, mods_ref, nrm_ref, wg_ref, wu_ref, wd_ref, 6)
    o_ref[...] = out if fin_ref is None else _rms_rows(out) * fin_ref[...]


def _merge_ffn(x_all, mods, ya, o_fwd, o_rev, bg, b_gain, segm, yc, yd, gates, wb, wo, nrm, wg, wu, wd, *,
               layer, t_lat, n_rows, final_gain=None):
    n, d = x_all.shape
    tm = TOKEN_TILE
    final = final_gain is not None

    def tile(w):
        return pl.BlockSpec((tm, w), lambda i: (i, 0))

    return pl.pallas_call(
        functools.partial(_merge_kernel, t_lat=t_lat, tm=tm),
        grid=(n_rows // tm,),
        in_specs=[tile(d), _const_spec((2, N_MOD, d)), tile(256), tile(256), tile(256), tile(256),
                  _const_spec((1, 256)), _const_spec((256, 256)), tile(256), tile(256),
                  tile(N_BRANCH * d), _layer_spec(wb.shape, layer), _layer_spec(wo.shape, layer),
                  _const_spec((1, d)), _layer_spec(wg.shape, layer), _layer_spec(wu.shape, layer),
                  _layer_spec(wd.shape, layer)] + ([_const_spec((1, d))] if final else []),
        out_specs=tile(d),
        out_shape=jax.ShapeDtypeStruct((n_rows if final else n, d), F32),
        input_output_aliases={} if final else {0: 0},
        compiler_params=_cparams(("parallel",)),
        name="merge_ffn",
    )(x_all, mods, ya, o_fwd, o_rev, bg, b_gain, segm, yc, yd, gates, wb, wo, nrm, wg, wu, wd,
      *([final_gain] if final else []))


def _rope_tables_padded(t_lat, n_ctx):
    rows = t_lat // GRID_W
    half = HEAD_DIM // 2
    nf = half // 2
    inv = 1.0 / (ROPE_THETA ** (jnp.arange(0, half, 2, dtype=F32) / half))
    ang_r = jnp.arange(rows, dtype=F32)[:, None] * inv
    ang_c = jnp.arange(GRID_W, dtype=F32)[:, None] * inv

    def over_rows(a):
        return jnp.broadcast_to(a[:, None, :], (rows, GRID_W, nf))

    def over_cols(a):
        return jnp.broadcast_to(a[None, :, :], (rows, GRID_W, nf))

    cos = jnp.concatenate([over_rows(jnp.cos(ang_r))] * 2 + [over_cols(jnp.cos(ang_c))] * 2, axis=2)
    sin = jnp.concatenate([over_rows(-jnp.sin(ang_r)), over_rows(jnp.sin(ang_r)),
                           over_cols(-jnp.sin(ang_c)), over_cols(jnp.sin(ang_c))], axis=2)
    cos = cos.reshape(t_lat, HEAD_DIM)
    sin = sin.reshape(t_lat, HEAD_DIM)
    cos = jnp.concatenate([cos, jnp.ones((n_ctx, HEAD_DIM), F32)], axis=0)
    sin = jnp.concatenate([sin, jnp.zeros((n_ctx, HEAD_DIM), F32)], axis=0)
    return jnp.tile(cos, (1, 2)), jnp.tile(sin, (1, 2))


def _block_diag(w_group):
    g, ci, co = w_group.shape
    out = jnp.zeros((g * ci, g * co), w_group.dtype)
    for n in range(g):
        out = out.at[n * ci:(n + 1) * ci, n * co:(n + 1) * co].set(w_group[n])
    return out


def kernel(x, c, ctx, c_ctx, w_ada, b_ada, ffn1_norm, ffn1_w_gate, ffn1_w_up, ffn1_w_down, mix_norm, w_in, a_q_norm, a_k_norm, b_lb_logits, b_o_norm, c_w_group, c_scale, d_rel_bias, w_branch, w_out, ffn2_norm, ffn2_w_gate, ffn2_w_up, ffn2_w_down, final_norm):
    assert x.shape[0] == 1 and ctx.shape[0] == 1
    depth = w_ada.shape[0]
    t_lat = x.shape[1]
    n_ctx = ctx.shape[1]
    n = t_lat + n_ctx
    rows = t_lat // GRID_W
    assert t_lat % max(NA_QB, min(FLASH_TQ, t_lat), n_ctx) == 0 and n_ctx % TOKEN_TILE == 0 and rows >= 2 * NA_QROWS

    x_all = None
    mods_all = _ada_mods(c, c_ctx, w_ada, b_ada)
    cos, sin = _rope_tables_padded(t_lat, n_ctx)
    segm = (jnp.kron(jnp.eye(BRANCH_WIDTH // HEAD_DIM), jnp.ones((HEAD_DIM, HEAD_DIM))) / HEAD_DIM).astype(BF16)
    lb_all = jnp.cumsum(jax.nn.softmax(b_lb_logits.astype(F32), axis=0), axis=0)
    lb_all = lb_all - lb_all[:1]
    wg1, wu1, wd1 = ffn1_w_gate.astype(BF16), ffn1_w_up.astype(BF16), ffn1_w_down.astype(BF16)
    wg2, wu2, wd2 = ffn2_w_gate.astype(BF16), ffn2_w_up.astype(BF16), ffn2_w_down.astype(BF16)
    w_in_b, w_branch_b, w_out_b = w_in.astype(BF16), w_branch.astype(BF16), w_out.astype(BF16)
    na_bias, na_masks = _na_bias_tables(d_rel_bias, rows)

    for l in range(depth):
        with_ctx_out = l < depth - 1
        n_rows = n if with_ctx_out else t_lat
        mods = mods_all[l]
        x_all = _ffn_half(x[0] if l == 0 else x_all, mods, ffn1_norm[l][None], wg1, wu1, wd1, layer=l, t_lat=t_lat,
                          k0=0, n_rows=n, x_ctx=ctx[0] if l == 0 else None)
        (aq, ak, av, bq, bff, bfb, bi, bg, cx, dq, dk, dv, gates) = _in_proj(
            x_all, mods, mix_norm[l][None], w_in_b, cos, sin,
            jnp.tile(a_q_norm[l], A_HEADS)[None], jnp.tile(a_k_norm[l], A_KV_HEADS)[None], segm, layer=l, t_lat=t_lat)
        if with_ctx_out:
            ya, yd = _ctx_attn(aq, ak, av, dq, dk, dv, t_lat=t_lat)
        else:
            ya = jnp.zeros((n, BRANCH_WIDTH), BF16)
            yd = ya
        ya = _gqa_latent(aq, ak, av, ya, t_lat=t_lat)
        yd = _na_latent(dq, dk, dv, na_bias, na_masks, yd, layer=l, t_lat=t_lat)
        yc = _pool_mixer(cx, _block_diag(c_w_group[l]).astype(BF16), c_scale[l][None], t_lat=t_lat)
        o_fwd = _hgrn_dir(bq, bff, bi, lb_all[l, 0][None], t_lat=t_lat, rev=False)
        o_rev = _hgrn_dir(bq, bfb, bi, lb_all[l, 1][None], t_lat=t_lat, rev=True)
        x_all = _merge_ffn(x_all, mods, ya, o_fwd, o_rev, bg, jnp.tile(b_o_norm[l], B_HEADS)[None], segm, yc, yd,
                           gates, w_branch_b, w_out_b, ffn2_norm[l][None], wg2, wu2, wd2, layer=l, t_lat=t_lat,
                           n_rows=n_rows, final_gain=None if with_ctx_out else final_norm[None])
    return x_all[None]
```

```python
import functools

import numpy as np
import jax
import jax.numpy as jnp
from jax import lax
from jax.experimental import pallas as pl
from jax.experimental.pallas import tpu as pltpu

F32 = jnp.float32
BF16 = jnp.bfloat16

D_MODEL = 1024
GRID_W = 64
HEAD_DIM = 64
BRANCH_WIDTH = 256
N_BRANCH = 4
A_HEADS = 4
A_KV_HEADS = 2
ROPE_THETA = 10000.0
B_HEADS = 4
B_KEY_DIM = 64
C_WINDOWS = (2, 4, 8, 16)
D_HEADS = 4
NA_WIN_R = 8
NA_WIN_C = 16
D_FF = 2816
N_MOD = 9
EPS = 1e-6
ATTN_SCALE = HEAD_DIM ** -0.5
LOG2_E = 1.4426950408889634
NEG_BIG = -1e30

V7X_VMEM_BYTES = 64 * 1024 * 1024
VMEM_LIMIT = V7X_VMEM_BYTES - 8 * 1024 * 1024

SEG_AQ = (0, 256)
SEG_AK = (256, 384)
SEG_AV = (384, 512)
SEG_BQ = (512, 768)
SEG_BFF = (768, 1024)
SEG_BFB = (1024, 1280)
SEG_BI = (1280, 1536)
SEG_BG = (1536, 1792)
SEG_CX = (1792, 2048)
SEG_DQ = (2048, 2304)
SEG_DK = (2304, 2560)
SEG_DV = (2560, 2816)
SEG_GATE = (2816, 2816 + N_BRANCH * D_MODEL)
IN_WIDTH = SEG_GATE[1]

TOKEN_TILE = 256
POOL_TILE = 2048
FLASH_TQ = 4096
FLASH_QC = 512
FLASH_SB = 256
FLASH_NBUF = 4
NA_QROWS = 8
NA_QB = NA_QROWS * GRID_W
NA_HALO = 256
HG_BLOCK = 256
HG_CHUNK = 128
HG_SUB = 8


def _cparams(sem):
    return pltpu.CompilerParams(dimension_semantics=sem, vmem_limit_bytes=VMEM_LIMIT)


def _const_spec(shape):
    nd = len(shape)
    return pl.BlockSpec(shape, lambda *_: (0,) * nd, pipeline_mode=pl.Buffered(1))


def _layer_spec(shape, layer):
    nd = len(shape)
    return pl.BlockSpec((None,) + tuple(shape[1:]), lambda *_: (layer,) + (0,) * (nd - 1),
                        pipeline_mode=pl.Buffered(1))


def _sigmoid(x):
    return 1.0 / (1.0 + jnp.exp(-x))


def _silu(x):
    return x * _sigmoid(x)


def _dot(a, b):
    return jnp.dot(a, b, preferred_element_type=F32)


def _dot_nt(a, b):
    return lax.dot_general(a, b, (((1,), (1,)), ((), ())), preferred_element_type=F32)


def _dot_tn(a, b):
    return lax.dot_general(a, b, (((0,), (0,)), ((), ())), preferred_element_type=F32)


def _split_dot(x, m):
    hi = x.astype(BF16)
    r1 = x - hi.astype(F32)
    mid = r1.astype(BF16)
    lo = (r1 - mid.astype(F32)).astype(BF16)
    return _dot(hi, m) + _dot(mid, m) + _dot(lo, m)


def _split_dot_left(m, x):
    hi = x.astype(BF16)
    r1 = x - hi.astype(F32)
    mid = r1.astype(BF16)
    lo = (r1 - mid.astype(F32)).astype(BF16)
    return _dot(m, hi) + _dot(m, mid) + _dot(m, lo)


def _row_mod(mods_ref, k, is_lat):
    return jnp.where(is_lat, mods_ref[0, k:k + 1, :], mods_ref[1, k:k + 1, :])


def _rms_rows(x):
    return x * lax.rsqrt(jnp.mean(x * x, axis=-1, keepdims=True) + EPS)


def _ada_kernel(s_ref, w_ref, b_ref, o_ref):
    o_ref[0] = _dot(_silu(s_ref[...]).astype(BF16), w_ref[0].astype(BF16)) + b_ref[0]


def _ada_mods(c, c_ctx, w_ada, b_ada):
    depth = w_ada.shape[0]
    width = w_ada.shape[2]
    tn = 1024
    s = jnp.zeros((8, D_MODEL), F32).at[0].set(c[0]).at[1].set(c_ctx)
    out = pl.pallas_call(
        _ada_kernel,
        grid=(depth, width // tn),
        in_specs=[
            pl.BlockSpec((8, D_MODEL), lambda l, j: (0, 0)),
            pl.BlockSpec((1, D_MODEL, tn), lambda l, j: (l, 0, j)),
            pl.BlockSpec((1, 1, tn), lambda l, j: (l, 0, j)),
        ],
        out_specs=pl.BlockSpec((1, 8, tn), lambda l, j: (l, 0, j)),
        out_shape=jax.ShapeDtypeStruct((depth, 8, width), F32),
        compiler_params=_cparams(("parallel", "parallel")),
        name="ada_mods",
    )(s, w_ada, b_ada.reshape(depth, 1, width))
    return out[:, :2].reshape(depth, 2, N_MOD, D_MODEL)


def _ffn_math(x, is_lat, mods_ref, nrm_ref, wg_ref, wu_ref, wd_ref, k0):
    z = _rms_rows(x) * nrm_ref[...]
    z = (z * (1.0 + _row_mod(mods_ref, k0 + 1, is_lat)) + _row_mod(mods_ref, k0, is_lat)).astype(BF16)
    g = _dot(z, wg_ref[...])
    u = _dot(z, wu_ref[...])
    a = (_silu(g) * u).astype(BF16)
    return x + 0.5 * _row_mod(mods_ref, k0 + 2, is_lat) * _dot(a, wd_ref[...])


def _ffn_kernel(*refs, t_lat, tm, k0, split_input):
    if split_input:
        xl_ref, xc_ref, mods_ref, nrm_ref, wg_ref, wu_ref, wd_ref, o_ref = refs
    else:
        x_ref, mods_ref, nrm_ref, wg_ref, wu_ref, wd_ref, o_ref = refs
    i = pl.program_id(0)
    row = i * tm + lax.broadcasted_iota(jnp.int32, (tm, 1), 0)
    is_lat = row < t_lat
    x = jnp.where(i * tm < t_lat, xl_ref[...], xc_ref[...]) if split_input else x_ref[...]
    o_ref[...] = _ffn_math(x, is_lat, mods_ref, nrm_ref, wg_ref, wu_ref, wd_ref, k0)


def _ffn_half(x_all, mods, nrm, wg, wu, wd, *, layer, t_lat, k0, n_rows, x_ctx=None):
    split_input = x_ctx is not None
    d = x_all.shape[1]
    n = x_all.shape[0] + (x_ctx.shape[0] if split_input else 0)
    tm = TOKEN_TILE
    if split_input:
        n_lat_tiles = t_lat // tm
        x_specs = [pl.BlockSpec((tm, d), lambda i: (jnp.minimum(i, n_lat_tiles - 1), 0)),
                   pl.BlockSpec((tm, d), lambda i: (jnp.maximum(i - n_lat_tiles, 0), 0))]
        x_args = [x_all, x_ctx]
    else:
        x_specs = [pl.BlockSpec((tm, d), lambda i: (i, 0))]
        x_args = [x_all]
    return pl.pallas_call(
        functools.partial(_ffn_kernel, t_lat=t_lat, tm=tm, k0=k0, split_input=split_input),
        grid=(n_rows // tm,),
        in_specs=x_specs + [
            _const_spec((2, N_MOD, d)),
            _const_spec((1, d)),
            _layer_spec(wg.shape, layer),
            _layer_spec(wu.shape, layer),
            _layer_spec(wd.shape, layer),
        ],
        out_specs=pl.BlockSpec((tm, d), lambda i: (i, 0)),
        out_shape=jax.ShapeDtypeStruct((n, d), F32),
        input_output_aliases={} if split_input else {0: 0},
        compiler_params=_cparams(("parallel",)),
        name="ffn_half",
    )(*x_args, mods, nrm, wg, wu, wd)


def _head_norm(y, gain, segm):
    ms = _split_dot(y * y, segm)
    return y * lax.rsqrt(ms + EPS) * gain


def _rope(y, cos, sin_signed):
    w = y.shape[1]
    lane = lax.broadcasted_iota(jnp.int32, y.shape, 1)
    partner = jnp.where((lane & 31) < 16, pltpu.roll(y, w - 16, 1), pltpu.roll(y, 16, 1))
    return y * cos + partner * sin_signed


def _inproj_kernel(x_ref, mods_ref, nrm_ref, w_ref, cos_ref, sin_ref, gq_ref, gk_ref, segm_ref,
                   aq_ref, ak_ref, av_ref, bq_ref, bff_ref, bfb_ref, bi_ref, bg_ref, cx_ref,
                   dq_ref, dk_ref, dv_ref, gt_ref, *, t_lat, tm):
    i = pl.program_id(0)
    row = i * tm + lax.broadcasted_iota(jnp.int32, (tm, 1), 0)
    is_lat = row < t_lat
    z = _rms_rows(x_ref[...]) * nrm_ref[...]
    z = (z * (1.0 + _row_mod(mods_ref, 4, is_lat)) + _row_mod(mods_ref, 3, is_lat)).astype(BF16)

    def proj(seg):
        return _dot(z, w_ref[:, seg[0]:seg[1]])

    cos = cos_ref[...]
    sin = sin_ref[...]
    q = _rope(_head_norm(proj(SEG_AQ), gq_ref[...], segm_ref[...]),
              jnp.concatenate([cos, cos], axis=1), jnp.concatenate([sin, sin], axis=1))
    aq_ref[...] = (q * (ATTN_SCALE * LOG2_E)).T.astype(BF16)
    k = _rope(_head_norm(proj(SEG_AK), gk_ref[...], segm_ref[0:128, 0:128]), cos, sin)
    zeros = jnp.zeros((tm, HEAD_DIM), F32)
    ak_ref[...] = jnp.concatenate([k[:, :HEAD_DIM], zeros, k[:, HEAD_DIM:], zeros], axis=1).astype(BF16)
    v = proj(SEG_AV)
    one_hot = jnp.where(lax.broadcasted_iota(jnp.int32, (tm, HEAD_DIM), 1) == 0, 1.0, 0.0)
    av_ref[...] = jnp.concatenate([v[:, :HEAD_DIM], one_hot, v[:, HEAD_DIM:], one_hot], axis=1).T.astype(BF16)

    bq_ref[...] = (proj(SEG_BQ) * (B_KEY_DIM ** -0.5)).astype(BF16)
    bff_ref[...] = proj(SEG_BFF)
    bfb_ref[...] = proj(SEG_BFB)
    bi_ref[...] = proj(SEG_BI).astype(BF16)
    bg_ref[...] = proj(SEG_BG).astype(BF16)
    cx_ref[...] = proj(SEG_CX)
    dq_ref[...] = (proj(SEG_DQ) * ATTN_SCALE).astype(BF16)
    dk_ref[...] = proj(SEG_DK).astype(BF16)
    dv = proj(SEG_DV)
    dv_ref[...] = jnp.concatenate(
        [piece for h in range(D_HEADS) for piece in (dv[:, HEAD_DIM * h:HEAD_DIM * (h + 1)], one_hot)],
        axis=1).astype(BF16)
    gw = 512
    for c0 in range(SEG_GATE[0], SEG_GATE[1], gw):
        o0 = c0 - SEG_GATE[0]
        gt_ref[:, o0:o0 + gw] = _sigmoid(proj((c0, c0 + gw))).astype(BF16)


def _in_proj(x_all, mods, nrm, w_in, cos, sin, gq, gk, segm, *, layer, t_lat):
    n, d = x_all.shape
    tm = TOKEN_TILE
    widths = [(256, BF16), (256, BF16), (256, BF16), (256, BF16), (256, F32), (256, F32), (256, BF16),
              (256, BF16), (256, F32), (256, BF16), (256, BF16), (2 * D_HEADS * HEAD_DIM, BF16),
              (N_BRANCH * D_MODEL, BF16)]
    transposed = (0, 2)
    out_specs = [pl.BlockSpec((w, tm), lambda i: (0, i)) if o in transposed else pl.BlockSpec((tm, w), lambda i: (i, 0))
                 for o, (w, _) in enumerate(widths)]
    out_shape = [jax.ShapeDtypeStruct((w, n) if o in transposed else (n, w), dt) for o, (w, dt) in enumerate(widths)]
    return pl.pallas_call(
        functools.partial(_inproj_kernel, t_lat=t_lat, tm=tm),
        grid=(n // tm,),
        in_specs=[
            pl.BlockSpec((tm, d), lambda i: (i, 0)),
            _const_spec((2, N_MOD, d)),
            _const_spec((1, d)),
            _layer_spec(w_in.shape, layer),
            pl.BlockSpec((tm, 128), lambda i: (i, 0)),
            pl.BlockSpec((tm, 128), lambda i: (i, 0)),
            _const_spec((1, 256)),
            _const_spec((1, 128)),
            _const_spec((256, 256)),
        ],
        out_specs=out_specs,
        out_shape=out_shape,
        compiler_params=_cparams(("parallel",)),
        name="in_proj",
    )(x_all, mods, nrm, w_in, cos, sin, gq, gk, segm)


def _softmax_attend(q, k, v, exp_fn):
    s = _dot_nt(q, k)
    p = exp_fn(s - jnp.max(s, axis=-1, keepdims=True))
    return _dot(p.astype(BF16), v) / jnp.sum(p, axis=-1, keepdims=True)


def _ctx_attn_kernel(aqt_ref, ak_ref, avt_ref, dq_ref, dk_ref, dv_ref, ya_ref, yd_ref):
    outs = []
    for h in range(A_HEADS):
        g = h // (A_HEADS // A_KV_HEADS)
        st = _dot(ak_ref[:, 128 * g:128 * g + 64], aqt_ref[64 * h:64 * h + 64, :])
        pt = jnp.exp2(st - jnp.max(st, axis=0, keepdims=True)).astype(BF16)
        acc = _dot(avt_ref[128 * g:128 * g + 128, :], pt)
        outs.append((acc[:HEAD_DIM] / acc[HEAD_DIM:HEAD_DIM + 1]).T)
    ya_ref[...] = jnp.concatenate(outs, axis=1).astype(BF16)
    outs = []
    for h in range(D_HEADS):
        hs = slice(64 * h, 64 * h + 64)
        outs.append(_softmax_attend(dq_ref[:, hs], dk_ref[:, hs], dv_ref[:, 128 * h:128 * h + 64], jnp.exp))
    yd_ref[...] = jnp.concatenate(outs, axis=1).astype(BF16)


def _ctx_attn(aqt, ak, avt, dq, dk, dv, *, t_lat):
    n = ak.shape[0]
    n_ctx = n - t_lat
    blk = t_lat // n_ctx

    def spec(w):
        return pl.BlockSpec((n_ctx, w), lambda i: (blk, 0))

    def spec_t(w):
        return pl.BlockSpec((w, n_ctx), lambda i: (0, blk))

    return pl.pallas_call(
        _ctx_attn_kernel,
        grid=(1,),
        in_specs=[spec_t(256), spec(256), spec_t(256), spec(256), spec(256), spec(dv.shape[1])],
        out_specs=[spec(256), spec(256)],
        out_shape=[jax.ShapeDtypeStruct((n, BRANCH_WIDTH), BF16)] * 2,
        compiler_params=_cparams(("arbitrary",)),
        name="ctx_attn",
    )(aqt, ak, avt, dq, dk, dv)


def _flash_kernel(qt_ref, k_ref, vt_ref, prev_ref, o_ref, m_ref, acc_ref, *bufs, tk, tq):
    del prev_ref
    s_refs, p_refs = bufs[:FLASH_NBUF], bufs[FLASH_NBUF:]
    j = pl.program_id(1)

    @pl.when(j == 0)
    def _():
        m_ref[...] = jnp.full(m_ref.shape, -jnp.inf, F32)
        acc_ref[...] = jnp.zeros(acc_ref.shape, F32)

    group = A_HEADS // A_KV_HEADS
    n_qc = tq // FLASH_QC
    n_sb = tk // FLASH_SB
    chains = [(h, c) for h in range(A_HEADS) for c in range(n_qc)]

    def score_block(ci, sb):
        h, c = chains[ci]
        g = h // group
        rows = slice(sb * FLASH_SB, (sb + 1) * FLASH_SB)
        st = _dot(k_ref[rows, 128 * g:128 * g + 64], qt_ref[64 * h:64 * h + 64, c * FLASH_QC:(c + 1) * FLASH_QC])
        s_refs[ci % FLASH_NBUF][rows, :] = st
        return jnp.max(st, axis=0, keepdims=True)

    def exp_block(ci, sb, m_new):
        rows = slice(sb * FLASH_SB, (sb + 1) * FLASH_SB)
        p_refs[ci % 2][rows, :] = jnp.exp2(s_refs[ci % FLASH_NBUF][rows, :] - m_new).astype(BF16)

    def value_block(ci, sb):
        g = chains[ci][0] // group
        rows = slice(sb * FLASH_SB, (sb + 1) * FLASH_SB)
        return _dot(vt_ref[128 * g:128 * g + 128, rows], p_refs[ci % 2][rows, :])

    def fold(a, b):
        return b if a is None else a + b

    def fold_max(a, b):
        return b if a is None else jnp.maximum(a, b)

    n_ch = len(chains)
    blk_max, m_new, alpha = {}, {}, {}
    for stage in range(-1, n_ch + 1):
        c_s, c_e, c_v = stage + 1, stage, stage - 1
        if 0 <= c_e < n_ch:
            h, c = chains[c_e]
            cols = slice(c * FLASH_QC, (c + 1) * FLASH_QC)
            m_old = m_ref[h, :, cols]
            m_new[c_e] = jnp.maximum(m_old, blk_max.pop(c_e))
            m_ref[h, :, cols] = m_new[c_e]
            alpha[c_e] = jnp.exp2(m_old - m_new[c_e])
        part, mx = None, None
        for sb in range(n_sb):
            if c_s < n_ch:
                mx = fold_max(mx, score_block(c_s, sb))
            if 0 <= c_e < n_ch:
                exp_block(c_e, sb, m_new[c_e])
            if 0 <= c_v:
                part = fold(part, value_block(c_v, sb))
        if c_s < n_ch:
            blk_max[c_s] = mx
        if 0 <= c_v:
            h, c = chains[c_v]
            cols = slice(c * FLASH_QC, (c + 1) * FLASH_QC)
            acc_ref[h, :, cols] = alpha.pop(c_v) * acc_ref[h, :, cols] + part

    @pl.when(j == pl.num_programs(1) - 1)
    def _():
        outs = []
        for h in range(A_HEADS):
            acc = acc_ref[h]
            outs.append((acc[:HEAD_DIM] / acc[HEAD_DIM:HEAD_DIM + 1]).T)
        o_ref[...] = jnp.concatenate(outs, axis=1).astype(BF16)


def _flash_tk(n):
    for tk in (1280, 768, 512, 256):
        if n % tk == 0:
            return tk
    raise ValueError(f"unsupported key count {n}")


def _gqa_latent(aqt, ak, avt, ya_prev, *, t_lat):
    n = ak.shape[0]
    tq = min(FLASH_TQ, t_lat)
    tk = _flash_tk(n)
    return pl.pallas_call(
        functools.partial(_flash_kernel, tk=tk, tq=tq),
        grid=(t_lat // tq, n // tk),
        in_specs=[
            pl.BlockSpec((256, tq), lambda i, j: (0, i)),
            pl.BlockSpec((tk, 256), lambda i, j: (j, 0)),
            pl.BlockSpec((256, tk), lambda i, j: (0, j)),
            pl.BlockSpec(memory_space=pl.ANY),
        ],
        out_specs=pl.BlockSpec((tq, 256), lambda i, j: (i, 0)),
        out_shape=jax.ShapeDtypeStruct((n, BRANCH_WIDTH), BF16),
        scratch_shapes=[pltpu.VMEM((A_HEADS, 1, tq), F32),
                        pltpu.VMEM((A_HEADS, 128, tq), F32)]
                       + [pltpu.VMEM((tk, FLASH_QC), F32)] * FLASH_NBUF
                       + [pltpu.VMEM((tk, FLASH_QC), BF16)] * 2,
        input_output_aliases={3: 0},
        compiler_params=_cparams(("parallel", "arbitrary")),
        name="gqa_flash",
    )(aqt, ak, avt, ya_prev)


def _na_kernel(q_ref, kp_ref, km_ref, kn_ref, vp_ref, vm_ref, vn_ref, kc_ref, vc_ref, bias_ref, mask_ref, prev_ref,
               o_ref, bm_ref, *, nb):
    del prev_ref
    b = pl.program_id(0)

    @pl.when((b == 0) | (b == 1) | (b == nb - 1))
    def _():
        for h in range(D_HEADS):
            bm_ref[h] = bias_ref[h] + mask_ref[0]

    def scores(h):
        hs = slice(64 * h, 64 * h + 64)
        q = q_ref[:, hs]
        kcat = jnp.concatenate([kp_ref[:, hs], km_ref[:, hs], kn_ref[:, hs]], axis=0)
        return _dot_nt(q, kcat) + bm_ref[h], _dot_nt(q, kc_ref[:, hs])

    def softmax(s_loc, s_ctx):
        m = jnp.maximum(jnp.max(s_loc, axis=-1, keepdims=True), jnp.max(s_ctx, axis=-1, keepdims=True))
        return jnp.exp(s_loc - m).astype(BF16), jnp.exp(s_ctx - m).astype(BF16)

    def attend(h, p_loc, p_ctx):
        vs = slice(128 * h, 128 * h + 128)
        vcat = jnp.concatenate([vp_ref[:, vs], vm_ref[:, vs], vn_ref[:, vs]], axis=0)
        o = _dot(p_loc, vcat) + _dot(p_ctx, vc_ref[:, vs])
        return o[:, :HEAD_DIM] / o[:, HEAD_DIM:HEAD_DIM + 1]

    s = {0: scores(0)}
    p, outs = {}, []
    for h in range(D_HEADS):
        if h + 1 < D_HEADS:
            s[h + 1] = scores(h + 1)
        p[h] = softmax(*s.pop(h))
        if h >= 1:
            outs.append(attend(h - 1, *p.pop(h - 1)))
    outs.append(attend(D_HEADS - 1, *p.pop(D_HEADS - 1)))
    o_ref[...] = jnp.concatenate(outs, axis=1).astype(BF16)


def _na_bias_tables(rel_bias, rows):
    wr = min(NA_WIN_R, rows)
    halo_rows = NA_HALO // GRID_W
    krows = NA_QROWS + 2 * halo_rows
    nb = rows // NA_QROWS
    qc = np.arange(GRID_W)[:, None]
    kc = np.arange(GRID_W)[None, :]
    cs = np.clip(qc - NA_WIN_C // 2, 0, GRID_W - NA_WIN_C)
    in_col = (kc >= cs) & (kc < cs + NA_WIN_C)
    e_col = (kc - qc + (NA_WIN_C - 1))[:, :, None] == np.arange(2 * NA_WIN_C - 1)
    qr_l = np.arange(NA_QROWS)[:, None]
    kr_l = np.arange(krows)[None, :]
    e_row = (kr_l - halo_rows - qr_l + (NA_WIN_R - 1))[:, :, None] == np.arange(2 * NA_WIN_R - 1)
    hi = lax.Precision.HIGHEST
    tmp = jnp.einsum("lhrc,qkr->lhqkc", rel_bias.astype(F32), e_row.astype(np.float32), precision=hi)
    full = jnp.einsum("lhqkc,pjc->lhqpkj", tmp, e_col.astype(np.float32), precision=hi)
    masks = []
    for b in (0, min(1, nb - 1), nb - 1):
        qr = NA_QROWS * b + qr_l
        kr = NA_QROWS * b - halo_rows + kr_l
        rs = np.clip(qr - wr // 2, 0, rows - wr)
        in_row = (kr >= rs) & (kr < rs + wr)
        mask = in_row[:, None, :, None] & in_col[None, :, None, :]
        masks.append(np.where(mask, 0.0, NEG_BIG).reshape(NA_QB, krows * GRID_W))
    return full.reshape(-1, D_HEADS, NA_QB, krows * GRID_W), np.stack(masks).astype(np.float32)


def _na_latent(dq, dk, dv, bias, masks, yd_prev, *, layer, t_lat):
    n = dq.shape[0]
    nb = t_lat // NA_QB
    r = NA_QB // NA_HALO
    last_halo = t_lat // NA_HALO - 1
    ctx_blk = t_lat // (n - t_lat)
    n_ctx = n - t_lat

    def prev_map(b):
        return (jnp.maximum(r * b - 1, 0), 0)

    def next_map(b):
        return (jnp.minimum(r * b + r, last_halo), 0)

    def variant(b):
        return (jnp.where(b == 0, 0, jnp.where(b == nb - 1, 2, 1)), 0, 0)

    def specs(w):
        return (pl.BlockSpec((NA_QB, w), lambda b: (b, 0)), pl.BlockSpec((NA_HALO, w), prev_map),
                pl.BlockSpec((NA_HALO, w), next_map), pl.BlockSpec((n_ctx, w), lambda b: (ctx_blk, 0)))

    main, prev, nxt, ctx = specs(256)
    vmain, vprev, vnxt, vctx = specs(dv.shape[1])
    return pl.pallas_call(
        functools.partial(_na_kernel, nb=nb),
        grid=(nb,),
        in_specs=[main, prev, main, nxt, vprev, vmain, vnxt, ctx, vctx,
                  _layer_spec(bias.shape, layer),
                  pl.BlockSpec((1, NA_QB, NA_QB + 2 * NA_HALO), variant),
                  pl.BlockSpec(memory_space=pl.ANY)],
        out_specs=main,
        out_shape=jax.ShapeDtypeStruct((n, BRANCH_WIDTH), BF16),
        scratch_shapes=[pltpu.VMEM((D_HEADS, NA_QB, NA_QB + 2 * NA_HALO), F32)],
        input_output_aliases={11: 0},
        compiler_params=_cparams(("arbitrary",)),
        name="na_attn",
    )(dq, dk, dk, dk, dv, dv, dv, dk, dv, bias, masks, yd_prev)


def _pool_kernel(x_ref, xp_ref, xn_ref, w_ref, sc_ref, *rest, seg_lo, seg_hi, tm):
    o_ref, cat_ref, a_ref, b_ref = rest[-4:]
    start = seg_lo + pl.program_id(0) * tm
    x = x_ref[...]
    cat_ref[0:8, :] = jnp.where(start > seg_lo, xp_ref[...], 0.0)
    cat_ref[8:8 + tm, :] = x
    cat_ref[8 + tm:16 + tm, :] = jnp.where(start + tm < seg_hi, xn_ref[...], 0.0)
    cat_ref[16 + tm:32 + tm, :] = jnp.zeros((16, BRANCH_WIDTH), F32)
    a_ref[0:tm + 24, :] = cat_ref[0:tm + 24, :] + cat_ref[1:tm + 25, :]
    s2 = a_ref[7:7 + tm, :]
    b_ref[0:tm + 16, :] = a_ref[0:tm + 16, :] + a_ref[2:tm + 18, :]
    s4 = b_ref[6:6 + tm, :]
    a_ref[0:tm + 8, :] = b_ref[0:tm + 8, :] + b_ref[4:tm + 12, :]
    s8 = a_ref[4:4 + tm, :]
    s16 = a_ref[0:tm, :] + a_ref[8:8 + tm, :]

    pos = start - seg_lo + lax.broadcasted_iota(jnp.int32, (tm, 1), 0)
    seg_len = seg_hi - seg_lo

    def mean(sm, w):
        lo = jnp.clip(pos - w // 2, 0, seg_len)
        hi = jnp.clip(pos - w // 2 + w, 0, seg_len)
        return sm / (hi - lo).astype(F32)

    lane = lax.broadcasted_iota(jnp.int32, (tm, BRANCH_WIDTH), 1)
    gw = BRANCH_WIDTH // len(C_WINDOWS)
    pooled = jnp.where(lane < gw, mean(s2, 2),
                       jnp.where(lane < 2 * gw, mean(s4, 4),
                                 jnp.where(lane < 3 * gw, mean(s8, 8), mean(s16, 16)))) - x
    o_ref[...] = (_dot(pooled.astype(BF16), w_ref[...]) * sc_ref[...]).astype(BF16)


def _pool_segment(cx, w_bd, scale, prev_out, *, seg_lo, seg_hi, tm):
    n = cx.shape[0]
    r = tm // 8
    b0 = seg_lo // tm
    last8 = n // 8 - 1
    in_specs = [
        pl.BlockSpec((tm, 256), lambda i: (b0 + i, 0)),
        pl.BlockSpec((8, 256), lambda i: (jnp.maximum(r * (b0 + i) - 1, 0), 0)),
        pl.BlockSpec((8, 256), lambda i: (jnp.minimum(r * (b0 + i) + r, last8), 0)),
        _const_spec((256, 256)),
        _const_spec((1, 256)),
    ]
    args = [cx, cx, cx, w_bd, scale]
    aliases = {}
    if prev_out is not None:
        in_specs.append(pl.BlockSpec(memory_space=pl.ANY))
        args.append(prev_out)
        aliases = {5: 0}
    return pl.pallas_call(
        functools.partial(_pool_kernel, seg_lo=seg_lo, seg_hi=seg_hi, tm=tm),
        grid=((seg_hi - seg_lo) // tm,),
        in_specs=in_specs,
        out_specs=pl.BlockSpec((tm, 256), lambda i: (b0 + i, 0)),
        out_shape=jax.ShapeDtypeStruct((n, BRANCH_WIDTH), BF16),
        scratch_shapes=[pltpu.VMEM((tm + 32, 256), F32)] * 3,
        input_output_aliases=aliases,
        compiler_params=_cparams(("parallel",)),
        name="pool_mixer",
    )(*args)


def _pool_mixer(cx, w_bd, scale, *, t_lat):
    n = cx.shape[0]
    n_ctx = n - t_lat
    out = _pool_segment(cx, w_bd, scale, None, seg_lo=t_lat, seg_hi=n, tm=n_ctx)
    return _pool_segment(cx, w_bd, scale, out, seg_lo=0, seg_hi=t_lat, tm=min(POOL_TILE, t_lat))


def _hgrn_tables(rev):
    c_len, s, w = HG_CHUNK, HG_SUB, B_HEADS * B_KEY_DIM
    t = np.arange(c_len)
    tri = (t[None, :] >= t[:, None]) if rev else (t[None, :] <= t[:, None])
    pos = np.arange(s)
    keep = (pos[None, :] <= pos[:, None]) if rev else (pos[None, :] >= pos[:, None])
    keep_add = np.where(keep, 0.0, NEG_BIG)[:, :, None] * np.ones((1, 1, w))
    head_of_lane = np.arange(w) // B_KEY_DIM
    col = np.arange(B_HEADS * s)
    sel = (col[None, None, :] == (head_of_lane[None, :, None] * s + pos[:, None, None]))
    bd = (col[:, None] // s) == head_of_lane[None, :]
    hh = head_of_lane[:, None] == head_of_lane[None, :]
    return (jnp.asarray(tri, BF16), jnp.asarray(keep_add, F32), jnp.asarray(sel, BF16),
            jnp.asarray(bd, F32), jnp.asarray(hh, F32))


def _hgrn_kernel(qf_ref, ff_ref, vf_ref, qr_ref, fr_ref, vr_ref, lbf_ref, lbr_ref, trif_ref, keepf_ref, trir_ref,
                 keepr_ref, sel_ref, bd_ref, hh_ref, of_ref, or_ref, stf_ref, str_ref):
    @pl.when(pl.program_id(0) == 0)
    def _():
        stf_ref[...] = jnp.zeros(stf_ref.shape, F32)
        str_ref[...] = jnp.zeros(str_ref.shape, F32)

    st_f, st_r = stf_ref[...], str_ref[...]
    n_chunks = HG_BLOCK // HG_CHUNK
    for step in range(n_chunks):
        rows = slice(step * HG_CHUNK, (step + 1) * HG_CHUNK)
        o, st_f = _hgrn_chunk(qf_ref[rows, :], ff_ref[rows, :], vf_ref[rows, :], st_f, lbf_ref, trif_ref,
                              keepf_ref, sel_ref, bd_ref, hh_ref, rev=False)
        of_ref[rows, :] = o
        ch = n_chunks - 1 - step
        rows = slice(ch * HG_CHUNK, (ch + 1) * HG_CHUNK)
        o, st_r = _hgrn_chunk(qr_ref[rows, :], fr_ref[rows, :], vr_ref[rows, :], st_r, lbr_ref, trir_ref,
                              keepr_ref, sel_ref, bd_ref, hh_ref, rev=True)
        or_ref[rows, :] = o
    stf_ref[...] = st_f
    str_ref[...] = st_r


def _hgrn_chunk(q_bf, f_pre, v, st, lb_ref, tri_ref, keep_ref, sel_ref, bd_ref, hh_ref, *, rev):
    c_len = HG_CHUNK
    n_sub = c_len // HG_SUB
    w = B_HEADS * B_KEY_DIM
    lb = lb_ref[...]
    f = lb + (1.0 - lb) * _sigmoid(f_pre)
    k = 1.0 - f
    lf = jnp.log(f)
    q = q_bf.astype(F32)
    bd = bd_ref[...]

    c = _split_dot_left(tri_ref[...], lf)

    q3 = q.reshape(n_sub, HG_SUB, w)
    k3 = k.reshape(n_sub, HG_SUB, w)
    c3 = c.reshape(n_sub, HG_SUB, w)
    a_diag = None
    for sg in range(HG_SUB):
        ks = jnp.broadcast_to(k3[:, sg:sg + 1, :], k3.shape)
        cs = jnp.broadcast_to(c3[:, sg:sg + 1, :], c3.shape)
        wgt = (q3 * ks) * jnp.exp((c3 - cs) + keep_ref[sg][None])
        part = _dot(wgt.reshape(c_len, w).astype(BF16), sel_ref[sg])
        a_diag = part if a_diag is None else a_diag + part

    o_parts = [None] * n_sub
    for j in range(n_sub):
        r0 = j * HG_SUB
        r_last = r0 if rev else r0 + HG_SUB - 1
        e_j = c[r_last:r_last + 1, :]
        kj = k[r0:r0 + HG_SUB, :] * jnp.exp(e_j - c[r0:r0 + HG_SUB, :])
        kbd = (jnp.concatenate([kj] * B_HEADS, axis=0) * bd).astype(BF16)
        vbd = (jnp.concatenate([v[r0:r0 + HG_SUB, :].astype(F32)] * B_HEADS, axis=0) * bd).astype(BF16)
        lo, hi = (0, r0) if rev else (r0 + HG_SUB, c_len)
        pieces = [a_diag[r0:r0 + HG_SUB]]
        if hi > lo:
            qj = (q[lo:hi] * jnp.exp(c[lo:hi] - e_j)).astype(BF16)
            a_off = _dot_nt(qj, kbd)
            pieces = [a_off] + pieces if rev else pieces + [a_off]
        first = 0 if rev else j
        if (sum(p.shape[0] for p in pieces) % 16) != 0:
            pad = jnp.zeros((HG_SUB, a_diag.shape[1]), F32)
            pieces = pieces + [pad] if rev else [pad] + pieces
            first = first if rev else first - 1
        a_j = jnp.concatenate(pieces, axis=0).astype(BF16)
        contrib = _dot(a_j, vbd)
        for i in range(contrib.shape[0] // HG_SUB):
            piece = contrib[i * HG_SUB:(i + 1) * HG_SUB]
            o_parts[first + i] = piece if o_parts[first + i] is None else o_parts[first + i] + piece
    o = jnp.concatenate(o_parts, axis=0)

    o = o + _dot_nt((q * jnp.exp(c)).astype(BF16), st.astype(BF16))
    r_end = 0 if rev else c_len - 1
    c_end = c[r_end:r_end + 1, :]
    k_end = (k * jnp.exp(c_end - c)).astype(BF16)
    return o, jnp.exp(c_end) * st + _dot_tn(v, k_end) * hh_ref[...]


def _hgrn_scans(bq, f_fwd, f_rev, bi, lb_fwd, lb_rev, *, t_lat):
    n = bq.shape[0]
    c_len = HG_BLOCK
    n_lat = t_lat // c_len
    n_all = n // c_len
    n_ctx = n_all - n_lat

    def blk_fwd(i):
        return (jnp.where(i < n_ctx, n_lat + i, i - n_ctx), 0)

    def blk_rev(i):
        return (jnp.where(i < n_ctx, n_all - 1 - i, n_lat - 1 - (i - n_ctx)), 0)

    tile_f = pl.BlockSpec((c_len, 256), blk_fwd)
    tile_r = pl.BlockSpec((c_len, 256), blk_rev)
    tri_f, keep_f, sel, bd, hh = _hgrn_tables(False)
    tri_r, keep_r, _, _, _ = _hgrn_tables(True)
    consts = [lb_fwd, lb_rev, tri_f, keep_f, tri_r, keep_r, sel, bd, hh]
    return pl.pallas_call(
        _hgrn_kernel,
        grid=(n_all,),
        in_specs=[tile_f, tile_f, tile_f, tile_r, tile_r, tile_r] + [_const_spec(t.shape) for t in consts],
        out_specs=[tile_f, tile_r],
        out_shape=[jax.ShapeDtypeStruct((n, BRANCH_WIDTH), F32)] * 2,
        scratch_shapes=[pltpu.VMEM((256, 256), F32)] * 2,
        compiler_params=_cparams(("arbitrary",)),
        name="hgrn_scans",
    )(bq, f_fwd, bi, bq, f_rev, bi, *consts)


def _merge_kernel(x_ref, mods_ref, ya_ref, of_ref, or_ref, bg_ref, gain_ref, segm_ref, yc_ref, yd_ref, gt_ref,
                  wb_ref, wo_ref, nrm_ref, wg_ref, wu_ref, wd_ref, *rest, t_lat, tm):
    fin_ref = rest[0] if len(rest) == 2 else None
    o_ref = rest[-1]
    i = pl.program_id(0)
    row = i * tm + lax.broadcasted_iota(jnp.int32, (tm, 1), 0)
    is_lat = row < t_lat
    yb = _head_norm(of_ref[...] + or_ref[...], gain_ref[...], segm_ref[...]) * _silu(bg_ref[...].astype(F32))
    branches = (ya_ref[...], yb.astype(BF16), yc_ref[...], yd_ref[...])
    merged = None
    for n, y in enumerate(branches):
        term = gt_ref[:, n * D_MODEL:(n + 1) * D_MODEL].astype(F32) * _dot(y, wb_ref[n])
        merged = term if merged is None else merged + term
    y = _dot(merged.astype(BF16), wo_ref[...])
    x_mid = x_ref[...] + _row_mod(mods_ref, 5, is_lat) * y
    out = _ffn_math(x_mid, is_lat, mods_ref, nrm_ref, wg_ref, wu_ref, wd_ref, 6)
    o_ref[...] = out if fin_ref is None else _rms_rows(out) * fin_ref[...]


def _merge_ffn(x_all, mods, ya, o_fwd, o_rev, bg, b_gain, segm, yc, yd, gates, wb, wo, nrm, wg, wu, wd, *,
               layer, t_lat, n_rows, final_gain=None):
    n, d = x_all.shape
    tm = TOKEN_TILE
    final = final_gain is not None

    def tile(w):
        return pl.BlockSpec((tm, w), lambda i: (i, 0))

    return pl.pallas_call(
        functools.partial(_merge_kernel, t_lat=t_lat, tm=tm),
        grid=(n_rows // tm,),
        in_specs=[tile(d), _const_spec((2, N_MOD, d)), tile(256), tile(256), tile(256), tile(256),
                  _const_spec((1, 256)), _const_spec((256, 256)), tile(256), tile(256),
                  tile(N_BRANCH * d), _layer_spec(wb.shape, layer), _layer_spec(wo.shape, layer),
                  _const_spec((1, d)), _layer_spec(wg.shape, layer), _layer_spec(wu.shape, layer),
                  _layer_spec(wd.shape, layer)] + ([_const_spec((1, d))] if final else []),
        out_specs=tile(d),
        out_shape=jax.ShapeDtypeStruct((n_rows if final else n, d), F32),
        input_output_aliases={} if final else {0: 0},
        compiler_params=_cparams(("parallel",)),
        name="merge_ffn",
    )(x_all, mods, ya, o_fwd, o_rev, bg, b_gain, segm, yc, yd, gates, wb, wo, nrm, wg, wu, wd,
      *([final_gain] if final else []))


def _rope_tables_padded(t_lat, n_ctx):
    rows = t_lat // GRID_W
    half = HEAD_DIM // 2
    nf = half // 2
    inv = 1.0 / (ROPE_THETA ** (jnp.arange(0, half, 2, dtype=F32) / half))
    ang_r = jnp.arange(rows, dtype=F32)[:, None] * inv
    ang_c = jnp.arange(GRID_W, dtype=F32)[:, None] * inv

    def over_rows(a):
        return jnp.broadcast_to(a[:, None, :], (rows, GRID_W, nf))

    def over_cols(a):
        return jnp.broadcast_to(a[None, :, :], (rows, GRID_W, nf))

    cos = jnp.concatenate([over_rows(jnp.cos(ang_r))] * 2 + [over_cols(jnp.cos(ang_c))] * 2, axis=2)
    sin = jnp.concatenate([over_rows(-jnp.sin(ang_r)), over_rows(jnp.sin(ang_r)),
                           over_cols(-jnp.sin(ang_c)), over_cols(jnp.sin(ang_c))], axis=2)
    cos = cos.reshape(t_lat, HEAD_DIM)
    sin = sin.reshape(t_lat, HEAD_DIM)
    cos = jnp.concatenate([cos, jnp.ones((n_ctx, HEAD_DIM), F32)], axis=0)
    sin = jnp.concatenate([sin, jnp.zeros((n_ctx, HEAD_DIM), F32)], axis=0)
    return jnp.tile(cos, (1, 2)), jnp.tile(sin, (1, 2))


def _block_diag(w_group):
    g, ci, co = w_group.shape
    out = jnp.zeros((g * ci, g * co), w_group.dtype)
    for n in range(g):
        out = out.at[n * ci:(n + 1) * ci, n * co:(n + 1) * co].set(w_group[n])
    return out


def kernel(x, c, ctx, c_ctx, w_ada, b_ada, ffn1_norm, ffn1_w_gate, ffn1_w_up, ffn1_w_down, mix_norm, w_in, a_q_norm, a_k_norm, b_lb_logits, b_o_norm, c_w_group, c_scale, d_rel_bias, w_branch, w_out, ffn2_norm, ffn2_w_gate, ffn2_w_up, ffn2_w_down, final_norm):
    assert x.shape[0] == 1 and ctx.shape[0] == 1
    depth = w_ada.shape[0]
    t_lat = x.shape[1]
    n_ctx = ctx.shape[1]
    n = t_lat + n_ctx
    rows = t_lat // GRID_W
    assert t_lat % max(NA_QB, min(FLASH_TQ, t_lat), n_ctx) == 0 and n_ctx % TOKEN_TILE == 0 and rows >= 2 * NA_QROWS

    x_all = None
    mods_all = _ada_mods(c, c_ctx, w_ada, b_ada)
    cos, sin = _rope_tables_padded(t_lat, n_ctx)
    segm = (jnp.kron(jnp.eye(BRANCH_WIDTH // HEAD_DIM), jnp.ones((HEAD_DIM, HEAD_DIM))) / HEAD_DIM).astype(BF16)
    lb_all = jnp.cumsum(jax.nn.softmax(b_lb_logits.astype(F32), axis=0), axis=0)
    lb_all = lb_all - lb_all[:1]
    wg1, wu1, wd1 = ffn1_w_gate.astype(BF16), ffn1_w_up.astype(BF16), ffn1_w_down.astype(BF16)
    wg2, wu2, wd2 = ffn2_w_gate.astype(BF16), ffn2_w_up.astype(BF16), ffn2_w_down.astype(BF16)
    w_in_b, w_branch_b, w_out_b = w_in.astype(BF16), w_branch.astype(BF16), w_out.astype(BF16)
    na_bias, na_masks = _na_bias_tables(d_rel_bias, rows)

    for l in range(depth):
        with_ctx_out = l < depth - 1
        n_rows = n if with_ctx_out else t_lat
        mods = mods_all[l]
        x_all = _ffn_half(x[0] if l == 0 else x_all, mods, ffn1_norm[l][None], wg1, wu1, wd1, layer=l, t_lat=t_lat,
                          k0=0, n_rows=n, x_ctx=ctx[0] if l == 0 else None)
        (aq, ak, av, bq, bff, bfb, bi, bg, cx, dq, dk, dv, gates) = _in_proj(
            x_all, mods, mix_norm[l][None], w_in_b, cos, sin,
            jnp.tile(a_q_norm[l], A_HEADS)[None], jnp.tile(a_k_norm[l], A_KV_HEADS)[None], segm, layer=l, t_lat=t_lat)
        if with_ctx_out:
            ya, yd = _ctx_attn(aq, ak, av, dq, dk, dv, t_lat=t_lat)
        else:
            ya = jnp.zeros((n, BRANCH_WIDTH), BF16)
            yd = ya
        ya = _gqa_latent(aq, ak, av, ya, t_lat=t_lat)
        yd = _na_latent(dq, dk, dv, na_bias, na_masks, yd, layer=l, t_lat=t_lat)
        yc = _pool_mixer(cx, _block_diag(c_w_group[l]).astype(BF16), c_scale[l][None], t_lat=t_lat)
        o_fwd, o_rev = _hgrn_scans(bq, bff, bfb, bi, lb_all[l, 0][None], lb_all[l, 1][None], t_lat=t_lat)
        x_all = _merge_ffn(x_all, mods, ya, o_fwd, o_rev, bg, jnp.tile(b_o_norm[l], B_HEADS)[None], segm, yc, yd,
                           gates, w_branch_b, w_out_b, ffn2_norm[l][None], wg2, wu2, wd2, layer=l, t_lat=t_lat,
                           n_rows=n_rows, final_gain=None if with_ctx_out else final_norm[None])
    return x_all[None]
```

```python
import functools

import numpy as np
import jax
import jax.numpy as jnp
from jax import lax
from jax.experimental import pallas as pl
from jax.experimental.pallas import tpu as pltpu

F32 = jnp.float32
BF16 = jnp.bfloat16

D_MODEL = 1024
GRID_W = 64
HEAD_DIM = 64
BRANCH_WIDTH = 256
N_BRANCH = 4
A_HEADS = 4
A_KV_HEADS = 2
ROPE_THETA = 10000.0
B_HEADS = 4
B_KEY_DIM = 64
C_WINDOWS = (2, 4, 8, 16)
D_HEADS = 4
NA_WIN_R = 8
NA_WIN_C = 16
D_FF = 2816
N_MOD = 9
EPS = 1e-6
ATTN_SCALE = HEAD_DIM ** -0.5
LOG2_E = 1.4426950408889634
NEG_BIG = -1e30

V7X_VMEM_BYTES = 64 * 1024 * 1024
VMEM_LIMIT = V7X_VMEM_BYTES - 8 * 1024 * 1024

SEG_AQ = (0, 256)
SEG_AK = (256, 384)
SEG_AV = (384, 512)
SEG_BQ = (512, 768)
SEG_BFF = (768, 1024)
SEG_BFB = (1024, 1280)
SEG_BI = (1280, 1536)
SEG_BG = (1536, 1792)
SEG_CX = (1792, 2048)
SEG_DQ = (2048, 2304)
SEG_DK = (2304, 2560)
SEG_DV = (2560, 2816)
SEG_GATE = (2816, 2816 + N_BRANCH * D_MODEL)
IN_WIDTH = SEG_GATE[1]

TOKEN_TILE = 256
POOL_TILE = 2048
FLASH_TQ = 4096
FLASH_QC = 512
FLASH_SB = 256
FLASH_NBUF = 4
NA_QROWS = 8
NA_QB = NA_QROWS * GRID_W
NA_HALO = 256
HG_BLOCK = 256
HG_CHUNK = 128
HG_SUB = 8


def _cparams(sem):
    return pltpu.CompilerParams(dimension_semantics=sem, vmem_limit_bytes=VMEM_LIMIT)


def _const_spec(shape):
    nd = len(shape)
    return pl.BlockSpec(shape, lambda *_: (0,) * nd, pipeline_mode=pl.Buffered(1))


def _layer_spec(shape, layer):
    nd = len(shape)
    return pl.BlockSpec((None,) + tuple(shape[1:]), lambda *_: (layer,) + (0,) * (nd - 1),
                        pipeline_mode=pl.Buffered(1))


def _sigmoid(x):
    return 1.0 / (1.0 + jnp.exp(-x))


def _silu(x):
    return x * _sigmoid(x)


def _dot(a, b):
    return jnp.dot(a, b, preferred_element_type=F32)


def _dot_nt(a, b):
    return lax.dot_general(a, b, (((1,), (1,)), ((), ())), preferred_element_type=F32)


def _dot_tn(a, b):
    return lax.dot_general(a, b, (((0,), (0,)), ((), ())), preferred_element_type=F32)


def _split_dot(x, m):
    hi = x.astype(BF16)
    r1 = x - hi.astype(F32)
    mid = r1.astype(BF16)
    lo = (r1 - mid.astype(F32)).astype(BF16)
    return _dot(hi, m) + _dot(mid, m) + _dot(lo, m)


def _split_dot_left(m, x):
    hi = x.astype(BF16)
    r1 = x - hi.astype(F32)
    mid = r1.astype(BF16)
    lo = (r1 - mid.astype(F32)).astype(BF16)
    return _dot(m, hi) + _dot(m, mid) + _dot(m, lo)


def _tile_mod(mods_ref, k, lat_tile):
    return jnp.where(lat_tile, mods_ref[0, k:k + 1, :], mods_ref[1, k:k + 1, :])


def _rms_rows(x):
    return x * lax.rsqrt(jnp.mean(x * x, axis=-1, keepdims=True) + EPS)


def _modulated_norm(x, nrm_ref, mods_ref, k_shift, lat_tile):
    col_scale = nrm_ref[...] * (1.0 + _tile_mod(mods_ref, k_shift + 1, lat_tile))
    return (_rms_rows(x) * col_scale + _tile_mod(mods_ref, k_shift, lat_tile)).astype(BF16)


def _ada_kernel(s_ref, w_ref, b_ref, o_ref):
    o_ref[0] = _dot(_silu(s_ref[...]).astype(BF16), w_ref[0].astype(BF16)) + b_ref[0]


def _ada_mods(c, c_ctx, w_ada, b_ada):
    depth = w_ada.shape[0]
    width = w_ada.shape[2]
    tn = 1024
    s = jnp.zeros((8, D_MODEL), F32).at[0].set(c[0]).at[1].set(c_ctx)
    out = pl.pallas_call(
        _ada_kernel,
        grid=(depth, width // tn),
        in_specs=[
            pl.BlockSpec((8, D_MODEL), lambda l, j: (0, 0)),
            pl.BlockSpec((1, D_MODEL, tn), lambda l, j: (l, 0, j)),
            pl.BlockSpec((1, 1, tn), lambda l, j: (l, 0, j)),
        ],
        out_specs=pl.BlockSpec((1, 8, tn), lambda l, j: (l, 0, j)),
        out_shape=jax.ShapeDtypeStruct((depth, 8, width), F32),
        compiler_params=_cparams(("parallel", "parallel")),
        name="ada_mods",
    )(s, w_ada, b_ada.reshape(depth, 1, width))
    return out[:, :2].reshape(depth, 2, N_MOD, D_MODEL)


def _ffn_math(x, lat_tile, mods_ref, nrm_ref, wg_ref, wu_ref, wd_ref, k0):
    z = _modulated_norm(x, nrm_ref, mods_ref, k0, lat_tile)
    g = _dot(z, wg_ref[...])
    u = _dot(z, wu_ref[...])
    a = (_silu(g) * u).astype(BF16)
    return x + (0.5 * _tile_mod(mods_ref, k0 + 2, lat_tile)) * _dot(a, wd_ref[...])


def _ffn_kernel(*refs, t_lat, tm, k0, split_input):
    if split_input:
        xl_ref, xc_ref, mods_ref, nrm_ref, wg_ref, wu_ref, wd_ref, o_ref = refs
    else:
        x_ref, mods_ref, nrm_ref, wg_ref, wu_ref, wd_ref, o_ref = refs
    lat_tile = pl.program_id(0) * tm < t_lat
    x = jnp.where(lat_tile, xl_ref[...], xc_ref[...]) if split_input else x_ref[...]
    o_ref[...] = _ffn_math(x, lat_tile, mods_ref, nrm_ref, wg_ref, wu_ref, wd_ref, k0)


def _ffn_half(x_all, mods, nrm, wg, wu, wd, *, layer, t_lat, k0, n_rows, x_ctx=None):
    split_input = x_ctx is not None
    d = x_all.shape[1]
    n = x_all.shape[0] + (x_ctx.shape[0] if split_input else 0)
    tm = TOKEN_TILE
    if split_input:
        n_lat_tiles = t_lat // tm
        x_specs = [pl.BlockSpec((tm, d), lambda i: (jnp.minimum(i, n_lat_tiles - 1), 0)),
                   pl.BlockSpec((tm, d), lambda i: (jnp.maximum(i - n_lat_tiles, 0), 0))]
        x_args = [x_all, x_ctx]
    else:
        x_specs = [pl.BlockSpec((tm, d), lambda i: (i, 0))]
        x_args = [x_all]
    return pl.pallas_call(
        functools.partial(_ffn_kernel, t_lat=t_lat, tm=tm, k0=k0, split_input=split_input),
        grid=(n_rows // tm,),
        in_specs=x_specs + [
            _const_spec((2, N_MOD, d)),
            _const_spec((1, d)),
            _layer_spec(wg.shape, layer),
            _layer_spec(wu.shape, layer),
            _layer_spec(wd.shape, layer),
        ],
        out_specs=pl.BlockSpec((tm, d), lambda i: (i, 0)),
        out_shape=jax.ShapeDtypeStruct((n, d), F32),
        input_output_aliases={} if split_input else {0: 0},
        compiler_params=_cparams(("parallel",)),
        name="ffn_half",
    )(*x_args, mods, nrm, wg, wu, wd)


def _head_norm(y, gain, segm):
    ms = _split_dot(y * y, segm)
    return y * lax.rsqrt(ms + EPS) * gain


def _rope(y, cos, sin_signed):
    w = y.shape[1]
    lane = lax.broadcasted_iota(jnp.int32, y.shape, 1)
    partner = jnp.where((lane & 31) < 16, pltpu.roll(y, w - 16, 1), pltpu.roll(y, 16, 1))
    return y * cos + partner * sin_signed


def _inproj_kernel(x_ref, mods_ref, nrm_ref, w_ref, cos_ref, sin_ref, gq_ref, gk_ref, segm_ref,
                   aq_ref, ak_ref, av_ref, bq_ref, bff_ref, bfb_ref, bi_ref, bg_ref, cx_ref,
                   dq_ref, dk_ref, dv_ref, gt_ref, *, t_lat, tm):
    z = _modulated_norm(x_ref[...], nrm_ref, mods_ref, 3, pl.program_id(0) * tm < t_lat)

    def proj(seg):
        return _dot(z, w_ref[:, seg[0]:seg[1]])

    cos = cos_ref[...]
    sin = sin_ref[...]
    q = _rope(_head_norm(proj(SEG_AQ), gq_ref[...], segm_ref[...]),
              jnp.concatenate([cos, cos], axis=1), jnp.concatenate([sin, sin], axis=1))
    aq_ref[...] = (q * (ATTN_SCALE * LOG2_E)).T.astype(BF16)
    k = _rope(_head_norm(proj(SEG_AK), gk_ref[...], segm_ref[0:128, 0:128]), cos, sin)
    zeros = jnp.zeros((tm, HEAD_DIM), F32)
    ak_ref[...] = jnp.concatenate([k[:, :HEAD_DIM], zeros, k[:, HEAD_DIM:], zeros], axis=1).astype(BF16)
    v = proj(SEG_AV)
    one_hot = jnp.where(lax.broadcasted_iota(jnp.int32, (tm, HEAD_DIM), 1) == 0, 1.0, 0.0)
    av_ref[...] = jnp.concatenate([v[:, :HEAD_DIM], one_hot, v[:, HEAD_DIM:], one_hot], axis=1).T.astype(BF16)

    bq_ref[...] = (proj(SEG_BQ) * (B_KEY_DIM ** -0.5)).astype(BF16)
    bff_ref[...] = proj(SEG_BFF)
    bfb_ref[...] = proj(SEG_BFB)
    bi_ref[...] = proj(SEG_BI).astype(BF16)
    bg_ref[...] = proj(SEG_BG).astype(BF16)
    cx_ref[...] = proj(SEG_CX)
    dq_ref[...] = (proj(SEG_DQ) * ATTN_SCALE).astype(BF16)
    dk_ref[...] = proj(SEG_DK).astype(BF16)
    dv = proj(SEG_DV)
    dv_ref[...] = jnp.concatenate(
        [piece for h in range(D_HEADS) for piece in (dv[:, HEAD_DIM * h:HEAD_DIM * (h + 1)], one_hot)],
        axis=1).astype(BF16)
    gw = 512
    for c0 in range(SEG_GATE[0], SEG_GATE[1], gw):
        o0 = c0 - SEG_GATE[0]
        gt_ref[:, o0:o0 + gw] = _sigmoid(proj((c0, c0 + gw))).astype(BF16)


def _in_proj(x_all, mods, nrm, w_in, cos, sin, gq, gk, segm, *, layer, t_lat):
    n, d = x_all.shape
    tm = TOKEN_TILE
    widths = [(256, BF16), (256, BF16), (256, BF16), (256, BF16), (256, F32), (256, F32), (256, BF16),
              (256, BF16), (256, F32), (256, BF16), (256, BF16), (2 * D_HEADS * HEAD_DIM, BF16),
              (N_BRANCH * D_MODEL, BF16)]
    transposed = (0, 2)
    out_specs = [pl.BlockSpec((w, tm), lambda i: (0, i)) if o in transposed else pl.BlockSpec((tm, w), lambda i: (i, 0))
                 for o, (w, _) in enumerate(widths)]
    out_shape = [jax.ShapeDtypeStruct((w, n) if o in transposed else (n, w), dt) for o, (w, dt) in enumerate(widths)]
    return pl.pallas_call(
        functools.partial(_inproj_kernel, t_lat=t_lat, tm=tm),
        grid=(n // tm,),
        in_specs=[
            pl.BlockSpec((tm, d), lambda i: (i, 0)),
            _const_spec((2, N_MOD, d)),
            _const_spec((1, d)),
            _layer_spec(w_in.shape, layer),
            pl.BlockSpec((tm, 128), lambda i: (i, 0)),
            pl.BlockSpec((tm, 128), lambda i: (i, 0)),
            _const_spec((1, 256)),
            _const_spec((1, 128)),
            _const_spec((256, 256)),
        ],
        out_specs=out_specs,
        out_shape=out_shape,
        compiler_params=_cparams(("parallel",)),
        name="in_proj",
    )(x_all, mods, nrm, w_in, cos, sin, gq, gk, segm)


def _softmax_attend(q, k, v, exp_fn):
    s = _dot_nt(q, k)
    p = exp_fn(s - jnp.max(s, axis=-1, keepdims=True))
    return _dot(p.astype(BF16), v) / jnp.sum(p, axis=-1, keepdims=True)


def _ctx_attn_kernel(aqt_ref, ak_ref, avt_ref, dq_ref, dk_ref, dv_ref, ya_ref, yd_ref):
    outs = []
    for h in range(A_HEADS):
        g = h // (A_HEADS // A_KV_HEADS)
        st = _dot(ak_ref[:, 128 * g:128 * g + 64], aqt_ref[64 * h:64 * h + 64, :])
        pt = jnp.exp2(st - jnp.max(st, axis=0, keepdims=True)).astype(BF16)
        acc = _dot(avt_ref[128 * g:128 * g + 128, :], pt)
        outs.append((acc[:HEAD_DIM] / acc[HEAD_DIM:HEAD_DIM + 1]).T)
    ya_ref[...] = jnp.concatenate(outs, axis=1).astype(BF16)
    outs = []
    for h in range(D_HEADS):
        hs = slice(64 * h, 64 * h + 64)
        outs.append(_softmax_attend(dq_ref[:, hs], dk_ref[:, hs], dv_ref[:, 128 * h:128 * h + 64], jnp.exp))
    yd_ref[...] = jnp.concatenate(outs, axis=1).astype(BF16)


def _ctx_attn(aqt, ak, avt, dq, dk, dv, *, t_lat):
    n = ak.shape[0]
    n_ctx = n - t_lat
    blk = t_lat // n_ctx

    def spec(w):
        return pl.BlockSpec((n_ctx, w), lambda i: (blk, 0))

    def spec_t(w):
        return pl.BlockSpec((w, n_ctx), lambda i: (0, blk))

    return pl.pallas_call(
        _ctx_attn_kernel,
        grid=(1,),
        in_specs=[spec_t(256), spec(256), spec_t(256), spec(256), spec(256), spec(dv.shape[1])],
        out_specs=[spec(256), spec(256)],
        out_shape=[jax.ShapeDtypeStruct((n, BRANCH_WIDTH), BF16)] * 2,
        compiler_params=_cparams(("arbitrary",)),
        name="ctx_attn",
    )(aqt, ak, avt, dq, dk, dv)


def _flash_kernel(qt_ref, k_ref, vt_ref, prev_ref, o_ref, m_ref, acc_ref, *bufs, tk, tq):
    del prev_ref
    s_refs, p_refs = bufs[:FLASH_NBUF], bufs[FLASH_NBUF:]
    j = pl.program_id(1)

    @pl.when(j == 0)
    def _():
        m_ref[...] = jnp.full(m_ref.shape, -jnp.inf, F32)
        acc_ref[...] = jnp.zeros(acc_ref.shape, F32)

    group = A_HEADS // A_KV_HEADS
    n_qc = tq // FLASH_QC
    n_sb = tk // FLASH_SB
    chains = [(h, c) for h in range(A_HEADS) for c in range(n_qc)]

    def score_block(ci, sb):
        h, c = chains[ci]
        g = h // group
        rows = slice(sb * FLASH_SB, (sb + 1) * FLASH_SB)
        st = _dot(k_ref[rows, 128 * g:128 * g + 64], qt_ref[64 * h:64 * h + 64, c * FLASH_QC:(c + 1) * FLASH_QC])
        s_refs[ci % FLASH_NBUF][rows, :] = st
        return jnp.max(st, axis=0, keepdims=True)

    def exp_block(ci, sb, m_new):
        rows = slice(sb * FLASH_SB, (sb + 1) * FLASH_SB)
        p_refs[ci % 2][rows, :] = jnp.exp2(s_refs[ci % FLASH_NBUF][rows, :] - m_new).astype(BF16)

    def value_block(ci, sb):
        g = chains[ci][0] // group
        rows = slice(sb * FLASH_SB, (sb + 1) * FLASH_SB)
        return _dot(vt_ref[128 * g:128 * g + 128, rows], p_refs[ci % 2][rows, :])

    def fold(a, b):
        return b if a is None else a + b

    def fold_max(a, b):
        return b if a is None else jnp.maximum(a, b)

    n_ch = len(chains)
    blk_max, m_new, alpha = {}, {}, {}
    for stage in range(-1, n_ch + 1):
        c_s, c_e, c_v = stage + 1, stage, stage - 1
        if 0 <= c_e < n_ch:
            h, c = chains[c_e]
            cols = slice(c * FLASH_QC, (c + 1) * FLASH_QC)
            m_old = m_ref[h, :, cols]
            m_new[c_e] = jnp.maximum(m_old, blk_max.pop(c_e))
            m_ref[h, :, cols] = m_new[c_e]
            alpha[c_e] = jnp.exp2(m_old - m_new[c_e])
        part, mx = None, None
        for sb in range(n_sb):
            if c_s < n_ch:
                mx = fold_max(mx, score_block(c_s, sb))
            if 0 <= c_e < n_ch:
                exp_block(c_e, sb, m_new[c_e])
            if 0 <= c_v:
                part = fold(part, value_block(c_v, sb))
        if c_s < n_ch:
            blk_max[c_s] = mx
        if 0 <= c_v:
            h, c = chains[c_v]
            cols = slice(c * FLASH_QC, (c + 1) * FLASH_QC)
            acc_ref[h, :, cols] = alpha.pop(c_v) * acc_ref[h, :, cols] + part

    @pl.when(j == pl.num_programs(1) - 1)
    def _():
        outs = []
        for h in range(A_HEADS):
            acc = acc_ref[h]
            outs.append((acc[:HEAD_DIM] / acc[HEAD_DIM:HEAD_DIM + 1]).T)
        o_ref[...] = jnp.concatenate(outs, axis=1).astype(BF16)


def _flash_tk(n):
    for tk in (1280, 768, 512, 256):
        if n % tk == 0:
            return tk
    raise ValueError(f"unsupported key count {n}")


def _gqa_latent(aqt, ak, avt, ya_prev, *, t_lat):
    n = ak.shape[0]
    tq = min(FLASH_TQ, t_lat)
    tk = _flash_tk(n)
    return pl.pallas_call(
        functools.partial(_flash_kernel, tk=tk, tq=tq),
        grid=(t_lat // tq, n // tk),
        in_specs=[
            pl.BlockSpec((256, tq), lambda i, j: (0, i)),
            pl.BlockSpec((tk, 256), lambda i, j: (j, 0)),
            pl.BlockSpec((256, tk), lambda i, j: (0, j)),
            pl.BlockSpec(memory_space=pl.ANY),
        ],
        out_specs=pl.BlockSpec((tq, 256), lambda i, j: (i, 0)),
        out_shape=jax.ShapeDtypeStruct((n, BRANCH_WIDTH), BF16),
        scratch_shapes=[pltpu.VMEM((A_HEADS, 1, tq), F32),
                        pltpu.VMEM((A_HEADS, 128, tq), F32)]
                       + [pltpu.VMEM((tk, FLASH_QC), F32)] * FLASH_NBUF
                       + [pltpu.VMEM((tk, FLASH_QC), BF16)] * 2,
        input_output_aliases={3: 0},
        compiler_params=_cparams(("parallel", "arbitrary")),
        name="gqa_flash",
    )(aqt, ak, avt, ya_prev)


def _na_kernel(q_ref, kp_ref, km_ref, kn_ref, vp_ref, vm_ref, vn_ref, kc_ref, vc_ref, bias_ref, mask_ref, prev_ref,
               o_ref, bm_ref, *, nb):
    del prev_ref
    b = pl.program_id(0)

    @pl.when((b == 0) | (b == 1) | (b == nb - 1))
    def _():
        for h in range(D_HEADS):
            bm_ref[h] = bias_ref[h] + mask_ref[0]

    def scores(h):
        hs = slice(64 * h, 64 * h + 64)
        q = q_ref[:, hs]
        kcat = jnp.concatenate([kp_ref[:, hs], km_ref[:, hs], kn_ref[:, hs]], axis=0)
        return _dot_nt(q, kcat) + bm_ref[h], _dot_nt(q, kc_ref[:, hs])

    def softmax(s_loc, s_ctx):
        m = jnp.maximum(jnp.max(s_loc, axis=-1, keepdims=True), jnp.max(s_ctx, axis=-1, keepdims=True))
        return jnp.exp(s_loc - m).astype(BF16), jnp.exp(s_ctx - m).astype(BF16)

    def attend(h, p_loc, p_ctx):
        vs = slice(128 * h, 128 * h + 128)
        vcat = jnp.concatenate([vp_ref[:, vs], vm_ref[:, vs], vn_ref[:, vs]], axis=0)
        o = _dot(p_loc, vcat) + _dot(p_ctx, vc_ref[:, vs])
        return o[:, :HEAD_DIM] / o[:, HEAD_DIM:HEAD_DIM + 1]

    s = {0: scores(0)}
    p, outs = {}, []
    for h in range(D_HEADS):
        if h + 1 < D_HEADS:
            s[h + 1] = scores(h + 1)
        p[h] = softmax(*s.pop(h))
        if h >= 1:
            outs.append(attend(h - 1, *p.pop(h - 1)))
    outs.append(attend(D_HEADS - 1, *p.pop(D_HEADS - 1)))
    o_ref[...] = jnp.concatenate(outs, axis=1).astype(BF16)


def _na_bias_tables(rel_bias, rows):
    wr = min(NA_WIN_R, rows)
    halo_rows = NA_HALO // GRID_W
    krows = NA_QROWS + 2 * halo_rows
    nb = rows // NA_QROWS
    qc = np.arange(GRID_W)[:, None]
    kc = np.arange(GRID_W)[None, :]
    cs = np.clip(qc - NA_WIN_C // 2, 0, GRID_W - NA_WIN_C)
    in_col = (kc >= cs) & (kc < cs + NA_WIN_C)
    e_col = (kc - qc + (NA_WIN_C - 1))[:, :, None] == np.arange(2 * NA_WIN_C - 1)
    qr_l = np.arange(NA_QROWS)[:, None]
    kr_l = np.arange(krows)[None, :]
    e_row = (kr_l - halo_rows - qr_l + (NA_WIN_R - 1))[:, :, None] == np.arange(2 * NA_WIN_R - 1)
    hi = lax.Precision.HIGHEST
    tmp = jnp.einsum("lhrc,qkr->lhqkc", rel_bias.astype(F32), e_row.astype(np.float32), precision=hi)
    full = jnp.einsum("lhqkc,pjc->lhqpkj", tmp, e_col.astype(np.float32), precision=hi)
    masks = []
    for b in (0, min(1, nb - 1), nb - 1):
        qr = NA_QROWS * b + qr_l
        kr = NA_QROWS * b - halo_rows + kr_l
        rs = np.clip(qr - wr // 2, 0, rows - wr)
        in_row = (kr >= rs) & (kr < rs + wr)
        mask = in_row[:, None, :, None] & in_col[None, :, None, :]
        masks.append(np.where(mask, 0.0, NEG_BIG).reshape(NA_QB, krows * GRID_W))
    return full.reshape(-1, D_HEADS, NA_QB, krows * GRID_W), np.stack(masks).astype(np.float32)


def _na_latent(dq, dk, dv, bias, masks, yd_prev, *, layer, t_lat):
    n = dq.shape[0]
    nb = t_lat // NA_QB
    r = NA_QB // NA_HALO
    last_halo = t_lat // NA_HALO - 1
    ctx_blk = t_lat // (n - t_lat)
    n_ctx = n - t_lat

    def prev_map(b):
        return (jnp.maximum(r * b - 1, 0), 0)

    def next_map(b):
        return (jnp.minimum(r * b + r, last_halo), 0)

    def variant(b):
        return (jnp.where(b == 0, 0, jnp.where(b == nb - 1, 2, 1)), 0, 0)

    def specs(w):
        return (pl.BlockSpec((NA_QB, w), lambda b: (b, 0)), pl.BlockSpec((NA_HALO, w), prev_map),
                pl.BlockSpec((NA_HALO, w), next_map), pl.BlockSpec((n_ctx, w), lambda b: (ctx_blk, 0)))

    main, prev, nxt, ctx = specs(256)
    vmain, vprev, vnxt, vctx = specs(dv.shape[1])
    return pl.pallas_call(
        functools.partial(_na_kernel, nb=nb),
        grid=(nb,),
        in_specs=[main, prev, main, nxt, vprev, vmain, vnxt, ctx, vctx,
                  _layer_spec(bias.shape, layer),
                  pl.BlockSpec((1, NA_QB, NA_QB + 2 * NA_HALO), variant),
                  pl.BlockSpec(memory_space=pl.ANY)],
        out_specs=main,
        out_shape=jax.ShapeDtypeStruct((n, BRANCH_WIDTH), BF16),
        scratch_shapes=[pltpu.VMEM((D_HEADS, NA_QB, NA_QB + 2 * NA_HALO), F32)],
        input_output_aliases={11: 0},
        compiler_params=_cparams(("arbitrary",)),
        name="na_attn",
    )(dq, dk, dk, dk, dv, dv, dv, dk, dv, bias, masks, yd_prev)


def _pool_kernel(x_ref, xp_ref, xn_ref, w_ref, sc_ref, *rest, seg_lo, seg_hi, tm):
    o_ref, cat_ref, a_ref, b_ref = rest[-4:]
    start = seg_lo + pl.program_id(0) * tm
    x = x_ref[...]
    cat_ref[0:8, :] = jnp.where(start > seg_lo, xp_ref[...], 0.0)
    cat_ref[8:8 + tm, :] = x
    cat_ref[8 + tm:16 + tm, :] = jnp.where(start + tm < seg_hi, xn_ref[...], 0.0)
    cat_ref[16 + tm:32 + tm, :] = jnp.zeros((16, BRANCH_WIDTH), F32)
    a_ref[0:tm + 24, :] = cat_ref[0:tm + 24, :] + cat_ref[1:tm + 25, :]
    s2 = a_ref[7:7 + tm, :]
    b_ref[0:tm + 16, :] = a_ref[0:tm + 16, :] + a_ref[2:tm + 18, :]
    s4 = b_ref[6:6 + tm, :]
    a_ref[0:tm + 8, :] = b_ref[0:tm + 8, :] + b_ref[4:tm + 12, :]
    s8 = a_ref[4:4 + tm, :]
    s16 = a_ref[0:tm, :] + a_ref[8:8 + tm, :]

    pos = start - seg_lo + lax.broadcasted_iota(jnp.int32, (tm, 1), 0)
    seg_len = seg_hi - seg_lo

    def mean(sm, w):
        lo = jnp.clip(pos - w // 2, 0, seg_len)
        hi = jnp.clip(pos - w // 2 + w, 0, seg_len)
        return sm / (hi - lo).astype(F32)

    lane = lax.broadcasted_iota(jnp.int32, (tm, BRANCH_WIDTH), 1)
    gw = BRANCH_WIDTH // len(C_WINDOWS)
    pooled = jnp.where(lane < gw, mean(s2, 2),
                       jnp.where(lane < 2 * gw, mean(s4, 4),
                                 jnp.where(lane < 3 * gw, mean(s8, 8), mean(s16, 16)))) - x
    o_ref[...] = (_dot(pooled.astype(BF16), w_ref[...]) * sc_ref[...]).astype(BF16)


def _pool_segment(cx, w_bd, scale, prev_out, *, seg_lo, seg_hi, tm):
    n = cx.shape[0]
    r = tm // 8
    b0 = seg_lo // tm
    last8 = n // 8 - 1
    in_specs = [
        pl.BlockSpec((tm, 256), lambda i: (b0 + i, 0)),
        pl.BlockSpec((8, 256), lambda i: (jnp.maximum(r * (b0 + i) - 1, 0), 0)),
        pl.BlockSpec((8, 256), lambda i: (jnp.minimum(r * (b0 + i) + r, last8), 0)),
        _const_spec((256, 256)),
        _const_spec((1, 256)),
    ]
    args = [cx, cx, cx, w_bd, scale]
    aliases = {}
    if prev_out is not None:
        in_specs.append(pl.BlockSpec(memory_space=pl.ANY))
        args.append(prev_out)
        aliases = {5: 0}
    return pl.pallas_call(
        functools.partial(_pool_kernel, seg_lo=seg_lo, seg_hi=seg_hi, tm=tm),
        grid=((seg_hi - seg_lo) // tm,),
        in_specs=in_specs,
        out_specs=pl.BlockSpec((tm, 256), lambda i: (b0 + i, 0)),
        out_shape=jax.ShapeDtypeStruct((n, BRANCH_WIDTH), BF16),
        scratch_shapes=[pltpu.VMEM((tm + 32, 256), F32)] * 3,
        input_output_aliases=aliases,
        compiler_params=_cparams(("parallel",)),
        name="pool_mixer",
    )(*args)


def _pool_mixer(cx, w_bd, scale, *, t_lat):
    n = cx.shape[0]
    n_ctx = n - t_lat
    out = _pool_segment(cx, w_bd, scale, None, seg_lo=t_lat, seg_hi=n, tm=n_ctx)
    return _pool_segment(cx, w_bd, scale, out, seg_lo=0, seg_hi=t_lat, tm=min(POOL_TILE, t_lat))


def _hgrn_tables(rev):
    c_len, s, w = HG_CHUNK, HG_SUB, B_HEADS * B_KEY_DIM
    t = np.arange(c_len)
    tri = (t[None, :] >= t[:, None]) if rev else (t[None, :] <= t[:, None])
    pos = np.arange(s)
    keep = (pos[None, :] <= pos[:, None]) if rev else (pos[None, :] >= pos[:, None])
    keep_add = np.where(keep, 0.0, NEG_BIG)[:, :, None] * np.ones((1, 1, w))
    head_of_lane = np.arange(w) // B_KEY_DIM
    col = np.arange(B_HEADS * s)
    sel = (col[None, None, :] == (head_of_lane[None, :, None] * s + pos[:, None, None]))
    bd = (col[:, None] // s) == head_of_lane[None, :]
    hh = head_of_lane[:, None] == head_of_lane[None, :]
    return (jnp.asarray(tri, BF16), jnp.asarray(keep_add, F32), jnp.asarray(sel, BF16),
            jnp.asarray(bd, F32), jnp.asarray(hh, F32))


def _hgrn_kernel(qf_ref, ff_ref, vf_ref, qr_ref, fr_ref, vr_ref, lbf_ref, lbr_ref, trif_ref, keepf_ref, trir_ref,
                 keepr_ref, sel_ref, bd_ref, hh_ref, of_ref, or_ref, stf_ref, str_ref):
    @pl.when(pl.program_id(0) == 0)
    def _():
        stf_ref[...] = jnp.zeros(stf_ref.shape, F32)
        str_ref[...] = jnp.zeros(str_ref.shape, F32)

    st_f, st_r = stf_ref[...], str_ref[...]
    n_chunks = HG_BLOCK // HG_CHUNK
    for step in range(n_chunks):
        rows = slice(step * HG_CHUNK, (step + 1) * HG_CHUNK)
        o, st_f = _hgrn_chunk(qf_ref[rows, :], ff_ref[rows, :], vf_ref[rows, :], st_f, lbf_ref, trif_ref,
                              keepf_ref, sel_ref, bd_ref, hh_ref, rev=False)
        of_ref[rows, :] = o
        ch = n_chunks - 1 - step
        rows = slice(ch * HG_CHUNK, (ch + 1) * HG_CHUNK)
        o, st_r = _hgrn_chunk(qr_ref[rows, :], fr_ref[rows, :], vr_ref[rows, :], st_r, lbr_ref, trir_ref,
                              keepr_ref, sel_ref, bd_ref, hh_ref, rev=True)
        or_ref[rows, :] = o
    stf_ref[...] = st_f
    str_ref[...] = st_r


def _hgrn_chunk(q_bf, f_pre, v, st, lb_ref, tri_ref, keep_ref, sel_ref, bd_ref, hh_ref, *, rev):
    c_len = HG_CHUNK
    n_sub = c_len // HG_SUB
    w = B_HEADS * B_KEY_DIM
    lb = lb_ref[...]
    f = lb + (1.0 - lb) * _sigmoid(f_pre)
    k = 1.0 - f
    lf = jnp.log(f)
    q = q_bf.astype(F32)
    bd = bd_ref[...]

    c = _split_dot_left(tri_ref[...], lf)

    q3 = q.reshape(n_sub, HG_SUB, w)
    k3 = k.reshape(n_sub, HG_SUB, w)
    c3 = c.reshape(n_sub, HG_SUB, w)
    a_diag = None
    for sg in range(HG_SUB):
        ks = jnp.broadcast_to(k3[:, sg:sg + 1, :], k3.shape)
        cs = jnp.broadcast_to(c3[:, sg:sg + 1, :], c3.shape)
        wgt = (q3 * ks) * jnp.exp((c3 - cs) + keep_ref[sg][None])
        part = _dot(wgt.reshape(c_len, w).astype(BF16), sel_ref[sg])
        a_diag = part if a_diag is None else a_diag + part

    o_parts = [None] * n_sub
    for j in range(n_sub):
        r0 = j * HG_SUB
        r_last = r0 if rev else r0 + HG_SUB - 1
        e_j = c[r_last:r_last + 1, :]
        kj = k[r0:r0 + HG_SUB, :] * jnp.exp(e_j - c[r0:r0 + HG_SUB, :])
        kbd = (jnp.concatenate([kj] * B_HEADS, axis=0) * bd).astype(BF16)
        vbd = (jnp.concatenate([v[r0:r0 + HG_SUB, :].astype(F32)] * B_HEADS, axis=0) * bd).astype(BF16)
        lo, hi = (0, r0) if rev else (r0 + HG_SUB, c_len)
        pieces = [a_diag[r0:r0 + HG_SUB]]
        if hi > lo:
            qj = (q[lo:hi] * jnp.exp(c[lo:hi] - e_j)).astype(BF16)
            a_off = _dot_nt(qj, kbd)
            pieces = [a_off] + pieces if rev else pieces + [a_off]
        first = 0 if rev else j
        if (sum(p.shape[0] for p in pieces) % 16) != 0:
            pad = jnp.zeros((HG_SUB, a_diag.shape[1]), F32)
            pieces = pieces + [pad] if rev else [pad] + pieces
            first = first if rev else first - 1
        a_j = jnp.concatenate(pieces, axis=0).astype(BF16)
        contrib = _dot(a_j, vbd)
        for i in range(contrib.shape[0] // HG_SUB):
            piece = contrib[i * HG_SUB:(i + 1) * HG_SUB]
            o_parts[first + i] = piece if o_parts[first + i] is None else o_parts[first + i] + piece
    o = jnp.concatenate(o_parts, axis=0)

    o = o + _dot_nt((q * jnp.exp(c)).astype(BF16), st.astype(BF16))
    r_end = 0 if rev else c_len - 1
    c_end = c[r_end:r_end + 1, :]
    k_end = (k * jnp.exp(c_end - c)).astype(BF16)
    return o, jnp.exp(c_end) * st + _dot_tn(v, k_end) * hh_ref[...]


def _hgrn_scans(bq, f_fwd, f_rev, bi, lb_fwd, lb_rev, *, t_lat):
    n = bq.shape[0]
    c_len = HG_BLOCK
    n_lat = t_lat // c_len
    n_all = n // c_len
    n_ctx = n_all - n_lat

    def blk_fwd(i):
        return (jnp.where(i < n_ctx, n_lat + i, i - n_ctx), 0)

    def blk_rev(i):
        return (jnp.where(i < n_ctx, n_all - 1 - i, n_lat - 1 - (i - n_ctx)), 0)

    tile_f = pl.BlockSpec((c_len, 256), blk_fwd)
    tile_r = pl.BlockSpec((c_len, 256), blk_rev)
    tri_f, keep_f, sel, bd, hh = _hgrn_tables(False)
    tri_r, keep_r, _, _, _ = _hgrn_tables(True)
    consts = [lb_fwd, lb_rev, tri_f, keep_f, tri_r, keep_r, sel, bd, hh]
    return pl.pallas_call(
        _hgrn_kernel,
        grid=(n_all,),
        in_specs=[tile_f, tile_f, tile_f, tile_r, tile_r, tile_r] + [_const_spec(t.shape) for t in consts],
        out_specs=[tile_f, tile_r],
        out_shape=[jax.ShapeDtypeStruct((n, BRANCH_WIDTH), F32)] * 2,
        scratch_shapes=[pltpu.VMEM((256, 256), F32)] * 2,
        compiler_params=_cparams(("arbitrary",)),
        name="hgrn_scans",
    )(bq, f_fwd, bi, bq, f_rev, bi, *consts)


def _merge_kernel(x_ref, mods_ref, ya_ref, of_ref, or_ref, bg_ref, gain_ref, segm_ref, yc_ref, yd_ref, gt_ref,
                  wb_ref, wo_ref, nrm_ref, wg_ref, wu_ref, wd_ref, *rest, t_lat, tm):
    fin_ref = rest[0] if len(rest) == 2 else None
    o_ref = rest[-1]
    lat_tile = pl.program_id(0) * tm < t_lat
    yb = _head_norm(of_ref[...] + or_ref[...], gain_ref[...], segm_ref[...]) * _silu(bg_ref[...].astype(F32))
    branches = (ya_ref[...], yb.astype(BF16), yc_ref[...], yd_ref[...])
    merged = None
    for n, y in enumerate(branches):
        term = gt_ref[:, n * D_MODEL:(n + 1) * D_MODEL].astype(F32) * _dot(y, wb_ref[n])
        merged = term if merged is None else merged + term
    y = _dot(merged.astype(BF16), wo_ref[...])
    x_mid = x_ref[...] + _tile_mod(mods_ref, 5, lat_tile) * y
    out = _ffn_math(x_mid, lat_tile, mods_ref, nrm_ref, wg_ref, wu_ref, wd_ref, 6)
    o_ref[...] = out if fin_ref is None else _rms_rows(out) * fin_ref[...]


def _merge_ffn(x_all, mods, ya, o_fwd, o_rev, bg, b_gain, segm, yc, yd, gates, wb, wo, nrm, wg, wu, wd, *,
               layer, t_lat, n_rows, final_gain=None):
    n, d = x_all.shape
    tm = TOKEN_TILE
    final = final_gain is not None

    def tile(w):
        return pl.BlockSpec((tm, w), lambda i: (i, 0))

    return pl.pallas_call(
        functools.partial(_merge_kernel, t_lat=t_lat, tm=tm),
        grid=(n_rows // tm,),
        in_specs=[tile(d), _const_spec((2, N_MOD, d)), tile(256), tile(256), tile(256), tile(256),
                  _const_spec((1, 256)), _const_spec((256, 256)), tile(256), tile(256),
                  tile(N_BRANCH * d), _layer_spec(wb.shape, layer), _layer_spec(wo.shape, layer),
                  _const_spec((1, d)), _layer_spec(wg.shape, layer), _layer_spec(wu.shape, layer),
                  _layer_spec(wd.shape, layer)] + ([_const_spec((1, d))] if final else []),
        out_specs=tile(d),
        out_shape=jax.ShapeDtypeStruct((n_rows if final else n, d), F32),
        input_output_aliases={} if final else {0: 0},
        compiler_params=_cparams(("parallel",)),
        name="merge_ffn",
    )(x_all, mods, ya, o_fwd, o_rev, bg, b_gain, segm, yc, yd, gates, wb, wo, nrm, wg, wu, wd,
      *([final_gain] if final else []))


def _rope_tables_padded(t_lat, n_ctx):
    rows = t_lat // GRID_W
    half = HEAD_DIM // 2
    nf = half // 2
    inv = 1.0 / (ROPE_THETA ** (jnp.arange(0, half, 2, dtype=F32) / half))
    ang_r = jnp.arange(rows, dtype=F32)[:, None] * inv
    ang_c = jnp.arange(GRID_W, dtype=F32)[:, None] * inv

    def over_rows(a):
        return jnp.broadcast_to(a[:, None, :], (rows, GRID_W, nf))

    def over_cols(a):
        return jnp.broadcast_to(a[None, :, :], (rows, GRID_W, nf))

    cos = jnp.concatenate([over_rows(jnp.cos(ang_r))] * 2 + [over_cols(jnp.cos(ang_c))] * 2, axis=2)
    sin = jnp.concatenate([over_rows(-jnp.sin(ang_r)), over_rows(jnp.sin(ang_r)),
                           over_cols(-jnp.sin(ang_c)), over_cols(jnp.sin(ang_c))], axis=2)
    cos = cos.reshape(t_lat, HEAD_DIM)
    sin = sin.reshape(t_lat, HEAD_DIM)
    cos = jnp.concatenate([cos, jnp.ones((n_ctx, HEAD_DIM), F32)], axis=0)
    sin = jnp.concatenate([sin, jnp.zeros((n_ctx, HEAD_DIM), F32)], axis=0)
    return jnp.tile(cos, (1, 2)), jnp.tile(sin, (1, 2))


def _block_diag(w_group):
    g, ci, co = w_group.shape
    out = jnp.zeros((g * ci, g * co), w_group.dtype)
    for n in range(g):
        out = out.at[n * ci:(n + 1) * ci, n * co:(n + 1) * co].set(w_group[n])
    return out


def kernel(x, c, ctx, c_ctx, w_ada, b_ada, ffn1_norm, ffn1_w_gate, ffn1_w_up, ffn1_w_down, mix_norm, w_in, a_q_norm, a_k_norm, b_lb_logits, b_o_norm, c_w_group, c_scale, d_rel_bias, w_branch, w_out, ffn2_norm, ffn2_w_gate, ffn2_w_up, ffn2_w_down, final_norm):
    assert x.shape[0] == 1 and ctx.shape[0] == 1
    depth = w_ada.shape[0]
    t_lat = x.shape[1]
    n_ctx = ctx.shape[1]
    n = t_lat + n_ctx
    rows = t_lat // GRID_W
    assert t_lat % max(NA_QB, min(FLASH_TQ, t_lat), n_ctx) == 0 and n_ctx % TOKEN_TILE == 0 and rows >= 2 * NA_QROWS

    x_all = None
    mods_all = _ada_mods(c, c_ctx, w_ada, b_ada)
    cos, sin = _rope_tables_padded(t_lat, n_ctx)
    segm = (jnp.kron(jnp.eye(BRANCH_WIDTH // HEAD_DIM), jnp.ones((HEAD_DIM, HEAD_DIM))) / HEAD_DIM).astype(BF16)
    lb_all = jnp.cumsum(jax.nn.softmax(b_lb_logits.astype(F32), axis=0), axis=0)
    lb_all = lb_all - lb_all[:1]
    wg1, wu1, wd1 = ffn1_w_gate.astype(BF16), ffn1_w_up.astype(BF16), ffn1_w_down.astype(BF16)
    wg2, wu2, wd2 = ffn2_w_gate.astype(BF16), ffn2_w_up.astype(BF16), ffn2_w_down.astype(BF16)
    w_in_b, w_branch_b, w_out_b = w_in.astype(BF16), w_branch.astype(BF16), w_out.astype(BF16)
    na_bias, na_masks = _na_bias_tables(d_rel_bias, rows)

    for l in range(depth):
        with_ctx_out = l < depth - 1
        n_rows = n if with_ctx_out else t_lat
        mods = mods_all[l]
        x_all = _ffn_half(x[0] if l == 0 else x_all, mods, ffn1_norm[l][None], wg1, wu1, wd1, layer=l, t_lat=t_lat,
                          k0=0, n_rows=n, x_ctx=ctx[0] if l == 0 else None)
        (aq, ak, av, bq, bff, bfb, bi, bg, cx, dq, dk, dv, gates) = _in_proj(
            x_all, mods, mix_norm[l][None], w_in_b, cos, sin,
            jnp.tile(a_q_norm[l], A_HEADS)[None], jnp.tile(a_k_norm[l], A_KV_HEADS)[None], segm, layer=l, t_lat=t_lat)
        if with_ctx_out:
            ya, yd = _ctx_attn(aq, ak, av, dq, dk, dv, t_lat=t_lat)
        else:
            ya = jnp.zeros((n, BRANCH_WIDTH), BF16)
            yd = ya
        ya = _gqa_latent(aq, ak, av, ya, t_lat=t_lat)
        yd = _na_latent(dq, dk, dv, na_bias, na_masks, yd, layer=l, t_lat=t_lat)
        yc = _pool_mixer(cx, _block_diag(c_w_group[l]).astype(BF16), c_scale[l][None], t_lat=t_lat)
        o_fwd, o_rev = _hgrn_scans(bq, bff, bfb, bi, lb_all[l, 0][None], lb_all[l, 1][None], t_lat=t_lat)
        x_all = _merge_ffn(x_all, mods, ya, o_fwd, o_rev, bg, jnp.tile(b_o_norm[l], B_HEADS)[None], segm, yc, yd,
                           gates, w_branch_b, w_out_b, ffn2_norm[l][None], wg2, wu2, wd2, layer=l, t_lat=t_lat,
                           n_rows=n_rows, final_gain=None if with_ctx_out else final_norm[None])
    return x_all[None]
```

```python
import functools

import numpy as np
import jax
import jax.numpy as jnp
from jax import lax
from jax.experimental import pallas as pl
from jax.experimental.pallas import tpu as pltpu

F32 = jnp.float32
BF16 = jnp.bfloat16

D_MODEL = 1024
GRID_W = 64
HEAD_DIM = 64
BRANCH_WIDTH = 256
N_BRANCH = 4
A_HEADS = 4
A_KV_HEADS = 2
ROPE_THETA = 10000.0
B_HEADS = 4
B_KEY_DIM = 64
C_WINDOWS = (2, 4, 8, 16)
D_HEADS = 4
NA_WIN_R = 8
NA_WIN_C = 16
D_FF = 2816
N_MOD = 9
EPS = 1e-6
ATTN_SCALE = HEAD_DIM ** -0.5
LOG2_E = 1.4426950408889634
NEG_BIG = -1e30

V7X_VMEM_BYTES = 64 * 1024 * 1024
VMEM_LIMIT = V7X_VMEM_BYTES - 8 * 1024 * 1024

SEG_AQ = (0, 256)
SEG_AK = (256, 384)
SEG_AV = (384, 512)
SEG_BQ = (512, 768)
SEG_BFF = (768, 1024)
SEG_BFB = (1024, 1280)
SEG_BI = (1280, 1536)
SEG_BG = (1536, 1792)
SEG_CX = (1792, 2048)
SEG_DQ = (2048, 2304)
SEG_DK = (2304, 2560)
SEG_DV = (2560, 2816)
SEG_GATE = (2816, 2816 + N_BRANCH * D_MODEL)
IN_WIDTH = SEG_GATE[1]

ADA_TILE = 3072
TOKEN_TILE = 256
POOL_TILE = 2048
FLASH_TQ = 2048
FLASH_QC = 512
FLASH_SB = 256
FLASH_NBUF = 4
FLASH_MAX_EXPONENT_SPAN = 100.0
FLASH_BOUND_MARGIN = 1.02
NA_QROWS = 8
NA_QB = NA_QROWS * GRID_W
NA_HALO = 256
HG_BLOCK = 256
HG_CHUNK = 128
HG_SUB = 8


def _cparams(sem):
    return pltpu.CompilerParams(dimension_semantics=sem, vmem_limit_bytes=VMEM_LIMIT)


def _const_spec(shape):
    nd = len(shape)
    return pl.BlockSpec(shape, lambda *_: (0,) * nd, pipeline_mode=pl.Buffered(1))


def _layer_spec(shape, layer):
    nd = len(shape)
    return pl.BlockSpec((None,) + tuple(shape[1:]), lambda *_: (layer,) + (0,) * (nd - 1),
                        pipeline_mode=pl.Buffered(1))


def _sigmoid(x):
    return 1.0 / (1.0 + jnp.exp(-x))


def _silu(x):
    return x * _sigmoid(x)


def _dot(a, b):
    return jnp.dot(a, b, preferred_element_type=F32)


def _dot_nt(a, b):
    return lax.dot_general(a, b, (((1,), (1,)), ((), ())), preferred_element_type=F32)


def _dot_tn(a, b):
    return lax.dot_general(a, b, (((0,), (0,)), ((), ())), preferred_element_type=F32)


def _split_dot(x, m):
    hi = x.astype(BF16)
    r1 = x - hi.astype(F32)
    mid = r1.astype(BF16)
    lo = (r1 - mid.astype(F32)).astype(BF16)
    return _dot(hi, m) + _dot(mid, m) + _dot(lo, m)


def _split_dot_left(m, x):
    hi = x.astype(BF16)
    r1 = x - hi.astype(F32)
    mid = r1.astype(BF16)
    lo = (r1 - mid.astype(F32)).astype(BF16)
    return _dot(m, hi) + _dot(m, mid) + _dot(m, lo)


def _tile_mod(mods_ref, k, lat_tile):
    return jnp.where(lat_tile, mods_ref[0, k:k + 1, :], mods_ref[1, k:k + 1, :])


def _rms_rows(x):
    return x * lax.rsqrt(jnp.mean(x * x, axis=-1, keepdims=True) + EPS)


def _modulated_norm(x, nrm_ref, mods_ref, k_shift, lat_tile):
    col_scale = nrm_ref[...] * (1.0 + _tile_mod(mods_ref, k_shift + 1, lat_tile))
    return (_rms_rows(x) * col_scale + _tile_mod(mods_ref, k_shift, lat_tile)).astype(BF16)


def _ada_kernel(s_ref, w_ref, b_ref, o_ref):
    o_ref[0] = _dot(_silu(s_ref[...]).astype(BF16), w_ref[0].astype(BF16)) + b_ref[0]


def _ada_mods(c, c_ctx, w_ada, b_ada):
    depth = w_ada.shape[0]
    width = w_ada.shape[2]
    tn = ADA_TILE
    s = jnp.zeros((8, D_MODEL), F32).at[0].set(c[0]).at[1].set(c_ctx)
    out = pl.pallas_call(
        _ada_kernel,
        grid=(depth, width // tn),
        in_specs=[
            pl.BlockSpec((8, D_MODEL), lambda l, j: (0, 0)),
            pl.BlockSpec((1, D_MODEL, tn), lambda l, j: (l, 0, j)),
            pl.BlockSpec((1, 1, tn), lambda l, j: (l, 0, j)),
        ],
        out_specs=pl.BlockSpec((1, 8, tn), lambda l, j: (l, 0, j)),
        out_shape=jax.ShapeDtypeStruct((depth, 8, width), F32),
        compiler_params=_cparams(("parallel", "parallel")),
        name="ada_mods",
    )(s, w_ada, b_ada.reshape(depth, 1, width))
    return out[:, :2].reshape(depth, 2, N_MOD, D_MODEL)


def _ffn_math(x, lat_tile, mods_ref, nrm_ref, wg_ref, wu_ref, wd_ref, k0):
    z = _modulated_norm(x, nrm_ref, mods_ref, k0, lat_tile)
    g = _dot(z, wg_ref[...])
    u = _dot(z, wu_ref[...])
    a = (_silu(g) * u).astype(BF16)
    return x + (0.5 * _tile_mod(mods_ref, k0 + 2, lat_tile)) * _dot(a, wd_ref[...])


def _ffn_kernel(*refs, t_lat, tm, k0, split_input):
    if split_input:
        xl_ref, xc_ref, mods_ref, nrm_ref, wg_ref, wu_ref, wd_ref, o_ref = refs
    else:
        x_ref, mods_ref, nrm_ref, wg_ref, wu_ref, wd_ref, o_ref = refs
    lat_tile = pl.program_id(0) * tm < t_lat
    x = jnp.where(lat_tile, xl_ref[...], xc_ref[...]) if split_input else x_ref[...]
    o_ref[...] = _ffn_math(x, lat_tile, mods_ref, nrm_ref, wg_ref, wu_ref, wd_ref, k0)


def _ffn_half(x_all, mods, nrm, wg, wu, wd, *, layer, t_lat, k0, n_rows, x_ctx=None):
    split_input = x_ctx is not None
    d = x_all.shape[1]
    n = x_all.shape[0] + (x_ctx.shape[0] if split_input else 0)
    tm = TOKEN_TILE
    if split_input:
        n_lat_tiles = t_lat // tm
        x_specs = [pl.BlockSpec((tm, d), lambda i: (jnp.minimum(i, n_lat_tiles - 1), 0)),
                   pl.BlockSpec((tm, d), lambda i: (jnp.maximum(i - n_lat_tiles, 0), 0))]
        x_args = [x_all, x_ctx]
    else:
        x_specs = [pl.BlockSpec((tm, d), lambda i: (i, 0))]
        x_args = [x_all]
    return pl.pallas_call(
        functools.partial(_ffn_kernel, t_lat=t_lat, tm=tm, k0=k0, split_input=split_input),
        grid=(n_rows // tm,),
        in_specs=x_specs + [
            _const_spec((2, N_MOD, d)),
            _const_spec((1, d)),
            _layer_spec(wg.shape, layer),
            _layer_spec(wu.shape, layer),
            _layer_spec(wd.shape, layer),
        ],
        out_specs=pl.BlockSpec((tm, d), lambda i: (i, 0)),
        out_shape=jax.ShapeDtypeStruct((n, d), F32),
        input_output_aliases={} if split_input else {0: 0},
        compiler_params=_cparams(("parallel",)),
        name="ffn_half",
    )(*x_args, mods, nrm, wg, wu, wd)


def _head_norm(y, gain, segm):
    ms = _split_dot(y * y, segm)
    return y * lax.rsqrt(ms + EPS) * gain


def _rope(y, cos, sin_signed):
    w = y.shape[1]
    lane = lax.broadcasted_iota(jnp.int32, y.shape, 1)
    partner = jnp.where((lane & 31) < 16, pltpu.roll(y, w - 16, 1), pltpu.roll(y, 16, 1))
    return y * cos + partner * sin_signed


def _inproj_kernel(x_ref, mods_ref, nrm_ref, w_ref, cos_ref, sin_ref, gq_ref, gk_ref, segm_ref,
                   aq_ref, ak_ref, av_ref, bq_ref, bff_ref, bfb_ref, bi_ref, bg_ref, cx_ref,
                   dq_ref, dk_ref, dv_ref, gt_ref, *, t_lat, tm):
    z = _modulated_norm(x_ref[...], nrm_ref, mods_ref, 3, pl.program_id(0) * tm < t_lat)

    def proj(seg):
        return _dot(z, w_ref[:, seg[0]:seg[1]])

    cos = cos_ref[...]
    sin = sin_ref[...]
    q = _rope(_head_norm(proj(SEG_AQ), gq_ref[...], segm_ref[...]),
              jnp.concatenate([cos, cos], axis=1), jnp.concatenate([sin, sin], axis=1))
    aq_ref[...] = (q * (ATTN_SCALE * LOG2_E)).T.astype(BF16)
    k = _rope(_head_norm(proj(SEG_AK), gk_ref[...], segm_ref[0:128, 0:128]), cos, sin)
    zeros = jnp.zeros((tm, HEAD_DIM), F32)
    ak_ref[...] = jnp.concatenate([k[:, :HEAD_DIM], zeros, k[:, HEAD_DIM:], zeros], axis=1).astype(BF16)
    v = proj(SEG_AV)
    one_hot = jnp.where(lax.broadcasted_iota(jnp.int32, (tm, HEAD_DIM), 1) == 0, 1.0, 0.0)
    av_ref[...] = jnp.concatenate([v[:, :HEAD_DIM], one_hot, v[:, HEAD_DIM:], one_hot], axis=1).T.astype(BF16)

    bq_ref[...] = (proj(SEG_BQ) * (B_KEY_DIM ** -0.5)).astype(BF16)
    bff_ref[...] = proj(SEG_BFF)
    bfb_ref[...] = proj(SEG_BFB)
    bi_ref[...] = proj(SEG_BI).astype(BF16)
    bg_ref[...] = proj(SEG_BG).astype(BF16)
    cx_ref[...] = proj(SEG_CX)
    dq_ref[...] = (proj(SEG_DQ) * ATTN_SCALE).astype(BF16)
    dk_ref[...] = proj(SEG_DK).astype(BF16)
    dv = proj(SEG_DV)
    dv_ref[...] = jnp.concatenate(
        [piece for h in range(D_HEADS) for piece in (dv[:, HEAD_DIM * h:HEAD_DIM * (h + 1)], one_hot)],
        axis=1).astype(BF16)
    gw = 512
    for c0 in range(SEG_GATE[0], SEG_GATE[1], gw):
        o0 = c0 - SEG_GATE[0]
        gt_ref[:, o0:o0 + gw] = _sigmoid(proj((c0, c0 + gw))).astype(BF16)


def _in_proj(x_all, mods, nrm, w_in, cos, sin, gq, gk, segm, *, layer, t_lat):
    n, d = x_all.shape
    tm = TOKEN_TILE
    widths = [(256, BF16), (256, BF16), (256, BF16), (256, BF16), (256, F32), (256, F32), (256, BF16),
              (256, BF16), (256, F32), (256, BF16), (256, BF16), (2 * D_HEADS * HEAD_DIM, BF16),
              (N_BRANCH * D_MODEL, BF16)]
    transposed = (0, 2)
    out_specs = [pl.BlockSpec((w, tm), lambda i: (0, i)) if o in transposed else pl.BlockSpec((tm, w), lambda i: (i, 0))
                 for o, (w, _) in enumerate(widths)]
    out_shape = [jax.ShapeDtypeStruct((w, n) if o in transposed else (n, w), dt) for o, (w, dt) in enumerate(widths)]
    return pl.pallas_call(
        functools.partial(_inproj_kernel, t_lat=t_lat, tm=tm),
        grid=(n // tm,),
        in_specs=[
            pl.BlockSpec((tm, d), lambda i: (i, 0)),
            _const_spec((2, N_MOD, d)),
            _const_spec((1, d)),
            _layer_spec(w_in.shape, layer),
            pl.BlockSpec((tm, 128), lambda i: (i, 0)),
            pl.BlockSpec((tm, 128), lambda i: (i, 0)),
            _const_spec((1, 256)),
            _const_spec((1, 128)),
            _const_spec((256, 256)),
        ],
        out_specs=out_specs,
        out_shape=out_shape,
        compiler_params=_cparams(("parallel",)),
        name="in_proj",
    )(x_all, mods, nrm, w_in, cos, sin, gq, gk, segm)


def _softmax_attend(q, k, v, exp_fn):
    s = _dot_nt(q, k)
    p = exp_fn(s - jnp.max(s, axis=-1, keepdims=True))
    return _dot(p.astype(BF16), v) / jnp.sum(p, axis=-1, keepdims=True)


def _ctx_attn_kernel(aqt_ref, ak_ref, avt_ref, dq_ref, dk_ref, dv_ref, ya_ref, yd_ref):
    outs = []
    for h in range(A_HEADS):
        g = h // (A_HEADS // A_KV_HEADS)
        st = _dot(ak_ref[:, 128 * g:128 * g + 64], aqt_ref[64 * h:64 * h + 64, :])
        pt = jnp.exp2(st - jnp.max(st, axis=0, keepdims=True)).astype(BF16)
        acc = _dot(avt_ref[128 * g:128 * g + 128, :], pt)
        outs.append((acc[:HEAD_DIM] / acc[HEAD_DIM:HEAD_DIM + 1]).T)
    ya_ref[...] = jnp.concatenate(outs, axis=1).astype(BF16)
    outs = []
    for h in range(D_HEADS):
        hs = slice(64 * h, 64 * h + 64)
        outs.append(_softmax_attend(dq_ref[:, hs], dk_ref[:, hs], dv_ref[:, 128 * h:128 * h + 64], jnp.exp))
    yd_ref[...] = jnp.concatenate(outs, axis=1).astype(BF16)


def _ctx_attn(aqt, ak, avt, dq, dk, dv, *, t_lat):
    n = ak.shape[0]
    n_ctx = n - t_lat
    blk = t_lat // n_ctx

    def spec(w):
        return pl.BlockSpec((n_ctx, w), lambda i: (blk, 0))

    def spec_t(w):
        return pl.BlockSpec((w, n_ctx), lambda i: (0, blk))

    return pl.pallas_call(
        _ctx_attn_kernel,
        grid=(1,),
        in_specs=[spec_t(256), spec(256), spec_t(256), spec(256), spec(256), spec(dv.shape[1])],
        out_specs=[spec(256), spec(256)],
        out_shape=[jax.ShapeDtypeStruct((n, BRANCH_WIDTH), BF16)] * 2,
        compiler_params=_cparams(("arbitrary",)),
        name="ctx_attn",
    )(aqt, ak, avt, dq, dk, dv)


def _flash_kernel(qt_ref, k_ref, vt_ref, prev_ref, o_ref, m_ref, acc_ref, *bufs, tk, tq):
    del prev_ref
    s_refs, p_refs = bufs[:FLASH_NBUF], bufs[FLASH_NBUF:]
    j = pl.program_id(1)

    @pl.when(j == 0)
    def _():
        m_ref[...] = jnp.full(m_ref.shape, -jnp.inf, F32)
        acc_ref[...] = jnp.zeros(acc_ref.shape, F32)

    group = A_HEADS // A_KV_HEADS
    n_qc = tq // FLASH_QC
    n_sb = tk // FLASH_SB
    chains = [(h, c) for h in range(A_HEADS) for c in range(n_qc)]

    def score_block(ci, sb):
        h, c = chains[ci]
        g = h // group
        rows = slice(sb * FLASH_SB, (sb + 1) * FLASH_SB)
        st = _dot(k_ref[rows, 128 * g:128 * g + 64], qt_ref[64 * h:64 * h + 64, c * FLASH_QC:(c + 1) * FLASH_QC])
        s_refs[ci % FLASH_NBUF][rows, :] = st
        return jnp.max(st, axis=0, keepdims=True)

    def exp_block(ci, sb, m_new):
        rows = slice(sb * FLASH_SB, (sb + 1) * FLASH_SB)
        p_refs[ci % 2][rows, :] = jnp.exp2(s_refs[ci % FLASH_NBUF][rows, :] - m_new).astype(BF16)

    def value_block(ci, sb):
        g = chains[ci][0] // group
        rows = slice(sb * FLASH_SB, (sb + 1) * FLASH_SB)
        return _dot(vt_ref[128 * g:128 * g + 128, rows], p_refs[ci % 2][rows, :])

    def fold(a, b):
        return b if a is None else a + b

    def fold_max(a, b):
        return b if a is None else jnp.maximum(a, b)

    n_ch = len(chains)
    blk_max, m_new, alpha = {}, {}, {}
    for stage in range(-1, n_ch + 1):
        c_s, c_e, c_v = stage + 1, stage, stage - 1
        if 0 <= c_e < n_ch:
            h, c = chains[c_e]
            cols = slice(c * FLASH_QC, (c + 1) * FLASH_QC)
            m_old = m_ref[h, :, cols]
            m_new[c_e] = jnp.maximum(m_old, blk_max.pop(c_e))
            m_ref[h, :, cols] = m_new[c_e]
            alpha[c_e] = jnp.exp2(m_old - m_new[c_e])
        part, mx = None, None
        for sb in range(n_sb):
            if c_s < n_ch:
                mx = fold_max(mx, score_block(c_s, sb))
            if 0 <= c_e < n_ch:
                exp_block(c_e, sb, m_new[c_e])
            if 0 <= c_v:
                part = fold(part, value_block(c_v, sb))
        if c_s < n_ch:
            blk_max[c_s] = mx
        if 0 <= c_v:
            h, c = chains[c_v]
            cols = slice(c * FLASH_QC, (c + 1) * FLASH_QC)
            acc_ref[h, :, cols] = alpha.pop(c_v) * acc_ref[h, :, cols] + part

    @pl.when(j == pl.num_programs(1) - 1)
    def _():
        outs = []
        for h in range(A_HEADS):
            acc = acc_ref[h]
            outs.append((acc[:HEAD_DIM] / acc[HEAD_DIM:HEAD_DIM + 1]).T)
        o_ref[...] = jnp.concatenate(outs, axis=1).astype(BF16)


def _flash_bounded_kernel(bound_ref, qt_ref, k_ref, vt_ref, prev_ref, o_ref, acc_ref, *p_refs, tk, tq):
    del prev_ref
    j = pl.program_id(1)

    @pl.when(j == 0)
    def _():
        acc_ref[...] = jnp.zeros(acc_ref.shape, F32)

    bound = bound_ref[0]
    group = A_HEADS // A_KV_HEADS
    n_qc = tq // FLASH_QC
    n_sb = tk // FLASH_SB
    chains = [(h, c) for h in range(A_HEADS) for c in range(n_qc)]

    def prob_block(ci, sb):
        h, c = chains[ci]
        g = h // group
        rows = slice(sb * FLASH_SB, (sb + 1) * FLASH_SB)
        st = _dot(k_ref[rows, 128 * g:128 * g + 64], qt_ref[64 * h:64 * h + 64, c * FLASH_QC:(c + 1) * FLASH_QC])
        p_refs[ci % 2][rows, :] = jnp.exp2(st - bound).astype(BF16)

    def value_block(ci, sb):
        g = chains[ci][0] // group
        rows = slice(sb * FLASH_SB, (sb + 1) * FLASH_SB)
        return _dot(vt_ref[128 * g:128 * g + 128, rows], p_refs[ci % 2][rows, :])

    n_ch = len(chains)
    for stage in range(n_ch + 1):
        part = None
        for sb in range(n_sb):
            if stage < n_ch:
                prob_block(stage, sb)
            if stage >= 1:
                blk = value_block(stage - 1, sb)
                part = blk if part is None else part + blk
        if stage >= 1:
            h, c = chains[stage - 1]
            cols = slice(c * FLASH_QC, (c + 1) * FLASH_QC)
            acc_ref[h, :, cols] = acc_ref[h, :, cols] + part

    @pl.when(j == pl.num_programs(1) - 1)
    def _():
        outs = []
        for h in range(A_HEADS):
            acc = acc_ref[h]
            outs.append((acc[:HEAD_DIM] / acc[HEAD_DIM:HEAD_DIM + 1]).T)
        o_ref[...] = jnp.concatenate(outs, axis=1).astype(BF16)


def _flash_tk(n):
    for tk in (1280, 768, 512, 256):
        if n % tk == 0:
            return tk
    raise ValueError(f"unsupported key count {n}")


def _gqa_latent(aqt, ak, avt, ya_prev, score_bound, *, t_lat):
    n = ak.shape[0]
    tk = _flash_tk(n)
    tq = min(FLASH_TQ, t_lat)

    def call(body, extra_specs, scratch, name):
        return pl.pallas_call(
            functools.partial(body, tk=tk, tq=tq),
            grid=(t_lat // tq, n // tk),
            in_specs=extra_specs + [
                pl.BlockSpec((256, tq), lambda i, j: (0, i)),
                pl.BlockSpec((tk, 256), lambda i, j: (j, 0)),
                pl.BlockSpec((256, tk), lambda i, j: (0, j)),
                pl.BlockSpec(memory_space=pl.ANY),
            ],
            out_specs=pl.BlockSpec((tq, 256), lambda i, j: (i, 0)),
            out_shape=jax.ShapeDtypeStruct((n, BRANCH_WIDTH), BF16),
            scratch_shapes=scratch,
            input_output_aliases={len(extra_specs) + 3: 0},
            compiler_params=_cparams(("parallel", "arbitrary")),
            name=name,
        )

    def online(_):
        scratch = ([pltpu.VMEM((A_HEADS, 1, tq), F32), pltpu.VMEM((A_HEADS, 128, tq), F32)]
                   + [pltpu.VMEM((tk, FLASH_QC), F32)] * FLASH_NBUF + [pltpu.VMEM((tk, FLASH_QC), BF16)] * 2)
        return call(_flash_kernel, [], scratch, "gqa_flash")(aqt, ak, avt, ya_prev)

    def bounded(_):
        scratch = [pltpu.VMEM((A_HEADS, 128, tq), F32)] + [pltpu.VMEM((tk, FLASH_QC), BF16)] * 2
        return call(_flash_bounded_kernel, [pl.BlockSpec(memory_space=pltpu.SMEM)], scratch,
                    "gqa_flash_bounded")(score_bound.reshape(1), aqt, ak, avt, ya_prev)

    return lax.cond(2.0 * score_bound <= FLASH_MAX_EXPONENT_SPAN, bounded, online, None)


def _na_kernel(q_ref, kp_ref, km_ref, kn_ref, vp_ref, vm_ref, vn_ref, kc_ref, vc_ref, bias_ref, mask_ref, prev_ref,
               o_ref, bm_ref, *, nb):
    del prev_ref
    b = pl.program_id(0)

    @pl.when((b == 0) | (b == 1) | (b == nb - 1))
    def _():
        for h in range(D_HEADS):
            bm_ref[h] = bias_ref[h] + mask_ref[0]

    def scores(h):
        hs = slice(64 * h, 64 * h + 64)
        q = q_ref[:, hs]
        kcat = jnp.concatenate([kp_ref[:, hs], km_ref[:, hs], kn_ref[:, hs]], axis=0)
        return _dot_nt(q, kcat) + bm_ref[h], _dot_nt(q, kc_ref[:, hs])

    def softmax(s_loc, s_ctx):
        m = jnp.maximum(jnp.max(s_loc, axis=-1, keepdims=True), jnp.max(s_ctx, axis=-1, keepdims=True))
        return jnp.exp(s_loc - m).astype(BF16), jnp.exp(s_ctx - m).astype(BF16)

    def attend(h, p_loc, p_ctx):
        vs = slice(128 * h, 128 * h + 128)
        vcat = jnp.concatenate([vp_ref[:, vs], vm_ref[:, vs], vn_ref[:, vs]], axis=0)
        o = _dot(p_loc, vcat) + _dot(p_ctx, vc_ref[:, vs])
        return o[:, :HEAD_DIM] / o[:, HEAD_DIM:HEAD_DIM + 1]

    s = {0: scores(0)}
    p, outs = {}, []
    for h in range(D_HEADS):
        if h + 1 < D_HEADS:
            s[h + 1] = scores(h + 1)
        p[h] = softmax(*s.pop(h))
        if h >= 1:
            outs.append(attend(h - 1, *p.pop(h - 1)))
    outs.append(attend(D_HEADS - 1, *p.pop(D_HEADS - 1)))
    o_ref[...] = jnp.concatenate(outs, axis=1).astype(BF16)


def _na_bias_tables(rel_bias, rows):
    wr = min(NA_WIN_R, rows)
    halo_rows = NA_HALO // GRID_W
    krows = NA_QROWS + 2 * halo_rows
    nb = rows // NA_QROWS
    qc = np.arange(GRID_W)[:, None]
    kc = np.arange(GRID_W)[None, :]
    cs = np.clip(qc - NA_WIN_C // 2, 0, GRID_W - NA_WIN_C)
    in_col = (kc >= cs) & (kc < cs + NA_WIN_C)
    e_col = (kc - qc + (NA_WIN_C - 1))[:, :, None] == np.arange(2 * NA_WIN_C - 1)
    qr_l = np.arange(NA_QROWS)[:, None]
    kr_l = np.arange(krows)[None, :]
    e_row = (kr_l - halo_rows - qr_l + (NA_WIN_R - 1))[:, :, None] == np.arange(2 * NA_WIN_R - 1)
    hi = lax.Precision.HIGHEST
    tmp = jnp.einsum("lhrc,qkr->lhqkc", rel_bias.astype(F32), e_row.astype(np.float32), precision=hi)
    full = jnp.einsum("lhqkc,pjc->lhqpkj", tmp, e_col.astype(np.float32), precision=hi)
    masks = []
    for b in (0, min(1, nb - 1), nb - 1):
        qr = NA_QROWS * b + qr_l
        kr = NA_QROWS * b - halo_rows + kr_l
        rs = np.clip(qr - wr // 2, 0, rows - wr)
        in_row = (kr >= rs) & (kr < rs + wr)
        mask = in_row[:, None, :, None] & in_col[None, :, None, :]
        masks.append(np.where(mask, 0.0, NEG_BIG).reshape(NA_QB, krows * GRID_W))
    return full.reshape(-1, D_HEADS, NA_QB, krows * GRID_W), np.stack(masks).astype(np.float32)


def _na_latent(dq, dk, dv, bias, masks, yd_prev, *, layer, t_lat):
    n = dq.shape[0]
    nb = t_lat // NA_QB
    r = NA_QB // NA_HALO
    last_halo = t_lat // NA_HALO - 1
    ctx_blk = t_lat // (n - t_lat)
    n_ctx = n - t_lat

    def prev_map(b):
        return (jnp.maximum(r * b - 1, 0), 0)

    def next_map(b):
        return (jnp.minimum(r * b + r, last_halo), 0)

    def variant(b):
        return (jnp.where(b == 0, 0, jnp.where(b == nb - 1, 2, 1)), 0, 0)

    def specs(w):
        return (pl.BlockSpec((NA_QB, w), lambda b: (b, 0)), pl.BlockSpec((NA_HALO, w), prev_map),
                pl.BlockSpec((NA_HALO, w), next_map), pl.BlockSpec((n_ctx, w), lambda b: (ctx_blk, 0)))

    main, prev, nxt, ctx = specs(256)
    vmain, vprev, vnxt, vctx = specs(dv.shape[1])
    return pl.pallas_call(
        functools.partial(_na_kernel, nb=nb),
        grid=(nb,),
        in_specs=[main, prev, main, nxt, vprev, vmain, vnxt, ctx, vctx,
                  _layer_spec(bias.shape, layer),
                  pl.BlockSpec((1, NA_QB, NA_QB + 2 * NA_HALO), variant),
                  pl.BlockSpec(memory_space=pl.ANY)],
        out_specs=main,
        out_shape=jax.ShapeDtypeStruct((n, BRANCH_WIDTH), BF16),
        scratch_shapes=[pltpu.VMEM((D_HEADS, NA_QB, NA_QB + 2 * NA_HALO), F32)],
        input_output_aliases={11: 0},
        compiler_params=_cparams(("arbitrary",)),
        name="na_attn",
    )(dq, dk, dk, dk, dv, dv, dv, dk, dv, bias, masks, yd_prev)


def _pool_kernel(x_ref, xp_ref, xn_ref, w_ref, sc_ref, *rest, seg_lo, seg_hi, tm):
    o_ref, cat_ref, a_ref, b_ref = rest[-4:]
    start = seg_lo + pl.program_id(0) * tm
    x = x_ref[...]
    cat_ref[0:8, :] = jnp.where(start > seg_lo, xp_ref[...], 0.0)
    cat_ref[8:8 + tm, :] = x
    cat_ref[8 + tm:16 + tm, :] = jnp.where(start + tm < seg_hi, xn_ref[...], 0.0)
    cat_ref[16 + tm:32 + tm, :] = jnp.zeros((16, BRANCH_WIDTH), F32)
    a_ref[0:tm + 24, :] = cat_ref[0:tm + 24, :] + cat_ref[1:tm + 25, :]
    s2 = a_ref[7:7 + tm, :]
    b_ref[0:tm + 16, :] = a_ref[0:tm + 16, :] + a_ref[2:tm + 18, :]
    s4 = b_ref[6:6 + tm, :]
    a_ref[0:tm + 8, :] = b_ref[0:tm + 8, :] + b_ref[4:tm + 12, :]
    s8 = a_ref[4:4 + tm, :]
    s16 = a_ref[0:tm, :] + a_ref[8:8 + tm, :]

    pos = start - seg_lo + lax.broadcasted_iota(jnp.int32, (tm, 1), 0)
    seg_len = seg_hi - seg_lo

    def mean(sm, w):
        lo = jnp.clip(pos - w // 2, 0, seg_len)
        hi = jnp.clip(pos - w // 2 + w, 0, seg_len)
        return sm / (hi - lo).astype(F32)

    lane = lax.broadcasted_iota(jnp.int32, (tm, BRANCH_WIDTH), 1)
    gw = BRANCH_WIDTH // len(C_WINDOWS)
    pooled = jnp.where(lane < gw, mean(s2, 2),
                       jnp.where(lane < 2 * gw, mean(s4, 4),
                                 jnp.where(lane < 3 * gw, mean(s8, 8), mean(s16, 16)))) - x
    o_ref[...] = (_dot(pooled.astype(BF16), w_ref[...]) * sc_ref[...]).astype(BF16)


def _pool_segment(cx, w_bd, scale, prev_out, *, seg_lo, seg_hi, tm):
    n = cx.shape[0]
    r = tm // 8
    b0 = seg_lo // tm
    last8 = n // 8 - 1
    in_specs = [
        pl.BlockSpec((tm, 256), lambda i: (b0 + i, 0)),
        pl.BlockSpec((8, 256), lambda i: (jnp.maximum(r * (b0 + i) - 1, 0), 0)),
        pl.BlockSpec((8, 256), lambda i: (jnp.minimum(r * (b0 + i) + r, last8), 0)),
        _const_spec((256, 256)),
        _const_spec((1, 256)),
    ]
    args = [cx, cx, cx, w_bd, scale]
    aliases = {}
    if prev_out is not None:
        in_specs.append(pl.BlockSpec(memory_space=pl.ANY))
        args.append(prev_out)
        aliases = {5: 0}
    return pl.pallas_call(
        functools.partial(_pool_kernel, seg_lo=seg_lo, seg_hi=seg_hi, tm=tm),
        grid=((seg_hi - seg_lo) // tm,),
        in_specs=in_specs,
        out_specs=pl.BlockSpec((tm, 256), lambda i: (b0 + i, 0)),
        out_shape=jax.ShapeDtypeStruct((n, BRANCH_WIDTH), BF16),
        scratch_shapes=[pltpu.VMEM((tm + 32, 256), F32)] * 3,
        input_output_aliases=aliases,
        compiler_params=_cparams(("parallel",)),
        name="pool_mixer",
    )(*args)


def _pool_mixer(cx, w_bd, scale, *, t_lat):
    n = cx.shape[0]
    n_ctx = n - t_lat
    out = _pool_segment(cx, w_bd, scale, None, seg_lo=t_lat, seg_hi=n, tm=n_ctx)
    return _pool_segment(cx, w_bd, scale, out, seg_lo=0, seg_hi=t_lat, tm=min(POOL_TILE, t_lat))


def _hgrn_tables(rev):
    c_len, s, w = HG_CHUNK, HG_SUB, B_HEADS * B_KEY_DIM
    t = np.arange(c_len)
    tri = (t[None, :] >= t[:, None]) if rev else (t[None, :] <= t[:, None])
    pos = np.arange(s)
    keep = (pos[None, :] <= pos[:, None]) if rev else (pos[None, :] >= pos[:, None])
    keep_add = np.where(keep, 0.0, NEG_BIG)[:, :, None] * np.ones((1, 1, w))
    head_of_lane = np.arange(w) // B_KEY_DIM
    col = np.arange(B_HEADS * s)
    sel = (col[None, None, :] == (head_of_lane[None, :, None] * s + pos[:, None, None]))
    bd = (col[:, None] // s) == head_of_lane[None, :]
    hh = head_of_lane[:, None] == head_of_lane[None, :]
    return (jnp.asarray(tri, BF16), jnp.asarray(keep_add, F32), jnp.asarray(sel, BF16),
            jnp.asarray(bd, F32), jnp.asarray(hh, F32))


def _hgrn_kernel(qf_ref, ff_ref, vf_ref, qr_ref, fr_ref, vr_ref, lbf_ref, lbr_ref, trif_ref, keepf_ref, trir_ref,
                 keepr_ref, sel_ref, bd_ref, hh_ref, of_ref, or_ref, stf_ref, str_ref):
    @pl.when(pl.program_id(0) == 0)
    def _():
        stf_ref[...] = jnp.zeros(stf_ref.shape, F32)
        str_ref[...] = jnp.zeros(str_ref.shape, F32)

    st_f, st_r = stf_ref[...], str_ref[...]
    n_chunks = HG_BLOCK // HG_CHUNK
    for step in range(n_chunks):
        rows = slice(step * HG_CHUNK, (step + 1) * HG_CHUNK)
        o, st_f = _hgrn_chunk(qf_ref[rows, :], ff_ref[rows, :], vf_ref[rows, :], st_f, lbf_ref, trif_ref,
                              keepf_ref, sel_ref, bd_ref, hh_ref, rev=False)
        of_ref[rows, :] = o
        ch = n_chunks - 1 - step
        rows = slice(ch * HG_CHUNK, (ch + 1) * HG_CHUNK)
        o, st_r = _hgrn_chunk(qr_ref[rows, :], fr_ref[rows, :], vr_ref[rows, :], st_r, lbr_ref, trir_ref,
                              keepr_ref, sel_ref, bd_ref, hh_ref, rev=True)
        or_ref[rows, :] = o
    stf_ref[...] = st_f
    str_ref[...] = st_r


def _hgrn_chunk(q_bf, f_pre, v, st, lb_ref, tri_ref, keep_ref, sel_ref, bd_ref, hh_ref, *, rev):
    c_len = HG_CHUNK
    n_sub = c_len // HG_SUB
    w = B_HEADS * B_KEY_DIM
    lb = lb_ref[...]
    f = lb + (1.0 - lb) * _sigmoid(f_pre)
    k = 1.0 - f
    lf = jnp.log(f)
    q = q_bf.astype(F32)
    bd = bd_ref[...]

    c = _split_dot_left(tri_ref[...], lf)

    q3 = q.reshape(n_sub, HG_SUB, w)
    k3 = k.reshape(n_sub, HG_SUB, w)
    c3 = c.reshape(n_sub, HG_SUB, w)
    a_diag = None
    for sg in range(HG_SUB):
        ks = jnp.broadcast_to(k3[:, sg:sg + 1, :], k3.shape)
        cs = jnp.broadcast_to(c3[:, sg:sg + 1, :], c3.shape)
        wgt = (q3 * ks) * jnp.exp((c3 - cs) + keep_ref[sg][None])
        part = _dot(wgt.reshape(c_len, w).astype(BF16), sel_ref[sg])
        a_diag = part if a_diag is None else a_diag + part

    o_parts = [None] * n_sub
    for j in range(n_sub):
        r0 = j * HG_SUB
        r_last = r0 if rev else r0 + HG_SUB - 1
        e_j = c[r_last:r_last + 1, :]
        kj = k[r0:r0 + HG_SUB, :] * jnp.exp(e_j - c[r0:r0 + HG_SUB, :])
        kbd = (jnp.concatenate([kj] * B_HEADS, axis=0) * bd).astype(BF16)
        vbd = (jnp.concatenate([v[r0:r0 + HG_SUB, :].astype(F32)] * B_HEADS, axis=0) * bd).astype(BF16)
        lo, hi = (0, r0) if rev else (r0 + HG_SUB, c_len)
        pieces = [a_diag[r0:r0 + HG_SUB]]
        if hi > lo:
            qj = (q[lo:hi] * jnp.exp(c[lo:hi] - e_j)).astype(BF16)
            a_off = _dot_nt(qj, kbd)
            pieces = [a_off] + pieces if rev else pieces + [a_off]
        first = 0 if rev else j
        if (sum(p.shape[0] for p in pieces) % 16) != 0:
            pad = jnp.zeros((HG_SUB, a_diag.shape[1]), F32)
            pieces = pieces + [pad] if rev else [pad] + pieces
            first = first if rev else first - 1
        a_j = jnp.concatenate(pieces, axis=0).astype(BF16)
        contrib = _dot(a_j, vbd)
        for i in range(contrib.shape[0] // HG_SUB):
            piece = contrib[i * HG_SUB:(i + 1) * HG_SUB]
            o_parts[first + i] = piece if o_parts[first + i] is None else o_parts[first + i] + piece
    o = jnp.concatenate(o_parts, axis=0)

    o = o + _dot_nt((q * jnp.exp(c)).astype(BF16), st.astype(BF16))
    r_end = 0 if rev else c_len - 1
    c_end = c[r_end:r_end + 1, :]
    k_end = (k * jnp.exp(c_end - c)).astype(BF16)
    return o, jnp.exp(c_end) * st + _dot_tn(v, k_end) * hh_ref[...]


def _hgrn_scans(bq, f_fwd, f_rev, bi, lb_fwd, lb_rev, *, t_lat):
    n = bq.shape[0]
    c_len = HG_BLOCK
    n_lat = t_lat // c_len
    n_all = n // c_len
    n_ctx = n_all - n_lat

    def blk_fwd(i):
        return (jnp.where(i < n_ctx, n_lat + i, i - n_ctx), 0)

    def blk_rev(i):
        return (jnp.where(i < n_ctx, n_all - 1 - i, n_lat - 1 - (i - n_ctx)), 0)

    tile_f = pl.BlockSpec((c_len, 256), blk_fwd)
    tile_r = pl.BlockSpec((c_len, 256), blk_rev)
    tri_f, keep_f, sel, bd, hh = _hgrn_tables(False)
    tri_r, keep_r, _, _, _ = _hgrn_tables(True)
    consts = [lb_fwd, lb_rev, tri_f, keep_f, tri_r, keep_r, sel, bd, hh]
    return pl.pallas_call(
        _hgrn_kernel,
        grid=(n_all,),
        in_specs=[tile_f, tile_f, tile_f, tile_r, tile_r, tile_r] + [_const_spec(t.shape) for t in consts],
        out_specs=[tile_f, tile_r],
        out_shape=[jax.ShapeDtypeStruct((n, BRANCH_WIDTH), F32)] * 2,
        scratch_shapes=[pltpu.VMEM((256, 256), F32)] * 2,
        compiler_params=_cparams(("arbitrary",)),
        name="hgrn_scans",
    )(bq, f_fwd, bi, bq, f_rev, bi, *consts)


def _merge_kernel(x_ref, mods_ref, ya_ref, of_ref, or_ref, bg_ref, gain_ref, segm_ref, yc_ref, yd_ref, gt_ref,
                  wb_ref, wo_ref, nrm_ref, wg_ref, wu_ref, wd_ref, *rest, t_lat, tm):
    fin_ref = rest[0] if len(rest) == 2 else None
    o_ref = rest[-1]
    lat_tile = pl.program_id(0) * tm < t_lat
    yb = _head_norm(of_ref[...] + or_ref[...], gain_ref[...], segm_ref[...]) * _silu(bg_ref[...].astype(F32))
    branches = (ya_ref[...], yb.astype(BF16), yc_ref[...], yd_ref[...])
    merged = None
    for n, y in enumerate(branches):
        term = gt_ref[:, n * D_MODEL:(n + 1) * D_MODEL].astype(F32) * _dot(y, wb_ref[n])
        merged = term if merged is None else merged + term
    y = _dot(merged.astype(BF16), wo_ref[...])
    x_mid = x_ref[...] + _tile_mod(mods_ref, 5, lat_tile) * y
    out = _ffn_math(x_mid, lat_tile, mods_ref, nrm_ref, wg_ref, wu_ref, wd_ref, 6)
    o_ref[...] = out if fin_ref is None else _rms_rows(out) * fin_ref[...]


def _merge_ffn(x_all, mods, ya, o_fwd, o_rev, bg, b_gain, segm, yc, yd, gates, wb, wo, nrm, wg, wu, wd, *,
               layer, t_lat, n_rows, final_gain=None):
    n, d = x_all.shape
    tm = TOKEN_TILE
    final = final_gain is not None

    def tile(w):
        return pl.BlockSpec((tm, w), lambda i: (i, 0))

    return pl.pallas_call(
        functools.partial(_merge_kernel, t_lat=t_lat, tm=tm),
        grid=(n_rows // tm,),
        in_specs=[tile(d), _const_spec((2, N_MOD, d)), tile(256), tile(256), tile(256), tile(256),
                  _const_spec((1, 256)), _const_spec((256, 256)), tile(256), tile(256),
                  tile(N_BRANCH * d), _layer_spec(wb.shape, layer), _layer_spec(wo.shape, layer),
                  _const_spec((1, d)), _layer_spec(wg.shape, layer), _layer_spec(wu.shape, layer),
                  _layer_spec(wd.shape, layer)] + ([_const_spec((1, d))] if final else []),
        out_specs=tile(d),
        out_shape=jax.ShapeDtypeStruct((n_rows if final else n, d), F32),
        input_output_aliases={} if final else {0: 0},
        compiler_params=_cparams(("parallel",)),
        name="merge_ffn",
    )(x_all, mods, ya, o_fwd, o_rev, bg, b_gain, segm, yc, yd, gates, wb, wo, nrm, wg, wu, wd,
      *([final_gain] if final else []))


def _rope_tables_padded(t_lat, n_ctx):
    rows = t_lat // GRID_W
    half = HEAD_DIM // 2
    nf = half // 2
    inv = 1.0 / (ROPE_THETA ** (jnp.arange(0, half, 2, dtype=F32) / half))
    ang_r = jnp.arange(rows, dtype=F32)[:, None] * inv
    ang_c = jnp.arange(GRID_W, dtype=F32)[:, None] * inv

    def over_rows(a):
        return jnp.broadcast_to(a[:, None, :], (rows, GRID_W, nf))

    def over_cols(a):
        return jnp.broadcast_to(a[None, :, :], (rows, GRID_W, nf))

    cos = jnp.concatenate([over_rows(jnp.cos(ang_r))] * 2 + [over_cols(jnp.cos(ang_c))] * 2, axis=2)
    sin = jnp.concatenate([over_rows(-jnp.sin(ang_r)), over_rows(jnp.sin(ang_r)),
                           over_cols(-jnp.sin(ang_c)), over_cols(jnp.sin(ang_c))], axis=2)
    cos = cos.reshape(t_lat, HEAD_DIM)
    sin = sin.reshape(t_lat, HEAD_DIM)
    cos = jnp.concatenate([cos, jnp.ones((n_ctx, HEAD_DIM), F32)], axis=0)
    sin = jnp.concatenate([sin, jnp.zeros((n_ctx, HEAD_DIM), F32)], axis=0)
    return jnp.tile(cos, (1, 2)), jnp.tile(sin, (1, 2))


def _block_diag(w_group):
    g, ci, co = w_group.shape
    out = jnp.zeros((g * ci, g * co), w_group.dtype)
    for n in range(g):
        out = out.at[n * ci:(n + 1) * ci, n * co:(n + 1) * co].set(w_group[n])
    return out


def kernel(x, c, ctx, c_ctx, w_ada, b_ada, ffn1_norm, ffn1_w_gate, ffn1_w_up, ffn1_w_down, mix_norm, w_in, a_q_norm, a_k_norm, b_lb_logits, b_o_norm, c_w_group, c_scale, d_rel_bias, w_branch, w_out, ffn2_norm, ffn2_w_gate, ffn2_w_up, ffn2_w_down, final_norm):
    assert x.shape[0] == 1 and ctx.shape[0] == 1
    depth = w_ada.shape[0]
    t_lat = x.shape[1]
    n_ctx = ctx.shape[1]
    n = t_lat + n_ctx
    rows = t_lat // GRID_W
    assert t_lat % max(NA_QB, min(FLASH_TQ, t_lat), n_ctx) == 0 and n_ctx % TOKEN_TILE == 0 and rows >= 2 * NA_QROWS

    x_all = None
    mods_all = _ada_mods(c, c_ctx, w_ada, b_ada)
    cos, sin = _rope_tables_padded(t_lat, n_ctx)
    segm = (jnp.kron(jnp.eye(BRANCH_WIDTH // HEAD_DIM), jnp.ones((HEAD_DIM, HEAD_DIM))) / HEAD_DIM).astype(BF16)
    lb_all = jnp.cumsum(jax.nn.softmax(b_lb_logits.astype(F32), axis=0), axis=0)
    lb_all = lb_all - lb_all[:1]
    wg1, wu1, wd1 = ffn1_w_gate.astype(BF16), ffn1_w_up.astype(BF16), ffn1_w_down.astype(BF16)
    wg2, wu2, wd2 = ffn2_w_gate.astype(BF16), ffn2_w_up.astype(BF16), ffn2_w_down.astype(BF16)
    w_in_b, w_branch_b, w_out_b = w_in.astype(BF16), w_branch.astype(BF16), w_out.astype(BF16)
    na_bias, na_masks = _na_bias_tables(d_rel_bias, rows)

    for l in range(depth):
        with_ctx_out = l < depth - 1
        n_rows = n if with_ctx_out else t_lat
        mods = mods_all[l]
        x_all = _ffn_half(x[0] if l == 0 else x_all, mods, ffn1_norm[l][None], wg1, wu1, wd1, layer=l, t_lat=t_lat,
                          k0=0, n_rows=n, x_ctx=ctx[0] if l == 0 else None)
        (aq, ak, av, bq, bff, bfb, bi, bg, cx, dq, dk, dv, gates) = _in_proj(
            x_all, mods, mix_norm[l][None], w_in_b, cos, sin,
            jnp.tile(a_q_norm[l], A_HEADS)[None], jnp.tile(a_k_norm[l], A_KV_HEADS)[None], segm, layer=l, t_lat=t_lat)
        if with_ctx_out:
            ya, yd = _ctx_attn(aq, ak, av, dq, dk, dv, t_lat=t_lat)
        else:
            ya = jnp.zeros((n, BRANCH_WIDTH), BF16)
            yd = ya
        score_bound = (HEAD_DIM * ATTN_SCALE * LOG2_E * FLASH_BOUND_MARGIN
                       * jnp.max(jnp.abs(a_q_norm[l])) * jnp.max(jnp.abs(a_k_norm[l]))).astype(F32)
        ya = _gqa_latent(aq, ak, av, ya, score_bound, t_lat=t_lat)
        yd = _na_latent(dq, dk, dv, na_bias, na_masks, yd, layer=l, t_lat=t_lat)
        yc = _pool_mixer(cx, _block_diag(c_w_group[l]).astype(BF16), c_scale[l][None], t_lat=t_lat)
        o_fwd, o_rev = _hgrn_scans(bq, bff, bfb, bi, lb_all[l, 0][None], lb_all[l, 1][None], t_lat=t_lat)
        x_all = _merge_ffn(x_all, mods, ya, o_fwd, o_rev, bg, jnp.tile(b_o_norm[l], B_HEADS)[None], segm, yc, yd,
                           gates, w_branch_b, w_out_b, ffn2_norm[l][None], wg2, wu2, wd2, layer=l, t_lat=t_lat,
                           n_rows=n_rows, final_gain=None if with_ctx_out else final_norm[None])
    return x_all[None]
```

```python
import functools

import numpy as np
import jax
import jax.numpy as jnp
from jax import lax
from jax.experimental import pallas as pl
from jax.experimental.pallas import tpu as pltpu

F32 = jnp.float32
BF16 = jnp.bfloat16

D_MODEL = 1024
GRID_W = 64
HEAD_DIM = 64
BRANCH_WIDTH = 256
N_BRANCH = 4
A_HEADS = 4
A_KV_HEADS = 2
ROPE_THETA = 10000.0
B_HEADS = 4
B_KEY_DIM = 64
C_WINDOWS = (2, 4, 8, 16)
D_HEADS = 4
NA_WIN_R = 8
NA_WIN_C = 16
D_FF = 2816
N_MOD = 9
EPS = 1e-6
ATTN_SCALE = HEAD_DIM ** -0.5
LOG2_E = 1.4426950408889634
NEG_BIG = -1e30

V7X_VMEM_BYTES = 64 * 1024 * 1024
VMEM_LIMIT = V7X_VMEM_BYTES - 8 * 1024 * 1024

SEG_AQ = (0, 256)
SEG_AK = (256, 384)
SEG_AV = (384, 512)
SEG_BQ = (512, 768)
SEG_BFF = (768, 1024)
SEG_BFB = (1024, 1280)
SEG_BI = (1280, 1536)
SEG_BG = (1536, 1792)
SEG_CX = (1792, 2048)
SEG_DQ = (2048, 2304)
SEG_DK = (2304, 2560)
SEG_DV = (2560, 2816)
SEG_GATE = (2816, 2816 + N_BRANCH * D_MODEL)
IN_WIDTH = SEG_GATE[1]

ADA_TILE = 3072
TOKEN_TILE = 256
POOL_TILE = 2048
FLASH_TQ = 2048
FLASH_QC = 512
FLASH_SB = 256
FLASH_NBUF = 4
FLASH_MAX_EXPONENT_SPAN = 100.0
FLASH_BOUND_MARGIN = 1.02
NA_QROWS = 8
NA_QB = NA_QROWS * GRID_W
NA_HALO = 256
HG_BLOCK = 256
HG_CHUNK = 128
HG_SUB = 8


def _cparams(sem):
    return pltpu.CompilerParams(dimension_semantics=sem, vmem_limit_bytes=VMEM_LIMIT)


def _const_spec(shape):
    nd = len(shape)
    return pl.BlockSpec(shape, lambda *_: (0,) * nd, pipeline_mode=pl.Buffered(1))


def _layer_spec(shape, layer):
    nd = len(shape)
    return pl.BlockSpec((None,) + tuple(shape[1:]), lambda *_: (layer,) + (0,) * (nd - 1),
                        pipeline_mode=pl.Buffered(1))


def _sigmoid(x):
    return 1.0 / (1.0 + jnp.exp(-x))


def _silu(x):
    return x * _sigmoid(x)


def _dot(a, b):
    return jnp.dot(a, b, preferred_element_type=F32)


def _dot_nt(a, b):
    return lax.dot_general(a, b, (((1,), (1,)), ((), ())), preferred_element_type=F32)


def _dot_tn(a, b):
    return lax.dot_general(a, b, (((0,), (0,)), ((), ())), preferred_element_type=F32)


def _split_dot(x, m):
    hi = x.astype(BF16)
    r1 = x - hi.astype(F32)
    mid = r1.astype(BF16)
    lo = (r1 - mid.astype(F32)).astype(BF16)
    return _dot(hi, m) + _dot(mid, m) + _dot(lo, m)


def _split_dot_left(m, x):
    hi = x.astype(BF16)
    r1 = x - hi.astype(F32)
    mid = r1.astype(BF16)
    lo = (r1 - mid.astype(F32)).astype(BF16)
    return _dot(m, hi) + _dot(m, mid) + _dot(m, lo)


def _tile_mod(mods_ref, k, lat_tile):
    return jnp.where(lat_tile, mods_ref[0, k:k + 1, :], mods_ref[1, k:k + 1, :])


def _rms_rows(x):
    return x * lax.rsqrt(jnp.mean(x * x, axis=-1, keepdims=True) + EPS)


def _modulated_norm(x, nrm_ref, mods_ref, k_shift, lat_tile):
    col_scale = nrm_ref[...] * (1.0 + _tile_mod(mods_ref, k_shift + 1, lat_tile))
    return (_rms_rows(x) * col_scale + _tile_mod(mods_ref, k_shift, lat_tile)).astype(BF16)


def _ada_kernel(s_ref, w_ref, b_ref, o_ref):
    o_ref[0] = _dot(_silu(s_ref[...]).astype(BF16), w_ref[0].astype(BF16)) + b_ref[0]


def _ada_mods(c, c_ctx, w_ada, b_ada):
    depth = w_ada.shape[0]
    width = w_ada.shape[2]
    tn = ADA_TILE
    s = jnp.zeros((8, D_MODEL), F32).at[0].set(c[0]).at[1].set(c_ctx)
    out = pl.pallas_call(
        _ada_kernel,
        grid=(depth, width // tn),
        in_specs=[
            pl.BlockSpec((8, D_MODEL), lambda l, j: (0, 0)),
            pl.BlockSpec((1, D_MODEL, tn), lambda l, j: (l, 0, j)),
            pl.BlockSpec((1, 1, tn), lambda l, j: (l, 0, j)),
        ],
        out_specs=pl.BlockSpec((1, 8, tn), lambda l, j: (l, 0, j)),
        out_shape=jax.ShapeDtypeStruct((depth, 8, width), F32),
        compiler_params=_cparams(("parallel", "parallel")),
        name="ada_mods",
    )(s, w_ada, b_ada.reshape(depth, 1, width))
    return out[:, :2].reshape(depth, 2, N_MOD, D_MODEL)


def _ffn_math(x, lat_tile, mods_ref, nrm_ref, wg_ref, wu_ref, wd_ref, k0):
    z = _modulated_norm(x, nrm_ref, mods_ref, k0, lat_tile)
    g = _dot(z, wg_ref[...])
    u = _dot(z, wu_ref[...])
    a = (_silu(g) * u).astype(BF16)
    return x + (0.5 * _tile_mod(mods_ref, k0 + 2, lat_tile)) * _dot(a, wd_ref[...])


def _ffn_kernel(*refs, t_lat, tm, k0, split_input):
    if split_input:
        xl_ref, xc_ref, mods_ref, nrm_ref, wg_ref, wu_ref, wd_ref, o_ref = refs
    else:
        x_ref, mods_ref, nrm_ref, wg_ref, wu_ref, wd_ref, o_ref = refs
    lat_tile = pl.program_id(0) * tm < t_lat
    x = jnp.where(lat_tile, xl_ref[...], xc_ref[...]) if split_input else x_ref[...]
    o_ref[...] = _ffn_math(x, lat_tile, mods_ref, nrm_ref, wg_ref, wu_ref, wd_ref, k0)


def _ffn_half(x_all, mods, nrm, wg, wu, wd, *, layer, t_lat, k0, n_rows, x_ctx=None):
    split_input = x_ctx is not None
    d = x_all.shape[1]
    n = x_all.shape[0] + (x_ctx.shape[0] if split_input else 0)
    tm = TOKEN_TILE
    if split_input:
        n_lat_tiles = t_lat // tm
        x_specs = [pl.BlockSpec((tm, d), lambda i: (jnp.minimum(i, n_lat_tiles - 1), 0)),
                   pl.BlockSpec((tm, d), lambda i: (jnp.maximum(i - n_lat_tiles, 0), 0))]
        x_args = [x_all, x_ctx]
    else:
        x_specs = [pl.BlockSpec((tm, d), lambda i: (i, 0))]
        x_args = [x_all]
    return pl.pallas_call(
        functools.partial(_ffn_kernel, t_lat=t_lat, tm=tm, k0=k0, split_input=split_input),
        grid=(n_rows // tm,),
        in_specs=x_specs + [
            _const_spec((2, N_MOD, d)),
            _const_spec((1, d)),
            _layer_spec(wg.shape, layer),
            _layer_spec(wu.shape, layer),
            _layer_spec(wd.shape, layer),
        ],
        out_specs=pl.BlockSpec((tm, d), lambda i: (i, 0)),
        out_shape=jax.ShapeDtypeStruct((n, d), F32),
        input_output_aliases={} if split_input else {0: 0},
        compiler_params=_cparams(("parallel",)),
        name="ffn_half",
    )(*x_args, mods, nrm, wg, wu, wd)


def _head_norm(y, gain, segm):
    ms = _split_dot(y * y, segm)
    return y * lax.rsqrt(ms + EPS) * gain


def _rope(y, cos, sin_signed):
    w = y.shape[1]
    lane = lax.broadcasted_iota(jnp.int32, y.shape, 1)
    partner = jnp.where((lane & 31) < 16, pltpu.roll(y, w - 16, 1), pltpu.roll(y, 16, 1))
    return y * cos + partner * sin_signed


def _inproj_kernel(x_ref, mods_ref, nrm_ref, w_ref, cos_ref, sin_ref, gq_ref, gk_ref, segm_ref,
                   aq_ref, ak_ref, av_ref, bq_ref, bff_ref, bfb_ref, bi_ref, bg_ref, cx_ref,
                   dq_ref, dk_ref, dv_ref, gt_ref, *, t_lat, tm):
    z = _modulated_norm(x_ref[...], nrm_ref, mods_ref, 3, pl.program_id(0) * tm < t_lat)

    def proj(seg):
        return _dot(z, w_ref[:, seg[0]:seg[1]])

    cos = cos_ref[...]
    sin = sin_ref[...]
    q = _rope(_head_norm(proj(SEG_AQ), gq_ref[...], segm_ref[...]),
              jnp.concatenate([cos, cos], axis=1), jnp.concatenate([sin, sin], axis=1))
    aq_ref[...] = (q * (ATTN_SCALE * LOG2_E)).T.astype(BF16)
    k = _rope(_head_norm(proj(SEG_AK), gk_ref[...], segm_ref[0:128, 0:128]), cos, sin)
    zeros = jnp.zeros((tm, HEAD_DIM), F32)
    ak_ref[...] = jnp.concatenate([k[:, :HEAD_DIM], zeros, k[:, HEAD_DIM:], zeros], axis=1).astype(BF16)
    v = proj(SEG_AV)
    one_hot = jnp.where(lax.broadcasted_iota(jnp.int32, (tm, HEAD_DIM), 1) == 0, 1.0, 0.0)
    av_ref[...] = jnp.concatenate([v[:, :HEAD_DIM], one_hot, v[:, HEAD_DIM:], one_hot], axis=1).T.astype(BF16)

    bq_ref[...] = (proj(SEG_BQ) * (B_KEY_DIM ** -0.5)).astype(BF16)
    bff_ref[...] = proj(SEG_BFF)
    bfb_ref[...] = proj(SEG_BFB)
    bi_ref[...] = proj(SEG_BI).astype(BF16)
    bg_ref[...] = proj(SEG_BG).astype(BF16)
    cx_ref[...] = proj(SEG_CX)
    dq_ref[...] = (proj(SEG_DQ) * ATTN_SCALE).astype(BF16)
    dk_ref[...] = proj(SEG_DK).astype(BF16)
    dv = proj(SEG_DV)
    dv_ref[...] = jnp.concatenate(
        [piece for h in range(D_HEADS) for piece in (dv[:, HEAD_DIM * h:HEAD_DIM * (h + 1)], one_hot)],
        axis=1).astype(BF16)
    gw = 512
    for c0 in range(SEG_GATE[0], SEG_GATE[1], gw):
        o0 = c0 - SEG_GATE[0]
        gt_ref[:, o0:o0 + gw] = _sigmoid(proj((c0, c0 + gw))).astype(BF16)


def _in_proj(x_all, mods, nrm, w_in, cos, sin, gq, gk, segm, *, layer, t_lat):
    n, d = x_all.shape
    tm = TOKEN_TILE
    widths = [(256, BF16), (256, BF16), (256, BF16), (256, BF16), (256, F32), (256, F32), (256, BF16),
              (256, BF16), (256, F32), (256, BF16), (256, BF16), (2 * D_HEADS * HEAD_DIM, BF16),
              (N_BRANCH * D_MODEL, BF16)]
    transposed = (0, 2)
    out_specs = [pl.BlockSpec((w, tm), lambda i: (0, i)) if o in transposed else pl.BlockSpec((tm, w), lambda i: (i, 0))
                 for o, (w, _) in enumerate(widths)]
    out_shape = [jax.ShapeDtypeStruct((w, n) if o in transposed else (n, w), dt) for o, (w, dt) in enumerate(widths)]
    return pl.pallas_call(
        functools.partial(_inproj_kernel, t_lat=t_lat, tm=tm),
        grid=(n // tm,),
        in_specs=[
            pl.BlockSpec((tm, d), lambda i: (i, 0)),
            _const_spec((2, N_MOD, d)),
            _const_spec((1, d)),
            _layer_spec(w_in.shape, layer),
            pl.BlockSpec((tm, 128), lambda i: (i, 0)),
            pl.BlockSpec((tm, 128), lambda i: (i, 0)),
            _const_spec((1, 256)),
            _const_spec((1, 128)),
            _const_spec((256, 256)),
        ],
        out_specs=out_specs,
        out_shape=out_shape,
        compiler_params=_cparams(("parallel",)),
        name="in_proj",
    )(x_all, mods, nrm, w_in, cos, sin, gq, gk, segm)


def _softmax_attend(q, k, v, exp_fn):
    s = _dot_nt(q, k)
    p = exp_fn(s - jnp.max(s, axis=-1, keepdims=True))
    return _dot(p.astype(BF16), v) / jnp.sum(p, axis=-1, keepdims=True)


def _ctx_attn_kernel(aqt_ref, ak_ref, avt_ref, dq_ref, dk_ref, dv_ref, ya_ref, yd_ref):
    outs = []
    for h in range(A_HEADS):
        g = h // (A_HEADS // A_KV_HEADS)
        st = _dot(ak_ref[:, 128 * g:128 * g + 64], aqt_ref[64 * h:64 * h + 64, :])
        pt = jnp.exp2(st - jnp.max(st, axis=0, keepdims=True)).astype(BF16)
        acc = _dot(avt_ref[128 * g:128 * g + 128, :], pt)
        outs.append((acc[:HEAD_DIM] / acc[HEAD_DIM:HEAD_DIM + 1]).T)
    ya_ref[...] = jnp.concatenate(outs, axis=1).astype(BF16)
    outs = []
    for h in range(D_HEADS):
        hs = slice(64 * h, 64 * h + 64)
        outs.append(_softmax_attend(dq_ref[:, hs], dk_ref[:, hs], dv_ref[:, 128 * h:128 * h + 64], jnp.exp))
    yd_ref[...] = jnp.concatenate(outs, axis=1).astype(BF16)


def _ctx_attn(aqt, ak, avt, dq, dk, dv, *, t_lat):
    n = ak.shape[0]
    n_ctx = n - t_lat
    blk = t_lat // n_ctx

    def spec(w):
        return pl.BlockSpec((n_ctx, w), lambda i: (blk, 0))

    def spec_t(w):
        return pl.BlockSpec((w, n_ctx), lambda i: (0, blk))

    return pl.pallas_call(
        _ctx_attn_kernel,
        grid=(1,),
        in_specs=[spec_t(256), spec(256), spec_t(256), spec(256), spec(256), spec(dv.shape[1])],
        out_specs=[spec(256), spec(256)],
        out_shape=[jax.ShapeDtypeStruct((n, BRANCH_WIDTH), BF16)] * 2,
        compiler_params=_cparams(("arbitrary",)),
        name="ctx_attn",
    )(aqt, ak, avt, dq, dk, dv)


def _flash_kernel(qt_ref, k_ref, vt_ref, prev_ref, o_ref, m_ref, acc_ref, *bufs, tk, tq):
    del prev_ref
    s_refs, p_refs = bufs[:FLASH_NBUF], bufs[FLASH_NBUF:]
    j = pl.program_id(1)

    @pl.when(j == 0)
    def _():
        m_ref[...] = jnp.full(m_ref.shape, -jnp.inf, F32)
        acc_ref[...] = jnp.zeros(acc_ref.shape, F32)

    group = A_HEADS // A_KV_HEADS
    n_qc = tq // FLASH_QC
    n_sb = tk // FLASH_SB
    chains = [(h, c) for h in range(A_HEADS) for c in range(n_qc)]

    def score_block(ci, sb):
        h, c = chains[ci]
        g = h // group
        rows = slice(sb * FLASH_SB, (sb + 1) * FLASH_SB)
        st = _dot(k_ref[rows, 128 * g:128 * g + 64], qt_ref[64 * h:64 * h + 64, c * FLASH_QC:(c + 1) * FLASH_QC])
        s_refs[ci % FLASH_NBUF][rows, :] = st
        return jnp.max(st, axis=0, keepdims=True)

    def exp_block(ci, sb, m_new):
        rows = slice(sb * FLASH_SB, (sb + 1) * FLASH_SB)
        p_refs[ci % 2][rows, :] = jnp.exp2(s_refs[ci % FLASH_NBUF][rows, :] - m_new).astype(BF16)

    def value_block(ci, sb):
        g = chains[ci][0] // group
        rows = slice(sb * FLASH_SB, (sb + 1) * FLASH_SB)
        return _dot(vt_ref[128 * g:128 * g + 128, rows], p_refs[ci % 2][rows, :])

    def fold(a, b):
        return b if a is None else a + b

    def fold_max(a, b):
        return b if a is None else jnp.maximum(a, b)

    n_ch = len(chains)
    blk_max, m_new, alpha = {}, {}, {}
    for stage in range(-1, n_ch + 1):
        c_s, c_e, c_v = stage + 1, stage, stage - 1
        if 0 <= c_e < n_ch:
            h, c = chains[c_e]
            cols = slice(c * FLASH_QC, (c + 1) * FLASH_QC)
            m_old = m_ref[h, :, cols]
            m_new[c_e] = jnp.maximum(m_old, blk_max.pop(c_e))
            m_ref[h, :, cols] = m_new[c_e]
            alpha[c_e] = jnp.exp2(m_old - m_new[c_e])
        part, mx = None, None
        for sb in range(n_sb):
            if c_s < n_ch:
                mx = fold_max(mx, score_block(c_s, sb))
            if 0 <= c_e < n_ch:
                exp_block(c_e, sb, m_new[c_e])
            if 0 <= c_v:
                part = fold(part, value_block(c_v, sb))
        if c_s < n_ch:
            blk_max[c_s] = mx
        if 0 <= c_v:
            h, c = chains[c_v]
            cols = slice(c * FLASH_QC, (c + 1) * FLASH_QC)
            acc_ref[h, :, cols] = alpha.pop(c_v) * acc_ref[h, :, cols] + part

    @pl.when(j == pl.num_programs(1) - 1)
    def _():
        outs = []
        for h in range(A_HEADS):
            acc = acc_ref[h]
            outs.append((acc[:HEAD_DIM] / acc[HEAD_DIM:HEAD_DIM + 1]).T)
        o_ref[...] = jnp.concatenate(outs, axis=1).astype(BF16)


def _flash_bounded_kernel(bound_ref, qt_ref, k_ref, vt_ref, prev_ref, o_ref, acc_ref, *p_refs, tk, tq):
    del prev_ref
    j = pl.program_id(1)

    @pl.when(j == 0)
    def _():
        acc_ref[...] = jnp.zeros(acc_ref.shape, F32)

    bound = bound_ref[0]
    group = A_HEADS // A_KV_HEADS
    n_qc = tq // FLASH_QC
    n_sb = tk // FLASH_SB
    chains = [(h, c) for h in range(A_HEADS) for c in range(n_qc)]

    def prob_block(ci, sb):
        h, c = chains[ci]
        g = h // group
        rows = slice(sb * FLASH_SB, (sb + 1) * FLASH_SB)
        st = _dot(k_ref[rows, 128 * g:128 * g + 64], qt_ref[64 * h:64 * h + 64, c * FLASH_QC:(c + 1) * FLASH_QC])
        p_refs[ci % 2][rows, :] = jnp.exp2(st - bound).astype(BF16)

    def value_block(ci, sb):
        g = chains[ci][0] // group
        rows = slice(sb * FLASH_SB, (sb + 1) * FLASH_SB)
        return _dot(vt_ref[128 * g:128 * g + 128, rows], p_refs[ci % 2][rows, :])

    n_ch = len(chains)
    for stage in range(n_ch + 1):
        part = None
        for sb in range(n_sb):
            if stage < n_ch:
                prob_block(stage, sb)
            if stage >= 1:
                blk = value_block(stage - 1, sb)
                part = blk if part is None else part + blk
        if stage >= 1:
            h, c = chains[stage - 1]
            cols = slice(c * FLASH_QC, (c + 1) * FLASH_QC)
            acc_ref[h, :, cols] = acc_ref[h, :, cols] + part

    @pl.when(j == pl.num_programs(1) - 1)
    def _():
        outs = []
        for h in range(A_HEADS):
            acc = acc_ref[h]
            outs.append((acc[:HEAD_DIM] / acc[HEAD_DIM:HEAD_DIM + 1]).T)
        o_ref[...] = jnp.concatenate(outs, axis=1).astype(BF16)


def _flash_tk(n):
    for tk in (1280, 768, 512, 256):
        if n % tk == 0:
            return tk
    raise ValueError(f"unsupported key count {n}")


def _gqa_latent(aqt, ak, avt, ya_prev, score_bound, *, t_lat):
    n = ak.shape[0]
    tk = _flash_tk(n)
    tq = min(FLASH_TQ, t_lat)

    def call(body, extra_specs, scratch, name):
        return pl.pallas_call(
            functools.partial(body, tk=tk, tq=tq),
            grid=(t_lat // tq, n // tk),
            in_specs=extra_specs + [
                pl.BlockSpec((256, tq), lambda i, j: (0, i)),
                pl.BlockSpec((tk, 256), lambda i, j: (j, 0)),
                pl.BlockSpec((256, tk), lambda i, j: (0, j)),
                pl.BlockSpec(memory_space=pl.ANY),
            ],
            out_specs=pl.BlockSpec((tq, 256), lambda i, j: (i, 0)),
            out_shape=jax.ShapeDtypeStruct((n, BRANCH_WIDTH), BF16),
            scratch_shapes=scratch,
            input_output_aliases={len(extra_specs) + 3: 0},
            compiler_params=_cparams(("parallel", "arbitrary")),
            name=name,
        )

    def online(_):
        scratch = ([pltpu.VMEM((A_HEADS, 1, tq), F32), pltpu.VMEM((A_HEADS, 128, tq), F32)]
                   + [pltpu.VMEM((tk, FLASH_QC), F32)] * FLASH_NBUF + [pltpu.VMEM((tk, FLASH_QC), BF16)] * 2)
        return call(_flash_kernel, [], scratch, "gqa_flash")(aqt, ak, avt, ya_prev)

    def bounded(_):
        scratch = [pltpu.VMEM((A_HEADS, 128, tq), F32)] + [pltpu.VMEM((tk, FLASH_QC), BF16)] * 2
        return call(_flash_bounded_kernel, [pl.BlockSpec(memory_space=pltpu.SMEM)], scratch,
                    "gqa_flash_bounded")(score_bound.reshape(1), aqt, ak, avt, ya_prev)

    return lax.cond(2.0 * score_bound <= FLASH_MAX_EXPONENT_SPAN, bounded, online, None)


def _na_kernel(q_ref, kp_ref, km_ref, kn_ref, vp_ref, vm_ref, vn_ref, kc_ref, vc_ref, bias_ref, mask_ref, prev_ref,
               o_ref, bm_ref, *, nb):
    del prev_ref
    b = pl.program_id(0)

    @pl.when((b == 0) | (b == 1) | (b == nb - 1))
    def _():
        for h in range(D_HEADS):
            bm_ref[h] = bias_ref[h].astype(F32) + mask_ref[0]

    def scores(h):
        hs = slice(64 * h, 64 * h + 64)
        q = q_ref[:, hs]
        kcat = jnp.concatenate([kp_ref[:, hs], km_ref[:, hs], kn_ref[:, hs]], axis=0)
        return _dot_nt(q, kcat) + bm_ref[h], _dot_nt(q, kc_ref[:, hs])

    def softmax(s_loc, s_ctx):
        m = jnp.maximum(jnp.max(s_loc, axis=-1, keepdims=True), jnp.max(s_ctx, axis=-1, keepdims=True))
        return jnp.exp(s_loc - m).astype(BF16), jnp.exp(s_ctx - m).astype(BF16)

    def attend(h, p_loc, p_ctx):
        vs = slice(128 * h, 128 * h + 128)
        vcat = jnp.concatenate([vp_ref[:, vs], vm_ref[:, vs], vn_ref[:, vs]], axis=0)
        o = _dot(p_loc, vcat) + _dot(p_ctx, vc_ref[:, vs])
        return o[:, :HEAD_DIM] / o[:, HEAD_DIM:HEAD_DIM + 1]

    s = {0: scores(0)}
    p, outs = {}, []
    for h in range(D_HEADS):
        if h + 1 < D_HEADS:
            s[h + 1] = scores(h + 1)
        p[h] = softmax(*s.pop(h))
        if h >= 1:
            outs.append(attend(h - 1, *p.pop(h - 1)))
    outs.append(attend(D_HEADS - 1, *p.pop(D_HEADS - 1)))
    o_ref[...] = jnp.concatenate(outs, axis=1).astype(BF16)


def _na_bias_tables(rel_bias, rows):
    wr = min(NA_WIN_R, rows)
    halo_rows = NA_HALO // GRID_W
    krows = NA_QROWS + 2 * halo_rows
    nb = rows // NA_QROWS
    qc = np.arange(GRID_W)[:, None]
    kc = np.arange(GRID_W)[None, :]
    cs = np.clip(qc - NA_WIN_C // 2, 0, GRID_W - NA_WIN_C)
    in_col = (kc >= cs) & (kc < cs + NA_WIN_C)
    e_col = (kc - qc + (NA_WIN_C - 1))[:, :, None] == np.arange(2 * NA_WIN_C - 1)
    qr_l = np.arange(NA_QROWS)[:, None]
    kr_l = np.arange(krows)[None, :]
    e_row = (kr_l - halo_rows - qr_l + (NA_WIN_R - 1))[:, :, None] == np.arange(2 * NA_WIN_R - 1)
    hi = lax.Precision.HIGHEST
    tmp = jnp.einsum("lhrc,qkr->lhqkc", rel_bias.astype(F32), e_row.astype(np.float32), precision=hi)
    full = jnp.einsum("lhqkc,pjc->lhqpkj", tmp, e_col.astype(np.float32), precision=hi)
    masks = []
    for b in (0, min(1, nb - 1), nb - 1):
        qr = NA_QROWS * b + qr_l
        kr = NA_QROWS * b - halo_rows + kr_l
        rs = np.clip(qr - wr // 2, 0, rows - wr)
        in_row = (kr >= rs) & (kr < rs + wr)
        mask = in_row[:, None, :, None] & in_col[None, :, None, :]
        masks.append(np.where(mask, 0.0, NEG_BIG).reshape(NA_QB, krows * GRID_W))
    return full.astype(BF16).reshape(-1, D_HEADS, NA_QB, krows * GRID_W), np.stack(masks).astype(np.float32)


def _na_latent(dq, dk, dv, bias, masks, yd_prev, *, layer, t_lat):
    n = dq.shape[0]
    nb = t_lat // NA_QB
    r = NA_QB // NA_HALO
    last_halo = t_lat // NA_HALO - 1
    ctx_blk = t_lat // (n - t_lat)
    n_ctx = n - t_lat

    def prev_map(b):
        return (jnp.maximum(r * b - 1, 0), 0)

    def next_map(b):
        return (jnp.minimum(r * b + r, last_halo), 0)

    def variant(b):
        return (jnp.where(b == 0, 0, jnp.where(b == nb - 1, 2, 1)), 0, 0)

    def specs(w):
        return (pl.BlockSpec((NA_QB, w), lambda b: (b, 0)), pl.BlockSpec((NA_HALO, w), prev_map),
                pl.BlockSpec((NA_HALO, w), next_map), pl.BlockSpec((n_ctx, w), lambda b: (ctx_blk, 0)))

    main, prev, nxt, ctx = specs(256)
    vmain, vprev, vnxt, vctx = specs(dv.shape[1])
    return pl.pallas_call(
        functools.partial(_na_kernel, nb=nb),
        grid=(nb,),
        in_specs=[main, prev, main, nxt, vprev, vmain, vnxt, ctx, vctx,
                  _layer_spec(bias.shape, layer),
                  pl.BlockSpec((1, NA_QB, NA_QB + 2 * NA_HALO), variant),
                  pl.BlockSpec(memory_space=pl.ANY)],
        out_specs=main,
        out_shape=jax.ShapeDtypeStruct((n, BRANCH_WIDTH), BF16),
        scratch_shapes=[pltpu.VMEM((D_HEADS, NA_QB, NA_QB + 2 * NA_HALO), F32)],
        input_output_aliases={11: 0},
        compiler_params=_cparams(("arbitrary",)),
        name="na_attn",
    )(dq, dk, dk, dk, dv, dv, dv, dk, dv, bias, masks, yd_prev)


def _pool_kernel(x_ref, xp_ref, xn_ref, w_ref, sc_ref, *rest, seg_lo, seg_hi, tm):
    o_ref, cat_ref, a_ref, b_ref = rest[-4:]
    start = seg_lo + pl.program_id(0) * tm
    x = x_ref[...]
    cat_ref[0:8, :] = jnp.where(start > seg_lo, xp_ref[...], 0.0)
    cat_ref[8:8 + tm, :] = x
    cat_ref[8 + tm:16 + tm, :] = jnp.where(start + tm < seg_hi, xn_ref[...], 0.0)
    cat_ref[16 + tm:32 + tm, :] = jnp.zeros((16, BRANCH_WIDTH), F32)
    a_ref[0:tm + 24, :] = cat_ref[0:tm + 24, :] + cat_ref[1:tm + 25, :]
    s2 = a_ref[7:7 + tm, :]
    b_ref[0:tm + 16, :] = a_ref[0:tm + 16, :] + a_ref[2:tm + 18, :]
    s4 = b_ref[6:6 + tm, :]
    a_ref[0:tm + 8, :] = b_ref[0:tm + 8, :] + b_ref[4:tm + 12, :]
    s8 = a_ref[4:4 + tm, :]
    s16 = a_ref[0:tm, :] + a_ref[8:8 + tm, :]

    pos = start - seg_lo + lax.broadcasted_iota(jnp.int32, (tm, 1), 0)
    seg_len = seg_hi - seg_lo

    def mean(sm, w):
        lo = jnp.clip(pos - w // 2, 0, seg_len)
        hi = jnp.clip(pos - w // 2 + w, 0, seg_len)
        return sm / (hi - lo).astype(F32)

    lane = lax.broadcasted_iota(jnp.int32, (tm, BRANCH_WIDTH), 1)
    gw = BRANCH_WIDTH // len(C_WINDOWS)
    pooled = jnp.where(lane < gw, mean(s2, 2),
                       jnp.where(lane < 2 * gw, mean(s4, 4),
                                 jnp.where(lane < 3 * gw, mean(s8, 8), mean(s16, 16)))) - x
    o_ref[...] = (_dot(pooled.astype(BF16), w_ref[...]) * sc_ref[...]).astype(BF16)


def _pool_segment(cx, w_bd, scale, prev_out, *, seg_lo, seg_hi, tm):
    n = cx.shape[0]
    r = tm // 8
    b0 = seg_lo // tm
    last8 = n // 8 - 1
    in_specs = [
        pl.BlockSpec((tm, 256), lambda i: (b0 + i, 0)),
        pl.BlockSpec((8, 256), lambda i: (jnp.maximum(r * (b0 + i) - 1, 0), 0)),
        pl.BlockSpec((8, 256), lambda i: (jnp.minimum(r * (b0 + i) + r, last8), 0)),
        _const_spec((256, 256)),
        _const_spec((1, 256)),
    ]
    args = [cx, cx, cx, w_bd, scale]
    aliases = {}
    if prev_out is not None:
        in_specs.append(pl.BlockSpec(memory_space=pl.ANY))
        args.append(prev_out)
        aliases = {5: 0}
    return pl.pallas_call(
        functools.partial(_pool_kernel, seg_lo=seg_lo, seg_hi=seg_hi, tm=tm),
        grid=((seg_hi - seg_lo) // tm,),
        in_specs=in_specs,
        out_specs=pl.BlockSpec((tm, 256), lambda i: (b0 + i, 0)),
        out_shape=jax.ShapeDtypeStruct((n, BRANCH_WIDTH), BF16),
        scratch_shapes=[pltpu.VMEM((tm + 32, 256), F32)] * 3,
        input_output_aliases=aliases,
        compiler_params=_cparams(("parallel",)),
        name="pool_mixer",
    )(*args)


def _pool_mixer(cx, w_bd, scale, *, t_lat):
    n = cx.shape[0]
    n_ctx = n - t_lat
    out = _pool_segment(cx, w_bd, scale, None, seg_lo=t_lat, seg_hi=n, tm=n_ctx)
    return _pool_segment(cx, w_bd, scale, out, seg_lo=0, seg_hi=t_lat, tm=min(POOL_TILE, t_lat))


def _hgrn_tables(rev):
    c_len, s, w = HG_CHUNK, HG_SUB, B_HEADS * B_KEY_DIM
    t = np.arange(c_len)
    tri = (t[None, :] >= t[:, None]) if rev else (t[None, :] <= t[:, None])
    pos = np.arange(s)
    keep = (pos[None, :] <= pos[:, None]) if rev else (pos[None, :] >= pos[:, None])
    keep_add = np.where(keep, 0.0, NEG_BIG)[:, :, None] * np.ones((1, 1, w))
    head_of_lane = np.arange(w) // B_KEY_DIM
    col = np.arange(B_HEADS * s)
    sel = (col[None, None, :] == (head_of_lane[None, :, None] * s + pos[:, None, None]))
    bd = (col[:, None] // s) == head_of_lane[None, :]
    hh = head_of_lane[:, None] == head_of_lane[None, :]
    return (jnp.asarray(tri, BF16), jnp.asarray(keep_add, F32), jnp.asarray(sel, BF16),
            jnp.asarray(bd, F32), jnp.asarray(hh, F32))


def _hgrn_kernel(qf_ref, ff_ref, vf_ref, qr_ref, fr_ref, vr_ref, lbf_ref, lbr_ref, trif_ref, keepf_ref, trir_ref,
                 keepr_ref, sel_ref, bd_ref, hh_ref, of_ref, or_ref, stf_ref, str_ref):
    @pl.when(pl.program_id(0) == 0)
    def _():
        stf_ref[...] = jnp.zeros(stf_ref.shape, F32)
        str_ref[...] = jnp.zeros(str_ref.shape, F32)

    st_f, st_r = stf_ref[...], str_ref[...]
    n_chunks = HG_BLOCK // HG_CHUNK
    for step in range(n_chunks):
        rows = slice(step * HG_CHUNK, (step + 1) * HG_CHUNK)
        o, st_f = _hgrn_chunk(qf_ref[rows, :], ff_ref[rows, :], vf_ref[rows, :], st_f, lbf_ref, trif_ref,
                              keepf_ref, sel_ref, bd_ref, hh_ref, rev=False)
        of_ref[rows, :] = o
        ch = n_chunks - 1 - step
        rows = slice(ch * HG_CHUNK, (ch + 1) * HG_CHUNK)
        o, st_r = _hgrn_chunk(qr_ref[rows, :], fr_ref[rows, :], vr_ref[rows, :], st_r, lbr_ref, trir_ref,
                              keepr_ref, sel_ref, bd_ref, hh_ref, rev=True)
        or_ref[rows, :] = o
    stf_ref[...] = st_f
    str_ref[...] = st_r


def _hgrn_chunk(q_bf, f_pre, v, st, lb_ref, tri_ref, keep_ref, sel_ref, bd_ref, hh_ref, *, rev):
    c_len = HG_CHUNK
    n_sub = c_len // HG_SUB
    w = B_HEADS * B_KEY_DIM
    lb = lb_ref[...]
    f = lb + (1.0 - lb) * _sigmoid(f_pre)
    k = 1.0 - f
    lf = jnp.log(f)
    q = q_bf.astype(F32)
    bd = bd_ref[...]

    c = _split_dot_left(tri_ref[...], lf)

    q3 = q.reshape(n_sub, HG_SUB, w)
    k3 = k.reshape(n_sub, HG_SUB, w)
    c3 = c.reshape(n_sub, HG_SUB, w)
    a_diag = None
    for sg in range(HG_SUB):
        ks = jnp.broadcast_to(k3[:, sg:sg + 1, :], k3.shape)
        cs = jnp.broadcast_to(c3[:, sg:sg + 1, :], c3.shape)
        wgt = (q3 * ks) * jnp.exp((c3 - cs) + keep_ref[sg][None])
        part = _dot(wgt.reshape(c_len, w).astype(BF16), sel_ref[sg])
        a_diag = part if a_diag is None else a_diag + part

    o_parts = [None] * n_sub
    for j in range(n_sub):
        r0 = j * HG_SUB
        r_last = r0 if rev else r0 + HG_SUB - 1
        e_j = c[r_last:r_last + 1, :]
        kj = k[r0:r0 + HG_SUB, :] * jnp.exp(e_j - c[r0:r0 + HG_SUB, :])
        kbd = (jnp.concatenate([kj] * B_HEADS, axis=0) * bd).astype(BF16)
        vbd = (jnp.concatenate([v[r0:r0 + HG_SUB, :].astype(F32)] * B_HEADS, axis=0) * bd).astype(BF16)
        lo, hi = (0, r0) if rev else (r0 + HG_SUB, c_len)
        pieces = [a_diag[r0:r0 + HG_SUB]]
        if hi > lo:
            qj = (q[lo:hi] * jnp.exp(c[lo:hi] - e_j)).astype(BF16)
            a_off = _dot_nt(qj, kbd)
            pieces = [a_off] + pieces if rev else pieces + [a_off]
        first = 0 if rev else j
        if (sum(p.shape[0] for p in pieces) % 16) != 0:
            pad = jnp.zeros((HG_SUB, a_diag.shape[1]), F32)
            pieces = pieces + [pad] if rev else [pad] + pieces
            first = first if rev else first - 1
        a_j = jnp.concatenate(pieces, axis=0).astype(BF16)
        contrib = _dot(a_j, vbd)
        for i in range(contrib.shape[0] // HG_SUB):
            piece = contrib[i * HG_SUB:(i + 1) * HG_SUB]
            o_parts[first + i] = piece if o_parts[first + i] is None else o_parts[first + i] + piece
    o = jnp.concatenate(o_parts, axis=0)

    o = o + _dot_nt((q * jnp.exp(c)).astype(BF16), st.astype(BF16))
    r_end = 0 if rev else c_len - 1
    c_end = c[r_end:r_end + 1, :]
    k_end = (k * jnp.exp(c_end - c)).astype(BF16)
    return o, jnp.exp(c_end) * st + _dot_tn(v, k_end) * hh_ref[...]


def _hgrn_scans(bq, f_fwd, f_rev, bi, lb_fwd, lb_rev, *, t_lat):
    n = bq.shape[0]
    c_len = HG_BLOCK
    n_lat = t_lat // c_len
    n_all = n // c_len
    n_ctx = n_all - n_lat

    def blk_fwd(i):
        return (jnp.where(i < n_ctx, n_lat + i, i - n_ctx), 0)

    def blk_rev(i):
        return (jnp.where(i < n_ctx, n_all - 1 - i, n_lat - 1 - (i - n_ctx)), 0)

    tile_f = pl.BlockSpec((c_len, 256), blk_fwd)
    tile_r = pl.BlockSpec((c_len, 256), blk_rev)
    tri_f, keep_f, sel, bd, hh = _hgrn_tables(False)
    tri_r, keep_r, _, _, _ = _hgrn_tables(True)
    consts = [lb_fwd, lb_rev, tri_f, keep_f, tri_r, keep_r, sel, bd, hh]
    return pl.pallas_call(
        _hgrn_kernel,
        grid=(n_all,),
        in_specs=[tile_f, tile_f, tile_f, tile_r, tile_r, tile_r] + [_const_spec(t.shape) for t in consts],
        out_specs=[tile_f, tile_r],
        out_shape=[jax.ShapeDtypeStruct((n, BRANCH_WIDTH), F32)] * 2,
        scratch_shapes=[pltpu.VMEM((256, 256), F32)] * 2,
        compiler_params=_cparams(("arbitrary",)),
        name="hgrn_scans",
    )(bq, f_fwd, bi, bq, f_rev, bi, *consts)


def _merge_kernel(x_ref, mods_ref, ya_ref, of_ref, or_ref, bg_ref, gain_ref, segm_ref, yc_ref, yd_ref, gt_ref,
                  wb_ref, wo_ref, nrm_ref, wg_ref, wu_ref, wd_ref, *rest, t_lat, tm):
    fin_ref = rest[0] if len(rest) == 2 else None
    o_ref = rest[-1]
    lat_tile = pl.program_id(0) * tm < t_lat
    yb = _head_norm(of_ref[...] + or_ref[...], gain_ref[...], segm_ref[...]) * _silu(bg_ref[...].astype(F32))
    branches = (ya_ref[...], yb.astype(BF16), yc_ref[...], yd_ref[...])
    merged = None
    for n, y in enumerate(branches):
        term = gt_ref[:, n * D_MODEL:(n + 1) * D_MODEL].astype(F32) * _dot(y, wb_ref[n])
        merged = term if merged is None else merged + term
    y = _dot(merged.astype(BF16), wo_ref[...])
    x_mid = x_ref[...] + _tile_mod(mods_ref, 5, lat_tile) * y
    out = _ffn_math(x_mid, lat_tile, mods_ref, nrm_ref, wg_ref, wu_ref, wd_ref, 6)
    o_ref[...] = out if fin_ref is None else _rms_rows(out) * fin_ref[...]


def _merge_ffn(x_all, mods, ya, o_fwd, o_rev, bg, b_gain, segm, yc, yd, gates, wb, wo, nrm, wg, wu, wd, *,
               layer, t_lat, n_rows, final_gain=None):
    n, d = x_all.shape
    tm = TOKEN_TILE
    final = final_gain is not None

    def tile(w):
        return pl.BlockSpec((tm, w), lambda i: (i, 0))

    return pl.pallas_call(
        functools.partial(_merge_kernel, t_lat=t_lat, tm=tm),
        grid=(n_rows // tm,),
        in_specs=[tile(d), _const_spec((2, N_MOD, d)), tile(256), tile(256), tile(256), tile(256),
                  _const_spec((1, 256)), _const_spec((256, 256)), tile(256), tile(256),
                  tile(N_BRANCH * d), _layer_spec(wb.shape, layer), _layer_spec(wo.shape, layer),
                  _const_spec((1, d)), _layer_spec(wg.shape, layer), _layer_spec(wu.shape, layer),
                  _layer_spec(wd.shape, layer)] + ([_const_spec((1, d))] if final else []),
        out_specs=tile(d),
        out_shape=jax.ShapeDtypeStruct((n_rows if final else n, d), F32),
        input_output_aliases={} if final else {0: 0},
        compiler_params=_cparams(("parallel",)),
        name="merge_ffn",
    )(x_all, mods, ya, o_fwd, o_rev, bg, b_gain, segm, yc, yd, gates, wb, wo, nrm, wg, wu, wd,
      *([final_gain] if final else []))


def _rope_tables_padded(t_lat, n_ctx):
    rows = t_lat // GRID_W
    half = HEAD_DIM // 2
    nf = half // 2
    inv = 1.0 / (ROPE_THETA ** (jnp.arange(0, half, 2, dtype=F32) / half))
    ang_r = jnp.arange(rows, dtype=F32)[:, None] * inv
    ang_c = jnp.arange(GRID_W, dtype=F32)[:, None] * inv

    def over_rows(a):
        return jnp.broadcast_to(a[:, None, :], (rows, GRID_W, nf))

    def over_cols(a):
        return jnp.broadcast_to(a[None, :, :], (rows, GRID_W, nf))

    cos = jnp.concatenate([over_rows(jnp.cos(ang_r))] * 2 + [over_cols(jnp.cos(ang_c))] * 2, axis=2)
    sin = jnp.concatenate([over_rows(-jnp.sin(ang_r)), over_rows(jnp.sin(ang_r)),
                           over_cols(-jnp.sin(ang_c)), over_cols(jnp.sin(ang_c))], axis=2)
    cos = cos.reshape(t_lat, HEAD_DIM)
    sin = sin.reshape(t_lat, HEAD_DIM)
    cos = jnp.concatenate([cos, jnp.ones((n_ctx, HEAD_DIM), F32)], axis=0)
    sin = jnp.concatenate([sin, jnp.zeros((n_ctx, HEAD_DIM), F32)], axis=0)
    return jnp.tile(cos, (1, 2)), jnp.tile(sin, (1, 2))


def _block_diag(w_group):
    g, ci, co = w_group.shape
    out = jnp.zeros((g * ci, g * co), w_group.dtype)
    for n in range(g):
        out = out.at[n * ci:(n + 1) * ci, n * co:(n + 1) * co].set(w_group[n])
    return out


def kernel(x, c, ctx, c_ctx, w_ada, b_ada, ffn1_norm, ffn1_w_gate, ffn1_w_up, ffn1_w_down, mix_norm, w_in, a_q_norm, a_k_norm, b_lb_logits, b_o_norm, c_w_group, c_scale, d_rel_bias, w_branch, w_out, ffn2_norm, ffn2_w_gate, ffn2_w_up, ffn2_w_down, final_norm):
    assert x.shape[0] == 1 and ctx.shape[0] == 1
    depth = w_ada.shape[0]
    t_lat = x.shape[1]
    n_ctx = ctx.shape[1]
    n = t_lat + n_ctx
    rows = t_lat // GRID_W
    assert t_lat % max(NA_QB, min(FLASH_TQ, t_lat), n_ctx) == 0 and n_ctx % TOKEN_TILE == 0 and rows >= 2 * NA_QROWS

    x_all = None
    mods_all = _ada_mods(c, c_ctx, w_ada, b_ada)
    cos, sin = _rope_tables_padded(t_lat, n_ctx)
    segm = (jnp.kron(jnp.eye(BRANCH_WIDTH // HEAD_DIM), jnp.ones((HEAD_DIM, HEAD_DIM))) / HEAD_DIM).astype(BF16)
    lb_all = jnp.cumsum(jax.nn.softmax(b_lb_logits.astype(F32), axis=0), axis=0)
    lb_all = lb_all - lb_all[:1]
    wg1, wu1, wd1 = ffn1_w_gate.astype(BF16), ffn1_w_up.astype(BF16), ffn1_w_down.astype(BF16)
    wg2, wu2, wd2 = ffn2_w_gate.astype(BF16), ffn2_w_up.astype(BF16), ffn2_w_down.astype(BF16)
    w_in_b, w_branch_b, w_out_b = w_in.astype(BF16), w_branch.astype(BF16), w_out.astype(BF16)
    na_bias, na_masks = _na_bias_tables(d_rel_bias, rows)

    for l in range(depth):
        with_ctx_out = l < depth - 1
        n_rows = n if with_ctx_out else t_lat
        mods = mods_all[l]
        x_all = _ffn_half(x[0] if l == 0 else x_all, mods, ffn1_norm[l][None], wg1, wu1, wd1, layer=l, t_lat=t_lat,
                          k0=0, n_rows=n, x_ctx=ctx[0] if l == 0 else None)
        (aq, ak, av, bq, bff, bfb, bi, bg, cx, dq, dk, dv, gates) = _in_proj(
            x_all, mods, mix_norm[l][None], w_in_b, cos, sin,
            jnp.tile(a_q_norm[l], A_HEADS)[None], jnp.tile(a_k_norm[l], A_KV_HEADS)[None], segm, layer=l, t_lat=t_lat)
        if with_ctx_out:
            ya, yd = _ctx_attn(aq, ak, av, dq, dk, dv, t_lat=t_lat)
        else:
            ya = jnp.zeros((n, BRANCH_WIDTH), BF16)
            yd = ya
        score_bound = (HEAD_DIM * ATTN_SCALE * LOG2_E * FLASH_BOUND_MARGIN
                       * jnp.max(jnp.abs(a_q_norm[l])) * jnp.max(jnp.abs(a_k_norm[l]))).astype(F32)
        ya = _gqa_latent(aq, ak, av, ya, score_bound, t_lat=t_lat)
        yd = _na_latent(dq, dk, dv, na_bias, na_masks, yd, layer=l, t_lat=t_lat)
        yc = _pool_mixer(cx, _block_diag(c_w_group[l]).astype(BF16), c_scale[l][None], t_lat=t_lat)
        o_fwd, o_rev = _hgrn_scans(bq, bff, bfb, bi, lb_all[l, 0][None], lb_all[l, 1][None], t_lat=t_lat)
        x_all = _merge_ffn(x_all, mods, ya, o_fwd, o_rev, bg, jnp.tile(b_o_norm[l], B_HEADS)[None], segm, yc, yd,
                           gates, w_branch_b, w_out_b, ffn2_norm[l][None], wg2, wu2, wd2, layer=l, t_lat=t_lat,
                           n_rows=n_rows, final_gain=None if with_ctx_out else final_norm[None])
    return x_all[None]
```

```python
import functools

import numpy as np
import jax
import jax.numpy as jnp
from jax import lax
from jax.experimental import pallas as pl
from jax.experimental.pallas import tpu as pltpu

F32 = jnp.float32
BF16 = jnp.bfloat16

D_MODEL = 1024
GRID_W = 64
HEAD_DIM = 64
BRANCH_WIDTH = 256
N_BRANCH = 4
A_HEADS = 4
A_KV_HEADS = 2
ROPE_THETA = 10000.0
B_HEADS = 4
B_KEY_DIM = 64
C_WINDOWS = (2, 4, 8, 16)
D_HEADS = 4
NA_WIN_R = 8
NA_WIN_C = 16
N_MOD = 9
EPS = 1e-6
ATTN_SCALE = HEAD_DIM ** -0.5
LOG2_E = 1.4426950408889634
NEG_BIG = -1e30

V7X_VMEM_BYTES = 64 * 1024 * 1024
VMEM_LIMIT = V7X_VMEM_BYTES - 8 * 1024 * 1024

SEG_AQ = (0, 256)
SEG_AK = (256, 384)
SEG_AV = (384, 512)
SEG_BQ = (512, 768)
SEG_BFF = (768, 1024)
SEG_BFB = (1024, 1280)
SEG_BI = (1280, 1536)
SEG_BG = (1536, 1792)
SEG_CX = (1792, 2048)
SEG_DQ = (2048, 2304)
SEG_DK = (2304, 2560)
SEG_DV = (2560, 2816)
SEG_GATE = (2816, 2816 + N_BRANCH * D_MODEL)
IN_WIDTH = SEG_GATE[1]

ADA_TILE = 3072
TOKEN_TILE = 256
POOL_TILE = 2048
FLASH_TQ = 4096
FLASH_QC = 512
FLASH_SB = 256
FLASH_NBUF = 4
FLASH_MAX_EXPONENT_SPAN = 100.0
FLASH_BOUND_MARGIN = 1.02
NA_QROWS = 8
NA_QB = NA_QROWS * GRID_W
NA_HALO = 256
HG_BLOCK = 256
HG_CHUNK = 128
HG_SUB = 8


def _cparams(sem):
    return pltpu.CompilerParams(dimension_semantics=sem, vmem_limit_bytes=VMEM_LIMIT)


def _const_spec(shape):
    nd = len(shape)
    return pl.BlockSpec(shape, lambda *_: (0,) * nd, pipeline_mode=pl.Buffered(1))


def _layer_spec(shape, layer):
    nd = len(shape)
    return pl.BlockSpec((None,) + tuple(shape[1:]), lambda *_: (layer,) + (0,) * (nd - 1),
                        pipeline_mode=pl.Buffered(1))


def _sigmoid(x):
    return 1.0 / (1.0 + jnp.exp(-x))


def _silu(x):
    return x * _sigmoid(x)


def _dot(a, b):
    return jnp.dot(a, b, preferred_element_type=F32)


def _dot_nt(a, b):
    return lax.dot_general(a, b, (((1,), (1,)), ((), ())), preferred_element_type=F32)


def _dot_tn(a, b):
    return lax.dot_general(a, b, (((0,), (0,)), ((), ())), preferred_element_type=F32)


def _split_dot(x, m):
    hi = x.astype(BF16)
    r1 = x - hi.astype(F32)
    mid = r1.astype(BF16)
    lo = (r1 - mid.astype(F32)).astype(BF16)
    return _dot(hi, m) + _dot(mid, m) + _dot(lo, m)


def _split_dot_left(m, x):
    hi = x.astype(BF16)
    r1 = x - hi.astype(F32)
    mid = r1.astype(BF16)
    lo = (r1 - mid.astype(F32)).astype(BF16)
    return _dot(m, hi) + _dot(m, mid) + _dot(m, lo)


def _tile_mod(mods_ref, k, lat_tile):
    return jnp.where(lat_tile, mods_ref[0, k:k + 1, :], mods_ref[1, k:k + 1, :])


def _rms_rows(x):
    return x * lax.rsqrt(jnp.mean(x * x, axis=-1, keepdims=True) + EPS)


def _modulated_norm(x, nrm_ref, mods_ref, k_shift, lat_tile):
    col_scale = nrm_ref[...] * (1.0 + _tile_mod(mods_ref, k_shift + 1, lat_tile))
    return (_rms_rows(x) * col_scale + _tile_mod(mods_ref, k_shift, lat_tile)).astype(BF16)


def _ada_kernel(s_ref, w_ref, b_ref, o_ref):
    o_ref[0] = _dot(_silu(s_ref[...]).astype(BF16), w_ref[0].astype(BF16)) + b_ref[0]


def _ada_mods(c, c_ctx, w_ada, b_ada):
    depth = w_ada.shape[0]
    width = w_ada.shape[2]
    tn = ADA_TILE
    s = jnp.zeros((8, D_MODEL), F32).at[0].set(c[0]).at[1].set(c_ctx)
    out = pl.pallas_call(
        _ada_kernel,
        grid=(depth, width // tn),
        in_specs=[
            pl.BlockSpec((8, D_MODEL), lambda l, j: (0, 0)),
            pl.BlockSpec((1, D_MODEL, tn), lambda l, j: (l, 0, j)),
            pl.BlockSpec((1, 1, tn), lambda l, j: (l, 0, j)),
        ],
        out_specs=pl.BlockSpec((1, 8, tn), lambda l, j: (l, 0, j)),
        out_shape=jax.ShapeDtypeStruct((depth, 8, width), F32),
        compiler_params=_cparams(("parallel", "parallel")),
        name="ada_mods",
    )(s, w_ada, b_ada.reshape(depth, 1, width))
    return out[:, :2].reshape(depth, 2, N_MOD, D_MODEL)


def _ffn_math(x, lat_tile, mods_ref, nrm_ref, wg_ref, wu_ref, wd_ref, k0):
    z = _modulated_norm(x, nrm_ref, mods_ref, k0, lat_tile)
    g = _dot(z, wg_ref[...])
    u = _dot(z, wu_ref[...])
    a = (_silu(g) * u).astype(BF16)
    return x + (0.5 * _tile_mod(mods_ref, k0 + 2, lat_tile)) * _dot(a, wd_ref[...])


def _ffn_kernel(*refs, t_lat, tm, k0, split_input):
    if split_input:
        xl_ref, xc_ref, mods_ref, nrm_ref, wg_ref, wu_ref, wd_ref, o_ref = refs
    else:
        x_ref, mods_ref, nrm_ref, wg_ref, wu_ref, wd_ref, o_ref = refs
    lat_tile = pl.program_id(0) * tm < t_lat
    x = jnp.where(lat_tile, xl_ref[...], xc_ref[...]) if split_input else x_ref[...]
    o_ref[...] = _ffn_math(x, lat_tile, mods_ref, nrm_ref, wg_ref, wu_ref, wd_ref, k0)


def _ffn_half(x_all, mods, nrm, wg, wu, wd, *, layer, t_lat, k0, n_rows, x_ctx=None):
    split_input = x_ctx is not None
    d = x_all.shape[1]
    n = x_all.shape[0] + (x_ctx.shape[0] if split_input else 0)
    tm = TOKEN_TILE
    if split_input:
        n_lat_tiles = t_lat // tm
        x_specs = [pl.BlockSpec((tm, d), lambda i: (jnp.minimum(i, n_lat_tiles - 1), 0)),
                   pl.BlockSpec((tm, d), lambda i: (jnp.maximum(i - n_lat_tiles, 0), 0))]
        x_args = [x_all, x_ctx]
    else:
        x_specs = [pl.BlockSpec((tm, d), lambda i: (i, 0))]
        x_args = [x_all]
    return pl.pallas_call(
        functools.partial(_ffn_kernel, t_lat=t_lat, tm=tm, k0=k0, split_input=split_input),
        grid=(n_rows // tm,),
        in_specs=x_specs + [
            _const_spec((2, N_MOD, d)),
            _const_spec((1, d)),
            _layer_spec(wg.shape, layer),
            _layer_spec(wu.shape, layer),
            _layer_spec(wd.shape, layer),
        ],
        out_specs=pl.BlockSpec((tm, d), lambda i: (i, 0)),
        out_shape=jax.ShapeDtypeStruct((n, d), F32),
        input_output_aliases={} if split_input else {0: 0},
        compiler_params=_cparams(("parallel",)),
        name="ffn_half",
    )(*x_args, mods, nrm, wg, wu, wd)


def _head_norm(y, gain, segm):
    ms = _split_dot(y * y, segm)
    return y * lax.rsqrt(ms + EPS) * gain


def _rope(y, cos, sin_signed):
    w = y.shape[1]
    lane = lax.broadcasted_iota(jnp.int32, y.shape, 1)
    partner = jnp.where((lane & 31) < 16, pltpu.roll(y, w - 16, 1), pltpu.roll(y, 16, 1))
    return y * cos + partner * sin_signed


def _inproj_kernel(x_ref, mods_ref, nrm_ref, w_ref, cos_ref, sin_ref, gq_ref, gk_ref, segm_ref,
                   aq_ref, ak_ref, av_ref, bq_ref, bff_ref, bfb_ref, bi_ref, bg_ref, cx_ref,
                   dq_ref, dk_ref, dv_ref, gt_ref, *, t_lat, tm):
    z = _modulated_norm(x_ref[...], nrm_ref, mods_ref, 3, pl.program_id(0) * tm < t_lat)

    def proj(seg):
        return _dot(z, w_ref[:, seg[0]:seg[1]])

    cos = cos_ref[...]
    sin = sin_ref[...]
    q = _rope(_head_norm(proj(SEG_AQ), gq_ref[...], segm_ref[...]),
              jnp.concatenate([cos, cos], axis=1), jnp.concatenate([sin, sin], axis=1))
    aq_ref[...] = (q * (ATTN_SCALE * LOG2_E)).T.astype(BF16)
    k = _rope(_head_norm(proj(SEG_AK), gk_ref[...], segm_ref[0:128, 0:128]), cos, sin)
    zeros = jnp.zeros((tm, HEAD_DIM), F32)
    ak_ref[...] = jnp.concatenate([k[:, :HEAD_DIM], zeros, k[:, HEAD_DIM:], zeros], axis=1).astype(BF16)
    v = proj(SEG_AV)
    one_hot = jnp.where(lax.broadcasted_iota(jnp.int32, (tm, HEAD_DIM), 1) == 0, 1.0, 0.0)
    av_ref[...] = jnp.concatenate([v[:, :HEAD_DIM], one_hot, v[:, HEAD_DIM:], one_hot], axis=1).T.astype(BF16)

    bq_ref[...] = (proj(SEG_BQ) * (B_KEY_DIM ** -0.5)).astype(BF16)
    bff_ref[...] = proj(SEG_BFF)
    bfb_ref[...] = proj(SEG_BFB)
    bi_ref[...] = proj(SEG_BI).astype(BF16)
    bg_ref[...] = proj(SEG_BG).astype(BF16)
    cx_ref[...] = proj(SEG_CX)
    dq_ref[...] = (proj(SEG_DQ) * ATTN_SCALE).astype(BF16)
    dk_ref[...] = proj(SEG_DK).astype(BF16)
    dv = proj(SEG_DV)
    dv_ref[...] = jnp.concatenate(
        [piece for h in range(D_HEADS) for piece in (dv[:, HEAD_DIM * h:HEAD_DIM * (h + 1)], one_hot)],
        axis=1).astype(BF16)
    gw = 512
    for c0 in range(SEG_GATE[0], SEG_GATE[1], gw):
        o0 = c0 - SEG_GATE[0]
        gt_ref[:, o0:o0 + gw] = _sigmoid(proj((c0, c0 + gw))).astype(BF16)


def _in_proj(x_all, mods, nrm, w_in, cos, sin, gq, gk, segm, *, layer, t_lat):
    n, d = x_all.shape
    tm = TOKEN_TILE
    widths = [(256, BF16), (256, BF16), (256, BF16), (256, BF16), (256, F32), (256, F32), (256, BF16),
              (256, BF16), (256, F32), (256, BF16), (256, BF16), (2 * D_HEADS * HEAD_DIM, BF16),
              (N_BRANCH * D_MODEL, BF16)]
    transposed = (0, 2)
    out_specs = [pl.BlockSpec((w, tm), lambda i: (0, i)) if o in transposed else pl.BlockSpec((tm, w), lambda i: (i, 0))
                 for o, (w, _) in enumerate(widths)]
    out_shape = [jax.ShapeDtypeStruct((w, n) if o in transposed else (n, w), dt) for o, (w, dt) in enumerate(widths)]
    return pl.pallas_call(
        functools.partial(_inproj_kernel, t_lat=t_lat, tm=tm),
        grid=(n // tm,),
        in_specs=[
            pl.BlockSpec((tm, d), lambda i: (i, 0)),
            _const_spec((2, N_MOD, d)),
            _const_spec((1, d)),
            _layer_spec(w_in.shape, layer),
            pl.BlockSpec((tm, 128), lambda i: (i, 0)),
            pl.BlockSpec((tm, 128), lambda i: (i, 0)),
            _const_spec((1, 256)),
            _const_spec((1, 128)),
            _const_spec((256, 256)),
        ],
        out_specs=out_specs,
        out_shape=out_shape,
        compiler_params=_cparams(("parallel",)),
        name="in_proj",
    )(x_all, mods, nrm, w_in, cos, sin, gq, gk, segm)


def _softmax_attend(q, k, v, exp_fn):
    s = _dot_nt(q, k)
    p = exp_fn(s - jnp.max(s, axis=-1, keepdims=True))
    return _dot(p.astype(BF16), v) / jnp.sum(p, axis=-1, keepdims=True)


def _ctx_attn_kernel(aqt_ref, ak_ref, avt_ref, dq_ref, dk_ref, dv_ref, ya_ref, yd_ref):
    outs = []
    for h in range(A_HEADS):
        g = h // (A_HEADS // A_KV_HEADS)
        st = _dot(ak_ref[:, 128 * g:128 * g + 64], aqt_ref[64 * h:64 * h + 64, :])
        pt = jnp.exp2(st - jnp.max(st, axis=0, keepdims=True)).astype(BF16)
        acc = _dot(avt_ref[128 * g:128 * g + 128, :], pt)
        outs.append((acc[:HEAD_DIM] / acc[HEAD_DIM:HEAD_DIM + 1]).T)
    ya_ref[...] = jnp.concatenate(outs, axis=1).astype(BF16)
    outs = []
    for h in range(D_HEADS):
        hs = slice(64 * h, 64 * h + 64)
        outs.append(_softmax_attend(dq_ref[:, hs], dk_ref[:, hs], dv_ref[:, 128 * h:128 * h + 64], jnp.exp))
    yd_ref[...] = jnp.concatenate(outs, axis=1).astype(BF16)


def _ctx_attn(aqt, ak, avt, dq, dk, dv, *, t_lat):
    n = ak.shape[0]
    n_ctx = n - t_lat
    blk = t_lat // n_ctx

    def spec(w):
        return pl.BlockSpec((n_ctx, w), lambda i: (blk, 0))

    def spec_t(w):
        return pl.BlockSpec((w, n_ctx), lambda i: (0, blk))

    return pl.pallas_call(
        _ctx_attn_kernel,
        grid=(1,),
        in_specs=[spec_t(256), spec(256), spec_t(256), spec(256), spec(256), spec(dv.shape[1])],
        out_specs=[spec(256), spec(256)],
        out_shape=[jax.ShapeDtypeStruct((n, BRANCH_WIDTH), BF16)] * 2,
        compiler_params=_cparams(("arbitrary",)),
        name="ctx_attn",
    )(aqt, ak, avt, dq, dk, dv)


def _flash_kernel(qt_ref, k_ref, vt_ref, prev_ref, o_ref, m_ref, acc_ref, *bufs, tk, tq):
    del prev_ref
    s_refs, p_refs = bufs[:FLASH_NBUF], bufs[FLASH_NBUF:]
    j = pl.program_id(1)

    @pl.when(j == 0)
    def _():
        m_ref[...] = jnp.full(m_ref.shape, -jnp.inf, F32)
        acc_ref[...] = jnp.zeros(acc_ref.shape, F32)

    group = A_HEADS // A_KV_HEADS
    n_qc = tq // FLASH_QC
    n_sb = tk // FLASH_SB
    chains = [(h, c) for h in range(A_HEADS) for c in range(n_qc)]

    def score_block(ci, sb):
        h, c = chains[ci]
        g = h // group
        rows = slice(sb * FLASH_SB, (sb + 1) * FLASH_SB)
        st = _dot(k_ref[rows, 128 * g:128 * g + 64], qt_ref[64 * h:64 * h + 64, c * FLASH_QC:(c + 1) * FLASH_QC])
        s_refs[ci % FLASH_NBUF][rows, :] = st
        return jnp.max(st, axis=0, keepdims=True)

    def exp_block(ci, sb, m_new):
        rows = slice(sb * FLASH_SB, (sb + 1) * FLASH_SB)
        p_refs[ci % 2][rows, :] = jnp.exp2(s_refs[ci % FLASH_NBUF][rows, :] - m_new).astype(BF16)

    def value_block(ci, sb):
        g = chains[ci][0] // group
        rows = slice(sb * FLASH_SB, (sb + 1) * FLASH_SB)
        return _dot(vt_ref[128 * g:128 * g + 128, rows], p_refs[ci % 2][rows, :])

    def fold(a, b):
        return b if a is None else a + b

    def fold_max(a, b):
        return b if a is None else jnp.maximum(a, b)

    n_ch = len(chains)
    blk_max, m_new, alpha = {}, {}, {}
    for stage in range(-1, n_ch + 1):
        c_s, c_e, c_v = stage + 1, stage, stage - 1
        if 0 <= c_e < n_ch:
            h, c = chains[c_e]
            cols = slice(c * FLASH_QC, (c + 1) * FLASH_QC)
            m_old = m_ref[h, :, cols]
            m_new[c_e] = jnp.maximum(m_old, blk_max.pop(c_e))
            m_ref[h, :, cols] = m_new[c_e]
            alpha[c_e] = jnp.exp2(m_old - m_new[c_e])
        part, mx = None, None
        for sb in range(n_sb):
            if c_s < n_ch:
                mx = fold_max(mx, score_block(c_s, sb))
            if 0 <= c_e < n_ch:
                exp_block(c_e, sb, m_new[c_e])
            if 0 <= c_v:
                part = fold(part, value_block(c_v, sb))
        if c_s < n_ch:
            blk_max[c_s] = mx
        if 0 <= c_v:
            h, c = chains[c_v]
            cols = slice(c * FLASH_QC, (c + 1) * FLASH_QC)
            acc_ref[h, :, cols] = alpha.pop(c_v) * acc_ref[h, :, cols] + part

    @pl.when(j == pl.num_programs(1) - 1)
    def _():
        outs = []
        for h in range(A_HEADS):
            acc = acc_ref[h]
            outs.append((acc[:HEAD_DIM] / acc[HEAD_DIM:HEAD_DIM + 1]).T)
        o_ref[...] = jnp.concatenate(outs, axis=1).astype(BF16)


def _flash_bounded_kernel(bound_ref, qt_ref, k_ref, vt_ref, prev_ref, o_ref, acc_ref, *p_refs, tk, tq):
    del prev_ref
    j = pl.program_id(1)

    @pl.when(j == 0)
    def _():
        acc_ref[...] = jnp.zeros(acc_ref.shape, F32)

    bound = bound_ref[0]
    group = A_HEADS // A_KV_HEADS
    n_qc = tq // FLASH_QC
    n_sb = tk // FLASH_SB
    chains = [(h, c) for h in range(A_HEADS) for c in range(n_qc)]

    def prob_block(ci, sb):
        h, c = chains[ci]
        g = h // group
        rows = slice(sb * FLASH_SB, (sb + 1) * FLASH_SB)
        st = _dot(k_ref[rows, 128 * g:128 * g + 64], qt_ref[64 * h:64 * h + 64, c * FLASH_QC:(c + 1) * FLASH_QC])
        p_refs[ci % 2][rows, :] = jnp.exp2(st - bound).astype(BF16)

    def value_block(ci, sb):
        g = chains[ci][0] // group
        rows = slice(sb * FLASH_SB, (sb + 1) * FLASH_SB)
        return _dot(vt_ref[128 * g:128 * g + 128, rows], p_refs[ci % 2][rows, :])

    n_ch = len(chains)
    for stage in range(n_ch + 1):
        part = None
        for sb in range(n_sb):
            if stage < n_ch:
                prob_block(stage, sb)
            if stage >= 1:
                blk = value_block(stage - 1, sb)
                part = blk if part is None else part + blk
        if stage >= 1:
            h, c = chains[stage - 1]
            cols = slice(c * FLASH_QC, (c + 1) * FLASH_QC)
            acc_ref[h, :, cols] = acc_ref[h, :, cols] + part

    @pl.when(j == pl.num_programs(1) - 1)
    def _():
        outs = []
        for h in range(A_HEADS):
            acc = acc_ref[h]
            outs.append((acc[:HEAD_DIM] / acc[HEAD_DIM:HEAD_DIM + 1]).T)
        o_ref[...] = jnp.concatenate(outs, axis=1).astype(BF16)


def _flash_tk(n):
    for tk in (1280, 768, 512, 256):
        if n % tk == 0:
            return tk
    raise ValueError(f"unsupported key count {n}")


def _gqa_latent(aqt, ak, avt, ya_prev, score_bound, *, t_lat):
    n = ak.shape[0]
    tk = _flash_tk(n)
    tq = min(FLASH_TQ, t_lat)

    def call(body, extra_specs, scratch, name):
        return pl.pallas_call(
            functools.partial(body, tk=tk, tq=tq),
            grid=(t_lat // tq, n // tk),
            in_specs=extra_specs + [
                pl.BlockSpec((256, tq), lambda i, j: (0, i)),
                pl.BlockSpec((tk, 256), lambda i, j: (j, 0)),
                pl.BlockSpec((256, tk), lambda i, j: (0, j)),
                pl.BlockSpec(memory_space=pl.ANY),
            ],
            out_specs=pl.BlockSpec((tq, 256), lambda i, j: (i, 0)),
            out_shape=jax.ShapeDtypeStruct((n, BRANCH_WIDTH), BF16),
            scratch_shapes=scratch,
            input_output_aliases={len(extra_specs) + 3: 0},
            compiler_params=_cparams(("parallel", "arbitrary")),
            name=name,
        )

    def online(_):
        scratch = ([pltpu.VMEM((A_HEADS, 1, tq), F32), pltpu.VMEM((A_HEADS, 128, tq), F32)]
                   + [pltpu.VMEM((tk, FLASH_QC), F32)] * FLASH_NBUF + [pltpu.VMEM((tk, FLASH_QC), BF16)] * 2)
        return call(_flash_kernel, [], scratch, "gqa_flash")(aqt, ak, avt, ya_prev)

    def bounded(_):
        scratch = [pltpu.VMEM((A_HEADS, 128, tq), F32)] + [pltpu.VMEM((tk, FLASH_QC), BF16)] * 2
        return call(_flash_bounded_kernel, [pl.BlockSpec(memory_space=pltpu.SMEM)], scratch,
                    "gqa_flash_bounded")(score_bound.reshape(1), aqt, ak, avt, ya_prev)

    return lax.cond(2.0 * score_bound <= FLASH_MAX_EXPONENT_SPAN, bounded, online, None)


def _na_kernel(q_ref, kp_ref, km_ref, kn_ref, vp_ref, vm_ref, vn_ref, kc_ref, vc_ref, bias_ref, mask_ref, prev_ref,
               o_ref, bm_ref, *, nb):
    del prev_ref
    b = pl.program_id(0)

    @pl.when((b == 0) | (b == 1) | (b == nb - 1))
    def _():
        for h in range(D_HEADS):
            bm_ref[h] = bias_ref[h].astype(F32) + mask_ref[0]

    def scores(h):
        hs = slice(64 * h, 64 * h + 64)
        q = q_ref[:, hs]
        kcat = jnp.concatenate([kp_ref[:, hs], km_ref[:, hs], kn_ref[:, hs]], axis=0)
        return _dot_nt(q, kcat) + bm_ref[h], _dot_nt(q, kc_ref[:, hs])

    def softmax(s_loc, s_ctx):
        m = jnp.maximum(jnp.max(s_loc, axis=-1, keepdims=True), jnp.max(s_ctx, axis=-1, keepdims=True))
        return jnp.exp(s_loc - m).astype(BF16), jnp.exp(s_ctx - m).astype(BF16)

    def attend(h, p_loc, p_ctx):
        vs = slice(128 * h, 128 * h + 128)
        vcat = jnp.concatenate([vp_ref[:, vs], vm_ref[:, vs], vn_ref[:, vs]], axis=0)
        o = _dot(p_loc, vcat) + _dot(p_ctx, vc_ref[:, vs])
        return o[:, :HEAD_DIM] / o[:, HEAD_DIM:HEAD_DIM + 1]

    s = {0: scores(0)}
    p, outs = {}, []
    for h in range(D_HEADS):
        if h + 1 < D_HEADS:
            s[h + 1] = scores(h + 1)
        p[h] = softmax(*s.pop(h))
        if h >= 1:
            outs.append(attend(h - 1, *p.pop(h - 1)))
    outs.append(attend(D_HEADS - 1, *p.pop(D_HEADS - 1)))
    o_ref[...] = jnp.concatenate(outs, axis=1).astype(BF16)


def _na_bias_tables(rel_bias, rows):
    wr = min(NA_WIN_R, rows)
    halo_rows = NA_HALO // GRID_W
    krows = NA_QROWS + 2 * halo_rows
    nb = rows // NA_QROWS
    qc = np.arange(GRID_W)[:, None]
    kc = np.arange(GRID_W)[None, :]
    cs = np.clip(qc - NA_WIN_C // 2, 0, GRID_W - NA_WIN_C)
    in_col = (kc >= cs) & (kc < cs + NA_WIN_C)
    e_col = (kc - qc + (NA_WIN_C - 1))[:, :, None] == np.arange(2 * NA_WIN_C - 1)
    qr_l = np.arange(NA_QROWS)[:, None]
    kr_l = np.arange(krows)[None, :]
    e_row = (kr_l - halo_rows - qr_l + (NA_WIN_R - 1))[:, :, None] == np.arange(2 * NA_WIN_R - 1)
    hi = lax.Precision.HIGHEST
    tmp = jnp.einsum("lhrc,qkr->lhqkc", rel_bias.astype(F32), e_row.astype(np.float32), precision=hi)
    full = jnp.einsum("lhqkc,pjc->lhqpkj", tmp, e_col.astype(np.float32), precision=hi)
    masks = []
    for b in (0, min(1, nb - 1), nb - 1):
        qr = NA_QROWS * b + qr_l
        kr = NA_QROWS * b - halo_rows + kr_l
        rs = np.clip(qr - wr // 2, 0, rows - wr)
        in_row = (kr >= rs) & (kr < rs + wr)
        mask = in_row[:, None, :, None] & in_col[None, :, None, :]
        masks.append(np.where(mask, 0.0, NEG_BIG).reshape(NA_QB, krows * GRID_W))
    return full.astype(BF16).reshape(-1, D_HEADS, NA_QB, krows * GRID_W), np.stack(masks).astype(np.float32)


def _na_latent(dq, dk, dv, bias, masks, yd_prev, *, layer, t_lat):
    n = dq.shape[0]
    nb = t_lat // NA_QB
    r = NA_QB // NA_HALO
    last_halo = t_lat // NA_HALO - 1
    ctx_blk = t_lat // (n - t_lat)
    n_ctx = n - t_lat

    def prev_map(b):
        return (jnp.maximum(r * b - 1, 0), 0)

    def next_map(b):
        return (jnp.minimum(r * b + r, last_halo), 0)

    def variant(b):
        return (jnp.where(b == 0, 0, jnp.where(b == nb - 1, 2, 1)), 0, 0)

    def specs(w):
        return (pl.BlockSpec((NA_QB, w), lambda b: (b, 0)), pl.BlockSpec((NA_HALO, w), prev_map),
                pl.BlockSpec((NA_HALO, w), next_map), pl.BlockSpec((n_ctx, w), lambda b: (ctx_blk, 0)))

    main, prev, nxt, ctx = specs(256)
    vmain, vprev, vnxt, vctx = specs(dv.shape[1])
    return pl.pallas_call(
        functools.partial(_na_kernel, nb=nb),
        grid=(nb,),
        in_specs=[main, prev, main, nxt, vprev, vmain, vnxt, ctx, vctx,
                  _layer_spec(bias.shape, layer),
                  pl.BlockSpec((1, NA_QB, NA_QB + 2 * NA_HALO), variant),
                  pl.BlockSpec(memory_space=pl.ANY)],
        out_specs=main,
        out_shape=jax.ShapeDtypeStruct((n, BRANCH_WIDTH), BF16),
        scratch_shapes=[pltpu.VMEM((D_HEADS, NA_QB, NA_QB + 2 * NA_HALO), F32)],
        input_output_aliases={11: 0},
        compiler_params=_cparams(("arbitrary",)),
        name="na_attn",
    )(dq, dk, dk, dk, dv, dv, dv, dk, dv, bias, masks, yd_prev)


def _pool_kernel(x_ref, xp_ref, xn_ref, w_ref, sc_ref, *rest, seg_lo, seg_hi, tm):
    o_ref, cat_ref, a_ref, b_ref = rest[-4:]
    start = seg_lo + pl.program_id(0) * tm
    x = x_ref[...]
    cat_ref[0:8, :] = jnp.where(start > seg_lo, xp_ref[...], 0.0)
    cat_ref[8:8 + tm, :] = x
    cat_ref[8 + tm:16 + tm, :] = jnp.where(start + tm < seg_hi, xn_ref[...], 0.0)
    cat_ref[16 + tm:32 + tm, :] = jnp.zeros((16, BRANCH_WIDTH), F32)
    a_ref[0:tm + 24, :] = cat_ref[0:tm + 24, :] + cat_ref[1:tm + 25, :]
    s2 = a_ref[7:7 + tm, :]
    b_ref[0:tm + 16, :] = a_ref[0:tm + 16, :] + a_ref[2:tm + 18, :]
    s4 = b_ref[6:6 + tm, :]
    a_ref[0:tm + 8, :] = b_ref[0:tm + 8, :] + b_ref[4:tm + 12, :]
    s8 = a_ref[4:4 + tm, :]
    s16 = a_ref[0:tm, :] + a_ref[8:8 + tm, :]

    pos = start - seg_lo + lax.broadcasted_iota(jnp.int32, (tm, 1), 0)
    seg_len = seg_hi - seg_lo

    def mean(sm, w):
        lo = jnp.clip(pos - w // 2, 0, seg_len)
        hi = jnp.clip(pos - w // 2 + w, 0, seg_len)
        return sm / (hi - lo).astype(F32)

    lane = lax.broadcasted_iota(jnp.int32, (tm, BRANCH_WIDTH), 1)
    gw = BRANCH_WIDTH // len(C_WINDOWS)
    pooled = jnp.where(lane < gw, mean(s2, 2),
                       jnp.where(lane < 2 * gw, mean(s4, 4),
                                 jnp.where(lane < 3 * gw, mean(s8, 8), mean(s16, 16)))) - x
    o_ref[...] = (_dot(pooled.astype(BF16), w_ref[...]) * sc_ref[...]).astype(BF16)


def _pool_segment(cx, w_bd, scale, prev_out, *, seg_lo, seg_hi, tm):
    n = cx.shape[0]
    r = tm // 8
    b0 = seg_lo // tm
    last8 = n // 8 - 1
    in_specs = [
        pl.BlockSpec((tm, 256), lambda i: (b0 + i, 0)),
        pl.BlockSpec((8, 256), lambda i: (jnp.maximum(r * (b0 + i) - 1, 0), 0)),
        pl.BlockSpec((8, 256), lambda i: (jnp.minimum(r * (b0 + i) + r, last8), 0)),
        _const_spec((256, 256)),
        _const_spec((1, 256)),
    ]
    args = [cx, cx, cx, w_bd, scale]
    aliases = {}
    if prev_out is not None:
        in_specs.append(pl.BlockSpec(memory_space=pl.ANY))
        args.append(prev_out)
        aliases = {5: 0}
    return pl.pallas_call(
        functools.partial(_pool_kernel, seg_lo=seg_lo, seg_hi=seg_hi, tm=tm),
        grid=((seg_hi - seg_lo) // tm,),
        in_specs=in_specs,
        out_specs=pl.BlockSpec((tm, 256), lambda i: (b0 + i, 0)),
        out_shape=jax.ShapeDtypeStruct((n, BRANCH_WIDTH), BF16),
        scratch_shapes=[pltpu.VMEM((tm + 32, 256), F32)] * 3,
        input_output_aliases=aliases,
        compiler_params=_cparams(("parallel",)),
        name="pool_mixer",
    )(*args)


def _pool_mixer(cx, w_bd, scale, *, t_lat):
    n = cx.shape[0]
    n_ctx = n - t_lat
    out = _pool_segment(cx, w_bd, scale, None, seg_lo=t_lat, seg_hi=n, tm=n_ctx)
    return _pool_segment(cx, w_bd, scale, out, seg_lo=0, seg_hi=t_lat, tm=min(POOL_TILE, t_lat))


def _hgrn_tables(rev):
    c_len, s, w = HG_CHUNK, HG_SUB, B_HEADS * B_KEY_DIM
    t = np.arange(c_len)
    tri = (t[None, :] >= t[:, None]) if rev else (t[None, :] <= t[:, None])
    pos = np.arange(s)
    keep = (pos[None, :] <= pos[:, None]) if rev else (pos[None, :] >= pos[:, None])
    keep_add = np.where(keep, 0.0, NEG_BIG)[:, :, None] * np.ones((1, 1, w))
    head_of_lane = np.arange(w) // B_KEY_DIM
    col = np.arange(B_HEADS * s)
    sel = (col[None, None, :] == (head_of_lane[None, :, None] * s + pos[:, None, None]))
    bd = (col[:, None] // s) == head_of_lane[None, :]
    hh = head_of_lane[:, None] == head_of_lane[None, :]
    return (jnp.asarray(tri, BF16), jnp.asarray(keep_add, F32), jnp.asarray(sel, BF16),
            jnp.asarray(bd, F32), jnp.asarray(hh, F32))


def _hgrn_kernel(qf_ref, ff_ref, vf_ref, qr_ref, fr_ref, vr_ref, lbf_ref, lbr_ref, trif_ref, keepf_ref, trir_ref,
                 keepr_ref, sel_ref, bd_ref, hh_ref, of_ref, or_ref, stf_ref, str_ref):
    @pl.when(pl.program_id(0) == 0)
    def _():
        stf_ref[...] = jnp.zeros(stf_ref.shape, F32)
        str_ref[...] = jnp.zeros(str_ref.shape, F32)

    st_f, st_r = stf_ref[...], str_ref[...]
    n_chunks = HG_BLOCK // HG_CHUNK
    for step in range(n_chunks):
        rows = slice(step * HG_CHUNK, (step + 1) * HG_CHUNK)
        o, st_f = _hgrn_chunk(qf_ref[rows, :], ff_ref[rows, :], vf_ref[rows, :], st_f, lbf_ref, trif_ref,
                              keepf_ref, sel_ref, bd_ref, hh_ref, rev=False)
        of_ref[rows, :] = o
        ch = n_chunks - 1 - step
        rows = slice(ch * HG_CHUNK, (ch + 1) * HG_CHUNK)
        o, st_r = _hgrn_chunk(qr_ref[rows, :], fr_ref[rows, :], vr_ref[rows, :], st_r, lbr_ref, trir_ref,
                              keepr_ref, sel_ref, bd_ref, hh_ref, rev=True)
        or_ref[rows, :] = o
    stf_ref[...] = st_f
    str_ref[...] = st_r


def _hgrn_chunk(q_bf, f_pre, v, st, lb_ref, tri_ref, keep_ref, sel_ref, bd_ref, hh_ref, *, rev):
    c_len = HG_CHUNK
    n_sub = c_len // HG_SUB
    w = B_HEADS * B_KEY_DIM
    lb = lb_ref[...]
    f = lb + (1.0 - lb) * _sigmoid(f_pre)
    k = 1.0 - f
    lf = jnp.log(f)
    q = q_bf.astype(F32)
    bd = bd_ref[...]

    c = _split_dot_left(tri_ref[...], lf)

    q3 = q.reshape(n_sub, HG_SUB, w)
    k3 = k.reshape(n_sub, HG_SUB, w)
    c3 = c.reshape(n_sub, HG_SUB, w)
    a_diag = None
    for sg in range(HG_SUB):
        ks = jnp.broadcast_to(k3[:, sg:sg + 1, :], k3.shape)
        cs = jnp.broadcast_to(c3[:, sg:sg + 1, :], c3.shape)
        wgt = (q3 * ks) * jnp.exp((c3 - cs) + keep_ref[sg][None])
        part = _dot(wgt.reshape(c_len, w).astype(BF16), sel_ref[sg])
        a_diag = part if a_diag is None else a_diag + part

    o_parts = [None] * n_sub
    for j in range(n_sub):
        r0 = j * HG_SUB
        r_last = r0 if rev else r0 + HG_SUB - 1
        e_j = c[r_last:r_last + 1, :]
        kj = k[r0:r0 + HG_SUB, :] * jnp.exp(e_j - c[r0:r0 + HG_SUB, :])
        kbd = (jnp.concatenate([kj] * B_HEADS, axis=0) * bd).astype(BF16)
        vbd = (jnp.concatenate([v[r0:r0 + HG_SUB, :].astype(F32)] * B_HEADS, axis=0) * bd).astype(BF16)
        lo, hi = (0, r0) if rev else (r0 + HG_SUB, c_len)
        pieces = [a_diag[r0:r0 + HG_SUB]]
        if hi > lo:
            qj = (q[lo:hi] * jnp.exp(c[lo:hi] - e_j)).astype(BF16)
            a_off = _dot_nt(qj, kbd)
            pieces = [a_off] + pieces if rev else pieces + [a_off]
        first = 0 if rev else j
        if (sum(p.shape[0] for p in pieces) % 16) != 0:
            pad = jnp.zeros((HG_SUB, a_diag.shape[1]), F32)
            pieces = pieces + [pad] if rev else [pad] + pieces
            first = first if rev else first - 1
        a_j = jnp.concatenate(pieces, axis=0).astype(BF16)
        contrib = _dot(a_j, vbd)
        for i in range(contrib.shape[0] // HG_SUB):
            piece = contrib[i * HG_SUB:(i + 1) * HG_SUB]
            o_parts[first + i] = piece if o_parts[first + i] is None else o_parts[first + i] + piece
    o = jnp.concatenate(o_parts, axis=0)

    o = o + _dot_nt((q * jnp.exp(c)).astype(BF16), st.astype(BF16))
    r_end = 0 if rev else c_len - 1
    c_end = c[r_end:r_end + 1, :]
    k_end = (k * jnp.exp(c_end - c)).astype(BF16)
    return o, jnp.exp(c_end) * st + _dot_tn(v, k_end) * hh_ref[...]


def _hgrn_scans(bq, f_fwd, f_rev, bi, lb_fwd, lb_rev, *, t_lat):
    n = bq.shape[0]
    c_len = HG_BLOCK
    n_lat = t_lat // c_len
    n_all = n // c_len
    n_ctx = n_all - n_lat

    def blk_fwd(i):
        return (jnp.where(i < n_ctx, n_lat + i, i - n_ctx), 0)

    def blk_rev(i):
        return (jnp.where(i < n_ctx, n_all - 1 - i, n_lat - 1 - (i - n_ctx)), 0)

    tile_f = pl.BlockSpec((c_len, 256), blk_fwd)
    tile_r = pl.BlockSpec((c_len, 256), blk_rev)
    tri_f, keep_f, sel, bd, hh = _hgrn_tables(False)
    tri_r, keep_r, _, _, _ = _hgrn_tables(True)
    consts = [lb_fwd, lb_rev, tri_f, keep_f, tri_r, keep_r, sel, bd, hh]
    return pl.pallas_call(
        _hgrn_kernel,
        grid=(n_all,),
        in_specs=[tile_f, tile_f, tile_f, tile_r, tile_r, tile_r] + [_const_spec(t.shape) for t in consts],
        out_specs=[tile_f, tile_r],
        out_shape=[jax.ShapeDtypeStruct((n, BRANCH_WIDTH), F32)] * 2,
        scratch_shapes=[pltpu.VMEM((256, 256), F32)] * 2,
        compiler_params=_cparams(("arbitrary",)),
        name="hgrn_scans",
    )(bq, f_fwd, bi, bq, f_rev, bi, *consts)


def _merge_kernel(x_ref, mods_ref, ya_ref, of_ref, or_ref, bg_ref, gain_ref, segm_ref, yc_ref, yd_ref, gt_ref,
                  wb_ref, wo_ref, nrm_ref, wg_ref, wu_ref, wd_ref, *rest, t_lat, tm):
    fin_ref = rest[0] if len(rest) == 2 else None
    o_ref = rest[-1]
    lat_tile = pl.program_id(0) * tm < t_lat
    yb = _head_norm(of_ref[...] + or_ref[...], gain_ref[...], segm_ref[...]) * _silu(bg_ref[...].astype(F32))
    branches = (ya_ref[...], yb.astype(BF16), yc_ref[...], yd_ref[...])
    merged = None
    for n, y in enumerate(branches):
        term = gt_ref[:, n * D_MODEL:(n + 1) * D_MODEL].astype(F32) * _dot(y, wb_ref[n])
        merged = term if merged is None else merged + term
    y = _dot(merged.astype(BF16), wo_ref[...])
    x_mid = x_ref[...] + _tile_mod(mods_ref, 5, lat_tile) * y
    out = _ffn_math(x_mid, lat_tile, mods_ref, nrm_ref, wg_ref, wu_ref, wd_ref, 6)
    o_ref[...] = out if fin_ref is None else _rms_rows(out) * fin_ref[...]


def _merge_ffn(x_all, mods, ya, o_fwd, o_rev, bg, b_gain, segm, yc, yd, gates, wb, wo, nrm, wg, wu, wd, *,
               layer, t_lat, n_rows, final_gain=None):
    n, d = x_all.shape
    tm = TOKEN_TILE
    final = final_gain is not None

    def tile(w):
        return pl.BlockSpec((tm, w), lambda i: (i, 0))

    return pl.pallas_call(
        functools.partial(_merge_kernel, t_lat=t_lat, tm=tm),
        grid=(n_rows // tm,),
        in_specs=[tile(d), _const_spec((2, N_MOD, d)), tile(256), tile(256), tile(256), tile(256),
                  _const_spec((1, 256)), _const_spec((256, 256)), tile(256), tile(256),
                  tile(N_BRANCH * d), _layer_spec(wb.shape, layer), _layer_spec(wo.shape, layer),
                  _const_spec((1, d)), _layer_spec(wg.shape, layer), _layer_spec(wu.shape, layer),
                  _layer_spec(wd.shape, layer)] + ([_const_spec((1, d))] if final else []),
        out_specs=tile(d),
        out_shape=jax.ShapeDtypeStruct((n_rows if final else n, d), F32),
        input_output_aliases={} if final else {0: 0},
        compiler_params=_cparams(("parallel",)),
        name="merge_ffn",
    )(x_all, mods, ya, o_fwd, o_rev, bg, b_gain, segm, yc, yd, gates, wb, wo, nrm, wg, wu, wd,
      *([final_gain] if final else []))


def _rope_tables_padded(t_lat, n_ctx):
    rows = t_lat // GRID_W
    half = HEAD_DIM // 2
    nf = half // 2
    inv = 1.0 / (ROPE_THETA ** (jnp.arange(0, half, 2, dtype=F32) / half))
    ang_r = jnp.arange(rows, dtype=F32)[:, None] * inv
    ang_c = jnp.arange(GRID_W, dtype=F32)[:, None] * inv

    def over_rows(a):
        return jnp.broadcast_to(a[:, None, :], (rows, GRID_W, nf))

    def over_cols(a):
        return jnp.broadcast_to(a[None, :, :], (rows, GRID_W, nf))

    cos = jnp.concatenate([over_rows(jnp.cos(ang_r))] * 2 + [over_cols(jnp.cos(ang_c))] * 2, axis=2)
    sin = jnp.concatenate([over_rows(-jnp.sin(ang_r)), over_rows(jnp.sin(ang_r)),
                           over_cols(-jnp.sin(ang_c)), over_cols(jnp.sin(ang_c))], axis=2)
    cos = cos.reshape(t_lat, HEAD_DIM)
    sin = sin.reshape(t_lat, HEAD_DIM)
    cos = jnp.concatenate([cos, jnp.ones((n_ctx, HEAD_DIM), F32)], axis=0)
    sin = jnp.concatenate([sin, jnp.zeros((n_ctx, HEAD_DIM), F32)], axis=0)
    return jnp.tile(cos, (1, 2)), jnp.tile(sin, (1, 2))


def _block_diag(w_group):
    g, ci, co = w_group.shape
    out = jnp.zeros((g * ci, g * co), w_group.dtype)
    for n in range(g):
        out = out.at[n * ci:(n + 1) * ci, n * co:(n + 1) * co].set(w_group[n])
    return out


def kernel(x, c, ctx, c_ctx, w_ada, b_ada, ffn1_norm, ffn1_w_gate, ffn1_w_up, ffn1_w_down, mix_norm, w_in, a_q_norm, a_k_norm, b_lb_logits, b_o_norm, c_w_group, c_scale, d_rel_bias, w_branch, w_out, ffn2_norm, ffn2_w_gate, ffn2_w_up, ffn2_w_down, final_norm):
    assert x.shape[0] == 1 and ctx.shape[0] == 1 and w_in.shape[2] == IN_WIDTH
    depth = w_ada.shape[0]
    t_lat = x.shape[1]
    n_ctx = ctx.shape[1]
    n = t_lat + n_ctx
    rows = t_lat // GRID_W
    assert t_lat % max(NA_QB, min(FLASH_TQ, t_lat), n_ctx) == 0 and n_ctx % TOKEN_TILE == 0 and rows >= 2 * NA_QROWS

    x_all = None
    mods_all = _ada_mods(c, c_ctx, w_ada, b_ada)
    cos, sin = _rope_tables_padded(t_lat, n_ctx)
    segm = (jnp.kron(jnp.eye(BRANCH_WIDTH // HEAD_DIM), jnp.ones((HEAD_DIM, HEAD_DIM))) / HEAD_DIM).astype(BF16)
    lb_all = jnp.cumsum(jax.nn.softmax(b_lb_logits.astype(F32), axis=0), axis=0)
    lb_all = lb_all - lb_all[:1]
    wg1, wu1, wd1 = ffn1_w_gate.astype(BF16), ffn1_w_up.astype(BF16), ffn1_w_down.astype(BF16)
    wg2, wu2, wd2 = ffn2_w_gate.astype(BF16), ffn2_w_up.astype(BF16), ffn2_w_down.astype(BF16)
    w_in_b, w_branch_b, w_out_b = w_in.astype(BF16), w_branch.astype(BF16), w_out.astype(BF16)
    na_bias, na_masks = _na_bias_tables(d_rel_bias, rows)

    for l in range(depth):
        with_ctx_out = l < depth - 1
        n_rows = n if with_ctx_out else t_lat
        mods = mods_all[l]
        x_all = _ffn_half(x[0] if l == 0 else x_all, mods, ffn1_norm[l][None], wg1, wu1, wd1, layer=l, t_lat=t_lat,
                          k0=0, n_rows=n, x_ctx=ctx[0] if l == 0 else None)
        (aq, ak, av, bq, bff, bfb, bi, bg, cx, dq, dk, dv, gates) = _in_proj(
            x_all, mods, mix_norm[l][None], w_in_b, cos, sin,
            jnp.tile(a_q_norm[l], A_HEADS)[None], jnp.tile(a_k_norm[l], A_KV_HEADS)[None], segm, layer=l, t_lat=t_lat)
        if with_ctx_out:
            ya, yd = _ctx_attn(aq, ak, av, dq, dk, dv, t_lat=t_lat)
        else:
            ya = jnp.zeros((n, BRANCH_WIDTH), BF16)
            yd = ya
        score_bound = (HEAD_DIM * ATTN_SCALE * LOG2_E * FLASH_BOUND_MARGIN
                       * jnp.max(jnp.abs(a_q_norm[l])) * jnp.max(jnp.abs(a_k_norm[l]))).astype(F32)
        ya = _gqa_latent(aq, ak, av, ya, score_bound, t_lat=t_lat)
        yd = _na_latent(dq, dk, dv, na_bias, na_masks, yd, layer=l, t_lat=t_lat)
        yc = _pool_mixer(cx, _block_diag(c_w_group[l]).astype(BF16), c_scale[l][None], t_lat=t_lat)
        o_fwd, o_rev = _hgrn_scans(bq, bff, bfb, bi, lb_all[l, 0][None], lb_all[l, 1][None], t_lat=t_lat)
        x_all = _merge_ffn(x_all, mods, ya, o_fwd, o_rev, bg, jnp.tile(b_o_norm[l], B_HEADS)[None], segm, yc, yd,
                           gates, w_branch_b, w_out_b, ffn2_norm[l][None], wg2, wu2, wd2, layer=l, t_lat=t_lat,
                           n_rows=n_rows, final_gain=None if with_ctx_out else final_norm[None])
    return x_all[None]
```

```python
import functools

import numpy as np
import jax
import jax.numpy as jnp
from jax import lax
from jax.experimental import pallas as pl
from jax.experimental.pallas import tpu as pltpu

F32 = jnp.float32
BF16 = jnp.bfloat16

D_MODEL = 1024
GRID_W = 64
HEAD_DIM = 64
BRANCH_WIDTH = 256
N_BRANCH = 4
A_HEADS = 4
A_KV_HEADS = 2
ROPE_THETA = 10000.0
B_HEADS = 4
B_KEY_DIM = 64
C_WINDOWS = (2, 4, 8, 16)
D_HEADS = 4
NA_WIN_R = 8
NA_WIN_C = 16
N_MOD = 9
EPS = 1e-6
ATTN_SCALE = HEAD_DIM ** -0.5
LOG2_E = 1.4426950408889634
NEG_BIG = -1e30

V7X_VMEM_BYTES = 64 * 1024 * 1024
VMEM_LIMIT = V7X_VMEM_BYTES - 8 * 1024 * 1024

SEG_AQ = (0, 256)
SEG_AK = (256, 384)
SEG_AV = (384, 512)
SEG_BQ = (512, 768)
SEG_BFF = (768, 1024)
SEG_BFB = (1024, 1280)
SEG_BI = (1280, 1536)
SEG_BG = (1536, 1792)
SEG_CX = (1792, 2048)
SEG_DQ = (2048, 2304)
SEG_DK = (2304, 2560)
SEG_DV = (2560, 2816)
SEG_GATE = (2816, 2816 + N_BRANCH * D_MODEL)
IN_WIDTH = SEG_GATE[1]

ADA_TILE = 3072
TOKEN_TILE = 256
POOL_TILE = 2048
FLASH_TQ = 4096
FLASH_QC = 512
FLASH_SB = 256
FLASH_NBUF = 4
FLASH_MAX_EXPONENT_SPAN = 100.0
FLASH_BOUND_MARGIN = 1.02
NA_QROWS = 8
NA_QB = NA_QROWS * GRID_W
NA_HALO = 256
HG_BLOCK = 256
HG_CHUNK = 128
HG_SUB = 8


def _cparams(sem):
    return pltpu.CompilerParams(dimension_semantics=sem, vmem_limit_bytes=VMEM_LIMIT)


def _const_spec(shape):
    nd = len(shape)
    return pl.BlockSpec(shape, lambda *_: (0,) * nd, pipeline_mode=pl.Buffered(1))


def _layer_spec(shape, layer):
    nd = len(shape)
    return pl.BlockSpec((None,) + tuple(shape[1:]), lambda *_: (layer,) + (0,) * (nd - 1),
                        pipeline_mode=pl.Buffered(1))


def _sigmoid(x):
    return 1.0 / (1.0 + jnp.exp(-x))


def _silu(x):
    return x * _sigmoid(x)


def _dot(a, b):
    return jnp.dot(a, b, preferred_element_type=F32)


def _dot_nt(a, b):
    return lax.dot_general(a, b, (((1,), (1,)), ((), ())), preferred_element_type=F32)


def _dot_tn(a, b):
    return lax.dot_general(a, b, (((0,), (0,)), ((), ())), preferred_element_type=F32)


def _split_dot(x, m):
    hi = x.astype(BF16)
    r1 = x - hi.astype(F32)
    mid = r1.astype(BF16)
    lo = (r1 - mid.astype(F32)).astype(BF16)
    return _dot(hi, m) + _dot(mid, m) + _dot(lo, m)


def _split_dot_left(m, x):
    hi = x.astype(BF16)
    r1 = x - hi.astype(F32)
    mid = r1.astype(BF16)
    lo = (r1 - mid.astype(F32)).astype(BF16)
    return _dot(m, hi) + _dot(m, mid) + _dot(m, lo)


def _tile_mod(mods_ref, k, lat_tile):
    return jnp.where(lat_tile, mods_ref[0, k:k + 1, :], mods_ref[1, k:k + 1, :])


def _rms_rows(x):
    return x * lax.rsqrt(jnp.mean(x * x, axis=-1, keepdims=True) + EPS)


def _modulated_norm(x, nrm_ref, mods_ref, k_shift, lat_tile):
    col_scale = nrm_ref[...] * (1.0 + _tile_mod(mods_ref, k_shift + 1, lat_tile))
    return (_rms_rows(x) * col_scale + _tile_mod(mods_ref, k_shift, lat_tile)).astype(BF16)


def _ada_kernel(s_ref, w_ref, b_ref, o_ref):
    o_ref[0] = _dot(_silu(s_ref[...]).astype(BF16), w_ref[0].astype(BF16)) + b_ref[0]


def _ada_mods(c, c_ctx, w_ada, b_ada):
    depth = w_ada.shape[0]
    width = w_ada.shape[2]
    tn = ADA_TILE
    s = jnp.zeros((8, D_MODEL), F32).at[0].set(c[0]).at[1].set(c_ctx)
    out = pl.pallas_call(
        _ada_kernel,
        grid=(depth, width // tn),
        in_specs=[
            pl.BlockSpec((8, D_MODEL), lambda l, j: (0, 0)),
            pl.BlockSpec((1, D_MODEL, tn), lambda l, j: (l, 0, j)),
            pl.BlockSpec((1, 1, tn), lambda l, j: (l, 0, j)),
        ],
        out_specs=pl.BlockSpec((1, 8, tn), lambda l, j: (l, 0, j)),
        out_shape=jax.ShapeDtypeStruct((depth, 8, width), F32),
        compiler_params=_cparams(("parallel", "parallel")),
        name="ada_mods",
    )(s, w_ada, b_ada.reshape(depth, 1, width))
    return out[:, :2].reshape(depth, 2, N_MOD, D_MODEL)


def _ffn_math(x, lat_tile, mods_ref, nrm_ref, wg_ref, wu_ref, wd_ref, k0):
    z = _modulated_norm(x, nrm_ref, mods_ref, k0, lat_tile)
    g = _dot(z, wg_ref[...])
    u = _dot(z, wu_ref[...])
    a = (_silu(g) * u).astype(BF16)
    return x + (0.5 * _tile_mod(mods_ref, k0 + 2, lat_tile)) * _dot(a, wd_ref[...])


def _ffn_kernel(*refs, t_lat, tm, k0, split_input):
    if split_input:
        xl_ref, xc_ref, mods_ref, nrm_ref, wg_ref, wu_ref, wd_ref, o_ref = refs
    else:
        x_ref, mods_ref, nrm_ref, wg_ref, wu_ref, wd_ref, o_ref = refs
    lat_tile = pl.program_id(0) * tm < t_lat
    x = jnp.where(lat_tile, xl_ref[...], xc_ref[...]) if split_input else x_ref[...]
    o_ref[...] = _ffn_math(x, lat_tile, mods_ref, nrm_ref, wg_ref, wu_ref, wd_ref, k0)


def _ffn_half(x_all, mods, nrm, wg, wu, wd, *, layer, t_lat, k0, n_rows, x_ctx=None):
    split_input = x_ctx is not None
    d = x_all.shape[1]
    n = x_all.shape[0] + (x_ctx.shape[0] if split_input else 0)
    tm = TOKEN_TILE
    if split_input:
        n_lat_tiles = t_lat // tm
        x_specs = [pl.BlockSpec((tm, d), lambda i: (jnp.minimum(i, n_lat_tiles - 1), 0)),
                   pl.BlockSpec((tm, d), lambda i: (jnp.maximum(i - n_lat_tiles, 0), 0))]
        x_args = [x_all, x_ctx]
    else:
        x_specs = [pl.BlockSpec((tm, d), lambda i: (i, 0))]
        x_args = [x_all]
    return pl.pallas_call(
        functools.partial(_ffn_kernel, t_lat=t_lat, tm=tm, k0=k0, split_input=split_input),
        grid=(n_rows // tm,),
        in_specs=x_specs + [
            _const_spec((2, N_MOD, d)),
            _const_spec((1, d)),
            _layer_spec(wg.shape, layer),
            _layer_spec(wu.shape, layer),
            _layer_spec(wd.shape, layer),
        ],
        out_specs=pl.BlockSpec((tm, d), lambda i: (i, 0)),
        out_shape=jax.ShapeDtypeStruct((n, d), F32),
        input_output_aliases={} if split_input else {0: 0},
        compiler_params=_cparams(("parallel",)),
        name="ffn_half",
    )(*x_args, mods, nrm, wg, wu, wd)


def _head_norm(y, gain, segm):
    ms = _split_dot(y * y, segm)
    return y * lax.rsqrt(ms + EPS) * gain


def _rope(y, cos, sin_signed):
    w = y.shape[1]
    lane = lax.broadcasted_iota(jnp.int32, y.shape, 1)
    partner = jnp.where((lane & 31) < 16, pltpu.roll(y, w - 16, 1), pltpu.roll(y, 16, 1))
    return y * cos + partner * sin_signed


def _inproj_kernel(x_ref, mods_ref, nrm_ref, w_ref, cos_ref, sin_ref, gq_ref, gk_ref, segm_ref,
                   aq_ref, ak_ref, av_ref, bq_ref, bff_ref, bfb_ref, bi_ref, bg_ref, cx_ref,
                   dq_ref, dk_ref, dv_ref, gt_ref, *, t_lat, tm):
    z = _modulated_norm(x_ref[...], nrm_ref, mods_ref, 3, pl.program_id(0) * tm < t_lat)

    def proj(seg):
        return _dot(z, w_ref[:, seg[0]:seg[1]])

    cos = cos_ref[...]
    sin = sin_ref[...]
    q = _rope(_head_norm(proj(SEG_AQ), gq_ref[...], segm_ref[...]),
              jnp.concatenate([cos, cos], axis=1), jnp.concatenate([sin, sin], axis=1))
    aq_ref[...] = (q * (ATTN_SCALE * LOG2_E)).T.astype(BF16)
    k = _rope(_head_norm(proj(SEG_AK), gk_ref[...], segm_ref[0:128, 0:128]), cos, sin)
    zeros = jnp.zeros((tm, HEAD_DIM), F32)
    ak_ref[...] = jnp.concatenate([k[:, :HEAD_DIM], zeros, k[:, HEAD_DIM:], zeros], axis=1).astype(BF16)
    v = proj(SEG_AV)
    one_hot = jnp.where(lax.broadcasted_iota(jnp.int32, (tm, HEAD_DIM), 1) == 0, 1.0, 0.0)
    av_ref[...] = jnp.concatenate([v[:, :HEAD_DIM], one_hot, v[:, HEAD_DIM:], one_hot], axis=1).T.astype(BF16)

    bq_ref[...] = (proj(SEG_BQ) * (B_KEY_DIM ** -0.5)).astype(BF16)
    bff_ref[...] = proj(SEG_BFF)
    bfb_ref[...] = proj(SEG_BFB)
    bi_ref[...] = proj(SEG_BI).astype(BF16)
    bg_ref[...] = proj(SEG_BG).astype(BF16)
    cx_ref[...] = proj(SEG_CX)
    dq_ref[...] = (proj(SEG_DQ) * ATTN_SCALE).astype(BF16)
    dk_ref[...] = proj(SEG_DK).astype(BF16)
    dv = proj(SEG_DV)
    dv_ref[...] = jnp.concatenate(
        [piece for h in range(D_HEADS) for piece in (dv[:, HEAD_DIM * h:HEAD_DIM * (h + 1)], one_hot)],
        axis=1).astype(BF16)
    gw = 512
    for c0 in range(SEG_GATE[0], SEG_GATE[1], gw):
        o0 = c0 - SEG_GATE[0]
        gt_ref[:, o0:o0 + gw] = _sigmoid(proj((c0, c0 + gw))).astype(BF16)


def _in_proj(x_all, mods, nrm, w_in, cos, sin, gq, gk, segm, *, layer, t_lat):
    n, d = x_all.shape
    tm = TOKEN_TILE
    widths = [(256, BF16), (256, BF16), (256, BF16), (256, BF16), (256, F32), (256, F32), (256, BF16),
              (256, BF16), (256, F32), (256, BF16), (256, BF16), (2 * D_HEADS * HEAD_DIM, BF16),
              (N_BRANCH * D_MODEL, BF16)]
    transposed = (0, 2)
    out_specs = [pl.BlockSpec((w, tm), lambda i: (0, i)) if o in transposed else pl.BlockSpec((tm, w), lambda i: (i, 0))
                 for o, (w, _) in enumerate(widths)]
    out_shape = [jax.ShapeDtypeStruct((w, n) if o in transposed else (n, w), dt) for o, (w, dt) in enumerate(widths)]
    return pl.pallas_call(
        functools.partial(_inproj_kernel, t_lat=t_lat, tm=tm),
        grid=(n // tm,),
        in_specs=[
            pl.BlockSpec((tm, d), lambda i: (i, 0)),
            _const_spec((2, N_MOD, d)),
            _const_spec((1, d)),
            _layer_spec(w_in.shape, layer),
            pl.BlockSpec((tm, 128), lambda i: (i, 0)),
            pl.BlockSpec((tm, 128), lambda i: (i, 0)),
            _const_spec((1, 256)),
            _const_spec((1, 128)),
            _const_spec((256, 256)),
        ],
        out_specs=out_specs,
        out_shape=out_shape,
        compiler_params=_cparams(("parallel",)),
        name="in_proj",
    )(x_all, mods, nrm, w_in, cos, sin, gq, gk, segm)


def _softmax_attend(q, k, v, exp_fn):
    s = _dot_nt(q, k)
    p = exp_fn(s - jnp.max(s, axis=-1, keepdims=True))
    return _dot(p.astype(BF16), v) / jnp.sum(p, axis=-1, keepdims=True)


def _ctx_attn_kernel(aqt_ref, ak_ref, avt_ref, dq_ref, dk_ref, dv_ref, ya_ref, yd_ref):
    outs = []
    for h in range(A_HEADS):
        g = h // (A_HEADS // A_KV_HEADS)
        st = _dot(ak_ref[:, 128 * g:128 * g + 64], aqt_ref[64 * h:64 * h + 64, :])
        pt = jnp.exp2(st - jnp.max(st, axis=0, keepdims=True)).astype(BF16)
        acc = _dot(avt_ref[128 * g:128 * g + 128, :], pt)
        outs.append((acc[:HEAD_DIM] / acc[HEAD_DIM:HEAD_DIM + 1]).T)
    ya_ref[...] = jnp.concatenate(outs, axis=1).astype(BF16)
    outs = []
    for h in range(D_HEADS):
        hs = slice(64 * h, 64 * h + 64)
        outs.append(_softmax_attend(dq_ref[:, hs], dk_ref[:, hs], dv_ref[:, 128 * h:128 * h + 64], jnp.exp))
    yd_ref[...] = jnp.concatenate(outs, axis=1).astype(BF16)


def _ctx_attn(aqt, ak, avt, dq, dk, dv, *, t_lat):
    n = ak.shape[0]
    n_ctx = n - t_lat
    blk = t_lat // n_ctx

    def spec(w):
        return pl.BlockSpec((n_ctx, w), lambda i: (blk, 0))

    def spec_t(w):
        return pl.BlockSpec((w, n_ctx), lambda i: (0, blk))

    return pl.pallas_call(
        _ctx_attn_kernel,
        grid=(1,),
        in_specs=[spec_t(256), spec(256), spec_t(256), spec(256), spec(256), spec(dv.shape[1])],
        out_specs=[spec(256), spec(256)],
        out_shape=[jax.ShapeDtypeStruct((n, BRANCH_WIDTH), BF16)] * 2,
        compiler_params=_cparams(("arbitrary",)),
        name="ctx_attn",
    )(aqt, ak, avt, dq, dk, dv)


def _flash_kernel(qt_ref, k_ref, vt_ref, prev_ref, o_ref, m_ref, acc_ref, *bufs, tk, tq):
    del prev_ref
    s_refs, p_refs = bufs[:FLASH_NBUF], bufs[FLASH_NBUF:]
    j = pl.program_id(1)

    @pl.when(j == 0)
    def _():
        m_ref[...] = jnp.full(m_ref.shape, -jnp.inf, F32)
        acc_ref[...] = jnp.zeros(acc_ref.shape, F32)

    group = A_HEADS // A_KV_HEADS
    n_qc = tq // FLASH_QC
    n_sb = tk // FLASH_SB
    chains = [(h, c) for h in range(A_HEADS) for c in range(n_qc)]

    def score_block(ci, sb):
        h, c = chains[ci]
        g = h // group
        rows = slice(sb * FLASH_SB, (sb + 1) * FLASH_SB)
        st = _dot(k_ref[rows, 128 * g:128 * g + 64], qt_ref[64 * h:64 * h + 64, c * FLASH_QC:(c + 1) * FLASH_QC])
        s_refs[ci % FLASH_NBUF][rows, :] = st
        return jnp.max(st, axis=0, keepdims=True)

    def exp_block(ci, sb, m_new):
        rows = slice(sb * FLASH_SB, (sb + 1) * FLASH_SB)
        p_refs[ci % 2][rows, :] = jnp.exp2(s_refs[ci % FLASH_NBUF][rows, :] - m_new).astype(BF16)

    def value_block(ci, sb):
        g = chains[ci][0] // group
        rows = slice(sb * FLASH_SB, (sb + 1) * FLASH_SB)
        return _dot(vt_ref[128 * g:128 * g + 128, rows], p_refs[ci % 2][rows, :])

    def fold(a, b):
        return b if a is None else a + b

    def fold_max(a, b):
        return b if a is None else jnp.maximum(a, b)

    n_ch = len(chains)
    blk_max, m_new, alpha = {}, {}, {}
    for stage in range(-1, n_ch + 1):
        c_s, c_e, c_v = stage + 1, stage, stage - 1
        if 0 <= c_e < n_ch:
            h, c = chains[c_e]
            cols = slice(c * FLASH_QC, (c + 1) * FLASH_QC)
            m_old = m_ref[h, :, cols]
            m_new[c_e] = jnp.maximum(m_old, blk_max.pop(c_e))
            m_ref[h, :, cols] = m_new[c_e]
            alpha[c_e] = jnp.exp2(m_old - m_new[c_e])
        part, mx = None, None
        for sb in range(n_sb):
            if c_s < n_ch:
                mx = fold_max(mx, score_block(c_s, sb))
            if 0 <= c_e < n_ch:
                exp_block(c_e, sb, m_new[c_e])
            if 0 <= c_v:
                part = fold(part, value_block(c_v, sb))
        if c_s < n_ch:
            blk_max[c_s] = mx
        if 0 <= c_v:
            h, c = chains[c_v]
            cols = slice(c * FLASH_QC, (c + 1) * FLASH_QC)
            acc_ref[h, :, cols] = alpha.pop(c_v) * acc_ref[h, :, cols] + part

    @pl.when(j == pl.num_programs(1) - 1)
    def _():
        outs = []
        for h in range(A_HEADS):
            acc = acc_ref[h]
            outs.append((acc[:HEAD_DIM] / acc[HEAD_DIM:HEAD_DIM + 1]).T)
        o_ref[...] = jnp.concatenate(outs, axis=1).astype(BF16)


def _flash_bounded_kernel(bound_ref, qt_ref, k_ref, vt_ref, prev_ref, o_ref, acc_ref, *p_refs, tk, tq):
    del prev_ref
    j = pl.program_id(1)

    @pl.when(j == 0)
    def _():
        acc_ref[...] = jnp.zeros(acc_ref.shape, F32)

    bound = bound_ref[0]
    group = A_HEADS // A_KV_HEADS
    n_qc = tq // FLASH_QC
    n_sb = tk // FLASH_SB
    chains = [(h, c) for h in range(A_HEADS) for c in range(n_qc)]

    def prob_block(ci, sb):
        h, c = chains[ci]
        g = h // group
        rows = slice(sb * FLASH_SB, (sb + 1) * FLASH_SB)
        st = _dot(k_ref[rows, 128 * g:128 * g + 64], qt_ref[64 * h:64 * h + 64, c * FLASH_QC:(c + 1) * FLASH_QC])
        p_refs[ci % 2][rows, :] = jnp.exp2(st - bound).astype(BF16)

    def value_block(ci, sb):
        g = chains[ci][0] // group
        rows = slice(sb * FLASH_SB, (sb + 1) * FLASH_SB)
        return _dot(vt_ref[128 * g:128 * g + 128, rows], p_refs[ci % 2][rows, :])

    n_ch = len(chains)
    for stage in range(n_ch + 1):
        part = None
        for sb in range(n_sb):
            if stage < n_ch:
                prob_block(stage, sb)
            if stage >= 1:
                blk = value_block(stage - 1, sb)
                part = blk if part is None else part + blk
        if stage >= 1:
            h, c = chains[stage - 1]
            cols = slice(c * FLASH_QC, (c + 1) * FLASH_QC)
            acc_ref[h, :, cols] = acc_ref[h, :, cols] + part

    @pl.when(j == pl.num_programs(1) - 1)
    def _():
        outs = []
        for h in range(A_HEADS):
            acc = acc_ref[h]
            outs.append((acc[:HEAD_DIM] / acc[HEAD_DIM:HEAD_DIM + 1]).T)
        o_ref[...] = jnp.concatenate(outs, axis=1).astype(BF16)


def _flash_tk(n):
    for tk in (1280, 768, 512, 256):
        if n % tk == 0:
            return tk
    raise ValueError(f"unsupported key count {n}")


def _gqa_latent(aqt, ak, avt, ya_prev, score_bound, *, t_lat):
    n = ak.shape[0]
    tk = _flash_tk(n)
    tq = min(FLASH_TQ, t_lat)

    def call(body, extra_specs, scratch, name):
        return pl.pallas_call(
            functools.partial(body, tk=tk, tq=tq),
            grid=(t_lat // tq, n // tk),
            in_specs=extra_specs + [
                pl.BlockSpec((256, tq), lambda i, j: (0, i)),
                pl.BlockSpec((tk, 256), lambda i, j: (j, 0)),
                pl.BlockSpec((256, tk), lambda i, j: (0, j)),
                pl.BlockSpec(memory_space=pl.ANY),
            ],
            out_specs=pl.BlockSpec((tq, 256), lambda i, j: (i, 0)),
            out_shape=jax.ShapeDtypeStruct((n, BRANCH_WIDTH), BF16),
            scratch_shapes=scratch,
            input_output_aliases={len(extra_specs) + 3: 0},
            compiler_params=_cparams(("parallel", "arbitrary")),
            name=name,
        )

    def online(_):
        scratch = ([pltpu.VMEM((A_HEADS, 1, tq), F32), pltpu.VMEM((A_HEADS, 128, tq), F32)]
                   + [pltpu.VMEM((tk, FLASH_QC), F32)] * FLASH_NBUF + [pltpu.VMEM((tk, FLASH_QC), BF16)] * 2)
        return call(_flash_kernel, [], scratch, "gqa_flash")(aqt, ak, avt, ya_prev)

    def bounded(_):
        scratch = [pltpu.VMEM((A_HEADS, 128, tq), F32)] + [pltpu.VMEM((tk, FLASH_QC), BF16)] * 2
        return call(_flash_bounded_kernel, [pl.BlockSpec(memory_space=pltpu.SMEM)], scratch,
                    "gqa_flash_bounded")(score_bound.reshape(1), aqt, ak, avt, ya_prev)

    return lax.cond(2.0 * score_bound <= FLASH_MAX_EXPONENT_SPAN, bounded, online, None)


def _na_kernel(q_ref, kp_ref, km_ref, kn_ref, vp_ref, vm_ref, vn_ref, kc_ref, vc_ref, bias_ref, mask_ref, prev_ref,
               o_ref, bm_ref, *, nb):
    del prev_ref
    b = pl.program_id(0)

    @pl.when((b == 0) | (b == 1) | (b == nb - 1))
    def _():
        for h in range(D_HEADS):
            bm_ref[h] = bias_ref[h].astype(F32) + mask_ref[0]

    def scores(h):
        hs = slice(64 * h, 64 * h + 64)
        q = q_ref[:, hs]
        kcat = jnp.concatenate([kp_ref[:, hs], km_ref[:, hs], kn_ref[:, hs]], axis=0)
        return _dot_nt(q, kcat) + bm_ref[h], _dot_nt(q, kc_ref[:, hs])

    def softmax(s_loc, s_ctx):
        m = jnp.maximum(jnp.max(s_loc, axis=-1, keepdims=True), jnp.max(s_ctx, axis=-1, keepdims=True))
        return jnp.exp(s_loc - m).astype(BF16), jnp.exp(s_ctx - m).astype(BF16)

    def attend(h, p_loc, p_ctx):
        vs = slice(128 * h, 128 * h + 128)
        vcat = jnp.concatenate([vp_ref[:, vs], vm_ref[:, vs], vn_ref[:, vs]], axis=0)
        o = _dot(p_loc, vcat) + _dot(p_ctx, vc_ref[:, vs])
        return o[:, :HEAD_DIM] / o[:, HEAD_DIM:HEAD_DIM + 1]

    s = {0: scores(0)}
    p, outs = {}, []
    for h in range(D_HEADS):
        if h + 1 < D_HEADS:
            s[h + 1] = scores(h + 1)
        p[h] = softmax(*s.pop(h))
        if h >= 1:
            outs.append(attend(h - 1, *p.pop(h - 1)))
    outs.append(attend(D_HEADS - 1, *p.pop(D_HEADS - 1)))
    o_ref[...] = jnp.concatenate(outs, axis=1).astype(BF16)


def _na_bias_tables(rel_bias, rows):
    wr = min(NA_WIN_R, rows)
    halo_rows = NA_HALO // GRID_W
    krows = NA_QROWS + 2 * halo_rows
    nb = rows // NA_QROWS
    qc = np.arange(GRID_W)[:, None]
    kc = np.arange(GRID_W)[None, :]
    cs = np.clip(qc - NA_WIN_C // 2, 0, GRID_W - NA_WIN_C)
    in_col = (kc >= cs) & (kc < cs + NA_WIN_C)
    e_col = (kc - qc + (NA_WIN_C - 1))[:, :, None] == np.arange(2 * NA_WIN_C - 1)
    qr_l = np.arange(NA_QROWS)[:, None]
    kr_l = np.arange(krows)[None, :]
    e_row = (kr_l - halo_rows - qr_l + (NA_WIN_R - 1))[:, :, None] == np.arange(2 * NA_WIN_R - 1)
    hi = lax.Precision.HIGHEST
    tmp = jnp.einsum("lhrc,qkr->lhqkc", rel_bias.astype(F32), e_row.astype(np.float32), precision=hi)
    full = jnp.einsum("lhqkc,pjc->lhqpkj", tmp, e_col.astype(np.float32), precision=hi)
    masks = []
    for b in (0, min(1, nb - 1), nb - 1):
        qr = NA_QROWS * b + qr_l
        kr = NA_QROWS * b - halo_rows + kr_l
        rs = np.clip(qr - wr // 2, 0, rows - wr)
        in_row = (kr >= rs) & (kr < rs + wr)
        mask = in_row[:, None, :, None] & in_col[None, :, None, :]
        masks.append(np.where(mask, 0.0, NEG_BIG).reshape(NA_QB, krows * GRID_W))
    return full.astype(BF16).reshape(-1, D_HEADS, NA_QB, krows * GRID_W), np.stack(masks).astype(np.float32)


def _na_latent(dq, dk, dv, bias, masks, yd_prev, *, layer, t_lat):
    n = dq.shape[0]
    nb = t_lat // NA_QB
    r = NA_QB // NA_HALO
    last_halo = t_lat // NA_HALO - 1
    ctx_blk = t_lat // (n - t_lat)
    n_ctx = n - t_lat

    def prev_map(b):
        return (jnp.maximum(r * b - 1, 0), 0)

    def next_map(b):
        return (jnp.minimum(r * b + r, last_halo), 0)

    def variant(b):
        return (jnp.where(b == 0, 0, jnp.where(b == nb - 1, 2, 1)), 0, 0)

    def specs(w):
        return (pl.BlockSpec((NA_QB, w), lambda b: (b, 0)), pl.BlockSpec((NA_HALO, w), prev_map),
                pl.BlockSpec((NA_HALO, w), next_map), pl.BlockSpec((n_ctx, w), lambda b: (ctx_blk, 0)))

    main, prev, nxt, ctx = specs(256)
    vmain, vprev, vnxt, vctx = specs(dv.shape[1])
    return pl.pallas_call(
        functools.partial(_na_kernel, nb=nb),
        grid=(nb,),
        in_specs=[main, prev, main, nxt, vprev, vmain, vnxt, ctx, vctx,
                  _layer_spec(bias.shape, layer),
                  pl.BlockSpec((1, NA_QB, NA_QB + 2 * NA_HALO), variant),
                  pl.BlockSpec(memory_space=pl.ANY)],
        out_specs=main,
        out_shape=jax.ShapeDtypeStruct((n, BRANCH_WIDTH), BF16),
        scratch_shapes=[pltpu.VMEM((D_HEADS, NA_QB, NA_QB + 2 * NA_HALO), F32)],
        input_output_aliases={11: 0},
        compiler_params=_cparams(("arbitrary",)),
        name="na_attn",
    )(dq, dk, dk, dk, dv, dv, dv, dk, dv, bias, masks, yd_prev)


def _pool_kernel(x_ref, xp_ref, xn_ref, w_ref, sc_ref, *rest, seg_lo, seg_hi, tm):
    o_ref, cat_ref, a_ref, b_ref = rest[-4:]
    start = seg_lo + pl.program_id(0) * tm
    x = x_ref[...]
    cat_ref[0:8, :] = jnp.where(start > seg_lo, xp_ref[...], 0.0)
    cat_ref[8:8 + tm, :] = x
    cat_ref[8 + tm:16 + tm, :] = jnp.where(start + tm < seg_hi, xn_ref[...], 0.0)
    cat_ref[16 + tm:32 + tm, :] = jnp.zeros((16, BRANCH_WIDTH), F32)
    a_ref[0:tm + 24, :] = cat_ref[0:tm + 24, :] + cat_ref[1:tm + 25, :]
    s2 = a_ref[7:7 + tm, :]
    b_ref[0:tm + 16, :] = a_ref[0:tm + 16, :] + a_ref[2:tm + 18, :]
    s4 = b_ref[6:6 + tm, :]
    a_ref[0:tm + 8, :] = b_ref[0:tm + 8, :] + b_ref[4:tm + 12, :]
    s8 = a_ref[4:4 + tm, :]
    s16 = a_ref[0:tm, :] + a_ref[8:8 + tm, :]

    pos = start - seg_lo + lax.broadcasted_iota(jnp.int32, (tm, 1), 0)
    seg_len = seg_hi - seg_lo

    def mean(sm, w):
        lo = jnp.clip(pos - w // 2, 0, seg_len)
        hi = jnp.clip(pos - w // 2 + w, 0, seg_len)
        return sm / (hi - lo).astype(F32)

    lane = lax.broadcasted_iota(jnp.int32, (tm, BRANCH_WIDTH), 1)
    gw = BRANCH_WIDTH // len(C_WINDOWS)
    pooled = jnp.where(lane < gw, mean(s2, 2),
                       jnp.where(lane < 2 * gw, mean(s4, 4),
                                 jnp.where(lane < 3 * gw, mean(s8, 8), mean(s16, 16)))) - x
    o_ref[...] = (_dot(pooled.astype(BF16), w_ref[...]) * sc_ref[...]).astype(BF16)


def _pool_segment(cx, w_bd, scale, prev_out, *, seg_lo, seg_hi, tm):
    n = cx.shape[0]
    r = tm // 8
    b0 = seg_lo // tm
    last8 = n // 8 - 1
    in_specs = [
        pl.BlockSpec((tm, 256), lambda i: (b0 + i, 0)),
        pl.BlockSpec((8, 256), lambda i: (jnp.maximum(r * (b0 + i) - 1, 0), 0)),
        pl.BlockSpec((8, 256), lambda i: (jnp.minimum(r * (b0 + i) + r, last8), 0)),
        _const_spec((256, 256)),
        _const_spec((1, 256)),
    ]
    args = [cx, cx, cx, w_bd, scale]
    aliases = {}
    if prev_out is not None:
        in_specs.append(pl.BlockSpec(memory_space=pl.ANY))
        args.append(prev_out)
        aliases = {5: 0}
    return pl.pallas_call(
        functools.partial(_pool_kernel, seg_lo=seg_lo, seg_hi=seg_hi, tm=tm),
        grid=((seg_hi - seg_lo) // tm,),
        in_specs=in_specs,
        out_specs=pl.BlockSpec((tm, 256), lambda i: (b0 + i, 0)),
        out_shape=jax.ShapeDtypeStruct((n, BRANCH_WIDTH), BF16),
        scratch_shapes=[pltpu.VMEM((tm + 32, 256), F32)] * 3,
        input_output_aliases=aliases,
        compiler_params=_cparams(("parallel",)),
        name="pool_mixer",
    )(*args)


def _pool_mixer(cx, w_bd, scale, *, t_lat):
    n = cx.shape[0]
    n_ctx = n - t_lat
    out = _pool_segment(cx, w_bd, scale, None, seg_lo=t_lat, seg_hi=n, tm=n_ctx)
    return _pool_segment(cx, w_bd, scale, out, seg_lo=0, seg_hi=t_lat, tm=min(POOL_TILE, t_lat))


def _hgrn_tables(rev):
    c_len, s, w = HG_CHUNK, HG_SUB, B_HEADS * B_KEY_DIM
    t = np.arange(c_len)
    tri = (t[None, :] >= t[:, None]) if rev else (t[None, :] <= t[:, None])
    pos = np.arange(s)
    keep = (pos[None, :] <= pos[:, None]) if rev else (pos[None, :] >= pos[:, None])
    keep_add = np.where(keep, 0.0, NEG_BIG)[:, :, None] * np.ones((1, 1, w))
    head_of_lane = np.arange(w) // B_KEY_DIM
    col = np.arange(B_HEADS * s)
    sel = (col[None, None, :] == (head_of_lane[None, :, None] * s + pos[:, None, None]))
    bd = (col[:, None] // s) == head_of_lane[None, :]
    hh = head_of_lane[:, None] == head_of_lane[None, :]
    return (jnp.asarray(tri, BF16), jnp.asarray(keep_add, F32), jnp.asarray(sel, BF16),
            jnp.asarray(bd, F32), jnp.asarray(hh, F32))


def _hgrn_kernel(qf_ref, ff_ref, vf_ref, qr_ref, fr_ref, vr_ref, lbf_ref, lbr_ref, trif_ref, keepf_ref, trir_ref,
                 keepr_ref, sel_ref, bd_ref, hh_ref, of_ref, or_ref, stf_ref, str_ref):
    @pl.when(pl.program_id(0) == 0)
    def _():
        stf_ref[...] = jnp.zeros(stf_ref.shape, F32)
        str_ref[...] = jnp.zeros(str_ref.shape, F32)

    st_f, st_r = stf_ref[...], str_ref[...]
    n_chunks = HG_BLOCK // HG_CHUNK
    for step in range(n_chunks):
        rows = slice(step * HG_CHUNK, (step + 1) * HG_CHUNK)
        o, st_f = _hgrn_chunk(qf_ref[rows, :], ff_ref[rows, :], vf_ref[rows, :], st_f, lbf_ref, trif_ref,
                              keepf_ref, sel_ref, bd_ref, hh_ref, rev=False)
        of_ref[rows, :] = o
        ch = n_chunks - 1 - step
        rows = slice(ch * HG_CHUNK, (ch + 1) * HG_CHUNK)
        o, st_r = _hgrn_chunk(qr_ref[rows, :], fr_ref[rows, :], vr_ref[rows, :], st_r, lbr_ref, trir_ref,
                              keepr_ref, sel_ref, bd_ref, hh_ref, rev=True)
        or_ref[rows, :] = o
    stf_ref[...] = st_f
    str_ref[...] = st_r


def _hgrn_chunk(q_bf, f_pre, v, st, lb_ref, tri_ref, keep_ref, sel_ref, bd_ref, hh_ref, *, rev):
    c_len = HG_CHUNK
    n_sub = c_len // HG_SUB
    w = B_HEADS * B_KEY_DIM
    lb = lb_ref[...]
    f = lb + (1.0 - lb) * _sigmoid(f_pre)
    k = 1.0 - f
    lf = jnp.log(f)
    q = q_bf.astype(F32)
    bd = bd_ref[...]

    c = _split_dot_left(tri_ref[...], lf)

    q3 = q.reshape(n_sub, HG_SUB, w)
    k3 = k.reshape(n_sub, HG_SUB, w)
    c3 = c.reshape(n_sub, HG_SUB, w)
    a_diag = None
    for sg in range(HG_SUB):
        ks = jnp.broadcast_to(k3[:, sg:sg + 1, :], k3.shape)
        cs = jnp.broadcast_to(c3[:, sg:sg + 1, :], c3.shape)
        wgt = (q3 * ks) * jnp.exp((c3 - cs) + keep_ref[sg][None])
        part = _dot(wgt.reshape(c_len, w).astype(BF16), sel_ref[sg])
        a_diag = part if a_diag is None else a_diag + part

    o_parts = [None] * n_sub
    for j in range(n_sub):
        r0 = j * HG_SUB
        r_last = r0 if rev else r0 + HG_SUB - 1
        e_j = c[r_last:r_last + 1, :]
        kj = k[r0:r0 + HG_SUB, :] * jnp.exp(e_j - c[r0:r0 + HG_SUB, :])
        kbd = (jnp.concatenate([kj] * B_HEADS, axis=0) * bd).astype(BF16)
        vbd = (jnp.concatenate([v[r0:r0 + HG_SUB, :].astype(F32)] * B_HEADS, axis=0) * bd).astype(BF16)
        lo, hi = (0, r0) if rev else (r0 + HG_SUB, c_len)
        pieces = [a_diag[r0:r0 + HG_SUB]]
        if hi > lo:
            qj = (q[lo:hi] * jnp.exp(c[lo:hi] - e_j)).astype(BF16)
            a_off = _dot_nt(qj, kbd)
            pieces = [a_off] + pieces if rev else pieces + [a_off]
        first = 0 if rev else j
        if (sum(p.shape[0] for p in pieces) % 16) != 0:
            pad = jnp.zeros((HG_SUB, a_diag.shape[1]), F32)
            pieces = pieces + [pad] if rev else [pad] + pieces
            first = first if rev else first - 1
        a_j = jnp.concatenate(pieces, axis=0).astype(BF16)
        contrib = _dot(a_j, vbd)
        for i in range(contrib.shape[0] // HG_SUB):
            piece = contrib[i * HG_SUB:(i + 1) * HG_SUB]
            o_parts[first + i] = piece if o_parts[first + i] is None else o_parts[first + i] + piece
    o = jnp.concatenate(o_parts, axis=0)

    o = o + _dot_nt((q * jnp.exp(c)).astype(BF16), st.astype(BF16))
    r_end = 0 if rev else c_len - 1
    c_end = c[r_end:r_end + 1, :]
    k_end = (k * jnp.exp(c_end - c)).astype(BF16)
    return o, jnp.exp(c_end) * st + _dot_tn(v, k_end) * hh_ref[...]


def _hgrn_scans(bq, f_fwd, f_rev, bi, lb_fwd, lb_rev, *, t_lat):
    n = bq.shape[0]
    c_len = HG_BLOCK
    n_lat = t_lat // c_len
    n_all = n // c_len
    n_ctx = n_all - n_lat

    def blk_fwd(i):
        return (jnp.where(i < n_ctx, n_lat + i, i - n_ctx), 0)

    def blk_rev(i):
        return (jnp.where(i < n_ctx, n_all - 1 - i, n_lat - 1 - (i - n_ctx)), 0)

    tile_f = pl.BlockSpec((c_len, 256), blk_fwd)
    tile_r = pl.BlockSpec((c_len, 256), blk_rev)
    tri_f, keep_f, sel, bd, hh = _hgrn_tables(False)
    tri_r, keep_r, _, _, _ = _hgrn_tables(True)
    consts = [lb_fwd, lb_rev, tri_f, keep_f, tri_r, keep_r, sel, bd, hh]
    return pl.pallas_call(
        _hgrn_kernel,
        grid=(n_all,),
        in_specs=[tile_f, tile_f, tile_f, tile_r, tile_r, tile_r] + [_const_spec(t.shape) for t in consts],
        out_specs=[tile_f, tile_r],
        out_shape=[jax.ShapeDtypeStruct((n, BRANCH_WIDTH), F32)] * 2,
        scratch_shapes=[pltpu.VMEM((256, 256), F32)] * 2,
        compiler_params=_cparams(("arbitrary",)),
        name="hgrn_scans",
    )(bq, f_fwd, bi, bq, f_rev, bi, *consts)


def _merge_kernel(x_ref, mods_ref, ya_ref, of_ref, or_ref, bg_ref, gain_ref, segm_ref, yc_ref, yd_ref, gt_ref,
                  wb_ref, wo_ref, nrm_ref, wg_ref, wu_ref, wd_ref, *rest, t_lat, tm):
    fin_ref = rest[0] if len(rest) == 2 else None
    o_ref = rest[-1]
    lat_tile = pl.program_id(0) * tm < t_lat
    def gated(n, y):
        return gt_ref[:, n * D_MODEL:(n + 1) * D_MODEL].astype(F32) * _dot(y, wb_ref[n])

    term_a, term_c, term_d = gated(0, ya_ref[...]), gated(2, yc_ref[...]), gated(3, yd_ref[...])
    yb = _head_norm(of_ref[...] + or_ref[...], gain_ref[...], segm_ref[...]) * _silu(bg_ref[...].astype(F32))
    merged = ((term_a + gated(1, yb.astype(BF16))) + term_c) + term_d
    y = _dot(merged.astype(BF16), wo_ref[...])
    x_mid = x_ref[...] + _tile_mod(mods_ref, 5, lat_tile) * y
    out = _ffn_math(x_mid, lat_tile, mods_ref, nrm_ref, wg_ref, wu_ref, wd_ref, 6)
    o_ref[...] = out if fin_ref is None else _rms_rows(out) * fin_ref[...]


def _merge_ffn(x_all, mods, ya, o_fwd, o_rev, bg, b_gain, segm, yc, yd, gates, wb, wo, nrm, wg, wu, wd, *,
               layer, t_lat, n_rows, final_gain=None):
    n, d = x_all.shape
    tm = TOKEN_TILE
    final = final_gain is not None

    def tile(w):
        return pl.BlockSpec((tm, w), lambda i: (i, 0))

    return pl.pallas_call(
        functools.partial(_merge_kernel, t_lat=t_lat, tm=tm),
        grid=(n_rows // tm,),
        in_specs=[tile(d), _const_spec((2, N_MOD, d)), tile(256), tile(256), tile(256), tile(256),
                  _const_spec((1, 256)), _const_spec((256, 256)), tile(256), tile(256),
                  tile(N_BRANCH * d), _layer_spec(wb.shape, layer), _layer_spec(wo.shape, layer),
                  _const_spec((1, d)), _layer_spec(wg.shape, layer), _layer_spec(wu.shape, layer),
                  _layer_spec(wd.shape, layer)] + ([_const_spec((1, d))] if final else []),
        out_specs=tile(d),
        out_shape=jax.ShapeDtypeStruct((n_rows if final else n, d), F32),
        input_output_aliases={} if final else {0: 0},
        compiler_params=_cparams(("parallel",)),
        name="merge_ffn",
    )(x_all, mods, ya, o_fwd, o_rev, bg, b_gain, segm, yc, yd, gates, wb, wo, nrm, wg, wu, wd,
      *([final_gain] if final else []))


def _rope_tables_padded(t_lat, n_ctx):
    rows = t_lat // GRID_W
    half = HEAD_DIM // 2
    nf = half // 2
    inv = 1.0 / (ROPE_THETA ** (jnp.arange(0, half, 2, dtype=F32) / half))
    ang_r = jnp.arange(rows, dtype=F32)[:, None] * inv
    ang_c = jnp.arange(GRID_W, dtype=F32)[:, None] * inv

    def over_rows(a):
        return jnp.broadcast_to(a[:, None, :], (rows, GRID_W, nf))

    def over_cols(a):
        return jnp.broadcast_to(a[None, :, :], (rows, GRID_W, nf))

    cos = jnp.concatenate([over_rows(jnp.cos(ang_r))] * 2 + [over_cols(jnp.cos(ang_c))] * 2, axis=2)
    sin = jnp.concatenate([over_rows(-jnp.sin(ang_r)), over_rows(jnp.sin(ang_r)),
                           over_cols(-jnp.sin(ang_c)), over_cols(jnp.sin(ang_c))], axis=2)
    cos = cos.reshape(t_lat, HEAD_DIM)
    sin = sin.reshape(t_lat, HEAD_DIM)
    cos = jnp.concatenate([cos, jnp.ones((n_ctx, HEAD_DIM), F32)], axis=0)
    sin = jnp.concatenate([sin, jnp.zeros((n_ctx, HEAD_DIM), F32)], axis=0)
    return jnp.tile(cos, (1, 2)), jnp.tile(sin, (1, 2))


def _block_diag(w_group):
    g, ci, co = w_group.shape
    out = jnp.zeros((g * ci, g * co), w_group.dtype)
    for n in range(g):
        out = out.at[n * ci:(n + 1) * ci, n * co:(n + 1) * co].set(w_group[n])
    return out


def kernel(x, c, ctx, c_ctx, w_ada, b_ada, ffn1_norm, ffn1_w_gate, ffn1_w_up, ffn1_w_down, mix_norm, w_in, a_q_norm, a_k_norm, b_lb_logits, b_o_norm, c_w_group, c_scale, d_rel_bias, w_branch, w_out, ffn2_norm, ffn2_w_gate, ffn2_w_up, ffn2_w_down, final_norm):
    assert x.shape[0] == 1 and ctx.shape[0] == 1 and w_in.shape[2] == IN_WIDTH
    depth = w_ada.shape[0]
    t_lat = x.shape[1]
    n_ctx = ctx.shape[1]
    n = t_lat + n_ctx
    rows = t_lat // GRID_W
    assert t_lat % max(NA_QB, min(FLASH_TQ, t_lat), n_ctx) == 0 and n_ctx % TOKEN_TILE == 0 and rows >= 2 * NA_QROWS

    x_all = None
    mods_all = _ada_mods(c, c_ctx, w_ada, b_ada)
    cos, sin = _rope_tables_padded(t_lat, n_ctx)
    segm = (jnp.kron(jnp.eye(BRANCH_WIDTH // HEAD_DIM), jnp.ones((HEAD_DIM, HEAD_DIM))) / HEAD_DIM).astype(BF16)
    lb_all = jnp.cumsum(jax.nn.softmax(b_lb_logits.astype(F32), axis=0), axis=0)
    lb_all = lb_all - lb_all[:1]
    wg1, wu1, wd1 = ffn1_w_gate.astype(BF16), ffn1_w_up.astype(BF16), ffn1_w_down.astype(BF16)
    wg2, wu2, wd2 = ffn2_w_gate.astype(BF16), ffn2_w_up.astype(BF16), ffn2_w_down.astype(BF16)
    w_in_b, w_branch_b, w_out_b = w_in.astype(BF16), w_branch.astype(BF16), w_out.astype(BF16)
    na_bias, na_masks = _na_bias_tables(d_rel_bias, rows)

    for l in range(depth):
        with_ctx_out = l < depth - 1
        n_rows = n if with_ctx_out else t_lat
        mods = mods_all[l]
        x_all = _ffn_half(x[0] if l == 0 else x_all, mods, ffn1_norm[l][None], wg1, wu1, wd1, layer=l, t_lat=t_lat,
                          k0=0, n_rows=n, x_ctx=ctx[0] if l == 0 else None)
        (aq, ak, av, bq, bff, bfb, bi, bg, cx, dq, dk, dv, gates) = _in_proj(
            x_all, mods, mix_norm[l][None], w_in_b, cos, sin,
            jnp.tile(a_q_norm[l], A_HEADS)[None], jnp.tile(a_k_norm[l], A_KV_HEADS)[None], segm, layer=l, t_lat=t_lat)
        if with_ctx_out:
            ya, yd = _ctx_attn(aq, ak, av, dq, dk, dv, t_lat=t_lat)
        else:
            ya = jnp.zeros((n, BRANCH_WIDTH), BF16)
            yd = ya
        score_bound = (HEAD_DIM * ATTN_SCALE * LOG2_E * FLASH_BOUND_MARGIN
                       * jnp.max(jnp.abs(a_q_norm[l])) * jnp.max(jnp.abs(a_k_norm[l]))).astype(F32)
        ya = _gqa_latent(aq, ak, av, ya, score_bound, t_lat=t_lat)
        yd = _na_latent(dq, dk, dv, na_bias, na_masks, yd, layer=l, t_lat=t_lat)
        yc = _pool_mixer(cx, _block_diag(c_w_group[l]).astype(BF16), c_scale[l][None], t_lat=t_lat)
        o_fwd, o_rev = _hgrn_scans(bq, bff, bfb, bi, lb_all[l, 0][None], lb_all[l, 1][None], t_lat=t_lat)
        x_all = _merge_ffn(x_all, mods, ya, o_fwd, o_rev, bg, jnp.tile(b_o_norm[l], B_HEADS)[None], segm, yc, yd,
                           gates, w_branch_b, w_out_b, ffn2_norm[l][None], wg2, wu2, wd2, layer=l, t_lat=t_lat,
                           n_rows=n_rows, final_gain=None if with_ctx_out else final_norm[None])
    return x_all[None]
```

```python
import functools

import numpy as np
import jax
import jax.numpy as jnp
from jax import lax
from jax.experimental import pallas as pl
from jax.experimental.pallas import tpu as pltpu

F32 = jnp.float32
BF16 = jnp.bfloat16

D_MODEL = 1024
GRID_W = 64
HEAD_DIM = 64
BRANCH_WIDTH = 256
N_BRANCH = 4
A_HEADS = 4
A_KV_HEADS = 2
ROPE_THETA = 10000.0
B_HEADS = 4
B_KEY_DIM = 64
C_WINDOWS = (2, 4, 8, 16)
D_HEADS = 4
NA_WIN_R = 8
NA_WIN_C = 16
N_MOD = 9
EPS = 1e-6
ATTN_SCALE = HEAD_DIM ** -0.5
LOG2_E = 1.4426950408889634
NEG_BIG = -1e30

V7X_VMEM_BYTES = 64 * 1024 * 1024
VMEM_LIMIT = V7X_VMEM_BYTES - 8 * 1024 * 1024

SEG_AQ = (0, 256)
SEG_AK = (256, 384)
SEG_AV = (384, 512)
SEG_BQ = (512, 768)
SEG_BFF = (768, 1024)
SEG_BFB = (1024, 1280)
SEG_BI = (1280, 1536)
SEG_BG = (1536, 1792)
SEG_CX = (1792, 2048)
SEG_DQ = (2048, 2304)
SEG_DK = (2304, 2560)
SEG_DV = (2560, 2816)
SEG_GATE = (2816, 2816 + N_BRANCH * D_MODEL)
IN_WIDTH = SEG_GATE[1]

ADA_TILE = 3072
TOKEN_TILE = 256
POOL_TILE = 2048
FLASH_TQ = 4096
FLASH_QC = 512
FLASH_SB = 256
FLASH_NBUF = 4
FLASH_MAX_EXPONENT_SPAN = 100.0
FLASH_BOUND_MARGIN = 1.02
NA_QROWS = 4
NA_QB = NA_QROWS * GRID_W
NA_HALO = 256
HG_BLOCK = 256
HG_CHUNK = 128
HG_SUB = 8


def _cparams(sem):
    return pltpu.CompilerParams(dimension_semantics=sem, vmem_limit_bytes=VMEM_LIMIT)


def _const_spec(shape):
    nd = len(shape)
    return pl.BlockSpec(shape, lambda *_: (0,) * nd, pipeline_mode=pl.Buffered(1))


def _layer_spec(shape, layer):
    nd = len(shape)
    return pl.BlockSpec((None,) + tuple(shape[1:]), lambda *_: (layer,) + (0,) * (nd - 1),
                        pipeline_mode=pl.Buffered(1))


def _sigmoid(x):
    return 1.0 / (1.0 + jnp.exp(-x))


def _silu(x):
    return x * _sigmoid(x)


def _dot(a, b):
    return jnp.dot(a, b, preferred_element_type=F32)


def _dot_nt(a, b):
    return lax.dot_general(a, b, (((1,), (1,)), ((), ())), preferred_element_type=F32)


def _dot_tn(a, b):
    return lax.dot_general(a, b, (((0,), (0,)), ((), ())), preferred_element_type=F32)


def _split_dot(x, m):
    hi = x.astype(BF16)
    r1 = x - hi.astype(F32)
    mid = r1.astype(BF16)
    lo = (r1 - mid.astype(F32)).astype(BF16)
    return _dot(hi, m) + _dot(mid, m) + _dot(lo, m)


def _split_dot_left(m, x):
    hi = x.astype(BF16)
    r1 = x - hi.astype(F32)
    mid = r1.astype(BF16)
    lo = (r1 - mid.astype(F32)).astype(BF16)
    return _dot(m, hi) + _dot(m, mid) + _dot(m, lo)


def _tile_mod(mods_ref, k, lat_tile):
    return jnp.where(lat_tile, mods_ref[0, k:k + 1, :], mods_ref[1, k:k + 1, :])


def _rms_rows(x):
    return x * lax.rsqrt(jnp.mean(x * x, axis=-1, keepdims=True) + EPS)


def _modulated_norm(x, nrm_ref, mods_ref, k_shift, lat_tile):
    col_scale = nrm_ref[...] * (1.0 + _tile_mod(mods_ref, k_shift + 1, lat_tile))
    return (_rms_rows(x) * col_scale + _tile_mod(mods_ref, k_shift, lat_tile)).astype(BF16)


def _ada_kernel(s_ref, w_ref, b_ref, o_ref):
    o_ref[0] = _dot(_silu(s_ref[...]).astype(BF16), w_ref[0].astype(BF16)) + b_ref[0]


def _ada_mods(c, c_ctx, w_ada, b_ada):
    depth = w_ada.shape[0]
    width = w_ada.shape[2]
    tn = ADA_TILE
    s = jnp.zeros((8, D_MODEL), F32).at[0].set(c[0]).at[1].set(c_ctx)
    out = pl.pallas_call(
        _ada_kernel,
        grid=(depth, width // tn),
        in_specs=[
            pl.BlockSpec((8, D_MODEL), lambda l, j: (0, 0)),
            pl.BlockSpec((1, D_MODEL, tn), lambda l, j: (l, 0, j)),
            pl.BlockSpec((1, 1, tn), lambda l, j: (l, 0, j)),
        ],
        out_specs=pl.BlockSpec((1, 8, tn), lambda l, j: (l, 0, j)),
        out_shape=jax.ShapeDtypeStruct((depth, 8, width), F32),
        compiler_params=_cparams(("parallel", "parallel")),
        name="ada_mods",
    )(s, w_ada, b_ada.reshape(depth, 1, width))
    return out[:, :2].reshape(depth, 2, N_MOD, D_MODEL)


def _ffn_math(x, lat_tile, mods_ref, nrm_ref, wg_ref, wu_ref, wd_ref, k0):
    z = _modulated_norm(x, nrm_ref, mods_ref, k0, lat_tile)
    g = _dot(z, wg_ref[...])
    u = _dot(z, wu_ref[...])
    a = (_silu(g) * u).astype(BF16)
    return x + (0.5 * _tile_mod(mods_ref, k0 + 2, lat_tile)) * _dot(a, wd_ref[...])


def _ffn_kernel(*refs, t_lat, tm, k0, split_input):
    if split_input:
        xl_ref, xc_ref, mods_ref, nrm_ref, wg_ref, wu_ref, wd_ref, o_ref = refs
    else:
        x_ref, mods_ref, nrm_ref, wg_ref, wu_ref, wd_ref, o_ref = refs
    lat_tile = pl.program_id(0) * tm < t_lat
    x = jnp.where(lat_tile, xl_ref[...], xc_ref[...]) if split_input else x_ref[...]
    o_ref[...] = _ffn_math(x, lat_tile, mods_ref, nrm_ref, wg_ref, wu_ref, wd_ref, k0)


def _ffn_half(x_all, mods, nrm, wg, wu, wd, *, layer, t_lat, k0, n_rows, x_ctx=None):
    split_input = x_ctx is not None
    d = x_all.shape[1]
    n = x_all.shape[0] + (x_ctx.shape[0] if split_input else 0)
    tm = TOKEN_TILE
    if split_input:
        n_lat_tiles = t_lat // tm
        x_specs = [pl.BlockSpec((tm, d), lambda i: (jnp.minimum(i, n_lat_tiles - 1), 0)),
                   pl.BlockSpec((tm, d), lambda i: (jnp.maximum(i - n_lat_tiles, 0), 0))]
        x_args = [x_all, x_ctx]
    else:
        x_specs = [pl.BlockSpec((tm, d), lambda i: (i, 0))]
        x_args = [x_all]
    return pl.pallas_call(
        functools.partial(_ffn_kernel, t_lat=t_lat, tm=tm, k0=k0, split_input=split_input),
        grid=(n_rows // tm,),
        in_specs=x_specs + [
            _const_spec((2, N_MOD, d)),
            _const_spec((1, d)),
            _layer_spec(wg.shape, layer),
            _layer_spec(wu.shape, layer),
            _layer_spec(wd.shape, layer),
        ],
        out_specs=pl.BlockSpec((tm, d), lambda i: (i, 0)),
        out_shape=jax.ShapeDtypeStruct((n, d), F32),
        input_output_aliases={} if split_input else {0: 0},
        compiler_params=_cparams(("parallel",)),
        name="ffn_half",
    )(*x_args, mods, nrm, wg, wu, wd)


def _head_norm(y, gain, segm):
    ms = _split_dot(y * y, segm)
    return y * lax.rsqrt(ms + EPS) * gain


def _rope(y, cos, sin_signed):
    w = y.shape[1]
    lane = lax.broadcasted_iota(jnp.int32, y.shape, 1)
    partner = jnp.where((lane & 31) < 16, pltpu.roll(y, w - 16, 1), pltpu.roll(y, 16, 1))
    return y * cos + partner * sin_signed


def _inproj_kernel(x_ref, mods_ref, nrm_ref, w_ref, cos_ref, sin_ref, gq_ref, gk_ref, segm_ref,
                   aq_ref, ak_ref, av_ref, bq_ref, bff_ref, bfb_ref, bi_ref, bg_ref, cx_ref,
                   dq_ref, dk_ref, dv_ref, gt_ref, *, t_lat, tm):
    z = _modulated_norm(x_ref[...], nrm_ref, mods_ref, 3, pl.program_id(0) * tm < t_lat)

    def proj(seg):
        return _dot(z, w_ref[:, seg[0]:seg[1]])

    cos = cos_ref[...]
    sin = sin_ref[...]
    q = _rope(_head_norm(proj(SEG_AQ), gq_ref[...], segm_ref[...]),
              jnp.concatenate([cos, cos], axis=1), jnp.concatenate([sin, sin], axis=1))
    aq_ref[...] = (q * (ATTN_SCALE * LOG2_E)).T.astype(BF16)
    k = _rope(_head_norm(proj(SEG_AK), gk_ref[...], segm_ref[0:128, 0:128]), cos, sin)
    zeros = jnp.zeros((tm, HEAD_DIM), F32)
    ak_ref[...] = jnp.concatenate([k[:, :HEAD_DIM], zeros, k[:, HEAD_DIM:], zeros], axis=1).astype(BF16)
    v = proj(SEG_AV)
    one_hot = jnp.where(lax.broadcasted_iota(jnp.int32, (tm, HEAD_DIM), 1) == 0, 1.0, 0.0)
    av_ref[...] = jnp.concatenate([v[:, :HEAD_DIM], one_hot, v[:, HEAD_DIM:], one_hot], axis=1).T.astype(BF16)

    bq_ref[...] = (proj(SEG_BQ) * (B_KEY_DIM ** -0.5)).astype(BF16)
    bff_ref[...] = proj(SEG_BFF)
    bfb_ref[...] = proj(SEG_BFB)
    bi_ref[...] = proj(SEG_BI).astype(BF16)
    bg_ref[...] = proj(SEG_BG).astype(BF16)
    cx_ref[...] = proj(SEG_CX)
    dq_ref[...] = (proj(SEG_DQ) * ATTN_SCALE).astype(BF16)
    dk_ref[...] = proj(SEG_DK).astype(BF16)
    dv = proj(SEG_DV)
    dv_ref[...] = jnp.concatenate(
        [piece for h in range(D_HEADS) for piece in (dv[:, HEAD_DIM * h:HEAD_DIM * (h + 1)], one_hot)],
        axis=1).astype(BF16)
    gw = 512
    for c0 in range(SEG_GATE[0], SEG_GATE[1], gw):
        o0 = c0 - SEG_GATE[0]
        gt_ref[:, o0:o0 + gw] = _sigmoid(proj((c0, c0 + gw))).astype(BF16)


def _in_proj(x_all, mods, nrm, w_in, cos, sin, gq, gk, segm, *, layer, t_lat):
    n, d = x_all.shape
    tm = TOKEN_TILE
    widths = [(256, BF16), (256, BF16), (256, BF16), (256, BF16), (256, F32), (256, F32), (256, BF16),
              (256, BF16), (256, F32), (256, BF16), (256, BF16), (2 * D_HEADS * HEAD_DIM, BF16),
              (N_BRANCH * D_MODEL, BF16)]
    transposed = (0, 2)
    out_specs = [pl.BlockSpec((w, tm), lambda i: (0, i)) if o in transposed else pl.BlockSpec((tm, w), lambda i: (i, 0))
                 for o, (w, _) in enumerate(widths)]
    out_shape = [jax.ShapeDtypeStruct((w, n) if o in transposed else (n, w), dt) for o, (w, dt) in enumerate(widths)]
    return pl.pallas_call(
        functools.partial(_inproj_kernel, t_lat=t_lat, tm=tm),
        grid=(n // tm,),
        in_specs=[
            pl.BlockSpec((tm, d), lambda i: (i, 0)),
            _const_spec((2, N_MOD, d)),
            _const_spec((1, d)),
            _layer_spec(w_in.shape, layer),
            pl.BlockSpec((tm, 128), lambda i: (i, 0)),
            pl.BlockSpec((tm, 128), lambda i: (i, 0)),
            _const_spec((1, 256)),
            _const_spec((1, 128)),
            _const_spec((256, 256)),
        ],
        out_specs=out_specs,
        out_shape=out_shape,
        compiler_params=_cparams(("parallel",)),
        name="in_proj",
    )(x_all, mods, nrm, w_in, cos, sin, gq, gk, segm)


def _softmax_attend(q, k, v, exp_fn):
    s = _dot_nt(q, k)
    p = exp_fn(s - jnp.max(s, axis=-1, keepdims=True))
    return _dot(p.astype(BF16), v) / jnp.sum(p, axis=-1, keepdims=True)


def _ctx_attn_kernel(aqt_ref, ak_ref, avt_ref, dq_ref, dk_ref, dv_ref, ya_ref, yd_ref):
    outs = []
    for h in range(A_HEADS):
        g = h // (A_HEADS // A_KV_HEADS)
        st = _dot(ak_ref[:, 128 * g:128 * g + 64], aqt_ref[64 * h:64 * h + 64, :])
        pt = jnp.exp2(st - jnp.max(st, axis=0, keepdims=True)).astype(BF16)
        acc = _dot(avt_ref[128 * g:128 * g + 128, :], pt)
        outs.append((acc[:HEAD_DIM] / acc[HEAD_DIM:HEAD_DIM + 1]).T)
    ya_ref[...] = jnp.concatenate(outs, axis=1).astype(BF16)
    outs = []
    for h in range(D_HEADS):
        hs = slice(64 * h, 64 * h + 64)
        outs.append(_softmax_attend(dq_ref[:, hs], dk_ref[:, hs], dv_ref[:, 128 * h:128 * h + 64], jnp.exp))
    yd_ref[...] = jnp.concatenate(outs, axis=1).astype(BF16)


def _ctx_attn(aqt, ak, avt, dq, dk, dv, *, t_lat):
    n = ak.shape[0]
    n_ctx = n - t_lat
    blk = t_lat // n_ctx

    def spec(w):
        return pl.BlockSpec((n_ctx, w), lambda i: (blk, 0))

    def spec_t(w):
        return pl.BlockSpec((w, n_ctx), lambda i: (0, blk))

    return pl.pallas_call(
        _ctx_attn_kernel,
        grid=(1,),
        in_specs=[spec_t(256), spec(256), spec_t(256), spec(256), spec(256), spec(dv.shape[1])],
        out_specs=[spec(256), spec(256)],
        out_shape=[jax.ShapeDtypeStruct((n, BRANCH_WIDTH), BF16)] * 2,
        compiler_params=_cparams(("arbitrary",)),
        name="ctx_attn",
    )(aqt, ak, avt, dq, dk, dv)


def _flash_kernel(qt_ref, k_ref, vt_ref, prev_ref, o_ref, m_ref, acc_ref, *bufs, tk, tq):
    del prev_ref
    s_refs, p_refs = bufs[:FLASH_NBUF], bufs[FLASH_NBUF:]
    j = pl.program_id(1)

    @pl.when(j == 0)
    def _():
        m_ref[...] = jnp.full(m_ref.shape, -jnp.inf, F32)
        acc_ref[...] = jnp.zeros(acc_ref.shape, F32)

    group = A_HEADS // A_KV_HEADS
    n_qc = tq // FLASH_QC
    n_sb = tk // FLASH_SB
    chains = [(h, c) for h in range(A_HEADS) for c in range(n_qc)]

    def score_block(ci, sb):
        h, c = chains[ci]
        g = h // group
        rows = slice(sb * FLASH_SB, (sb + 1) * FLASH_SB)
        st = _dot(k_ref[rows, 128 * g:128 * g + 64], qt_ref[64 * h:64 * h + 64, c * FLASH_QC:(c + 1) * FLASH_QC])
        s_refs[ci % FLASH_NBUF][rows, :] = st
        return jnp.max(st, axis=0, keepdims=True)

    def exp_block(ci, sb, m_new):
        rows = slice(sb * FLASH_SB, (sb + 1) * FLASH_SB)
        p_refs[ci % 2][rows, :] = jnp.exp2(s_refs[ci % FLASH_NBUF][rows, :] - m_new).astype(BF16)

    def value_block(ci, sb):
        g = chains[ci][0] // group
        rows = slice(sb * FLASH_SB, (sb + 1) * FLASH_SB)
        return _dot(vt_ref[128 * g:128 * g + 128, rows], p_refs[ci % 2][rows, :])

    def fold(a, b):
        return b if a is None else a + b

    def fold_max(a, b):
        return b if a is None else jnp.maximum(a, b)

    n_ch = len(chains)
    blk_max, m_new, alpha = {}, {}, {}
    for stage in range(-1, n_ch + 1):
        c_s, c_e, c_v = stage + 1, stage, stage - 1
        if 0 <= c_e < n_ch:
            h, c = chains[c_e]
            cols = slice(c * FLASH_QC, (c + 1) * FLASH_QC)
            m_old = m_ref[h, :, cols]
            m_new[c_e] = jnp.maximum(m_old, blk_max.pop(c_e))
            m_ref[h, :, cols] = m_new[c_e]
            alpha[c_e] = jnp.exp2(m_old - m_new[c_e])
        part, mx = None, None
        for sb in range(n_sb):
            if c_s < n_ch:
                mx = fold_max(mx, score_block(c_s, sb))
            if 0 <= c_e < n_ch:
                exp_block(c_e, sb, m_new[c_e])
            if 0 <= c_v:
                part = fold(part, value_block(c_v, sb))
        if c_s < n_ch:
            blk_max[c_s] = mx
        if 0 <= c_v:
            h, c = chains[c_v]
            cols = slice(c * FLASH_QC, (c + 1) * FLASH_QC)
            acc_ref[h, :, cols] = alpha.pop(c_v) * acc_ref[h, :, cols] + part

    @pl.when(j == pl.num_programs(1) - 1)
    def _():
        outs = []
        for h in range(A_HEADS):
            acc = acc_ref[h]
            outs.append((acc[:HEAD_DIM] / acc[HEAD_DIM:HEAD_DIM + 1]).T)
        o_ref[...] = jnp.concatenate(outs, axis=1).astype(BF16)


def _flash_bounded_kernel(bound_ref, qt_ref, k_ref, vt_ref, prev_ref, o_ref, acc_ref, *p_refs, tk, tq):
    del prev_ref
    j = pl.program_id(1)

    @pl.when(j == 0)
    def _():
        acc_ref[...] = jnp.zeros(acc_ref.shape, F32)

    bound = bound_ref[0]
    group = A_HEADS // A_KV_HEADS
    n_qc = tq // FLASH_QC
    n_sb = tk // FLASH_SB
    chains = [(h, c) for h in range(A_HEADS) for c in range(n_qc)]

    def prob_block(ci, sb):
        h, c = chains[ci]
        g = h // group
        rows = slice(sb * FLASH_SB, (sb + 1) * FLASH_SB)
        st = _dot(k_ref[rows, 128 * g:128 * g + 64], qt_ref[64 * h:64 * h + 64, c * FLASH_QC:(c + 1) * FLASH_QC])
        p_refs[ci % 2][rows, :] = jnp.exp2(st - bound).astype(BF16)

    def value_block(ci, sb):
        g = chains[ci][0] // group
        rows = slice(sb * FLASH_SB, (sb + 1) * FLASH_SB)
        return _dot(vt_ref[128 * g:128 * g + 128, rows], p_refs[ci % 2][rows, :])

    n_ch = len(chains)
    for stage in range(n_ch + 1):
        part = None
        for sb in range(n_sb):
            if stage < n_ch:
                prob_block(stage, sb)
            if stage >= 1:
                blk = value_block(stage - 1, sb)
                part = blk if part is None else part + blk
        if stage >= 1:
            h, c = chains[stage - 1]
            cols = slice(c * FLASH_QC, (c + 1) * FLASH_QC)
            acc_ref[h, :, cols] = acc_ref[h, :, cols] + part

    @pl.when(j == pl.num_programs(1) - 1)
    def _():
        outs = []
        for h in range(A_HEADS):
            acc = acc_ref[h]
            outs.append((acc[:HEAD_DIM] / acc[HEAD_DIM:HEAD_DIM + 1]).T)
        o_ref[...] = jnp.concatenate(outs, axis=1).astype(BF16)


def _flash_tk(n):
    for tk in (1280, 768, 512, 256):
        if n % tk == 0:
            return tk
    raise ValueError(f"unsupported key count {n}")


def _gqa_latent(aqt, ak, avt, ya_prev, score_bound, *, t_lat):
    n = ak.shape[0]
    tk = _flash_tk(n)
    tq = min(FLASH_TQ, t_lat)

    def call(body, extra_specs, scratch, name):
        return pl.pallas_call(
            functools.partial(body, tk=tk, tq=tq),
            grid=(t_lat // tq, n // tk),
            in_specs=extra_specs + [
                pl.BlockSpec((256, tq), lambda i, j: (0, i)),
                pl.BlockSpec((tk, 256), lambda i, j: (j, 0)),
                pl.BlockSpec((256, tk), lambda i, j: (0, j)),
                pl.BlockSpec(memory_space=pl.ANY),
            ],
            out_specs=pl.BlockSpec((tq, 256), lambda i, j: (i, 0)),
            out_shape=jax.ShapeDtypeStruct((n, BRANCH_WIDTH), BF16),
            scratch_shapes=scratch,
            input_output_aliases={len(extra_specs) + 3: 0},
            compiler_params=_cparams(("parallel", "arbitrary")),
            name=name,
        )

    def online(_):
        scratch = ([pltpu.VMEM((A_HEADS, 1, tq), F32), pltpu.VMEM((A_HEADS, 128, tq), F32)]
                   + [pltpu.VMEM((tk, FLASH_QC), F32)] * FLASH_NBUF + [pltpu.VMEM((tk, FLASH_QC), BF16)] * 2)
        return call(_flash_kernel, [], scratch, "gqa_flash")(aqt, ak, avt, ya_prev)

    def bounded(_):
        scratch = [pltpu.VMEM((A_HEADS, 128, tq), F32)] + [pltpu.VMEM((tk, FLASH_QC), BF16)] * 2
        return call(_flash_bounded_kernel, [pl.BlockSpec(memory_space=pltpu.SMEM)], scratch,
                    "gqa_flash_bounded")(score_bound.reshape(1), aqt, ak, avt, ya_prev)

    return lax.cond(2.0 * score_bound <= FLASH_MAX_EXPONENT_SPAN, bounded, online, None)


def _na_kernel(q_ref, kp_ref, km_ref, kn_ref, vp_ref, vm_ref, vn_ref, kc_ref, vc_ref, bias_ref, mask_ref, prev_ref,
               o_ref, bm_ref, *, nb):
    del prev_ref
    b = pl.program_id(0)

    @pl.when((b == 0) | (b == 1) | (b == nb - 1))
    def _():
        for h in range(D_HEADS):
            bm_ref[h] = bias_ref[h].astype(F32) + mask_ref[0]

    def scores(h):
        hs = slice(64 * h, 64 * h + 64)
        q = q_ref[:, hs]
        kcat = jnp.concatenate([kp_ref[:, hs], km_ref[:, hs], kn_ref[:, hs]], axis=0)
        return _dot_nt(q, kcat) + bm_ref[h], _dot_nt(q, kc_ref[:, hs])

    def softmax(s_loc, s_ctx):
        m = jnp.maximum(jnp.max(s_loc, axis=-1, keepdims=True), jnp.max(s_ctx, axis=-1, keepdims=True))
        return jnp.exp(s_loc - m).astype(BF16), jnp.exp(s_ctx - m).astype(BF16)

    def attend(h, p_loc, p_ctx):
        vs = slice(128 * h, 128 * h + 128)
        vcat = jnp.concatenate([vp_ref[:, vs], vm_ref[:, vs], vn_ref[:, vs]], axis=0)
        o = _dot(p_loc, vcat) + _dot(p_ctx, vc_ref[:, vs])
        return o[:, :HEAD_DIM] / o[:, HEAD_DIM:HEAD_DIM + 1]

    s = {0: scores(0)}
    p, outs = {}, []
    for h in range(D_HEADS):
        if h + 1 < D_HEADS:
            s[h + 1] = scores(h + 1)
        p[h] = softmax(*s.pop(h))
        if h >= 1:
            outs.append(attend(h - 1, *p.pop(h - 1)))
    outs.append(attend(D_HEADS - 1, *p.pop(D_HEADS - 1)))
    o_ref[...] = jnp.concatenate(outs, axis=1).astype(BF16)


def _na_bias_tables(rel_bias, rows):
    wr = min(NA_WIN_R, rows)
    halo_rows = NA_HALO // GRID_W
    krows = NA_QROWS + 2 * halo_rows
    nb = rows // NA_QROWS
    qc = np.arange(GRID_W)[:, None]
    kc = np.arange(GRID_W)[None, :]
    cs = np.clip(qc - NA_WIN_C // 2, 0, GRID_W - NA_WIN_C)
    in_col = (kc >= cs) & (kc < cs + NA_WIN_C)
    e_col = (kc - qc + (NA_WIN_C - 1))[:, :, None] == np.arange(2 * NA_WIN_C - 1)
    qr_l = np.arange(NA_QROWS)[:, None]
    kr_l = np.arange(krows)[None, :]
    e_row = (kr_l - halo_rows - qr_l + (NA_WIN_R - 1))[:, :, None] == np.arange(2 * NA_WIN_R - 1)
    hi = lax.Precision.HIGHEST
    tmp = jnp.einsum("lhrc,qkr->lhqkc", rel_bias.astype(F32), e_row.astype(np.float32), precision=hi)
    full = jnp.einsum("lhqkc,pjc->lhqpkj", tmp, e_col.astype(np.float32), precision=hi)
    masks = []
    for b in (0, min(1, nb - 1), nb - 1):
        qr = NA_QROWS * b + qr_l
        kr = NA_QROWS * b - halo_rows + kr_l
        rs = np.clip(qr - wr // 2, 0, rows - wr)
        in_row = (kr >= rs) & (kr < rs + wr)
        mask = in_row[:, None, :, None] & in_col[None, :, None, :]
        masks.append(np.where(mask, 0.0, NEG_BIG).reshape(NA_QB, krows * GRID_W))
    return full.astype(BF16).reshape(-1, D_HEADS, NA_QB, krows * GRID_W), np.stack(masks).astype(np.float32)


def _na_latent(dq, dk, dv, bias, masks, yd_prev, *, layer, t_lat):
    n = dq.shape[0]
    nb = t_lat // NA_QB
    r = NA_QB // NA_HALO
    last_halo = t_lat // NA_HALO - 1
    ctx_blk = t_lat // (n - t_lat)
    n_ctx = n - t_lat

    def prev_map(b):
        return (jnp.maximum(r * b - 1, 0), 0)

    def next_map(b):
        return (jnp.minimum(r * b + r, last_halo), 0)

    def variant(b):
        return (jnp.where(b == 0, 0, jnp.where(b == nb - 1, 2, 1)), 0, 0)

    def specs(w):
        return (pl.BlockSpec((NA_QB, w), lambda b: (b, 0)), pl.BlockSpec((NA_HALO, w), prev_map),
                pl.BlockSpec((NA_HALO, w), next_map), pl.BlockSpec((n_ctx, w), lambda b: (ctx_blk, 0)))

    main, prev, nxt, ctx = specs(256)
    vmain, vprev, vnxt, vctx = specs(dv.shape[1])
    return pl.pallas_call(
        functools.partial(_na_kernel, nb=nb),
        grid=(nb,),
        in_specs=[main, prev, main, nxt, vprev, vmain, vnxt, ctx, vctx,
                  _layer_spec(bias.shape, layer),
                  pl.BlockSpec((1, NA_QB, NA_QB + 2 * NA_HALO), variant),
                  pl.BlockSpec(memory_space=pl.ANY)],
        out_specs=main,
        out_shape=jax.ShapeDtypeStruct((n, BRANCH_WIDTH), BF16),
        scratch_shapes=[pltpu.VMEM((D_HEADS, NA_QB, NA_QB + 2 * NA_HALO), F32)],
        input_output_aliases={11: 0},
        compiler_params=_cparams(("arbitrary",)),
        name="na_attn",
    )(dq, dk, dk, dk, dv, dv, dv, dk, dv, bias, masks, yd_prev)


def _pool_kernel(x_ref, xp_ref, xn_ref, w_ref, sc_ref, *rest, seg_lo, seg_hi, tm):
    o_ref, cat_ref, a_ref, b_ref = rest[-4:]
    start = seg_lo + pl.program_id(0) * tm
    x = x_ref[...]
    cat_ref[0:8, :] = jnp.where(start > seg_lo, xp_ref[...], 0.0)
    cat_ref[8:8 + tm, :] = x
    cat_ref[8 + tm:16 + tm, :] = jnp.where(start + tm < seg_hi, xn_ref[...], 0.0)
    cat_ref[16 + tm:32 + tm, :] = jnp.zeros((16, BRANCH_WIDTH), F32)
    a_ref[0:tm + 24, :] = cat_ref[0:tm + 24, :] + cat_ref[1:tm + 25, :]
    s2 = a_ref[7:7 + tm, :]
    b_ref[0:tm + 16, :] = a_ref[0:tm + 16, :] + a_ref[2:tm + 18, :]
    s4 = b_ref[6:6 + tm, :]
    a_ref[0:tm + 8, :] = b_ref[0:tm + 8, :] + b_ref[4:tm + 12, :]
    s8 = a_ref[4:4 + tm, :]
    s16 = a_ref[0:tm, :] + a_ref[8:8 + tm, :]

    pos = start - seg_lo + lax.broadcasted_iota(jnp.int32, (tm, 1), 0)
    seg_len = seg_hi - seg_lo

    def mean(sm, w):
        lo = jnp.clip(pos - w // 2, 0, seg_len)
        hi = jnp.clip(pos - w // 2 + w, 0, seg_len)
        return sm / (hi - lo).astype(F32)

    lane = lax.broadcasted_iota(jnp.int32, (tm, BRANCH_WIDTH), 1)
    gw = BRANCH_WIDTH // len(C_WINDOWS)
    pooled = jnp.where(lane < gw, mean(s2, 2),
                       jnp.where(lane < 2 * gw, mean(s4, 4),
                                 jnp.where(lane < 3 * gw, mean(s8, 8), mean(s16, 16)))) - x
    o_ref[...] = (_dot(pooled.astype(BF16), w_ref[...]) * sc_ref[...]).astype(BF16)


def _pool_segment(cx, w_bd, scale, prev_out, *, seg_lo, seg_hi, tm):
    n = cx.shape[0]
    r = tm // 8
    b0 = seg_lo // tm
    last8 = n // 8 - 1
    in_specs = [
        pl.BlockSpec((tm, 256), lambda i: (b0 + i, 0)),
        pl.BlockSpec((8, 256), lambda i: (jnp.maximum(r * (b0 + i) - 1, 0), 0)),
        pl.BlockSpec((8, 256), lambda i: (jnp.minimum(r * (b0 + i) + r, last8), 0)),
        _const_spec((256, 256)),
        _const_spec((1, 256)),
    ]
    args = [cx, cx, cx, w_bd, scale]
    aliases = {}
    if prev_out is not None:
        in_specs.append(pl.BlockSpec(memory_space=pl.ANY))
        args.append(prev_out)
        aliases = {5: 0}
    return pl.pallas_call(
        functools.partial(_pool_kernel, seg_lo=seg_lo, seg_hi=seg_hi, tm=tm),
        grid=((seg_hi - seg_lo) // tm,),
        in_specs=in_specs,
        out_specs=pl.BlockSpec((tm, 256), lambda i: (b0 + i, 0)),
        out_shape=jax.ShapeDtypeStruct((n, BRANCH_WIDTH), BF16),
        scratch_shapes=[pltpu.VMEM((tm + 32, 256), F32)] * 3,
        input_output_aliases=aliases,
        compiler_params=_cparams(("parallel",)),
        name="pool_mixer",
    )(*args)


def _pool_mixer(cx, w_bd, scale, *, t_lat):
    n = cx.shape[0]
    n_ctx = n - t_lat
    out = _pool_segment(cx, w_bd, scale, None, seg_lo=t_lat, seg_hi=n, tm=n_ctx)
    return _pool_segment(cx, w_bd, scale, out, seg_lo=0, seg_hi=t_lat, tm=min(POOL_TILE, t_lat))


def _hgrn_tables(rev):
    c_len, s, w = HG_CHUNK, HG_SUB, B_HEADS * B_KEY_DIM
    t = np.arange(c_len)
    tri = (t[None, :] >= t[:, None]) if rev else (t[None, :] <= t[:, None])
    pos = np.arange(s)
    keep = (pos[None, :] <= pos[:, None]) if rev else (pos[None, :] >= pos[:, None])
    keep_add = np.where(keep, 0.0, NEG_BIG)[:, :, None] * np.ones((1, 1, w))
    head_of_lane = np.arange(w) // B_KEY_DIM
    col = np.arange(B_HEADS * s)
    sel = (col[None, None, :] == (head_of_lane[None, :, None] * s + pos[:, None, None]))
    bd = (col[:, None] // s) == head_of_lane[None, :]
    hh = head_of_lane[:, None] == head_of_lane[None, :]
    return (jnp.asarray(tri, BF16), jnp.asarray(keep_add, F32), jnp.asarray(sel, BF16),
            jnp.asarray(bd, F32), jnp.asarray(hh, F32))


def _hgrn_kernel(qf_ref, ff_ref, vf_ref, qr_ref, fr_ref, vr_ref, lbf_ref, lbr_ref, trif_ref, keepf_ref, trir_ref,
                 keepr_ref, sel_ref, bd_ref, hh_ref, of_ref, or_ref, stf_ref, str_ref):
    @pl.when(pl.program_id(0) == 0)
    def _():
        stf_ref[...] = jnp.zeros(stf_ref.shape, F32)
        str_ref[...] = jnp.zeros(str_ref.shape, F32)

    st_f, st_r = stf_ref[...], str_ref[...]
    n_chunks = HG_BLOCK // HG_CHUNK
    for step in range(n_chunks):
        rows = slice(step * HG_CHUNK, (step + 1) * HG_CHUNK)
        o, st_f = _hgrn_chunk(qf_ref[rows, :], ff_ref[rows, :], vf_ref[rows, :], st_f, lbf_ref, trif_ref,
                              keepf_ref, sel_ref, bd_ref, hh_ref, rev=False)
        of_ref[rows, :] = o
        ch = n_chunks - 1 - step
        rows = slice(ch * HG_CHUNK, (ch + 1) * HG_CHUNK)
        o, st_r = _hgrn_chunk(qr_ref[rows, :], fr_ref[rows, :], vr_ref[rows, :], st_r, lbr_ref, trir_ref,
                              keepr_ref, sel_ref, bd_ref, hh_ref, rev=True)
        or_ref[rows, :] = o
    stf_ref[...] = st_f
    str_ref[...] = st_r


def _hgrn_chunk(q_bf, f_pre, v, st, lb_ref, tri_ref, keep_ref, sel_ref, bd_ref, hh_ref, *, rev):
    c_len = HG_CHUNK
    n_sub = c_len // HG_SUB
    w = B_HEADS * B_KEY_DIM
    lb = lb_ref[...]
    f = lb + (1.0 - lb) * _sigmoid(f_pre)
    k = 1.0 - f
    lf = jnp.log(f)
    q = q_bf.astype(F32)
    bd = bd_ref[...]

    c = _split_dot_left(tri_ref[...], lf)

    q3 = q.reshape(n_sub, HG_SUB, w)
    k3 = k.reshape(n_sub, HG_SUB, w)
    c3 = c.reshape(n_sub, HG_SUB, w)
    a_diag = None
    for sg in range(HG_SUB):
        ks = jnp.broadcast_to(k3[:, sg:sg + 1, :], k3.shape)
        cs = jnp.broadcast_to(c3[:, sg:sg + 1, :], c3.shape)
        wgt = (q3 * ks) * jnp.exp((c3 - cs) + keep_ref[sg][None])
        part = _dot(wgt.reshape(c_len, w).astype(BF16), sel_ref[sg])
        a_diag = part if a_diag is None else a_diag + part

    o_parts = [None] * n_sub
    for j in range(n_sub):
        r0 = j * HG_SUB
        r_last = r0 if rev else r0 + HG_SUB - 1
        e_j = c[r_last:r_last + 1, :]
        kj = k[r0:r0 + HG_SUB, :] * jnp.exp(e_j - c[r0:r0 + HG_SUB, :])
        kbd = (jnp.concatenate([kj] * B_HEADS, axis=0) * bd).astype(BF16)
        vbd = (jnp.concatenate([v[r0:r0 + HG_SUB, :].astype(F32)] * B_HEADS, axis=0) * bd).astype(BF16)
        lo, hi = (0, r0) if rev else (r0 + HG_SUB, c_len)
        pieces = [a_diag[r0:r0 + HG_SUB]]
        if hi > lo:
            qj = (q[lo:hi] * jnp.exp(c[lo:hi] - e_j)).astype(BF16)
            a_off = _dot_nt(qj, kbd)
            pieces = [a_off] + pieces if rev else pieces + [a_off]
        first = 0 if rev else j
        if (sum(p.shape[0] for p in pieces) % 16) != 0:
            pad = jnp.zeros((HG_SUB, a_diag.shape[1]), F32)
            pieces = pieces + [pad] if rev else [pad] + pieces
            first = first if rev else first - 1
        a_j = jnp.concatenate(pieces, axis=0).astype(BF16)
        contrib = _dot(a_j, vbd)
        for i in range(contrib.shape[0] // HG_SUB):
            piece = contrib[i * HG_SUB:(i + 1) * HG_SUB]
            o_parts[first + i] = piece if o_parts[first + i] is None else o_parts[first + i] + piece
    o = jnp.concatenate(o_parts, axis=0)

    o = o + _dot_nt((q * jnp.exp(c)).astype(BF16), st.astype(BF16))
    r_end = 0 if rev else c_len - 1
    c_end = c[r_end:r_end + 1, :]
    k_end = (k * jnp.exp(c_end - c)).astype(BF16)
    return o, jnp.exp(c_end) * st + _dot_tn(v, k_end) * hh_ref[...]


def _hgrn_scans(bq, f_fwd, f_rev, bi, lb_fwd, lb_rev, *, t_lat):
    n = bq.shape[0]
    c_len = HG_BLOCK
    n_lat = t_lat // c_len
    n_all = n // c_len
    n_ctx = n_all - n_lat

    def blk_fwd(i):
        return (jnp.where(i < n_ctx, n_lat + i, i - n_ctx), 0)

    def blk_rev(i):
        return (jnp.where(i < n_ctx, n_all - 1 - i, n_lat - 1 - (i - n_ctx)), 0)

    tile_f = pl.BlockSpec((c_len, 256), blk_fwd)
    tile_r = pl.BlockSpec((c_len, 256), blk_rev)
    tri_f, keep_f, sel, bd, hh = _hgrn_tables(False)
    tri_r, keep_r, _, _, _ = _hgrn_tables(True)
    consts = [lb_fwd, lb_rev, tri_f, keep_f, tri_r, keep_r, sel, bd, hh]
    return pl.pallas_call(
        _hgrn_kernel,
        grid=(n_all,),
        in_specs=[tile_f, tile_f, tile_f, tile_r, tile_r, tile_r] + [_const_spec(t.shape) for t in consts],
        out_specs=[tile_f, tile_r],
        out_shape=[jax.ShapeDtypeStruct((n, BRANCH_WIDTH), F32)] * 2,
        scratch_shapes=[pltpu.VMEM((256, 256), F32)] * 2,
        compiler_params=_cparams(("arbitrary",)),
        name="hgrn_scans",
    )(bq, f_fwd, bi, bq, f_rev, bi, *consts)


def _merge_kernel(x_ref, mods_ref, ya_ref, of_ref, or_ref, bg_ref, gain_ref, segm_ref, yc_ref, yd_ref, gt_ref,
                  wb_ref, wo_ref, nrm_ref, wg_ref, wu_ref, wd_ref, *rest, t_lat, tm):
    fin_ref = rest[0] if len(rest) == 2 else None
    o_ref = rest[-1]
    lat_tile = pl.program_id(0) * tm < t_lat
    yb = _head_norm(of_ref[...] + or_ref[...], gain_ref[...], segm_ref[...]) * _silu(bg_ref[...].astype(F32))
    branches = (ya_ref[...], yb.astype(BF16), yc_ref[...], yd_ref[...])
    merged = None
    for n, y in enumerate(branches):
        term = gt_ref[:, n * D_MODEL:(n + 1) * D_MODEL].astype(F32) * _dot(y, wb_ref[n])
        merged = term if merged is None else merged + term
    y = _dot(merged.astype(BF16), wo_ref[...])
    x_mid = x_ref[...] + _tile_mod(mods_ref, 5, lat_tile) * y
    out = _ffn_math(x_mid, lat_tile, mods_ref, nrm_ref, wg_ref, wu_ref, wd_ref, 6)
    o_ref[...] = out if fin_ref is None else _rms_rows(out) * fin_ref[...]


def _merge_ffn(x_all, mods, ya, o_fwd, o_rev, bg, b_gain, segm, yc, yd, gates, wb, wo, nrm, wg, wu, wd, *,
               layer, t_lat, n_rows, final_gain=None):
    n, d = x_all.shape
    tm = TOKEN_TILE
    final = final_gain is not None

    def tile(w):
        return pl.BlockSpec((tm, w), lambda i: (i, 0))

    return pl.pallas_call(
        functools.partial(_merge_kernel, t_lat=t_lat, tm=tm),
        grid=(n_rows // tm,),
        in_specs=[tile(d), _const_spec((2, N_MOD, d)), tile(256), tile(256), tile(256), tile(256),
                  _const_spec((1, 256)), _const_spec((256, 256)), tile(256), tile(256),
                  tile(N_BRANCH * d), _layer_spec(wb.shape, layer), _layer_spec(wo.shape, layer),
                  _const_spec((1, d)), _layer_spec(wg.shape, layer), _layer_spec(wu.shape, layer),
                  _layer_spec(wd.shape, layer)] + ([_const_spec((1, d))] if final else []),
        out_specs=tile(d),
        out_shape=jax.ShapeDtypeStruct((n_rows if final else n, d), F32),
        input_output_aliases={} if final else {0: 0},
        compiler_params=_cparams(("parallel",)),
        name="merge_ffn",
    )(x_all, mods, ya, o_fwd, o_rev, bg, b_gain, segm, yc, yd, gates, wb, wo, nrm, wg, wu, wd,
      *([final_gain] if final else []))


def _rope_tables_padded(t_lat, n_ctx):
    rows = t_lat // GRID_W
    half = HEAD_DIM // 2
    nf = half // 2
    inv = 1.0 / (ROPE_THETA ** (jnp.arange(0, half, 2, dtype=F32) / half))
    ang_r = jnp.arange(rows, dtype=F32)[:, None] * inv
    ang_c = jnp.arange(GRID_W, dtype=F32)[:, None] * inv

    def over_rows(a):
        return jnp.broadcast_to(a[:, None, :], (rows, GRID_W, nf))

    def over_cols(a):
        return jnp.broadcast_to(a[None, :, :], (rows, GRID_W, nf))

    cos = jnp.concatenate([over_rows(jnp.cos(ang_r))] * 2 + [over_cols(jnp.cos(ang_c))] * 2, axis=2)
    sin = jnp.concatenate([over_rows(-jnp.sin(ang_r)), over_rows(jnp.sin(ang_r)),
                           over_cols(-jnp.sin(ang_c)), over_cols(jnp.sin(ang_c))], axis=2)
    cos = cos.reshape(t_lat, HEAD_DIM)
    sin = sin.reshape(t_lat, HEAD_DIM)
    cos = jnp.concatenate([cos, jnp.ones((n_ctx, HEAD_DIM), F32)], axis=0)
    sin = jnp.concatenate([sin, jnp.zeros((n_ctx, HEAD_DIM), F32)], axis=0)
    return jnp.tile(cos, (1, 2)), jnp.tile(sin, (1, 2))


def _block_diag(w_group):
    g, ci, co = w_group.shape
    out = jnp.zeros((g * ci, g * co), w_group.dtype)
    for n in range(g):
        out = out.at[n * ci:(n + 1) * ci, n * co:(n + 1) * co].set(w_group[n])
    return out


def kernel(x, c, ctx, c_ctx, w_ada, b_ada, ffn1_norm, ffn1_w_gate, ffn1_w_up, ffn1_w_down, mix_norm, w_in, a_q_norm, a_k_norm, b_lb_logits, b_o_norm, c_w_group, c_scale, d_rel_bias, w_branch, w_out, ffn2_norm, ffn2_w_gate, ffn2_w_up, ffn2_w_down, final_norm):
    assert x.shape[0] == 1 and ctx.shape[0] == 1 and w_in.shape[2] == IN_WIDTH
    depth = w_ada.shape[0]
    t_lat = x.shape[1]
    n_ctx = ctx.shape[1]
    n = t_lat + n_ctx
    rows = t_lat // GRID_W
    assert t_lat % max(NA_QB, min(FLASH_TQ, t_lat), n_ctx) == 0 and n_ctx % TOKEN_TILE == 0 and rows >= 2 * NA_QROWS

    x_all = None
    mods_all = _ada_mods(c, c_ctx, w_ada, b_ada)
    cos, sin = _rope_tables_padded(t_lat, n_ctx)
    segm = (jnp.kron(jnp.eye(BRANCH_WIDTH // HEAD_DIM), jnp.ones((HEAD_DIM, HEAD_DIM))) / HEAD_DIM).astype(BF16)
    lb_all = jnp.cumsum(jax.nn.softmax(b_lb_logits.astype(F32), axis=0), axis=0)
    lb_all = lb_all - lb_all[:1]
    wg1, wu1, wd1 = ffn1_w_gate.astype(BF16), ffn1_w_up.astype(BF16), ffn1_w_down.astype(BF16)
    wg2, wu2, wd2 = ffn2_w_gate.astype(BF16), ffn2_w_up.astype(BF16), ffn2_w_down.astype(BF16)
    w_in_b, w_branch_b, w_out_b = w_in.astype(BF16), w_branch.astype(BF16), w_out.astype(BF16)
    na_bias, na_masks = _na_bias_tables(d_rel_bias, rows)

    for l in range(depth):
        with_ctx_out = l < depth - 1
        n_rows = n if with_ctx_out else t_lat
        mods = mods_all[l]
        x_all = _ffn_half(x[0] if l == 0 else x_all, mods, ffn1_norm[l][None], wg1, wu1, wd1, layer=l, t_lat=t_lat,
                          k0=0, n_rows=n, x_ctx=ctx[0] if l == 0 else None)
        (aq, ak, av, bq, bff, bfb, bi, bg, cx, dq, dk, dv, gates) = _in_proj(
            x_all, mods, mix_norm[l][None], w_in_b, cos, sin,
            jnp.tile(a_q_norm[l], A_HEADS)[None], jnp.tile(a_k_norm[l], A_KV_HEADS)[None], segm, layer=l, t_lat=t_lat)
        if with_ctx_out:
            ya, yd = _ctx_attn(aq, ak, av, dq, dk, dv, t_lat=t_lat)
        else:
            ya = jnp.zeros((n, BRANCH_WIDTH), BF16)
            yd = ya
        score_bound = (HEAD_DIM * ATTN_SCALE * LOG2_E * FLASH_BOUND_MARGIN
                       * jnp.max(jnp.abs(a_q_norm[l])) * jnp.max(jnp.abs(a_k_norm[l]))).astype(F32)
        ya = _gqa_latent(aq, ak, av, ya, score_bound, t_lat=t_lat)
        yd = _na_latent(dq, dk, dv, na_bias, na_masks, yd, layer=l, t_lat=t_lat)
        yc = _pool_mixer(cx, _block_diag(c_w_group[l]).astype(BF16), c_scale[l][None], t_lat=t_lat)
        o_fwd, o_rev = _hgrn_scans(bq, bff, bfb, bi, lb_all[l, 0][None], lb_all[l, 1][None], t_lat=t_lat)
        x_all = _merge_ffn(x_all, mods, ya, o_fwd, o_rev, bg, jnp.tile(b_o_norm[l], B_HEADS)[None], segm, yc, yd,
                           gates, w_branch_b, w_out_b, ffn2_norm[l][None], wg2, wu2, wd2, layer=l, t_lat=t_lat,
                           n_rows=n_rows, final_gain=None if with_ctx_out else final_norm[None])
    return x_all[None]
```
